```python
import jax, jax.numpy as jnp
from jax import lax
import numpy as np

D_MODEL = 1024
BATCH = 8
SEQ = 8192
DEPTH = 4

D_PLE = 256
N_MIXERS = 2
GDN_HEADS = 8
GDN_DK = 128
GDN_DV = 128
GDN_CONV = 4
GDN_CHUNK = 64
GDN_QK = GDN_HEADS * GDN_DK
GDN_V = GDN_HEADS * GDN_DV
GDN_IN = 2 * GDN_QK + 2 * GDN_V + 2 * GDN_HEADS
FOX_HEADS = 16
FOX_DH = 64
FOX_BLOCK = 128
FOX_W = FOX_HEADS * FOX_DH
FOX_IN = 4 * FOX_W + FOX_HEADS
N_GDN = (DEPTH + 1) // 2
N_FOX = DEPTH // 2
DN_ALPHA = (2 * DEPTH) ** 0.25
DN_BETA = (8 * DEPTH) ** -0.25
LN_EPS = 1e-5
RMS_EPS = 1e-6

kernel_name = 'hybrid_gdn_fox_deepnorm_ple'

F32 = jnp.float32


def layer_norm(x, g, b):
    xf = x.astype(F32)
    mu = jnp.mean(xf, -1, keepdims=True)
    var = jnp.mean(jnp.square(xf - mu), -1, keepdims=True)
    return ((xf - mu) * lax.rsqrt(var + LN_EPS) * g + b).astype(x.dtype)


def rms_norm(x, g):
    xf = x.astype(F32)
    return (xf * lax.rsqrt(jnp.mean(xf * xf, -1, keepdims=True) + RMS_EPS) * g).astype(x.dtype)


def l2_normalize(x):
    xf = x.astype(F32)
    return xf * lax.rsqrt(jnp.sum(xf * xf, -1, keepdims=True) + RMS_EPS)


def causal_depthwise_conv(x, w):
    K, C = w.shape
    return lax.conv_general_dilated(x, w[:, None, :].astype(x.dtype), window_strides=(1,),
                                    padding=[(K - 1, 0)], dimension_numbers=('NWC', 'WIO', 'NWC'),
                                    feature_group_count=C)


def gated_delta_rule(q, k, v, beta, g):
    B, S, H, dk = q.shape
    dv = v.shape[-1]
    C = GDN_CHUNK
    N = S // C

    def to_chunks(t):
        return t.astype(F32).reshape(B, N, C, H, *t.shape[3:]).swapaxes(2, 3)

    q = to_chunks(q) * (dk ** -0.5)
    k = to_chunks(k)
    v = to_chunks(v)
    beta = to_chunks(beta)
    g = jnp.cumsum(to_chunks(g), axis=-1)
    causal = jnp.tril(jnp.ones((C, C), bool))
    strict = jnp.tril(jnp.ones((C, C), bool), -1)
    decay = jnp.exp(jnp.where(causal, g[..., :, None] - g[..., None, :], -jnp.inf))
    kb = k * beta[..., None]
    L = jnp.where(strict, jnp.einsum('bnhid,bnhjd->bnhij', kb, k) * decay, 0.0)
    eye = jnp.eye(C, dtype=F32)
    rhs = jnp.concatenate([v * beta[..., None], kb * jnp.exp(g)[..., None]], axis=-1)
    sol = lax.linalg.triangular_solve(eye + L, rhs, left_side=True, lower=True, unit_diagonal=True)
    u, w = sol[..., :dv], sol[..., dv:]
    a_qk = jnp.where(causal, jnp.einsum('bnhid,bnhjd->bnhij', q, k) * decay, 0.0)
    q_dec = q * jnp.exp(g)[..., None]
    k_dec = k * jnp.exp(g[..., -1:] - g)[..., None]
    g_last = jnp.exp(g[..., -1])

    def step(state, xs):
        q_c, k_c, u_c, w_c, a_c, gl = xs
        v_new = u_c - jnp.einsum('bhcd,bhde->bhce', w_c, state)
        o_c = jnp.einsum('bhcd,bhde->bhce', q_c, state) + jnp.einsum('bhij,bhje->bhie', a_c, v_new)
        state = state * gl[..., None, None] + jnp.einsum('bhcd,bhce->bhde', k_c, v_new)
        return state, o_c

    xs = tuple(jnp.moveaxis(t, 1, 0) for t in (q_dec, k_dec, u, w, a_qk, g_last))
    s0 = jnp.zeros((B, H, dk, dv), F32)
    _, o = lax.scan(step, s0, xs)
    return o.transpose(1, 0, 3, 2, 4).reshape(B, S, H, dv)


def gdn_mixer(x, w_in, conv_w, a_log, dt_bias, norm_g, w_out):
    B, S, _ = x.shape
    h = x @ w_in
    qkv, z, b_raw, a_raw = jnp.split(h, [2 * GDN_QK + GDN_V, 2 * GDN_QK + 2 * GDN_V,
                                         2 * GDN_QK + 2 * GDN_V + GDN_HEADS], axis=-1)
    qkv = jax.nn.silu(causal_depthwise_conv(qkv, conv_w))
    q, k, v = jnp.split(qkv, [GDN_QK, 2 * GDN_QK], axis=-1)
    q = l2_normalize(q.reshape(B, S, GDN_HEADS, GDN_DK))
    k = l2_normalize(k.reshape(B, S, GDN_HEADS, GDN_DK))
    v = v.reshape(B, S, GDN_HEADS, GDN_DV)
    beta = jax.nn.sigmoid(b_raw.astype(F32))
    g = -jnp.exp(a_log.astype(F32)) * jax.nn.softplus(a_raw.astype(F32) + dt_bias.astype(F32))
    o = gated_delta_rule(q, k, v, beta, g)
    o = rms_norm(o, norm_g) * jax.nn.silu(z.reshape(B, S, GDN_HEADS, GDN_DV).astype(F32))
    return o.reshape(B, S, GDN_V).astype(x.dtype) @ w_out


def forgetting_attention(q, k, v, c):
    B, S, H, Dh = q.shape
    nb = S // FOX_BLOCK
    kf = k.astype(F32)
    vf = v.astype(F32)
    c_t = c.transpose(0, 2, 1)
    q_blocks = (q.astype(F32) * (Dh ** -0.5)).reshape(B, nb, FOX_BLOCK, H, Dh).transpose(1, 0, 3, 2, 4)
    c_blocks = c_t.reshape(B, H, nb, FOX_BLOCK).transpose(2, 0, 1, 3)
    key_pos = jnp.arange(S)

    def one_block(args):
        qb, cb, blk = args
        logits = jnp.einsum('bhtd,bshd->bhts', qb, kf) + cb[..., None] - c_t[:, :, None, :]
        q_pos = blk * FOX_BLOCK + jnp.arange(FOX_BLOCK)
        logits = jnp.where(key_pos[None, :] <= q_pos[:, None], logits, -jnp.inf)
        probs = jax.nn.softmax(logits, axis=-1)
        return jnp.einsum('bhts,bshd->bthd', probs, vf)

    o = lax.map(one_block, (q_blocks, c_blocks, jnp.arange(nb)))
    return o.transpose(1, 0, 2, 3, 4).reshape(B, S, H, Dh)


def fox_mixer(x, w_in, b_f, q_norm_g, k_norm_g, w_out):
    B, S, _ = x.shape
    h = x @ w_in
    q, k, v, z, f_raw = jnp.split(h, [FOX_W, 2 * FOX_W, 3 * FOX_W, 4 * FOX_W], axis=-1)
    q = rms_norm(q.reshape(B, S, FOX_HEADS, FOX_DH), q_norm_g)
    k = rms_norm(k.reshape(B, S, FOX_HEADS, FOX_DH), k_norm_g)
    v = v.reshape(B, S, FOX_HEADS, FOX_DH)
    log_f = jax.nn.log_sigmoid(f_raw.astype(F32) + b_f.astype(F32))
    c = jnp.cumsum(log_f, axis=1)
    o = forgetting_attention(q, k, v, c)
    o = o * jax.nn.silu(z.reshape(B, S, FOX_HEADS, FOX_DH).astype(F32))
    return o.reshape(B, S, FOX_W).astype(x.dtype) @ w_out


def _fwd_setup_inputs(seed: int = 0) -> dict:
    key = jax.random.key(seed)
    ks = jax.random.split(key, 20)
    nrm = jax.random.normal
    x = nrm(ks[0], (BATCH, SEQ, D_MODEL), F32)
    p = nrm(ks[1], (DEPTH, BATCH, SEQ, D_PLE), F32)
    ln_g = 1.0 + 0.02 * nrm(ks[2], (DEPTH, D_MODEL), F32)
    ln_b = 0.02 * nrm(ks[3], (DEPTH, D_MODEL), F32)
    ple_w_gate = nrm(ks[4], (DEPTH, D_MODEL, D_MODEL), F32) * D_MODEL ** -0.5
    ple_w_proj = nrm(ks[5], (DEPTH, D_PLE, D_MODEL), F32) * D_PLE ** -0.5
    gdn_w_in = nrm(ks[6], (N_GDN, D_MODEL, GDN_IN), F32) * D_MODEL ** -0.5
    gdn_conv_w = nrm(ks[7], (N_GDN, GDN_CONV, 2 * GDN_QK + GDN_V), F32) * GDN_CONV ** -0.5
    gdn_a_log = jnp.log(jax.random.uniform(ks[8], (N_GDN, GDN_HEADS), F32, 1.0, 16.0))
    dt = jnp.exp(jax.random.uniform(ks[9], (N_GDN, GDN_HEADS), F32, np.log(1e-3), np.log(1e-1)))
    gdn_dt_bias = dt + jnp.log(-jnp.expm1(-dt))
    gdn_norm_g = 1.0 + 0.02 * nrm(ks[10], (N_GDN, GDN_DV), F32)
    gdn_w_out = nrm(ks[11], (N_GDN, GDN_V, D_MODEL), F32) * (GDN_V ** -0.5) * DN_BETA
    fox_w_in = nrm(ks[12], (N_FOX, D_MODEL, FOX_IN), F32) * D_MODEL ** -0.5
    fox_b_f = jax.random.uniform(ks[13], (N_FOX, FOX_HEADS), F32, 1.0, 5.0)
    fox_q_norm_g = 1.0 + 0.02 * nrm(ks[14], (N_FOX, FOX_DH), F32)
    fox_k_norm_g = 1.0 + 0.02 * nrm(ks[15], (N_FOX, FOX_DH), F32)
    fox_w_out = nrm(ks[16], (N_FOX, FOX_W, D_MODEL), F32) * (FOX_W ** -0.5) * DN_BETA
    return {'x': x, 'p': p, 'ln_g': ln_g, 'ln_b': ln_b, 'ple_w_gate': ple_w_gate,
            'ple_w_proj': ple_w_proj, 'gdn_w_in': gdn_w_in, 'gdn_conv_w': gdn_conv_w,
            'gdn_a_log': gdn_a_log, 'gdn_dt_bias': gdn_dt_bias, 'gdn_norm_g': gdn_norm_g,
            'gdn_w_out': gdn_w_out, 'fox_w_in': fox_w_in, 'fox_b_f': fox_b_f,
            'fox_q_norm_g': fox_q_norm_g, 'fox_k_norm_g': fox_k_norm_g, 'fox_w_out': fox_w_out}


def _fwd_reference(x, p, ln_g, ln_b, ple_w_gate, ple_w_proj, gdn_w_in, gdn_conv_w, gdn_a_log,
              gdn_dt_bias, gdn_norm_g, gdn_w_out, fox_w_in, fox_b_f, fox_q_norm_g,
              fox_k_norm_g, fox_w_out):
    for i in range(DEPTH):
        j = i // N_MIXERS
        if i % N_MIXERS == 0:
            y = gdn_mixer(x, gdn_w_in[j], gdn_conv_w[j], gdn_a_log[j], gdn_dt_bias[j],
                          gdn_norm_g[j], gdn_w_out[j])
        else:
            y = fox_mixer(x, fox_w_in[j], fox_b_f[j], fox_q_norm_g[j], fox_k_norm_g[j],
                          fox_w_out[j])
        x = layer_norm(DN_ALPHA * x + y, ln_g[i], ln_b[i])
        gate = jax.nn.sigmoid((x @ ple_w_gate[i]).astype(F32))
        x = x + (gate * (p[i] @ ple_w_proj[i]).astype(F32)).astype(x.dtype)
    return x


import jax as _jax
import jax.numpy as _jnp

TWIN_FORMAT = 'train_step'
FWD_PARAMS = ['x', 'p', 'ln_g', 'ln_b', 'ple_w_gate', 'ple_w_proj', 'gdn_w_in', 'gdn_conv_w', 'gdn_a_log', 'gdn_dt_bias', 'gdn_norm_g', 'gdn_w_out', 'fox_w_in', 'fox_b_f', 'fox_q_norm_g', 'fox_k_norm_g', 'fox_w_out']
TWIN_WEIGHTS = ['ln_g', 'ln_b', 'ple_w_gate', 'ple_w_proj', 'gdn_w_in', 'gdn_conv_w', 'gdn_a_log', 'gdn_dt_bias', 'gdn_norm_g', 'gdn_w_out', 'fox_w_in', 'fox_b_f', 'fox_q_norm_g', 'fox_k_norm_g', 'fox_w_out']
TWIN_DIFF_INPUT = 'x'
TWIN_INPUTS = ['x', 'p', 'ln_g', 'ln_b', 'ple_w_gate', 'ple_w_proj', 'gdn_w_in', 'gdn_conv_w', 'gdn_a_log', 'gdn_dt_bias', 'gdn_norm_g', 'gdn_w_out', 'fox_w_in', 'fox_b_f', 'fox_q_norm_g', 'fox_k_norm_g', 'fox_w_out', 'loss_target', 'm_ln_g', 'm_ln_b', 'm_ple_w_gate', 'm_ple_w_proj', 'm_gdn_w_in', 'm_gdn_conv_w', 'm_gdn_a_log', 'm_gdn_dt_bias', 'm_gdn_norm_g', 'm_gdn_w_out', 'm_fox_w_in', 'm_fox_b_f', 'm_fox_q_norm_g', 'm_fox_k_norm_g', 'm_fox_w_out', 'v_ln_g', 'v_ln_b', 'v_ple_w_gate', 'v_ple_w_proj', 'v_gdn_w_in', 'v_gdn_conv_w', 'v_gdn_a_log', 'v_gdn_dt_bias', 'v_gdn_norm_g', 'v_gdn_w_out', 'v_fox_w_in', 'v_fox_b_f', 'v_fox_q_norm_g', 'v_fox_k_norm_g', 'v_fox_w_out']
TWIN_OUTPUTS = ['loss', 'grad_x', 'grad_ln_g', 'grad_ln_b', 'grad_ple_w_gate', 'grad_ple_w_proj', 'grad_gdn_w_in', 'grad_gdn_conv_w', 'grad_gdn_a_log', 'grad_gdn_dt_bias', 'grad_gdn_norm_g', 'grad_gdn_w_out', 'grad_fox_w_in', 'grad_fox_b_f', 'grad_fox_q_norm_g', 'grad_fox_k_norm_g', 'grad_fox_w_out', 'delta_ln_g', 'delta_ln_b', 'delta_ple_w_gate', 'delta_ple_w_proj', 'delta_gdn_w_in', 'delta_gdn_conv_w', 'delta_gdn_a_log', 'delta_gdn_dt_bias', 'delta_gdn_norm_g', 'delta_gdn_w_out', 'delta_fox_w_in', 'delta_fox_b_f', 'delta_fox_q_norm_g', 'delta_fox_k_norm_g', 'delta_fox_w_out', 'new_m_ln_g', 'new_m_ln_b', 'new_m_ple_w_gate', 'new_m_ple_w_proj', 'new_m_gdn_w_in', 'new_m_gdn_conv_w', 'new_m_gdn_a_log', 'new_m_gdn_dt_bias', 'new_m_gdn_norm_g', 'new_m_gdn_w_out', 'new_m_fox_w_in', 'new_m_fox_b_f', 'new_m_fox_q_norm_g', 'new_m_fox_k_norm_g', 'new_m_fox_w_out', 'new_v_ln_g', 'new_v_ln_b', 'new_v_ple_w_gate', 'new_v_ple_w_proj', 'new_v_gdn_w_in', 'new_v_gdn_conv_w', 'new_v_gdn_a_log', 'new_v_gdn_dt_bias', 'new_v_gdn_norm_g', 'new_v_gdn_w_out', 'new_v_fox_w_in', 'new_v_fox_b_f', 'new_v_fox_q_norm_g', 'new_v_fox_k_norm_g', 'new_v_fox_w_out']
TWIN_LEAF_KINDS = {'loss': 'loss', 'grad_x': 'grad_x', 'grad_ln_g': 'grad_w', 'grad_ln_b': 'grad_w', 'grad_ple_w_gate': 'grad_w', 'grad_ple_w_proj': 'grad_w', 'grad_gdn_w_in': 'grad_w', 'grad_gdn_conv_w': 'grad_w', 'grad_gdn_a_log': 'grad_w', 'grad_gdn_dt_bias': 'grad_w', 'grad_gdn_norm_g': 'grad_w', 'grad_gdn_w_out': 'grad_w', 'grad_fox_w_in': 'grad_w', 'grad_fox_b_f': 'grad_w', 'grad_fox_q_norm_g': 'grad_w', 'grad_fox_k_norm_g': 'grad_w', 'grad_fox_w_out': 'grad_w', 'delta_ln_g': 'delta_w', 'delta_ln_b': 'delta_w', 'delta_ple_w_gate': 'delta_w', 'delta_ple_w_proj': 'delta_w', 'delta_gdn_w_in': 'delta_w', 'delta_gdn_conv_w': 'delta_w', 'delta_gdn_a_log': 'delta_w', 'delta_gdn_dt_bias': 'delta_w', 'delta_gdn_norm_g': 'delta_w', 'delta_gdn_w_out': 'delta_w', 'delta_fox_w_in': 'delta_w', 'delta_fox_b_f': 'delta_w', 'delta_fox_q_norm_g': 'delta_w', 'delta_fox_k_norm_g': 'delta_w', 'delta_fox_w_out': 'delta_w', 'new_m_ln_g': 'new_m', 'new_m_ln_b': 'new_m', 'new_m_ple_w_gate': 'new_m', 'new_m_ple_w_proj': 'new_m', 'new_m_gdn_w_in': 'new_m', 'new_m_gdn_conv_w': 'new_m', 'new_m_gdn_a_log': 'new_m', 'new_m_gdn_dt_bias': 'new_m', 'new_m_gdn_norm_g': 'new_m', 'new_m_gdn_w_out': 'new_m', 'new_m_fox_w_in': 'new_m', 'new_m_fox_b_f': 'new_m', 'new_m_fox_q_norm_g': 'new_m', 'new_m_fox_k_norm_g': 'new_m', 'new_m_fox_w_out': 'new_m', 'new_v_ln_g': 'new_v', 'new_v_ln_b': 'new_v', 'new_v_ple_w_gate': 'new_v', 'new_v_ple_w_proj': 'new_v', 'new_v_gdn_w_in': 'new_v', 'new_v_gdn_conv_w': 'new_v', 'new_v_gdn_a_log': 'new_v', 'new_v_gdn_dt_bias': 'new_v', 'new_v_gdn_norm_g': 'new_v', 'new_v_gdn_w_out': 'new_v', 'new_v_fox_w_in': 'new_v', 'new_v_fox_b_f': 'new_v', 'new_v_fox_q_norm_g': 'new_v', 'new_v_fox_k_norm_g': 'new_v', 'new_v_fox_w_out': 'new_v'}


def _forward(args):
    return _fwd_reference(*[args[k] for k in FWD_PARAMS])


def _output_shape():
    def fwd():
        inp = _fwd_setup_inputs(0)
        return _fwd_reference(*[inp[k] for k in FWD_PARAMS])
    out = _jax.eval_shape(fwd)
    return out.shape, out.dtype

N_MICROBATCH = 1
ADAM_LR = 0.001
ADAM_B1 = 0.9
ADAM_B2 = 0.999
ADAM_EPS = 1e-08
ADAM_WD = 0.01
ADAM_STEP = 10
PER_EXAMPLE_BATCH_AXIS = {'x': 0, 'p': 1, 'loss_target': 0}
SHARED_INPUTS = []
_WEIGHT_DTYPES = {'ln_g': _jnp.float32, 'ln_b': _jnp.float32, 'ple_w_gate': _jnp.float32, 'ple_w_proj': _jnp.float32, 'gdn_w_in': _jnp.float32, 'gdn_conv_w': _jnp.float32, 'gdn_a_log': _jnp.float32, 'gdn_dt_bias': _jnp.float32, 'gdn_norm_g': _jnp.float32, 'gdn_w_out': _jnp.float32, 'fox_w_in': _jnp.float32, 'fox_b_f': _jnp.float32, 'fox_q_norm_g': _jnp.float32, 'fox_k_norm_g': _jnp.float32, 'fox_w_out': _jnp.float32}
MOMENT_SCALE = {'ln_g': 3.309023e+01, 'ln_b': 5.630052e+00, 'ple_w_gate': 1.255210e-01, 'ple_w_proj': 4.229983e-01, 'gdn_w_in': 3.053424e-02, 'gdn_conv_w': 5.238395e-02, 'gdn_a_log': 1.646415e-01, 'gdn_dt_bias': 1.560807e-01, 'gdn_norm_g': 3.916455e-01, 'gdn_w_out': 3.162954e-01, 'fox_w_in': 1.511307e-02, 'fox_b_f': 5.526051e-02, 'fox_q_norm_g': 5.103885e-02, 'fox_k_norm_g': 5.148913e-02, 'fox_w_out': 4.044848e-02}


def _to_microbatches(a, axis):
    t = _jnp.moveaxis(a, axis, 0)
    t = t.reshape((N_MICROBATCH, t.shape[0] // N_MICROBATCH) + t.shape[1:])
    return _jnp.moveaxis(t, 1, axis + 1)


def setup_inputs(seed: int = 0) -> dict:
    inp = _fwd_setup_inputs(seed)
    key = _jax.random.fold_in(_jax.random.key(seed), 7919)
    shape, _ = _output_shape()
    out = dict(inp)
    out["loss_target"] = _jax.random.normal(_jax.random.fold_in(key, 0), shape, _jnp.float32)
    for i, name in enumerate(TWIN_WEIGHTS):
        w = inp[name].astype(_jnp.float32)
        if MOMENT_SCALE is None:
            s = _jnp.sqrt(_jnp.mean(_jnp.square(w)) + 1e-30)
        else:
            s = MOMENT_SCALE[name]
        km, kv = _jax.random.split(_jax.random.fold_in(key, i + 1))
        out[name] = w
        out["m_" + name] = s * _jax.random.normal(km, w.shape, _jnp.float32)
        out["v_" + name] = (s * s) * _jax.random.uniform(kv, w.shape, _jnp.float32, 0.5, 1.5)
    if N_MICROBATCH > 1:
        for name, axis in PER_EXAMPLE_BATCH_AXIS.items():
            out[name] = _to_microbatches(out[name], axis)
    return {'x': out['x'], 'p': out['p'], 'ln_g': out['ln_g'], 'ln_b': out['ln_b'], 'ple_w_gate': out['ple_w_gate'], 'ple_w_proj': out['ple_w_proj'], 'gdn_w_in': out['gdn_w_in'], 'gdn_conv_w': out['gdn_conv_w'], 'gdn_a_log': out['gdn_a_log'], 'gdn_dt_bias': out['gdn_dt_bias'], 'gdn_norm_g': out['gdn_norm_g'], 'gdn_w_out': out['gdn_w_out'], 'fox_w_in': out['fox_w_in'], 'fox_b_f': out['fox_b_f'], 'fox_q_norm_g': out['fox_q_norm_g'], 'fox_k_norm_g': out['fox_k_norm_g'], 'fox_w_out': out['fox_w_out'], 'loss_target': out['loss_target'], 'm_ln_g': out['m_ln_g'], 'm_ln_b': out['m_ln_b'], 'm_ple_w_gate': out['m_ple_w_gate'], 'm_ple_w_proj': out['m_ple_w_proj'], 'm_gdn_w_in': out['m_gdn_w_in'], 'm_gdn_conv_w': out['m_gdn_conv_w'], 'm_gdn_a_log': out['m_gdn_a_log'], 'm_gdn_dt_bias': out['m_gdn_dt_bias'], 'm_gdn_norm_g': out['m_gdn_norm_g'], 'm_gdn_w_out': out['m_gdn_w_out'], 'm_fox_w_in': out['m_fox_w_in'], 'm_fox_b_f': out['m_fox_b_f'], 'm_fox_q_norm_g': out['m_fox_q_norm_g'], 'm_fox_k_norm_g': out['m_fox_k_norm_g'], 'm_fox_w_out': out['m_fox_w_out'], 'v_ln_g': out['v_ln_g'], 'v_ln_b': out['v_ln_b'], 'v_ple_w_gate': out['v_ple_w_gate'], 'v_ple_w_proj': out['v_ple_w_proj'], 'v_gdn_w_in': out['v_gdn_w_in'], 'v_gdn_conv_w': out['v_gdn_conv_w'], 'v_gdn_a_log': out['v_gdn_a_log'], 'v_gdn_dt_bias': out['v_gdn_dt_bias'], 'v_gdn_norm_g': out['v_gdn_norm_g'], 'v_gdn_w_out': out['v_gdn_w_out'], 'v_fox_w_in': out['v_fox_w_in'], 'v_fox_b_f': out['v_fox_b_f'], 'v_fox_q_norm_g': out['v_fox_q_norm_g'], 'v_fox_k_norm_g': out['v_fox_k_norm_g'], 'v_fox_w_out': out['v_fox_w_out']}


def _loss(weights, diff, rest, loss_target):
    with _jax.named_scope("forward"):
        args = {**rest, TWIN_DIFF_INPUT: diff, **{k: w.astype(_WEIGHT_DTYPES[k]) for k, w in weights.items()}}
        y = _forward(args)
    with _jax.named_scope("loss_head"):
        err = _jnp.square(y.astype(_jnp.float32) - loss_target)
        return 0.5 * _jnp.sum(_jnp.mean(err, axis=-1)) if err.ndim else 0.5 * err


def _adamw(w, g, m, v):
    m = ADAM_B1 * m + (1.0 - ADAM_B1) * g
    v = ADAM_B2 * v + (1.0 - ADAM_B2) * _jnp.square(g)
    m_hat = m / (1.0 - ADAM_B1 ** ADAM_STEP)
    v_hat = v / (1.0 - ADAM_B2 ** ADAM_STEP)
    delta = -ADAM_LR * (m_hat / (_jnp.sqrt(v_hat) + ADAM_EPS) + ADAM_WD * w)
    return delta, m, v


def reference(x, p, ln_g, ln_b, ple_w_gate, ple_w_proj, gdn_w_in, gdn_conv_w, gdn_a_log, gdn_dt_bias, gdn_norm_g, gdn_w_out, fox_w_in, fox_b_f, fox_q_norm_g, fox_k_norm_g, fox_w_out, loss_target, m_ln_g, m_ln_b, m_ple_w_gate, m_ple_w_proj, m_gdn_w_in, m_gdn_conv_w, m_gdn_a_log, m_gdn_dt_bias, m_gdn_norm_g, m_gdn_w_out, m_fox_w_in, m_fox_b_f, m_fox_q_norm_g, m_fox_k_norm_g, m_fox_w_out, v_ln_g, v_ln_b, v_ple_w_gate, v_ple_w_proj, v_gdn_w_in, v_gdn_conv_w, v_gdn_a_log, v_gdn_dt_bias, v_gdn_norm_g, v_gdn_w_out, v_fox_w_in, v_fox_b_f, v_fox_q_norm_g, v_fox_k_norm_g, v_fox_w_out):
    given = dict(x=x, p=p, ln_g=ln_g, ln_b=ln_b, ple_w_gate=ple_w_gate, ple_w_proj=ple_w_proj, gdn_w_in=gdn_w_in, gdn_conv_w=gdn_conv_w, gdn_a_log=gdn_a_log, gdn_dt_bias=gdn_dt_bias, gdn_norm_g=gdn_norm_g, gdn_w_out=gdn_w_out, fox_w_in=fox_w_in, fox_b_f=fox_b_f, fox_q_norm_g=fox_q_norm_g, fox_k_norm_g=fox_k_norm_g, fox_w_out=fox_w_out, loss_target=loss_target, m_ln_g=m_ln_g, m_ln_b=m_ln_b, m_ple_w_gate=m_ple_w_gate, m_ple_w_proj=m_ple_w_proj, m_gdn_w_in=m_gdn_w_in, m_gdn_conv_w=m_gdn_conv_w, m_gdn_a_log=m_gdn_a_log, m_gdn_dt_bias=m_gdn_dt_bias, m_gdn_norm_g=m_gdn_norm_g, m_gdn_w_out=m_gdn_w_out, m_fox_w_in=m_fox_w_in, m_fox_b_f=m_fox_b_f, m_fox_q_norm_g=m_fox_q_norm_g, m_fox_k_norm_g=m_fox_k_norm_g, m_fox_w_out=m_fox_w_out, v_ln_g=v_ln_g, v_ln_b=v_ln_b, v_ple_w_gate=v_ple_w_gate, v_ple_w_proj=v_ple_w_proj, v_gdn_w_in=v_gdn_w_in, v_gdn_conv_w=v_gdn_conv_w, v_gdn_a_log=v_gdn_a_log, v_gdn_dt_bias=v_gdn_dt_bias, v_gdn_norm_g=v_gdn_norm_g, v_gdn_w_out=v_gdn_w_out, v_fox_w_in=v_fox_w_in, v_fox_b_f=v_fox_b_f, v_fox_q_norm_g=v_fox_q_norm_g, v_fox_k_norm_g=v_fox_k_norm_g, v_fox_w_out=v_fox_w_out)
    weights = {n: given[n] for n in TWIN_WEIGHTS}
    shared = {n: given[n] for n in SHARED_INPUTS}
    per_example = {n: given[n] for n in ['x', 'p']}
    grad_fn = _jax.value_and_grad(_loss, argnums=(0, 1))

    def one_microbatch(ex, loss_target):
        ex = dict(ex)
        diff = ex.pop(TWIN_DIFF_INPUT)
        return grad_fn(weights, diff, {**shared, **ex}, loss_target)

    if N_MICROBATCH == 1:
        loss, (grad_w, grad_x) = one_microbatch(per_example, given["loss_target"])
    else:
        def body(carry, xs):
            loss_sum, grad_sum = carry
            l_k, (gw_k, gx_k) = one_microbatch(xs[0], xs[1])
            with _jax.named_scope("update"):
                return (loss_sum + l_k, _jax.tree.map(_jnp.add, grad_sum, gw_k)), gx_k

        init = (_jnp.zeros((), _jnp.float32), _jax.tree.map(_jnp.zeros_like, weights))
        (loss, grad_w), grad_x = _jax.lax.scan(body, init, (per_example, given["loss_target"]))
    with _jax.named_scope("update"):
        delta_w, new_m, new_v = {}, {}, {}
        for n in TWIN_WEIGHTS:
            delta_w[n], new_m[n], new_v[n] = _adamw(weights[n], grad_w[n], given["m_" + n], given["v_" + n])
    return (loss, grad_x, *[grad_w[n] for n in TWIN_WEIGHTS], *[delta_w[n] for n in TWIN_WEIGHTS],
            *[new_m[n] for n in TWIN_WEIGHTS], *[new_v[n] for n in TWIN_WEIGHTS])
```

```python
import functools

import jax
import jax.numpy as jnp
from jax import lax
from jax.experimental import pallas as pl
from jax.experimental.pallas import tpu as pltpu

F32 = jnp.float32
BF16 = jnp.bfloat16

LANES = 128
SUBLANES = 8
VMEM_LIMIT_BYTES = 56 * 1024 * 1024

GDN_DK = 128
GDN_CHUNK = 64
GDN_CONV = 4
FOX_DH = 64
LN_EPS = 1e-5
RMS_EPS = 1e-6

ADAM_LR = 0.001
ADAM_B1 = 0.9
ADAM_B2 = 0.999
ADAM_EPS = 1e-08
ADAM_WD = 0.01
ADAM_STEP = 10

HI = lax.Precision.HIGHEST
_DIMS = {"nn": (((1,), (0,)), ((), ())), "nt": (((1,), (1,)), ((), ())), "tn": (((0,), (0,)), ((), ()))}


def _params(*sem):
    return pltpu.CompilerParams(dimension_semantics=sem, vmem_limit_bytes=VMEM_LIMIT_BYTES)


def _raw_mm(a, b, form, hi):
    if hi:
        return lax.dot_general(a.astype(F32), b.astype(F32), _DIMS[form], precision=HI, preferred_element_type=F32)
    return lax.dot_general(a.astype(BF16), b.astype(BF16), _DIMS[form], preferred_element_type=F32)


@functools.partial(jax.custom_vjp, nondiff_argnums=(2, 3))
def _mm(a, b, form, hi):
    return _raw_mm(a, b, form, hi)


def _mm_fwd(a, b, form, hi):
    return _raw_mm(a, b, form, hi), (a, b)


def _mm_bwd(form, hi, res, g):
    a, b = res
    if form == "nn":
        return _mm(g, b, "nt", hi), _mm(a, g, "tn", hi)
    if form == "nt":
        return _mm(g, b, "nn", hi), _mm(g, a, "tn", hi)
    return _mm(b, g, "nt", hi), _mm(a, g, "nn", hi)


_mm.defvjp(_mm_fwd, _mm_bwd)


def _silu(x):
    return x * jax.nn.sigmoid(x)


def _softplus(x):
    return jnp.maximum(x, 0.0) + jnp.log1p(jnp.exp(-jnp.abs(x)))


def _iota(shape, dim):
    return lax.broadcasted_iota(jnp.int32, shape, dim)


def _matmul(a, b, *, ta=False, tb=False, out_dtype=F32, add=None, add_scale=1.0, tm=512, tn=512, tk=512, name):
    if ta:
        K, M = a.shape
    else:
        M, K = a.shape
    if tb:
        N, K2 = b.shape
    else:
        K2, N = b.shape
    assert K == K2, (a.shape, b.shape, ta, tb)
    tm, tn, tk = min(tm, M), min(tn, N), min(tk, K)
    assert M % tm == 0 and N % tn == 0 and K % tk == 0, (M, N, K, tm, tn, tk)
    nk = K // tk
    form = ("t" if ta else "n") + ("t" if tb else "n")
    dims = (((0 if ta else 1,), (1 if tb else 0,)), ((), ()))
    del form
    a_spec = pl.BlockSpec((tk, tm), lambda i, j, k: (k, i)) if ta else pl.BlockSpec((tm, tk), lambda i, j, k: (i, k))
    b_spec = pl.BlockSpec((tn, tk), lambda i, j, k: (j, k)) if tb else pl.BlockSpec((tk, tn), lambda i, j, k: (k, j))
    o_spec = pl.BlockSpec((tm, tn), lambda i, j, k: (i, j))
    has_add = add is not None

    def body(*refs):
        if has_add:
            a_ref, b_ref, add_ref, o_ref, acc_ref = refs
        else:
            a_ref, b_ref, o_ref, acc_ref = refs
        k = pl.program_id(2)
        part = lax.dot_general(a_ref[...].astype(BF16), b_ref[...].astype(BF16), dims, preferred_element_type=F32)

        def finish(total):
            if has_add:
                total = total + add_scale * add_ref[...].astype(F32)
            o_ref[...] = total.astype(o_ref.dtype)

        if nk == 1:
            finish(part)
        else:
            @pl.when(k == 0)
            def _():
                acc_ref[...] = part

            @pl.when(jnp.logical_and(k > 0, k < nk - 1))
            def _():
                acc_ref[...] += part

            @pl.when(k == nk - 1)
            def _():
                finish(acc_ref[...] + part)

    in_specs = [a_spec, b_spec] + ([o_spec] if has_add else [])
    args = (a, b) + ((add,) if has_add else ())
    return pl.pallas_call(
        body, name=name, grid=(M // tm, N // tn, nk),
        in_specs=in_specs, out_specs=o_spec,
        out_shape=jax.ShapeDtypeStruct((M, N), out_dtype),
        scratch_shapes=[pltpu.VMEM((tm, tn), F32)],
        compiler_params=_params("parallel", "parallel", "arbitrary"),
    )(*args)


def _rowcall(body_fn, rows, consts, out_rows, out_accs, *, tr, name, reverse=False, scratch=()):
    def arr_spec(r):
        return r if isinstance(r, tuple) else (r, None)

    S = arr_spec(rows[0])[0].shape[0]
    tr = min(tr, S)
    assert S % tr == 0
    n = S // tr
    ridx = (lambda i: (n - 1 - i, 0)) if reverse else (lambda i: (i, 0))
    in_specs, args = [], []
    for r in rows:
        arr, spec = arr_spec(r)
        args.append(arr)
        in_specs.append(spec(tr, n) if spec is not None else pl.BlockSpec((tr, arr.shape[1]), ridx))
    for c in consts:
        args.append(c)
        in_specs.append(pl.BlockSpec(c.shape, lambda i: (0, 0)))
    out_specs, out_shape = [], []
    for (ncol, dt) in out_rows:
        out_specs.append(pl.BlockSpec((tr, ncol), ridx))
        out_shape.append(jax.ShapeDtypeStruct((S, ncol), dt))
    for shp in out_accs:
        out_specs.append(pl.BlockSpec(shp, lambda i: (0, 0)))
        out_shape.append(jax.ShapeDtypeStruct(shp, F32))
    nr, nc, no, na = len(rows), len(consts), len(out_rows), len(out_accs)

    def kernel(*refs):
        row_refs = refs[:nr]
        const_refs = refs[nr:nr + nc]
        orow_refs = refs[nr + nc:nr + nc + no]
        oacc_refs = refs[nr + nc + no:nr + nc + no + na]
        scr = refs[nr + nc + no + na:]
        step = pl.program_id(0)
        blk = (n - 1 - step) if reverse else step

        @pl.when(step == 0)
        def _():
            for acc in oacc_refs:
                acc[...] = jnp.zeros(acc.shape, F32)

        body_fn(row_refs, const_refs, orow_refs, oacc_refs, scr, step, blk)

    outs = pl.pallas_call(
        kernel, name=name, grid=(n,), in_specs=in_specs, out_specs=out_specs, out_shape=out_shape,
        scratch_shapes=list(scratch), compiler_params=_params("arbitrary"),
    )(*args)
    return outs


def _row(v):
    return v.astype(F32).reshape(1, -1)


def _pad_lanes(v, width=LANES, offset=0):
    pad = [(0, 0)] * (v.ndim - 1) + [(offset, width - offset - v.shape[-1])]
    return jnp.pad(v, pad)


def _ln_tile(x, y, g, b, alpha):
    u = alpha * x + y
    mu = jnp.mean(u, -1, keepdims=True)
    d = u - mu
    var = jnp.mean(d * d, -1, keepdims=True)
    return d * lax.rsqrt(var + LN_EPS) * g + b


def _ln_fwd(x, y, g, b, alpha, *, tr, name):
    D = x.shape[1]

    def body(rows, consts, orows, oaccs, scr, step, blk):
        orows[0][...] = _ln_tile(rows[0][...], rows[1][...], consts[0][...], consts[1][...], alpha)

    return _rowcall(body, [x, y], [_row(g), _row(b)], [(D, F32)], [], tr=tr, name=name)[0]


def _ln_bwd(x, y, g, b, dxo, t, alpha, *, tr, name):
    D = x.shape[1]

    def body(rows, consts, orows, oaccs, scr, step, blk):
        xv, yv = rows[0][...], rows[1][...]
        ct = rows[2][...] + rows[3][...]
        _, vjp = jax.vjp(lambda yy, gg, bb: _ln_tile(xv, yy, gg, bb, alpha), yv, consts[0][...], consts[1][...])
        du, dg, db = vjp(ct)
        orows[0][...] = du
        oaccs[0][...] += dg
        oaccs[1][...] += db

    du, dg, db = _rowcall(body, [x, y, dxo, t], [_row(g), _row(b)], [(D, F32)], [(1, D), (1, D)], tr=tr, name=name)
    return du, dg[0], db[0]


def _ple_fwd(x_ln, gp, pp, *, tr, name):
    D = x_ln.shape[1]

    def body(rows, consts, orows, oaccs, scr, step, blk):
        orows[0][...] = rows[0][...] + jax.nn.sigmoid(rows[1][...]) * rows[2][...]

    return _rowcall(body, [x_ln, gp, pp], [], [(D, F32)], [], tr=tr, name=name)[0]


def _ple_bwd(dxo, gp, pp, *, tr, name):
    D = dxo.shape[1]

    def body(rows, consts, orows, oaccs, scr, step, blk):
        d = rows[0][...]
        s = jax.nn.sigmoid(rows[1][...])
        orows[0][...] = (d * rows[2][...] * s * (1.0 - s)).astype(BF16)
        orows[1][...] = (d * s).astype(BF16)

    return _rowcall(body, [dxo, gp, pp], [], [(D, BF16), (D, BF16)], [], tr=tr, name=name)


def _loss_fwd_bwd(xf, target, *, tr, name):
    D = xf.shape[1]

    def body(rows, consts, orows, oaccs, scr, step, blk):
        err = rows[0][...] - rows[1][...]
        orows[0][...] = err * (1.0 / D)
        part = jnp.sum(err * err, axis=0, keepdims=True) * (0.5 / D)
        oaccs[0][...] += part

    dx, lrow = _rowcall(body, [xf, target], [], [(D, F32)], [(1, D)], tr=tr, name=name)
    return lrow, dx


def _gdn_qk_tile(c):
    y = _silu(c)
    return y * lax.rsqrt(jnp.sum(y * y, -1, keepdims=True) + RMS_EPS)


def _make_bg_fn(H):
    def fn(hs, alog, dtb):
        lane = _iota((1, LANES), 1)
        beta = jax.nn.sigmoid(hs)
        g = -jnp.exp(alog) * _softplus(hs + dtb)
        return jnp.where(lane < H, beta, jnp.where(lane < 2 * H, g, 0.0))
    return fn


def _halo_spec(ncol):
    def make(tr, n):
        per = tr // SUBLANES
        return pl.BlockSpec((SUBLANES, ncol), lambda i: (jnp.maximum(i * per - 1, 0), 0))
    return make


def _halo_spec_rev(ncol):
    def make(tr, n):
        per = tr // SUBLANES
        return pl.BlockSpec((SUBLANES, ncol), lambda i: (jnp.maximum((n - 1 - i) * per - 1, 0), 0))
    return make


def _gdn_pre_fwd(h_main, h_small, conv_w8, alog_row, dtb_row, H, *, tr, name):
    W = H * GDN_DK
    C3 = 3 * W
    bg_fn = _make_bg_fn(H)

    def body(rows, consts, orows, oaccs, scr, step, blk):
        main_ref, halo_ref, hs_ref = rows
        w_ref, alog_ref, dtb_ref = consts
        q_ref, k_ref, v_ref, bg_ref = orows
        xs = scr[0]
        trr = main_ref.shape[0]
        xs[pl.ds(SUBLANES, trr), :] = main_ref[...]
        xs[pl.ds(0, SUBLANES), :] = jnp.where(blk > 0, halo_ref[...], 0.0)
        for s in range(C3 // LANES):
            ls = slice(s * LANES, (s + 1) * LANES)
            c = jnp.zeros((trr, LANES), F32)
            for j in range(GDN_CONV):
                c = c + w_ref[GDN_CONV - 1 - j:GDN_CONV - j, ls] * xs[pl.ds(SUBLANES - j, trr), ls]
            if s < 2 * H:
                out = _gdn_qk_tile(c)
                (q_ref if s < H else k_ref)[:, (s % H) * LANES:(s % H + 1) * LANES] = out
            else:
                v_ref[:, (s - 2 * H) * LANES:(s - 2 * H + 1) * LANES] = _silu(c)
        bg_ref[...] = bg_fn(hs_ref[...], alog_ref[...], dtb_ref[...])

    main = (h_main, lambda tr_, n: pl.BlockSpec((tr_, C3), lambda i: (i, 0)))
    halo = (h_main, _halo_spec(C3))
    trr = min(tr, h_main.shape[0])
    return _rowcall(body, [main, halo, h_small], [conv_w8, alog_row, dtb_row],
                    [(W, F32), (W, F32), (W, F32), (LANES, F32)], [], tr=tr, name=name,
                    scratch=[pltpu.VMEM((trr + SUBLANES, C3), F32)])


def _gdn_pre_bwd(h_main, h_small, conv_w8, alog_row, dtb_row, dq, dk, dv, dbg, dz, H, *, tr, name):
    W = H * GDN_DK
    C3 = 3 * W
    bg_fn = _make_bg_fn(H)

    def body(rows, consts, orows, oaccs, scr, step, blk):
        main_ref, halo_ref, hs_ref, dq_ref, dk_ref, dv_ref, dbg_ref, dz_ref = rows
        w_ref, alog_ref, dtb_ref = consts
        dmain_ref, dhs_ref = orows
        dw_ref, dalog_ref, ddtb_ref = oaccs
        xs, dcs = scr
        trr = main_ref.shape[0]
        xs[pl.ds(SUBLANES, trr), :] = main_ref[...]
        xs[pl.ds(0, SUBLANES), :] = jnp.where(blk > 0, halo_ref[...], 0.0)

        @pl.when(step == 0)
        def _():
            dcs[pl.ds(trr, SUBLANES), :] = jnp.zeros((SUBLANES, C3), F32)

        for s in range(C3 // LANES):
            ls = slice(s * LANES, (s + 1) * LANES)
            c = jnp.zeros((trr, LANES), F32)
            for j in range(GDN_CONV):
                c = c + w_ref[GDN_CONV - 1 - j:GDN_CONV - j, ls] * xs[pl.ds(SUBLANES - j, trr), ls]
            if s < 2 * H:
                src = dq_ref if s < H else dk_ref
                ct = src[:, (s % H) * LANES:(s % H + 1) * LANES]
                _, vjp = jax.vjp(_gdn_qk_tile, c)
            else:
                ct = dv_ref[:, (s - 2 * H) * LANES:(s - 2 * H + 1) * LANES]
                _, vjp = jax.vjp(_silu, c)
            dcs[pl.ds(0, trr), ls] = vjp(ct)[0]
        for s in range(C3 // LANES):
            ls = slice(s * LANES, (s + 1) * LANES)
            dx = jnp.zeros((trr, LANES), F32)
            dc0 = dcs[pl.ds(0, trr), ls]
            for j in range(GDN_CONV):
                wrow = w_ref[GDN_CONV - 1 - j:GDN_CONV - j, ls]
                dx = dx + wrow * dcs[pl.ds(j, trr), ls]
                dw_ref[GDN_CONV - 1 - j:GDN_CONV - j, ls] += jnp.sum(dc0 * xs[pl.ds(SUBLANES - j, trr), ls], axis=0, keepdims=True)
            dmain_ref[:, ls] = dx.astype(BF16)
        dmain_ref[:, C3:] = dz_ref[...]
        dcs[pl.ds(trr, SUBLANES), :] = dcs[pl.ds(0, SUBLANES), :]
        _, vjp = jax.vjp(bg_fn, hs_ref[...], alog_ref[...], dtb_ref[...])
        dhs, dalog, ddtb = vjp(dbg_ref[...])
        dhs_ref[...] = dhs.astype(BF16)
        dalog_ref[...] += dalog
        ddtb_ref[...] += ddtb

    trr = min(tr, h_main.shape[0])
    main = (h_main, lambda tr_, n: pl.BlockSpec((tr_, C3), lambda i: (n - 1 - i, 0)))
    halo = (h_main, _halo_spec_rev(C3))
    return _rowcall(body, [main, halo, h_small, dq, dk, dv, dbg, dz], [conv_w8, alog_row, dtb_row],
                    [(4 * W, BF16), (LANES, BF16)], [(SUBLANES, C3), (1, LANES), (1, LANES)],
                    tr=tr, name=name, reverse=True,
                    scratch=[pltpu.VMEM((trr + SUBLANES, C3), F32), pltpu.VMEM((trr + SUBLANES, C3), F32)])


def _gdn_chunk(q, k, v, beta, g, S):
    C, dk = q.shape
    dv = v.shape[1]
    ri, ci = _iota((C, C), 0), _iota((C, C), 1)
    causal, strict = ri >= ci, ri > ci
    lane0 = (_iota((1, LANES), 1) == 0).astype(F32)
    G = g * jnp.ones((1, LANES), F32)
    gcB = _mm(causal.astype(F32), G, "nn", True)
    gc = jnp.sum(gcB * lane0, -1, keepdims=True)
    gc_row = _mm(jnp.ones((C, 1), F32) * lane0, gcB, "nt", True)
    decay = jnp.where(causal, jnp.exp(jnp.where(causal, gc - gc_row, 0.0)), 0.0)
    kb = k * beta
    L = jnp.where(strict, _mm(kb, k, "nt", False) * decay, 0.0)
    X = (ri == ci).astype(F32) - L
    P = _mm(L, L, "nn", True)
    n_sq = max(1, (C - 1).bit_length() - 1)
    for it in range(n_sq):
        X = X + _mm(X, P, "nn", True)
        if it < n_sq - 1:
            P = _mm(P, P, "nn", True)
    egc = jnp.exp(gc)
    u = _mm(X, v * beta, "nn", True)
    w = _mm(X, kb * egc, "nn", True)
    qs = q * (dk ** -0.5)
    A = jnp.where(causal, _mm(qs, k, "nt", False) * decay, 0.0)
    q_dec = qs * egc
    glB = _mm(jnp.ones((C, C), F32), G, "nn", True)
    gl = jnp.sum(glB * lane0, -1, keepdims=True)
    k_dec = k * jnp.exp(gl - gc)
    glS = _mm(jnp.ones((dk, C), F32), g * jnp.ones((1, dv), F32), "nn", True)
    v_new = u - _mm(w, S, "nn", False)
    o = _mm(q_dec, S, "nn", False) + _mm(A, v_new, "nn", False)
    S_new = S * jnp.exp(glS) + _mm(k_dec, v_new, "tn", False)
    return o, S_new


def _head_col(tile, idx):
    lane = _iota((1, LANES), 1)
    return jnp.sum(jnp.where(lane == idx, tile, 0.0), -1, keepdims=True)


def _gdn_rule_fwd(q, k, v, bg, H, *, name):
    S_len = q.shape[0]
    C = min(GDN_CHUNK, S_len)
    N = S_len // C
    dk = dv = GDN_DK

    def body(q_ref, k_ref, v_ref, bg_ref, o_ref, st_ref, s_scr):
        n, h = pl.program_id(0), pl.program_id(1)

        @pl.when(n == 0)
        def _():
            s_scr[h] = jnp.zeros((dk, dv), F32)

        S = s_scr[h]
        st_ref[0] = S
        bgt = bg_ref[...]
        o, S_new = _gdn_chunk(q_ref[...], k_ref[...], v_ref[...], _head_col(bgt, h), _head_col(bgt, h + H), S)
        o_ref[...] = o
        s_scr[h] = S_new

    hs = pl.BlockSpec((C, dk), lambda n, h: (n, h))
    return pl.pallas_call(
        body, name=name, grid=(N, H),
        in_specs=[hs, hs, hs, pl.BlockSpec((C, LANES), lambda n, h: (n, 0))],
        out_specs=[hs, pl.BlockSpec((1, dk, dv), lambda n, h: (n * H + h, 0, 0))],
        out_shape=[jax.ShapeDtypeStruct((S_len, H * dv), F32), jax.ShapeDtypeStruct((N * H, dk, dv), F32)],
        scratch_shapes=[pltpu.VMEM((H, dk, dv), F32)],
        compiler_params=_params("arbitrary", "arbitrary"),
    )(q, k, v, bg)


def _gdn_rule_bwd(q, k, v, bg, states, do, H, *, name):
    S_len = q.shape[0]
    C = min(GDN_CHUNK, S_len)
    N = S_len // C
    dk = dv = GDN_DK

    def body(q_ref, k_ref, v_ref, bg_ref, st_ref, do_ref, dq_ref, dk_ref, dv_ref, dbg_ref, ds_scr):
        step, h = pl.program_id(0), pl.program_id(1)

        @pl.when(step == 0)
        def _():
            ds_scr[h] = jnp.zeros((dk, dv), F32)

        @pl.when(h == 0)
        def _():
            dbg_ref[...] = jnp.zeros(dbg_ref.shape, F32)

        bgt = bg_ref[...]
        beta, g = _head_col(bgt, h), _head_col(bgt, h + H)
        _, vjp = jax.vjp(_gdn_chunk, q_ref[...], k_ref[...], v_ref[...], beta, g, st_ref[0])
        dq, dkk, dvv, dbeta, dg, dS = vjp((do_ref[...], ds_scr[h]))
        dq_ref[...] = dq
        dk_ref[...] = dkk
        dv_ref[...] = dvv
        lane = _iota((1, LANES), 1)
        dbg_ref[...] += jnp.where(lane == h, dbeta, 0.0) + jnp.where(lane == h + H, dg, 0.0)
        ds_scr[h] = dS

    hs = pl.BlockSpec((C, dk), lambda s, h: (N - 1 - s, h))
    bgs = pl.BlockSpec((C, LANES), lambda s, h: (N - 1 - s, 0))
    return pl.pallas_call(
        body, name=name, grid=(N, H),
        in_specs=[hs, hs, hs, bgs, pl.BlockSpec((1, dk, dv), lambda s, h: ((N - 1 - s) * H + h, 0, 0)), hs],
        out_specs=[hs, hs, hs, bgs],
        out_shape=[jax.ShapeDtypeStruct((S_len, H * dk), F32)] * 3 + [jax.ShapeDtypeStruct((S_len, LANES), F32)],
        scratch_shapes=[pltpu.VMEM((H, dk, dv), F32)],
        compiler_params=_params("arbitrary", "arbitrary"),
    )(q, k, v, bg, states, do)


def _gdn_post_tile(o, z, g):
    return o * lax.rsqrt(jnp.mean(o * o, -1, keepdims=True) + RMS_EPS) * g * _silu(z)


def _gdn_post_fwd(o, h_main, norm_g, H, *, tr, name):
    W = H * GDN_DK

    def body(rows, consts, orows, oaccs, scr, step, blk):
        for h in range(H):
            ls = slice(h * LANES, (h + 1) * LANES)
            orows[0][:, ls] = _gdn_post_tile(rows[0][:, ls], rows[1][:, ls], consts[0][...]).astype(BF16)

    z = (h_main, lambda tr_, n: pl.BlockSpec((tr_, W), lambda i: (i, 3)))
    return _rowcall(body, [o, z], [_row(norm_g)], [(W, BF16)], [], tr=tr, name=name)[0]


def _gdn_post_bwd(o, h_main, norm_g, d_on, H, *, tr, name):
    W = H * GDN_DK

    def body(rows, consts, orows, oaccs, scr, step, blk):
        for h in range(H):
            ls = slice(h * LANES, (h + 1) * LANES)
            _, vjp = jax.vjp(_gdn_post_tile, rows[0][:, ls], rows[1][:, ls], consts[0][...])
            d_o, d_z, d_g = vjp(rows[2][:, ls])
            orows[0][:, ls] = d_o
            orows[1][:, ls] = d_z.astype(BF16)
            oaccs[0][...] += d_g

    z = (h_main, lambda tr_, n: pl.BlockSpec((tr_, W), lambda i: (i, 3)))
    return _rowcall(body, [o, z, d_on], [_row(norm_g)], [(W, F32), (W, BF16)], [(1, LANES)], tr=tr, name=name)


def _seg_ones():
    ri, ci = _iota((LANES, LANES), 0), _iota((LANES, LANES), 1)
    return ((ri < FOX_DH) == (ci < FOX_DH)).astype(F32)


def _fox_qk_tile(x, g2):
    ms = _mm(x * x, _seg_ones(), "nn", True) * (1.0 / FOX_DH)
    return x * lax.rsqrt(ms + RMS_EPS) * g2


def _make_lf_fn(Hf):
    def fn(hs, bf):
        lane = _iota((1, LANES), 1)
        return jnp.where(lane < Hf, -_softplus(-(hs + bf)), 0.0)
    return fn


def _fox_pre_fwd(h_main, h_small, gq2, gk2, bf_row, Hf, *, tr, name):
    W = Hf * FOX_DH
    lf_fn = _make_lf_fn(Hf)

    def body(rows, consts, orows, oaccs, scr, step, blk):
        qk_ref, v_ref, hs_ref = rows
        gq_ref, gk_ref, bf_ref = consts
        qn_ref, kn_ref, vb_ref, c_ref, cb_ref = orows
        carry = scr[0]
        trr = qk_ref.shape[0]

        @pl.when(step == 0)
        def _():
            carry[...] = jnp.zeros(carry.shape, F32)

        for s in range(W // LANES):
            ls = slice(s * LANES, (s + 1) * LANES)
            qn_ref[:, ls] = _fox_qk_tile(qk_ref[:, ls], gq_ref[...]).astype(BF16)
            kn_ref[:, ls] = _fox_qk_tile(qk_ref[:, W + s * LANES:W + (s + 1) * LANES], gk_ref[...]).astype(BF16)
        vb_ref[...] = v_ref[...].astype(BF16)
        lf = lf_fn(hs_ref[...], bf_ref[...])
        tril = (_iota((trr, trr), 0) >= _iota((trr, trr), 1)).astype(F32)
        c = _raw_mm(tril, lf, "nn", True) + carry[0:1, :]
        c_ref[...] = c
        carry[0:1, :] = c[trr - 1:trr, :]
        expand = (jnp.right_shift(_iota((LANES, W), 1), 6) == _iota((LANES, W), 0)).astype(F32)
        cb_ref[...] = _raw_mm(c, expand, "nn", True)

    qk = (h_main, lambda tr_, n: pl.BlockSpec((tr_, 2 * W), lambda i: (i, 0)))
    vv = (h_main, lambda tr_, n: pl.BlockSpec((tr_, W), lambda i: (i, 2)))
    return _rowcall(body, [qk, vv, h_small], [gq2, gk2, bf_row],
                    [(W, BF16), (W, BF16), (W, BF16), (LANES, F32), (W, F32)], [], tr=tr, name=name,
                    scratch=[pltpu.VMEM((SUBLANES, LANES), F32)])


def _fox_pre_bwd(h_main, h_small, gq2, gk2, bf_row, dqn, dkn, dvv, dz, dc, Hf, *, tr, name):
    W = Hf * FOX_DH
    lf_fn = _make_lf_fn(Hf)

    def body(rows, consts, orows, oaccs, scr, step, blk):
        qk_ref, hs_ref, dqn_ref, dkn_ref, dvv_ref, dz_ref, dc_ref = rows
        gq_ref, gk_ref, bf_ref = consts
        dmain_ref, dhs_ref = orows
        dgq_ref, dgk_ref, dbf_ref = oaccs
        carry = scr[0]
        trr = qk_ref.shape[0]

        @pl.when(step == 0)
        def _():
            carry[...] = jnp.zeros(carry.shape, F32)

        for s in range(W // LANES):
            ls = slice(s * LANES, (s + 1) * LANES)
            lk = slice(W + s * LANES, W + (s + 1) * LANES)
            _, vjp = jax.vjp(_fox_qk_tile, qk_ref[:, ls], gq_ref[...])
            dx, dg = vjp(dqn_ref[:, ls])
            dmain_ref[:, ls] = dx.astype(BF16)
            dgq_ref[...] += dg
            _, vjp = jax.vjp(_fox_qk_tile, qk_ref[:, lk], gk_ref[...])
            dx, dg = vjp(dkn_ref[:, ls])
            dmain_ref[:, lk] = dx.astype(BF16)
            dgk_ref[...] += dg
        dmain_ref[:, 2 * W:3 * W] = dvv_ref[...].astype(BF16)
        dmain_ref[:, 3 * W:] = dz_ref[...]
        dcv = dc_ref[...]
        triu = (_iota((trr, trr), 0) <= _iota((trr, trr), 1)).astype(F32)
        dlf = _raw_mm(triu, dcv, "nn", True) + carry[0:1, :]
        carry[0:1, :] = dlf[0:1, :]
        _, vjp = jax.vjp(lf_fn, hs_ref[...], bf_ref[...])
        dhs, dbf = vjp(dlf)
        dhs_ref[...] = dhs.astype(BF16)
        dbf_ref[...] += dbf

    qk = (h_main, lambda tr_, n: pl.BlockSpec((tr_, 2 * W), lambda i: (n - 1 - i, 0)))
    return _rowcall(body, [qk, h_small, dqn, dkn, dvv, dz, dc], [gq2, gk2, bf_row],
                    [(4 * W, BF16), (LANES, BF16)], [(1, LANES), (1, LANES), (1, LANES)],
                    tr=tr, name=name, reverse=True, scratch=[pltpu.VMEM((SUBLANES, LANES), F32)])


def _fox_attn_fwd(qn, kn, vb, c_b, c_rowp, *, tb, name):
    S_len, W = qn.shape
    HP = W // LANES
    tb = min(tb, S_len)
    nb = S_len // tb
    scale = FOX_DH ** -0.5

    def body(q_ref, k_ref, v_ref, cb_ref, cr_ref, o_ref, lse_ref, m_scr, l_scr, acc_scr):
        i, j = pl.program_id(1), pl.program_id(2)

        @pl.when(j == 0)
        def _():
            m_scr[...] = jnp.full(m_scr.shape, -jnp.inf, F32)
            l_scr[...] = jnp.zeros(l_scr.shape, F32)
            acc_scr[...] = jnp.zeros(acc_scr.shape, F32)

        @pl.when(j <= i)
        def _():
            q, k, v = q_ref[...], k_ref[...], v_ref[...]
            lo = _iota((1, LANES), 1) < FOX_DH
            causal = (j * tb + _iota((tb, tb), 1)) <= (i * tb + _iota((tb, tb), 0))
            pv = jnp.zeros((tb, LANES), F32)
            alpha_full = jnp.zeros((tb, LANES), F32)
            for hh in range(2):
                msk = lo if hh == 0 else jnp.logical_not(lo)
                qh = jnp.where(msk, q, jnp.zeros_like(q))
                s = lax.dot_general(qh, k, _DIMS["nt"], preferred_element_type=F32) * scale
                ct = cb_ref[:, hh * FOX_DH:hh * FOX_DH + 1]
                cs = cr_ref[hh:hh + 1, :]
                s = jnp.where(causal, s + (ct - cs), -jnp.inf)
                m_prev = m_scr[hh]
                m_new = jnp.maximum(m_prev, jnp.max(s, -1, keepdims=True))
                alpha = jnp.exp(m_prev - m_new)
                p = jnp.exp(s - m_new)
                l_scr[hh] = alpha * l_scr[hh] + jnp.sum(p, -1, keepdims=True)
                m_scr[hh] = m_new
                vh = jnp.where(msk, v, jnp.zeros_like(v))
                pv = pv + lax.dot_general(p.astype(BF16), vh, _DIMS["nn"], preferred_element_type=F32)
                alpha_full = jnp.where(msk, alpha, alpha_full)
            acc_scr[...] = acc_scr[...] * alpha_full + pv

        @pl.when(j == nb - 1)
        def _():
            lo = _iota((1, LANES), 1) < FOX_DH
            l_full = jnp.where(lo, l_scr[0], l_scr[1])
            m_full = jnp.where(lo, m_scr[0], m_scr[1])
            o_ref[...] = acc_scr[...] / l_full
            lse_ref[...] = m_full + jnp.log(l_full)

    qs = pl.BlockSpec((tb, LANES), lambda h, i, j: (i, h))
    ks = pl.BlockSpec((tb, LANES), lambda h, i, j: (jnp.minimum(j, i), h))
    crs = pl.BlockSpec((None, SUBLANES, tb), lambda h, i, j: (h, 0, jnp.minimum(j, i)))
    return pl.pallas_call(
        body, name=name, grid=(HP, nb, nb),
        in_specs=[qs, ks, ks, qs, crs], out_specs=[qs, qs],
        out_shape=[jax.ShapeDtypeStruct((S_len, W), F32), jax.ShapeDtypeStruct((S_len, W), F32)],
        scratch_shapes=[pltpu.VMEM((2, tb, 1), F32), pltpu.VMEM((2, tb, 1), F32), pltpu.VMEM((tb, LANES), F32)],
        compiler_params=_params("parallel", "parallel", "arbitrary"),
    )(qn, kn, vb, c_b, c_rowp)


def _fox_attn_bwd(qn, kn, vb, c_b, c_rowp, o, lse_b, do, *, tb, name):
    S_len, W = qn.shape
    HP = W // LANES
    tb = min(tb, S_len)
    nb = S_len // tb
    scale = FOX_DH ** -0.5

    def body(q_ref, k_ref, v_ref, cb_ref, cr_ref, o_ref, lse_ref, do_ref, dq_ref, dk_ref, dv_ref, dcr_ref, dct_ref,
             dk_scr, dv_scr, dc_scr):
        j, i = pl.program_id(1), pl.program_id(2)

        @pl.when(jnp.logical_and(j == 0, i == 0))
        def _():
            dq_ref[...] = jnp.zeros(dq_ref.shape, F32)
            dct_ref[...] = jnp.zeros(dct_ref.shape, F32)

        @pl.when(i == 0)
        def _():
            dk_scr[...] = jnp.zeros(dk_scr.shape, F32)
            dv_scr[...] = jnp.zeros(dv_scr.shape, F32)
            dc_scr[...] = jnp.zeros(dc_scr.shape, F32)

        @pl.when(i >= j)
        def _():
            q, k, v = q_ref[...], k_ref[...], v_ref[...]
            dof = do_ref[...]
            dob = dof.astype(BF16)
            doo = dof * o_ref[...]
            lo = _iota((1, LANES), 1) < FOX_DH
            causal = (j * tb + _iota((tb, tb), 1)) <= (i * tb + _iota((tb, tb), 0))
            dq_part = jnp.zeros((tb, LANES), F32)
            dct_part = jnp.zeros((tb, LANES), F32)
            for hh in range(2):
                msk = lo if hh == 0 else jnp.logical_not(lo)
                qh = jnp.where(msk, q, jnp.zeros_like(q))
                kh = jnp.where(msk, k, jnp.zeros_like(k))
                doh = jnp.where(msk, dob, jnp.zeros_like(dob))
                s = lax.dot_general(qh, k, _DIMS["nt"], preferred_element_type=F32) * scale
                ct = cb_ref[:, hh * FOX_DH:hh * FOX_DH + 1]
                cs = cr_ref[hh:hh + 1, :]
                lse = lse_ref[:, hh * FOX_DH:hh * FOX_DH + 1]
                p = jnp.exp(jnp.where(causal, s + (ct - cs), -jnp.inf) - lse)
                pb = p.astype(BF16)
                dv_scr[...] += lax.dot_general(pb, doh, _DIMS["tn"], preferred_element_type=F32)
                dp = lax.dot_general(doh, v, _DIMS["nt"], preferred_element_type=F32)
                delta = jnp.sum(jnp.where(msk, doo, 0.0), -1, keepdims=True)
                ds = p * (dp - delta)
                dsb = ds.astype(BF16)
                dq_part = dq_part + lax.dot_general(dsb, kh, _DIMS["nn"], preferred_element_type=F32)
                dk_scr[...] += lax.dot_general(dsb, qh, _DIMS["tn"], preferred_element_type=F32)
                dc_scr[hh:hh + 1, :] -= jnp.sum(ds, axis=0, keepdims=True)
                dct_part = jnp.where(msk, jnp.sum(ds, -1, keepdims=True), dct_part)
            rows = pl.ds(pl.multiple_of(i * tb, tb), tb)
            dq_ref[rows, :] += dq_part * scale
            dct_ref[rows, :] += dct_part

        @pl.when(i == nb - 1)
        def _():
            dk_ref[...] = dk_scr[...] * scale
            dv_ref[...] = dv_scr[...]
            dcr_ref[...] = dc_scr[...]

    qs = pl.BlockSpec((tb, LANES), lambda h, j, i: (jnp.maximum(i, j), h))
    ks = pl.BlockSpec((tb, LANES), lambda h, j, i: (j, h))
    crs = pl.BlockSpec((None, SUBLANES, tb), lambda h, j, i: (h, 0, j))
    return pl.pallas_call(
        body, name=name, grid=(HP, nb, nb),
        in_specs=[qs, ks, ks, qs, crs, qs, qs, qs],
        out_specs=[pl.BlockSpec((S_len, LANES), lambda h, j, i: (0, h)), ks, ks, crs,
                   pl.BlockSpec((S_len, LANES), lambda h, j, i: (0, h))],
        out_shape=[jax.ShapeDtypeStruct((S_len, W), F32)] * 3 + [jax.ShapeDtypeStruct((HP, SUBLANES, S_len), F32),
                                                                 jax.ShapeDtypeStruct((S_len, W), F32)],
        scratch_shapes=[pltpu.VMEM((tb, LANES), F32), pltpu.VMEM((tb, LANES), F32), pltpu.VMEM((SUBLANES, tb), F32)],
        compiler_params=_params("parallel", "arbitrary", "arbitrary"),
    )(qn, kn, vb, c_b, c_rowp, o, lse_b, do)


def _fox_post_tile(o, z):
    return o * _silu(z)


def _fox_post_fwd(o, h_main, *, tr, name):
    W = o.shape[1]

    def body(rows, consts, orows, oaccs, scr, step, blk):
        orows[0][...] = _fox_post_tile(rows[0][...], rows[1][...]).astype(BF16)

    z = (h_main, lambda tr_, n: pl.BlockSpec((tr_, W), lambda i: (i, 3)))
    return _rowcall(body, [o, z], [], [(W, BF16)], [], tr=tr, name=name)[0]


def _fox_post_bwd(o, h_main, d_og, *, tr, name):
    W = o.shape[1]

    def body(rows, consts, orows, oaccs, scr, step, blk):
        _, vjp = jax.vjp(_fox_post_tile, rows[0][...], rows[1][...])
        d_o, d_z = vjp(rows[2][...])
        orows[0][...] = d_o
        orows[1][...] = d_z.astype(BF16)

    z = (h_main, lambda tr_, n: pl.BlockSpec((tr_, W), lambda i: (i, 3)))
    return _rowcall(body, [o, z, d_og], [], [(W, F32), (W, BF16)], [], tr=tr, name=name)


MESH_IDS = pl.DeviceIdType.MESH
N_CHIPS = 4
N_DEV = 8
_ANY = pl.BlockSpec(memory_space=pl.ANY)


def _xy_exchange(src, *, gather, name):
    shape = (N_CHIPS,) + tuple(src.shape[-2:])

    def body(src_ref, out_ref, send_sems, recv_sems, local_sem):
        x, y, c = lax.axis_index("x"), lax.axis_index("y"), lax.axis_index("c")
        me = 2 * x + y
        peers = [(1 - x, y), (x, 1 - y), (1 - x, 1 - y)]

        def outgoing(px, py):
            return src_ref if gather else src_ref.at[2 * px + py]

        mine = pltpu.make_async_copy(outgoing(x, y), out_ref.at[me], local_sem)
        mine.start()
        copies = []
        for j, (px, py) in enumerate(peers):
            cp = pltpu.make_async_remote_copy(
                src_ref=outgoing(px, py), dst_ref=out_ref.at[me],
                send_sem=send_sems.at[j], recv_sem=recv_sems.at[j],
                device_id=(px, py, c), device_id_type=MESH_IDS)
            cp.start()
            copies.append(cp)
        for j, (px, py) in enumerate(peers):
            pltpu.make_async_remote_copy(
                src_ref=outgoing(px, py), dst_ref=out_ref.at[2 * px + py],
                send_sem=send_sems.at[j], recv_sem=recv_sems.at[j],
                device_id=(px, py, c), device_id_type=MESH_IDS).wait_recv()
        for cp in copies:
            cp.wait_send()
        mine.wait()

    return pl.pallas_call(
        body, name=name, in_specs=[_ANY], out_specs=_ANY,
        out_shape=jax.ShapeDtypeStruct(shape, src.dtype),
        scratch_shapes=[pltpu.SemaphoreType.DMA((3,)), pltpu.SemaphoreType.DMA((3,)), pltpu.SemaphoreType.DMA],
    )(src)


def _c_swap(src, *, name):
    def body(src_ref, out_ref, send_sem, recv_sem):
        x, y, c = lax.axis_index("x"), lax.axis_index("y"), lax.axis_index("c")
        cp = pltpu.make_async_remote_copy(
            src_ref=src_ref, dst_ref=out_ref, send_sem=send_sem, recv_sem=recv_sem,
            device_id=(x, y, 1 - c), device_id_type=MESH_IDS)
        cp.start()
        cp.wait()

    return pl.pallas_call(
        body, name=name, in_specs=[_ANY], out_specs=_ANY,
        out_shape=jax.ShapeDtypeStruct(src.shape, src.dtype),
        scratch_shapes=[pltpu.SemaphoreType.DMA, pltpu.SemaphoreType.DMA],
    )(src)


def _all_gather8(blk, *, name):
    m_per, n = blk.shape

    def body(x_ref, out_ref, send_sems, recv_sems, local_sem):
        x, y, c = lax.axis_index("x"), lax.axis_index("y"), lax.axis_index("c")
        me, sibling = (x, y, c), (x, y, 1 - c)
        chips = [(1 - x, y), (x, 1 - y), (1 - x, 1 - y)]

        def rows(px, py, pc):
            return out_ref.at[pl.ds((4 * px + 2 * py + pc) * m_per, m_per), :]

        def copy(k, block, to, src=None):
            return pltpu.make_async_remote_copy(
                src_ref=rows(*block) if src is None else src, dst_ref=rows(*block),
                send_sem=send_sems.at[k], recv_sem=recv_sems.at[k], device_id=to, device_id_type=MESH_IDS)

        mine = pltpu.make_async_copy(x_ref, rows(*me), local_sem)
        mine.start()
        first = [copy(0, me, sibling, src=x_ref)]
        first += [copy(1 + j, me, (*chip, c), src=x_ref) for j, chip in enumerate(chips)]
        for cp in first:
            cp.start()
        passed = [copy(4 + j, (*chip, c), sibling) for j, chip in enumerate(chips)]
        for j, chip in enumerate(chips):
            copy(1 + j, (*chip, c), me).wait_recv()
            passed[j].start()
        copy(0, sibling, me).wait_recv()
        for j, chip in enumerate(chips):
            copy(4 + j, (*chip, 1 - c), me).wait_recv()
        for cp in first + passed:
            cp.wait_send()
        mine.wait()

    return pl.pallas_call(
        body, name=name,
        out_shape=jax.ShapeDtypeStruct((N_DEV * m_per, n), blk.dtype),
        in_specs=[pl.BlockSpec(memory_space=pltpu.VMEM)], out_specs=pl.BlockSpec(memory_space=pltpu.VMEM),
        scratch_shapes=[pltpu.SemaphoreType.DMA((7,)), pltpu.SemaphoreType.DMA((7,)), pltpu.SemaphoreType.DMA],
    )(blk)


def _sum_slots(parts, *, tr, name):
    n, R, _ = parts.shape
    pack = 2 * SUBLANES
    tr = max(t for t in range(pack, min(tr, R) + 1, pack) if R % t == 0) if R % pack == 0 else R

    def body(p_ref, o_ref):
        tot = p_ref[0].astype(F32)
        for s in range(1, n):
            tot = tot + p_ref[s].astype(F32)
        o_ref[...] = tot

    return pl.pallas_call(
        body, name=name, grid=(R // tr,),
        in_specs=[pl.BlockSpec((n, tr, LANES), lambda i: (0, i, 0))], out_specs=pl.BlockSpec((tr, LANES), lambda i: (i, 0)),
        out_shape=jax.ShapeDtypeStruct((R, LANES), F32), compiler_params=_params("parallel"),
    )(parts)


def _adamw(w, g_parts, m, v, *, name):
    shape = w.shape
    as2d = lambda a: a.reshape(-1, shape[-1])
    w2, m2, v2 = as2d(w), as2d(m), as2d(v)
    gs = [as2d(g) for g in g_parts]
    R, C = w2.shape
    tr = R
    while tr * C * 4 > (1 << 20) and tr % 2 == 0 and (tr // 2) % SUBLANES == 0:
        tr //= 2
    ng = len(gs)

    def body(*refs):
        w_ref, m_ref, v_ref = refs[:3]
        g_refs = refs[3:3 + ng]
        go_ref, d_ref, mo_ref, vo_ref = refs[3 + ng:]
        g = g_refs[0][...]
        for r in g_refs[1:]:
            g = g + r[...]
        mn = ADAM_B1 * m_ref[...] + (1.0 - ADAM_B1) * g
        vn = ADAM_B2 * v_ref[...] + (1.0 - ADAM_B2) * jnp.square(g)
        m_hat = mn / (1.0 - ADAM_B1 ** ADAM_STEP)
        v_hat = vn / (1.0 - ADAM_B2 ** ADAM_STEP)
        go_ref[...] = g
        d_ref[...] = -ADAM_LR * (m_hat / (jnp.sqrt(v_hat) + ADAM_EPS) + ADAM_WD * w_ref[...])
        mo_ref[...] = mn
        vo_ref[...] = vn

    spec = pl.BlockSpec((tr, C), lambda i: (i, 0))
    outs = pl.pallas_call(
        body, name=name, grid=(R // tr,), in_specs=[spec] * (3 + ng), out_specs=[spec] * 4,
        out_shape=[jax.ShapeDtypeStruct((R, C), F32)] * 4, compiler_params=_params("parallel"),
    )(w2, m2, v2, *gs)
    return tuple(o.reshape(shape) for o in outs)


TR = 256
ATTN_TILE = 512


def _mm_nn(a, b, name, **kw):
    return _matmul(a, b, tm=512, tn=1024, tk=1024, name=name, **kw)


def _mm_nt(a, b, name, **kw):
    return _matmul(a, b, tb=True, tm=512, tn=1024, tk=1024, name=name, **kw)


def _mm_tn(a, b, name, **kw):
    return _matmul(a, b, ta=True, tm=1024, tn=1024, tk=512, name=name, **kw)


def _c_rows(c, Hf):
    S_len = c.shape[0]
    ct = c[:, :Hf].T.reshape(Hf // 2, 2, S_len)
    return jnp.pad(ct, ((0, 0), (0, SUBLANES - 2), (0, 0)))


def _local_step(x, p, target, wts):
    L = wts["ln_g"].shape[0]
    alpha = (2 * L) ** 0.25
    Hg = wts["gdn_a_log"].shape[1]
    Hf = wts["fox_b_f"].shape[1]
    Wg_ = Hg * GDN_DK
    Wf_ = Hf * FOX_DH
    saved = []
    for i in range(L):
        j = i // 2
        sv = {"x": x}
        if i % 2 == 0:
            w_in = wts["gdn_w_in"][j]
            wm, ws = w_in[:, :4 * Wg_], _pad_lanes(w_in[:, 4 * Wg_:])
            cw8 = jnp.pad(wts["gdn_conv_w"][j], ((0, SUBLANES - GDN_CONV), (0, 0)))
            alog = _pad_lanes(_row(wts["gdn_a_log"][j]), offset=Hg)
            dtb = _pad_lanes(_row(wts["gdn_dt_bias"][j]), offset=Hg)
            hm = _mm_nn(x, wm, f"gdn{j}_in_main")
            hs = _mm_nn(x, ws, f"gdn{j}_in_small")
            q, k, v, bg = _gdn_pre_fwd(hm, hs, cw8, alog, dtb, Hg, tr=TR, name=f"gdn{j}_pre")
            o, states = _gdn_rule_fwd(q, k, v, bg, Hg, name=f"gdn{j}_rule")
            on = _gdn_post_fwd(o, hm, wts["gdn_norm_g"][j], Hg, tr=TR, name=f"gdn{j}_post")
            y = _mm_nn(on, wts["gdn_w_out"][j], f"gdn{j}_out")
            sv.update(wm=wm, ws=ws, cw8=cw8, alog=alog, dtb=dtb, hm=hm, hs=hs, q=q, k=k, v=v, bg=bg, o=o,
                      states=states, on=on)
        else:
            w_in = wts["fox_w_in"][j]
            wm, ws = w_in[:, :4 * Wf_], _pad_lanes(w_in[:, 4 * Wf_:])
            gq2 = _row(jnp.tile(wts["fox_q_norm_g"][j], 2))
            gk2 = _row(jnp.tile(wts["fox_k_norm_g"][j], 2))
            bf = _pad_lanes(_row(wts["fox_b_f"][j]))
            hm = _mm_nn(x, wm, f"fox{j}_in_main")
            hs = _mm_nn(x, ws, f"fox{j}_in_small")
            qn, kn, vb, c, c_b = _fox_pre_fwd(hm, hs, gq2, gk2, bf, Hf, tr=TR, name=f"fox{j}_pre")
            c_rowp = _c_rows(c, Hf)
            o, lse_b = _fox_attn_fwd(qn, kn, vb, c_b, c_rowp, tb=ATTN_TILE, name=f"fox{j}_attn")
            on = _fox_post_fwd(o, hm, tr=TR, name=f"fox{j}_post")
            y = _mm_nn(on, wts["fox_w_out"][j], f"fox{j}_out")
            sv.update(wm=wm, ws=ws, gq2=gq2, gk2=gk2, bf=bf, hm=hm, hs=hs, qn=qn, kn=kn, vb=vb, c_b=c_b,
                      c_rowp=c_rowp, o=o, lse_b=lse_b, on=on)
        x_ln = _ln_fwd(x, y, wts["ln_g"][i], wts["ln_b"][i], alpha, tr=TR, name=f"ln{i}")
        gp = _mm_nn(x_ln, wts["ple_w_gate"][i], f"ple{i}_gate")
        pp = _mm_nn(p[i], wts["ple_w_proj"][i], f"ple{i}_proj")
        x_out = _ple_fwd(x_ln, gp, pp, tr=TR, name=f"ple{i}_mix")
        sv.update(y=y, x_ln=x_ln, gp=gp, pp=pp)
        saved.append(sv)
        x = x_out

    loss_row, dx = _loss_fwd_bwd(x, target, tr=TR, name="loss")

    g = {n: [None] * wts[n].shape[0] for n in wts}
    for i in reversed(range(L)):
        j = i // 2
        sv = saved[i]
        d_pre, d_pp = _ple_bwd(dx, sv["gp"], sv["pp"], tr=TR, name=f"ple{i}_mix_bwd")
        g["ple_w_gate"][i] = _mm_tn(sv["x_ln"], d_pre, f"ple{i}_gate_dw")
        g["ple_w_proj"][i] = _mm_tn(p[i], d_pp, f"ple{i}_proj_dw")
        t = _mm_nt(d_pre, wts["ple_w_gate"][i], f"ple{i}_gate_dx")
        du, g["ln_g"][i], g["ln_b"][i] = _ln_bwd(sv["x"], sv["y"], wts["ln_g"][i], wts["ln_b"][i], dx, t, alpha,
                                                 tr=TR, name=f"ln{i}_bwd")
        if i % 2 == 0:
            g["gdn_w_out"][j] = _mm_tn(sv["on"], du, f"gdn{j}_out_dw")
            d_on = _mm_nt(du, wts["gdn_w_out"][j], f"gdn{j}_out_dx")
            d_o, d_z, d_ng = _gdn_post_bwd(sv["o"], sv["hm"], wts["gdn_norm_g"][j], d_on, Hg, tr=TR, name=f"gdn{j}_post_bwd")
            dq, dk, dv, dbg = _gdn_rule_bwd(sv["q"], sv["k"], sv["v"], sv["bg"], sv["states"], d_o, Hg, name=f"gdn{j}_rule_bwd")
            d_hm, d_hs, d_cw, d_al, d_dtb = _gdn_pre_bwd(sv["hm"], sv["hs"], sv["cw8"], sv["alog"], sv["dtb"],
                                                         dq, dk, dv, dbg, d_z, Hg, tr=TR, name=f"gdn{j}_pre_bwd")
            g["gdn_norm_g"][j] = d_ng[0]
            g["gdn_conv_w"][j] = d_cw[:GDN_CONV]
            g["gdn_a_log"][j] = d_al[0, Hg:2 * Hg]
            g["gdn_dt_bias"][j] = d_dtb[0, Hg:2 * Hg]
            wname, nsmall, Wd = "gdn_w_in", 2 * Hg, Wg_
        else:
            g["fox_w_out"][j] = _mm_tn(sv["on"], du, f"fox{j}_out_dw")
            d_og = _mm_nt(du, wts["fox_w_out"][j], f"fox{j}_out_dx")
            d_o, d_z = _fox_post_bwd(sv["o"], sv["hm"], d_og, tr=TR, name=f"fox{j}_post_bwd")
            dqn, dkn, dvv, dcr, dct = _fox_attn_bwd(sv["qn"], sv["kn"], sv["vb"], sv["c_b"], sv["c_rowp"], sv["o"], sv["lse_b"],
                                               d_o, tb=ATTN_TILE, name=f"fox{j}_attn_bwd")
            dc = _pad_lanes(dcr[:, :2, :].reshape(Hf, -1).T + dct[:, ::FOX_DH])
            d_hm, d_hs, d_gq, d_gk, d_bf = _fox_pre_bwd(sv["hm"], sv["hs"], sv["gq2"], sv["gk2"], sv["bf"],
                                                        dqn, dkn, dvv, d_z, dc, Hf, tr=TR, name=f"fox{j}_pre_bwd")
            g["fox_q_norm_g"][j] = d_gq[0, :FOX_DH] + d_gq[0, FOX_DH:]
            g["fox_k_norm_g"][j] = d_gk[0, :FOX_DH] + d_gk[0, FOX_DH:]
            g["fox_b_f"][j] = d_bf[0, :Hf]
            wname, nsmall, Wd = "fox_w_in", Hf, Wf_
        dwm = _mm_tn(sv["x"], d_hm, f"{wname}{j}_main_dw")
        dws = _mm_tn(sv["x"], d_hs, f"{wname}{j}_small_dw")
        g[wname][j] = jnp.concatenate([dwm, dws[:, :nsmall]], axis=1)
        t1 = _mm_nt(d_hs, sv["ws"], f"{wname}{j}_small_dx", add=du, add_scale=alpha)
        dx = _mm_nt(d_hm, sv["wm"], f"{wname}{j}_main_dx", add=t1)
        del Wd
    grads = {n: jnp.stack(v) for n, v in g.items()}
    return loss_row, dx, grads


_SHARDED = (("ple_w_gate", 1), ("ple_w_proj", 2), ("gdn_w_in", 2), ("gdn_conv_w", 2), ("gdn_w_out", 1),
            ("fox_w_in", 2), ("fox_w_out", 1))
_REPLICATED = ("ln_g", "ln_b", "gdn_a_log", "gdn_dt_bias", "gdn_norm_g", "fox_b_f", "fox_q_norm_g", "fox_k_norm_g")
_EXACT = ("gdn_conv_w",)
_ORDER = ("ln_g", "ln_b", "ple_w_gate", "ple_w_proj", "gdn_w_in", "gdn_conv_w", "gdn_a_log", "gdn_dt_bias",
          "gdn_norm_g", "gdn_w_out", "fox_w_in", "fox_b_f", "fox_q_norm_g", "fox_k_norm_g", "fox_w_out")


def _as_rows(a):
    return a.reshape(-1, LANES)


def _gather_weights(local):
    parts = []
    for name, _ in _SHARDED:
        w = local[name]
        wb = lax.bitcast_convert_type(w, BF16) if name in _EXACT else w.astype(BF16)
        parts.append(_as_rows(wb))
    packed = jnp.concatenate(parts, axis=0)
    got = _xy_exchange(packed, gather=True, name="gather_weights")
    full, r0 = {}, 0
    for (name, axis), part in zip(_SHARDED, parts):
        nrow = part.shape[0]
        seg = got[:, r0:r0 + nrow]
        r0 += nrow
        shp = local[name].shape
        if name in _EXACT:
            blocks = lax.bitcast_convert_type(seg.reshape((N_CHIPS,) + shp + (2,)), F32)
        else:
            blocks = seg.reshape((N_CHIPS,) + shp)
        full[name] = jnp.concatenate([blocks[s] for s in range(N_CHIPS)], axis=axis)
    return full


def _reduce_sharded(grads):
    per_owner = []
    sizes = []
    for s in range(N_CHIPS):
        parts = []
        for name, axis in _SHARDED:
            gfull = grads[name]
            n = gfull.shape[axis] // N_CHIPS
            blk = lax.slice_in_dim(gfull, s * n, (s + 1) * n, axis=axis)
            parts.append(_as_rows(blk.astype(BF16)))
        sizes = [q.shape[0] for q in parts]
        per_owner.append(jnp.concatenate(parts, axis=0))
    packed = jnp.stack(per_owner)
    got = _xy_exchange(packed, gather=False, name="exchange_grads")
    mine = _sum_slots(got, tr=4096, name="sum_grads")
    other = _c_swap(mine, name="swap_grads")
    out, r0 = {}, 0
    for (name, axis), nrow in zip(_SHARDED, sizes):
        shp = list(grads[name].shape)
        shp[axis] //= N_CHIPS
        out[name] = (mine[r0:r0 + nrow].reshape(shp), other[r0:r0 + nrow].reshape(shp))
        r0 += nrow
    return out


def _reduce_replicated(grads, loss_part):
    rows = [_pad_lanes(jnp.reshape(loss_part, (1, 1)))]
    for name in _REPLICATED:
        gr = grads[name]
        rows.append(_as_rows(gr) if gr.shape[-1] % LANES == 0 else _pad_lanes(gr))
    sizes = [r.shape[0] for r in rows]
    blk = jnp.concatenate(rows, axis=0)
    nrow = blk.shape[0]
    npad = -nrow % SUBLANES
    blk = jnp.pad(blk, ((0, npad), (0, 0)))
    allb = _all_gather8(blk, name="gather_small_grads").reshape(N_DEV, nrow + npad, LANES)
    tot = _sum_slots(allb, tr=nrow + npad, name="sum_small_grads")
    out, r0 = {}, sizes[0]
    loss = tot[0, 0]
    for name, n in zip(_REPLICATED, sizes[1:]):
        gr = grads[name]
        seg = tot[r0:r0 + n]
        out[name] = seg.reshape(gr.shape) if gr.shape[-1] % LANES == 0 else seg[:, :gr.shape[-1]]
        r0 += n
    return loss, out


def kernel(x, p, ln_g, ln_b, ple_w_gate, ple_w_proj, gdn_w_in, gdn_conv_w, gdn_a_log, gdn_dt_bias, gdn_norm_g, gdn_w_out, fox_w_in, fox_b_f, fox_q_norm_g, fox_k_norm_g, fox_w_out, loss_target, m_ln_g, m_ln_b, m_ple_w_gate, m_ple_w_proj, m_gdn_w_in, m_gdn_conv_w, m_gdn_a_log, m_gdn_dt_bias, m_gdn_norm_g, m_gdn_w_out, m_fox_w_in, m_fox_b_f, m_fox_q_norm_g, m_fox_k_norm_g, m_fox_w_out, v_ln_g, v_ln_b, v_ple_w_gate, v_ple_w_proj, v_gdn_w_in, v_gdn_conv_w, v_gdn_a_log, v_gdn_dt_bias, v_gdn_norm_g, v_gdn_w_out, v_fox_w_in, v_fox_b_f, v_fox_q_norm_g, v_fox_k_norm_g, v_fox_w_out):
    local = dict(ln_g=ln_g, ln_b=ln_b, ple_w_gate=ple_w_gate, ple_w_proj=ple_w_proj, gdn_w_in=gdn_w_in,
                 gdn_conv_w=gdn_conv_w, gdn_a_log=gdn_a_log, gdn_dt_bias=gdn_dt_bias, gdn_norm_g=gdn_norm_g,
                 gdn_w_out=gdn_w_out, fox_w_in=fox_w_in, fox_b_f=fox_b_f, fox_q_norm_g=fox_q_norm_g,
                 fox_k_norm_g=fox_k_norm_g, fox_w_out=fox_w_out)
    mom_m = dict(ln_g=m_ln_g, ln_b=m_ln_b, ple_w_gate=m_ple_w_gate, ple_w_proj=m_ple_w_proj, gdn_w_in=m_gdn_w_in,
                 gdn_conv_w=m_gdn_conv_w, gdn_a_log=m_gdn_a_log, gdn_dt_bias=m_gdn_dt_bias, gdn_norm_g=m_gdn_norm_g,
                 gdn_w_out=m_gdn_w_out, fox_w_in=m_fox_w_in, fox_b_f=m_fox_b_f, fox_q_norm_g=m_fox_q_norm_g,
                 fox_k_norm_g=m_fox_k_norm_g, fox_w_out=m_fox_w_out)
    mom_v = dict(ln_g=v_ln_g, ln_b=v_ln_b, ple_w_gate=v_ple_w_gate, ple_w_proj=v_ple_w_proj, gdn_w_in=v_gdn_w_in,
                 gdn_conv_w=v_gdn_conv_w, gdn_a_log=v_gdn_a_log, gdn_dt_bias=v_gdn_dt_bias, gdn_norm_g=v_gdn_norm_g,
                 gdn_w_out=v_gdn_w_out, fox_w_in=v_fox_w_in, fox_b_f=v_fox_b_f, fox_q_norm_g=v_fox_q_norm_g,
                 fox_k_norm_g=v_fox_k_norm_g, fox_w_out=v_fox_w_out)

    wts = dict(_gather_weights(local))
    for name in _REPLICATED:
        wts[name] = local[name]
    loss_row, dx, grads = _local_step(x[0], p[:, 0], loss_target[0], wts)
    loss, small = _reduce_replicated(grads, jnp.sum(loss_row))
    big = _reduce_sharded(grads)

    outs = {}
    for name in _ORDER:
        parts = list(big[name]) if name in big else [small[name]]
        outs[name] = _adamw(local[name], parts, mom_m[name], mom_v[name], name=f"adamw_{name}")
    return (loss, dx[None], *[outs[n][0] for n in _ORDER], *[outs[n][1] for n in _ORDER],
            *[outs[n][2] for n in _ORDER], *[outs[n][3] for n in _ORDER])
```

```python
import functools

import jax
import jax.numpy as jnp
from jax import lax
from jax.experimental import pallas as pl
from jax.experimental.pallas import tpu as pltpu

F32 = jnp.float32
BF16 = jnp.bfloat16

LANES = 128
SUBLANES = 8
VMEM_LIMIT_BYTES = 56 * 1024 * 1024

GDN_DK = 128
GDN_CHUNK = 64
GDN_CONV = 4
FOX_DH = 64
LN_EPS = 1e-5
RMS_EPS = 1e-6

ADAM_LR = 0.001
ADAM_B1 = 0.9
ADAM_B2 = 0.999
ADAM_EPS = 1e-08
ADAM_WD = 0.01
ADAM_STEP = 10

HI = lax.Precision.HIGHEST
_DIMS = {"nn": (((1,), (0,)), ((), ())), "nt": (((1,), (1,)), ((), ())), "tn": (((0,), (0,)), ((), ()))}


def _params(*sem):
    return pltpu.CompilerParams(dimension_semantics=sem, vmem_limit_bytes=VMEM_LIMIT_BYTES)


def _raw_mm(a, b, form, hi):
    if hi:
        return lax.dot_general(a.astype(F32), b.astype(F32), _DIMS[form], precision=HI, preferred_element_type=F32)
    return lax.dot_general(a.astype(BF16), b.astype(BF16), _DIMS[form], preferred_element_type=F32)


@functools.partial(jax.custom_vjp, nondiff_argnums=(2, 3))
def _mm(a, b, form, hi):
    return _raw_mm(a, b, form, hi)


def _mm_fwd(a, b, form, hi):
    return _raw_mm(a, b, form, hi), (a, b)


def _mm_bwd(form, hi, res, g):
    a, b = res
    if form == "nn":
        return _mm(g, b, "nt", hi), _mm(a, g, "tn", hi)
    if form == "nt":
        return _mm(g, b, "nn", hi), _mm(g, a, "tn", hi)
    return _mm(b, g, "nt", hi), _mm(a, g, "nn", hi)


_mm.defvjp(_mm_fwd, _mm_bwd)


def _silu(x):
    return x * jax.nn.sigmoid(x)


def _softplus(x):
    return jnp.maximum(x, 0.0) + jnp.log1p(jnp.exp(-jnp.abs(x)))


def _iota(shape, dim):
    return lax.broadcasted_iota(jnp.int32, shape, dim)


def _matmul(a, b, *, ta=False, tb=False, out_dtype=F32, add=None, add_scale=1.0, tm=512, tn=512, tk=512, name):
    if ta:
        K, M = a.shape
    else:
        M, K = a.shape
    if tb:
        N, K2 = b.shape
    else:
        K2, N = b.shape
    assert K == K2, (a.shape, b.shape, ta, tb)
    tm, tn, tk = min(tm, M), min(tn, N), min(tk, K)
    assert M % tm == 0 and N % tn == 0 and K % tk == 0, (M, N, K, tm, tn, tk)
    nk = K // tk
    form = ("t" if ta else "n") + ("t" if tb else "n")
    dims = (((0 if ta else 1,), (1 if tb else 0,)), ((), ()))
    del form
    a_spec = pl.BlockSpec((tk, tm), lambda i, j, k: (k, i)) if ta else pl.BlockSpec((tm, tk), lambda i, j, k: (i, k))
    b_spec = pl.BlockSpec((tn, tk), lambda i, j, k: (j, k)) if tb else pl.BlockSpec((tk, tn), lambda i, j, k: (k, j))
    o_spec = pl.BlockSpec((tm, tn), lambda i, j, k: (i, j))
    has_add = add is not None

    def body(*refs):
        if has_add:
            a_ref, b_ref, add_ref, o_ref, acc_ref = refs
        else:
            a_ref, b_ref, o_ref, acc_ref = refs
        k = pl.program_id(2)
        part = lax.dot_general(a_ref[...].astype(BF16), b_ref[...].astype(BF16), dims, preferred_element_type=F32)

        def finish(total):
            if has_add:
                total = total + add_scale * add_ref[...].astype(F32)
            o_ref[...] = total.astype(o_ref.dtype)

        if nk == 1:
            finish(part)
        else:
            @pl.when(k == 0)
            def _():
                acc_ref[...] = part

            @pl.when(jnp.logical_and(k > 0, k < nk - 1))
            def _():
                acc_ref[...] += part

            @pl.when(k == nk - 1)
            def _():
                finish(acc_ref[...] + part)

    in_specs = [a_spec, b_spec] + ([o_spec] if has_add else [])
    args = (a, b) + ((add,) if has_add else ())
    return pl.pallas_call(
        body, name=name, grid=(M // tm, N // tn, nk),
        in_specs=in_specs, out_specs=o_spec,
        out_shape=jax.ShapeDtypeStruct((M, N), out_dtype),
        scratch_shapes=[pltpu.VMEM((tm, tn), F32)],
        compiler_params=_params("parallel", "parallel", "arbitrary"),
    )(*args)


def _rowcall(body_fn, rows, consts, out_rows, out_accs, *, tr, name, reverse=False, scratch=()):
    def arr_spec(r):
        return r if isinstance(r, tuple) else (r, None)

    S = arr_spec(rows[0])[0].shape[0]
    tr = min(tr, S)
    assert S % tr == 0
    n = S // tr
    ridx = (lambda i: (n - 1 - i, 0)) if reverse else (lambda i: (i, 0))
    in_specs, args = [], []
    for r in rows:
        arr, spec = arr_spec(r)
        args.append(arr)
        in_specs.append(spec(tr, n) if spec is not None else pl.BlockSpec((tr, arr.shape[1]), ridx))
    for c in consts:
        args.append(c)
        in_specs.append(pl.BlockSpec(c.shape, lambda i: (0, 0)))
    out_specs, out_shape = [], []
    for (ncol, dt) in out_rows:
        out_specs.append(pl.BlockSpec((tr, ncol), ridx))
        out_shape.append(jax.ShapeDtypeStruct((S, ncol), dt))
    for shp in out_accs:
        out_specs.append(pl.BlockSpec(shp, lambda i: (0, 0)))
        out_shape.append(jax.ShapeDtypeStruct(shp, F32))
    nr, nc, no, na = len(rows), len(consts), len(out_rows), len(out_accs)

    def kernel(*refs):
        row_refs = refs[:nr]
        const_refs = refs[nr:nr + nc]
        orow_refs = refs[nr + nc:nr + nc + no]
        oacc_refs = refs[nr + nc + no:nr + nc + no + na]
        scr = refs[nr + nc + no + na:]
        step = pl.program_id(0)
        blk = (n - 1 - step) if reverse else step

        @pl.when(step == 0)
        def _():
            for acc in oacc_refs:
                acc[...] = jnp.zeros(acc.shape, F32)

        body_fn(row_refs, const_refs, orow_refs, oacc_refs, scr, step, blk)

    outs = pl.pallas_call(
        kernel, name=name, grid=(n,), in_specs=in_specs, out_specs=out_specs, out_shape=out_shape,
        scratch_shapes=list(scratch), compiler_params=_params("arbitrary"),
    )(*args)
    return outs


def _row(v):
    return v.astype(F32).reshape(1, -1)


def _pad_lanes(v, width=LANES, offset=0):
    pad = [(0, 0)] * (v.ndim - 1) + [(offset, width - offset - v.shape[-1])]
    return jnp.pad(v, pad)


def _ln_tile(x, y, g, b, alpha):
    u = alpha * x + y
    mu = jnp.mean(u, -1, keepdims=True)
    d = u - mu
    var = jnp.mean(d * d, -1, keepdims=True)
    return d * lax.rsqrt(var + LN_EPS) * g + b


def _ln_fwd(x, y, g, b, alpha, *, tr, name):
    D = x.shape[1]

    def body(rows, consts, orows, oaccs, scr, step, blk):
        orows[0][...] = _ln_tile(rows[0][...], rows[1][...], consts[0][...], consts[1][...], alpha)

    return _rowcall(body, [x, y], [_row(g), _row(b)], [(D, F32)], [], tr=tr, name=name)[0]


def _ln_bwd(x, y, g, b, dxo, t, alpha, *, tr, name):
    D = x.shape[1]

    def body(rows, consts, orows, oaccs, scr, step, blk):
        xv, yv = rows[0][...], rows[1][...]
        ct = rows[2][...] + rows[3][...]
        _, vjp = jax.vjp(lambda yy, gg, bb: _ln_tile(xv, yy, gg, bb, alpha), yv, consts[0][...], consts[1][...])
        du, dg, db = vjp(ct)
        orows[0][...] = du
        oaccs[0][...] += dg
        oaccs[1][...] += db

    du, dg, db = _rowcall(body, [x, y, dxo, t], [_row(g), _row(b)], [(D, F32)], [(1, D), (1, D)], tr=tr, name=name)
    return du, dg[0], db[0]


def _ple_fwd(x_ln, gp, pp, *, tr, name):
    D = x_ln.shape[1]

    def body(rows, consts, orows, oaccs, scr, step, blk):
        orows[0][...] = rows[0][...] + jax.nn.sigmoid(rows[1][...]) * rows[2][...]

    return _rowcall(body, [x_ln, gp, pp], [], [(D, F32)], [], tr=tr, name=name)[0]


def _ple_bwd(dxo, gp, pp, *, tr, name):
    D = dxo.shape[1]

    def body(rows, consts, orows, oaccs, scr, step, blk):
        d = rows[0][...]
        s = jax.nn.sigmoid(rows[1][...])
        orows[0][...] = (d * rows[2][...] * s * (1.0 - s)).astype(BF16)
        orows[1][...] = (d * s).astype(BF16)

    return _rowcall(body, [dxo, gp, pp], [], [(D, BF16), (D, BF16)], [], tr=tr, name=name)


def _loss_fwd_bwd(xf, target, *, tr, name):
    D = xf.shape[1]

    def body(rows, consts, orows, oaccs, scr, step, blk):
        err = rows[0][...] - rows[1][...]
        orows[0][...] = err * (1.0 / D)
        part = jnp.sum(err * err, axis=0, keepdims=True) * (0.5 / D)
        oaccs[0][...] += part

    dx, lrow = _rowcall(body, [xf, target], [], [(D, F32)], [(1, D)], tr=tr, name=name)
    return lrow, dx


def _gdn_qk_tile(c):
    y = _silu(c)
    return y * lax.rsqrt(jnp.sum(y * y, -1, keepdims=True) + RMS_EPS)


def _make_bg_fn(H):
    def fn(hs, alog, dtb):
        lane = _iota((1, LANES), 1)
        beta = jax.nn.sigmoid(hs)
        g = -jnp.exp(alog) * _softplus(hs + dtb)
        return jnp.where(lane < H, beta, jnp.where(lane < 2 * H, g, 0.0))
    return fn


def _halo_spec(ncol):
    def make(tr, n):
        per = tr // SUBLANES
        return pl.BlockSpec((SUBLANES, ncol), lambda i: (jnp.maximum(i * per - 1, 0), 0))
    return make


def _halo_spec_rev(ncol):
    def make(tr, n):
        per = tr // SUBLANES
        return pl.BlockSpec((SUBLANES, ncol), lambda i: (jnp.maximum((n - 1 - i) * per - 1, 0), 0))
    return make


def _gdn_pre_fwd(h_main, h_small, conv_w8, alog_row, dtb_row, H, *, tr, name):
    W = H * GDN_DK
    C3 = 3 * W
    bg_fn = _make_bg_fn(H)

    def body(rows, consts, orows, oaccs, scr, step, blk):
        main_ref, halo_ref, hs_ref = rows
        w_ref, alog_ref, dtb_ref = consts
        q_ref, k_ref, v_ref, bg_ref = orows
        xs = scr[0]
        trr = main_ref.shape[0]
        xs[pl.ds(SUBLANES, trr), :] = main_ref[...]
        xs[pl.ds(0, SUBLANES), :] = jnp.where(blk > 0, halo_ref[...], 0.0)
        for s in range(C3 // LANES):
            ls = slice(s * LANES, (s + 1) * LANES)
            c = jnp.zeros((trr, LANES), F32)
            for j in range(GDN_CONV):
                c = c + w_ref[GDN_CONV - 1 - j:GDN_CONV - j, ls] * xs[pl.ds(SUBLANES - j, trr), ls]
            if s < 2 * H:
                out = _gdn_qk_tile(c)
                (q_ref if s < H else k_ref)[:, (s % H) * LANES:(s % H + 1) * LANES] = out
            else:
                v_ref[:, (s - 2 * H) * LANES:(s - 2 * H + 1) * LANES] = _silu(c)
        bg_ref[...] = bg_fn(hs_ref[...], alog_ref[...], dtb_ref[...])

    main = (h_main, lambda tr_, n: pl.BlockSpec((tr_, C3), lambda i: (i, 0)))
    halo = (h_main, _halo_spec(C3))
    trr = min(tr, h_main.shape[0])
    return _rowcall(body, [main, halo, h_small], [conv_w8, alog_row, dtb_row],
                    [(W, F32), (W, F32), (W, F32), (LANES, F32)], [], tr=tr, name=name,
                    scratch=[pltpu.VMEM((trr + SUBLANES, C3), F32)])


def _gdn_pre_bwd(h_main, h_small, conv_w8, alog_row, dtb_row, dq, dk, dv, dbg, dz, H, *, tr, name):
    W = H * GDN_DK
    C3 = 3 * W
    bg_fn = _make_bg_fn(H)

    def body(rows, consts, orows, oaccs, scr, step, blk):
        main_ref, halo_ref, hs_ref, dq_ref, dk_ref, dv_ref, dbg_ref, dz_ref = rows
        w_ref, alog_ref, dtb_ref = consts
        dmain_ref, dhs_ref = orows
        dw_ref, dalog_ref, ddtb_ref = oaccs
        xs, dcs = scr
        trr = main_ref.shape[0]
        xs[pl.ds(SUBLANES, trr), :] = main_ref[...]
        xs[pl.ds(0, SUBLANES), :] = jnp.where(blk > 0, halo_ref[...], 0.0)

        @pl.when(step == 0)
        def _():
            dcs[pl.ds(trr, SUBLANES), :] = jnp.zeros((SUBLANES, C3), F32)

        for s in range(C3 // LANES):
            ls = slice(s * LANES, (s + 1) * LANES)
            c = jnp.zeros((trr, LANES), F32)
            for j in range(GDN_CONV):
                c = c + w_ref[GDN_CONV - 1 - j:GDN_CONV - j, ls] * xs[pl.ds(SUBLANES - j, trr), ls]
            if s < 2 * H:
                src = dq_ref if s < H else dk_ref
                ct = src[:, (s % H) * LANES:(s % H + 1) * LANES]
                _, vjp = jax.vjp(_gdn_qk_tile, c)
            else:
                ct = dv_ref[:, (s - 2 * H) * LANES:(s - 2 * H + 1) * LANES]
                _, vjp = jax.vjp(_silu, c)
            dcs[pl.ds(0, trr), ls] = vjp(ct)[0]
        for s in range(C3 // LANES):
            ls = slice(s * LANES, (s + 1) * LANES)
            dx = jnp.zeros((trr, LANES), F32)
            dc0 = dcs[pl.ds(0, trr), ls]
            for j in range(GDN_CONV):
                wrow = w_ref[GDN_CONV - 1 - j:GDN_CONV - j, ls]
                dx = dx + wrow * dcs[pl.ds(j, trr), ls]
                dw_ref[GDN_CONV - 1 - j:GDN_CONV - j, ls] += jnp.sum(dc0 * xs[pl.ds(SUBLANES - j, trr), ls], axis=0, keepdims=True)
            dmain_ref[:, ls] = dx.astype(BF16)
        dmain_ref[:, C3:] = dz_ref[...]
        dcs[pl.ds(trr, SUBLANES), :] = dcs[pl.ds(0, SUBLANES), :]
        _, vjp = jax.vjp(bg_fn, hs_ref[...], alog_ref[...], dtb_ref[...])
        dhs, dalog, ddtb = vjp(dbg_ref[...])
        dhs_ref[...] = dhs.astype(BF16)
        dalog_ref[...] += dalog
        ddtb_ref[...] += ddtb

    trr = min(tr, h_main.shape[0])
    main = (h_main, lambda tr_, n: pl.BlockSpec((tr_, C3), lambda i: (n - 1 - i, 0)))
    halo = (h_main, _halo_spec_rev(C3))
    return _rowcall(body, [main, halo, h_small, dq, dk, dv, dbg, dz], [conv_w8, alog_row, dtb_row],
                    [(4 * W, BF16), (LANES, BF16)], [(SUBLANES, C3), (1, LANES), (1, LANES)],
                    tr=tr, name=name, reverse=True,
                    scratch=[pltpu.VMEM((trr + SUBLANES, C3), F32), pltpu.VMEM((trr + SUBLANES, C3), F32)])


def _gdn_chunk(qs, ks, vs, betas, gs, Ss):
    C, dk = qs[0].shape
    dv = vs[0].shape[1]
    ri, ci = _iota((C, C), 0), _iota((C, C), 1)
    causal, strict = ri >= ci, ri > ci
    tril = causal.astype(F32)
    eye = (ri == ci).astype(F32)
    lane0 = (_iota((1, LANES), 1) == 0).astype(F32)
    e0 = jnp.ones((C, 1), F32) * lane0
    ones_cc = jnp.ones((C, C), F32)
    ones_kc = jnp.ones((dk, C), F32)

    def each(f, *lists):
        return [f(*a) for a in zip(*lists)]

    G = each(lambda g: g * jnp.ones((1, LANES), F32), gs)
    gcB = each(lambda x: _mm(tril, x, "nn", True), G)
    gc = each(lambda x: jnp.sum(x * lane0, -1, keepdims=True), gcB)
    gc_row = each(lambda x: _mm(e0, x, "nt", True), gcB)
    decay = each(lambda a, b: jnp.where(causal, jnp.exp(jnp.where(causal, a - b, 0.0)), 0.0), gc, gc_row)
    kb = each(lambda k, b: k * b, ks, betas)
    kk = each(lambda a, k: _mm(a, k, "nt", False), kb, ks)
    L = each(lambda a, d: jnp.where(strict, a * d, 0.0), kk, decay)
    X = each(lambda l: eye - l, L)
    P = each(lambda l: _mm(l, l, "nn", True), L)
    n_sq = max(1, (C - 1).bit_length() - 1)
    for it in range(n_sq):
        XP = each(lambda x, p_: _mm(x, p_, "nn", True), X, P)
        if it < n_sq - 1:
            P = each(lambda p_: _mm(p_, p_, "nn", True), P)
        X = each(lambda x, y: x + y, X, XP)
    egc = each(jnp.exp, gc)
    u = each(lambda x, v, b: _mm(x, v * b, "nn", True), X, vs, betas)
    w = each(lambda x, a, e: _mm(x, a * e, "nn", True), X, kb, egc)
    qsc = each(lambda q: q * (dk ** -0.5), qs)
    qk = each(lambda q, k: _mm(q, k, "nt", False), qsc, ks)
    A = each(lambda a, d: jnp.where(causal, a * d, 0.0), qk, decay)
    q_dec = each(lambda q, e: q * e, qsc, egc)
    glB = each(lambda x: _mm(ones_cc, x, "nn", True), G)
    gl = each(lambda x: jnp.sum(x * lane0, -1, keepdims=True), glB)
    k_dec = each(lambda k, a, b: k * jnp.exp(a - b), ks, gl, gc)
    glS = each(lambda g: _mm(ones_kc, g * jnp.ones((1, dv), F32), "nn", True), gs)
    wS = each(lambda a, s: _mm(a, s, "nn", False), w, Ss)
    qS = each(lambda a, s: _mm(a, s, "nn", False), q_dec, Ss)
    v_new = each(lambda a, b: a - b, u, wS)
    Av = each(lambda a, b: _mm(a, b, "nn", False), A, v_new)
    kv = each(lambda a, b: _mm(a, b, "tn", False), k_dec, v_new)
    o = each(lambda a, b: a + b, qS, Av)
    S_new = each(lambda s, e, x: s * jnp.exp(e) + x, Ss, glS, kv)
    return o, S_new


def _gdn_rule_fwd(q, k, v, bg, H, *, name):
    S_len = q.shape[0]
    C = min(GDN_CHUNK, S_len)
    N = S_len // C
    dk = dv = GDN_DK

    def body(q_ref, k_ref, v_ref, bg_ref, o_ref, st_ref, s_scr):
        n = pl.program_id(0)

        @pl.when(n == 0)
        def _():
            s_scr[...] = jnp.zeros(s_scr.shape, F32)

        bgt = bg_ref[...]
        sl = [slice(h * dk, (h + 1) * dk) for h in range(H)]
        Ss = [s_scr[h] for h in range(H)]
        for h in range(H):
            st_ref[h] = Ss[h]
        os_, S_new = _gdn_chunk([q_ref[:, s] for s in sl], [k_ref[:, s] for s in sl], [v_ref[:, s] for s in sl],
                                [bgt[:, h:h + 1] for h in range(H)], [bgt[:, H + h:H + h + 1] for h in range(H)], Ss)
        for h in range(H):
            o_ref[:, sl[h]] = os_[h]
            s_scr[h] = S_new[h]

    rows = pl.BlockSpec((C, H * dk), lambda n: (n, 0))
    return pl.pallas_call(
        body, name=name, grid=(N,),
        in_specs=[rows, rows, rows, pl.BlockSpec((C, LANES), lambda n: (n, 0))],
        out_specs=[rows, pl.BlockSpec((H, dk, dv), lambda n: (n, 0, 0))],
        out_shape=[jax.ShapeDtypeStruct((S_len, H * dv), F32), jax.ShapeDtypeStruct((N * H, dk, dv), F32)],
        scratch_shapes=[pltpu.VMEM((H, dk, dv), F32)],
        compiler_params=_params("arbitrary"),
    )(q, k, v, bg)


def _gdn_rule_bwd(q, k, v, bg, states, do, H, *, name):
    S_len = q.shape[0]
    C = min(GDN_CHUNK, S_len)
    N = S_len // C
    dk = dv = GDN_DK

    def body(q_ref, k_ref, v_ref, bg_ref, st_ref, do_ref, dq_ref, dk_ref, dv_ref, dbg_ref, ds_scr):
        step = pl.program_id(0)

        @pl.when(step == 0)
        def _():
            ds_scr[...] = jnp.zeros(ds_scr.shape, F32)

        bgt = bg_ref[...]
        lane = _iota((1, LANES), 1)
        dbg = jnp.zeros((C, LANES), F32)
        sl = [slice(h * dk, (h + 1) * dk) for h in range(H)]
        _, vjp = jax.vjp(_gdn_chunk, [q_ref[:, s] for s in sl], [k_ref[:, s] for s in sl], [v_ref[:, s] for s in sl],
                         [bgt[:, h:h + 1] for h in range(H)], [bgt[:, H + h:H + h + 1] for h in range(H)],
                         [st_ref[h] for h in range(H)])
        dq, dkk, dvv, dbeta, dg, dS = vjp(([do_ref[:, s] for s in sl], [ds_scr[h] for h in range(H)]))
        for h in range(H):
            dq_ref[:, sl[h]] = dq[h]
            dk_ref[:, sl[h]] = dkk[h]
            dv_ref[:, sl[h]] = dvv[h]
            dbg = dbg + jnp.where(lane == h, dbeta[h], 0.0) + jnp.where(lane == h + H, dg[h], 0.0)
            ds_scr[h] = dS[h]
        dbg_ref[...] = dbg

    rows = pl.BlockSpec((C, H * dk), lambda s: (N - 1 - s, 0))
    bgs = pl.BlockSpec((C, LANES), lambda s: (N - 1 - s, 0))
    return pl.pallas_call(
        body, name=name, grid=(N,),
        in_specs=[rows, rows, rows, bgs, pl.BlockSpec((H, dk, dv), lambda s: (N - 1 - s, 0, 0)), rows],
        out_specs=[rows, rows, rows, bgs],
        out_shape=[jax.ShapeDtypeStruct((S_len, H * dk), F32)] * 3 + [jax.ShapeDtypeStruct((S_len, LANES), F32)],
        scratch_shapes=[pltpu.VMEM((H, dk, dv), F32)],
        compiler_params=_params("arbitrary"),
    )(q, k, v, bg, states, do)


def _gdn_post_tile(o, z, g):
    return o * lax.rsqrt(jnp.mean(o * o, -1, keepdims=True) + RMS_EPS) * g * _silu(z)


def _gdn_post_fwd(o, h_main, norm_g, H, *, tr, name):
    W = H * GDN_DK

    def body(rows, consts, orows, oaccs, scr, step, blk):
        for h in range(H):
            ls = slice(h * LANES, (h + 1) * LANES)
            orows[0][:, ls] = _gdn_post_tile(rows[0][:, ls], rows[1][:, ls], consts[0][...]).astype(BF16)

    z = (h_main, lambda tr_, n: pl.BlockSpec((tr_, W), lambda i: (i, 3)))
    return _rowcall(body, [o, z], [_row(norm_g)], [(W, BF16)], [], tr=tr, name=name)[0]


def _gdn_post_bwd(o, h_main, norm_g, d_on, H, *, tr, name):
    W = H * GDN_DK

    def body(rows, consts, orows, oaccs, scr, step, blk):
        for h in range(H):
            ls = slice(h * LANES, (h + 1) * LANES)
            _, vjp = jax.vjp(_gdn_post_tile, rows[0][:, ls], rows[1][:, ls], consts[0][...])
            d_o, d_z, d_g = vjp(rows[2][:, ls])
            orows[0][:, ls] = d_o
            orows[1][:, ls] = d_z.astype(BF16)
            oaccs[0][...] += d_g

    z = (h_main, lambda tr_, n: pl.BlockSpec((tr_, W), lambda i: (i, 3)))
    return _rowcall(body, [o, z, d_on], [_row(norm_g)], [(W, F32), (W, BF16)], [(1, LANES)], tr=tr, name=name)


def _seg_ones():
    ri, ci = _iota((LANES, LANES), 0), _iota((LANES, LANES), 1)
    return ((ri < FOX_DH) == (ci < FOX_DH)).astype(F32)


def _fox_qk_tile(x, g2):
    ms = _mm(x * x, _seg_ones(), "nn", True) * (1.0 / FOX_DH)
    return x * lax.rsqrt(ms + RMS_EPS) * g2


def _make_lf_fn(Hf):
    def fn(hs, bf):
        lane = _iota((1, LANES), 1)
        return jnp.where(lane < Hf, -_softplus(-(hs + bf)), 0.0)
    return fn


def _fox_pre_fwd(h_main, h_small, gq2, gk2, bf_row, Hf, *, tr, name):
    W = Hf * FOX_DH
    lf_fn = _make_lf_fn(Hf)

    def body(rows, consts, orows, oaccs, scr, step, blk):
        qk_ref, v_ref, hs_ref = rows
        gq_ref, gk_ref, bf_ref = consts
        qn_ref, kn_ref, vb_ref, c_ref, cb_ref = orows
        carry = scr[0]
        trr = qk_ref.shape[0]

        @pl.when(step == 0)
        def _():
            carry[...] = jnp.zeros(carry.shape, F32)

        for s in range(W // LANES):
            ls = slice(s * LANES, (s + 1) * LANES)
            qn_ref[:, ls] = _fox_qk_tile(qk_ref[:, ls], gq_ref[...]).astype(BF16)
            kn_ref[:, ls] = _fox_qk_tile(qk_ref[:, W + s * LANES:W + (s + 1) * LANES], gk_ref[...]).astype(BF16)
        vb_ref[...] = v_ref[...].astype(BF16)
        lf = lf_fn(hs_ref[...], bf_ref[...])
        tril = (_iota((trr, trr), 0) >= _iota((trr, trr), 1)).astype(F32)
        c = _raw_mm(tril, lf, "nn", True) + carry[0:1, :]
        c_ref[...] = c
        carry[0:1, :] = c[trr - 1:trr, :]
        expand = (jnp.right_shift(_iota((LANES, W), 1), 6) == _iota((LANES, W), 0)).astype(F32)
        cb_ref[...] = _raw_mm(c, expand, "nn", True)

    qk = (h_main, lambda tr_, n: pl.BlockSpec((tr_, 2 * W), lambda i: (i, 0)))
    vv = (h_main, lambda tr_, n: pl.BlockSpec((tr_, W), lambda i: (i, 2)))
    return _rowcall(body, [qk, vv, h_small], [gq2, gk2, bf_row],
                    [(W, BF16), (W, BF16), (W, BF16), (LANES, F32), (W, F32)], [], tr=tr, name=name,
                    scratch=[pltpu.VMEM((SUBLANES, LANES), F32)])


def _fox_pre_bwd(h_main, h_small, gq2, gk2, bf_row, dqn, dkn, dvv, dz, dc, Hf, *, tr, name):
    W = Hf * FOX_DH
    lf_fn = _make_lf_fn(Hf)

    def body(rows, consts, orows, oaccs, scr, step, blk):
        qk_ref, hs_ref, dqn_ref, dkn_ref, dvv_ref, dz_ref, dc_ref = rows
        gq_ref, gk_ref, bf_ref = consts
        dmain_ref, dhs_ref = orows
        dgq_ref, dgk_ref, dbf_ref = oaccs
        carry = scr[0]
        trr = qk_ref.shape[0]

        @pl.when(step == 0)
        def _():
            carry[...] = jnp.zeros(carry.shape, F32)

        for s in range(W // LANES):
            ls = slice(s * LANES, (s + 1) * LANES)
            lk = slice(W + s * LANES, W + (s + 1) * LANES)
            _, vjp = jax.vjp(_fox_qk_tile, qk_ref[:, ls], gq_ref[...])
            dx, dg = vjp(dqn_ref[:, ls])
            dmain_ref[:, ls] = dx.astype(BF16)
            dgq_ref[...] += dg
            _, vjp = jax.vjp(_fox_qk_tile, qk_ref[:, lk], gk_ref[...])
            dx, dg = vjp(dkn_ref[:, ls])
            dmain_ref[:, lk] = dx.astype(BF16)
            dgk_ref[...] += dg
        dmain_ref[:, 2 * W:3 * W] = dvv_ref[...].astype(BF16)
        dmain_ref[:, 3 * W:] = dz_ref[...]
        dcv = dc_ref[...]
        triu = (_iota((trr, trr), 0) <= _iota((trr, trr), 1)).astype(F32)
        dlf = _raw_mm(triu, dcv, "nn", True) + carry[0:1, :]
        carry[0:1, :] = dlf[0:1, :]
        _, vjp = jax.vjp(lf_fn, hs_ref[...], bf_ref[...])
        dhs, dbf = vjp(dlf)
        dhs_ref[...] = dhs.astype(BF16)
        dbf_ref[...] += dbf

    qk = (h_main, lambda tr_, n: pl.BlockSpec((tr_, 2 * W), lambda i: (n - 1 - i, 0)))
    return _rowcall(body, [qk, h_small, dqn, dkn, dvv, dz, dc], [gq2, gk2, bf_row],
                    [(4 * W, BF16), (LANES, BF16)], [(1, LANES), (1, LANES), (1, LANES)],
                    tr=tr, name=name, reverse=True, scratch=[pltpu.VMEM((SUBLANES, LANES), F32)])


def _fox_attn_fwd(qn, kn, vb, c_b, c_rowp, *, tb, name):
    S_len, W = qn.shape
    HP = W // LANES
    tb = min(tb, S_len)
    nb = S_len // tb
    scale = FOX_DH ** -0.5

    def body(q_ref, k_ref, v_ref, cb_ref, cr_ref, o_ref, lse_ref, m_scr, l_scr, acc_scr):
        i, j = pl.program_id(1), pl.program_id(2)

        @pl.when(j == 0)
        def _():
            m_scr[...] = jnp.full(m_scr.shape, -jnp.inf, F32)
            l_scr[...] = jnp.zeros(l_scr.shape, F32)
            acc_scr[...] = jnp.zeros(acc_scr.shape, F32)

        @pl.when(j <= i)
        def _():
            q, k, v = q_ref[...], k_ref[...], v_ref[...]
            lo = _iota((1, LANES), 1) < FOX_DH
            causal = (j * tb + _iota((tb, tb), 1)) <= (i * tb + _iota((tb, tb), 0))
            pv = jnp.zeros((tb, LANES), F32)
            alpha_full = jnp.zeros((tb, LANES), F32)
            for hh in range(2):
                msk = lo if hh == 0 else jnp.logical_not(lo)
                qh = jnp.where(msk, q, jnp.zeros_like(q))
                s = lax.dot_general(qh, k, _DIMS["nt"], preferred_element_type=F32) * scale
                ct = cb_ref[:, hh * FOX_DH:hh * FOX_DH + 1]
                cs = cr_ref[hh:hh + 1, :]
                s = jnp.where(causal, s + (ct - cs), -jnp.inf)
                m_prev = m_scr[hh]
                m_new = jnp.maximum(m_prev, jnp.max(s, -1, keepdims=True))
                alpha = jnp.exp(m_prev - m_new)
                p = jnp.exp(s - m_new)
                l_scr[hh] = alpha * l_scr[hh] + jnp.sum(p, -1, keepdims=True)
                m_scr[hh] = m_new
                vh = jnp.where(msk, v, jnp.zeros_like(v))
                pv = pv + lax.dot_general(p.astype(BF16), vh, _DIMS["nn"], preferred_element_type=F32)
                alpha_full = jnp.where(msk, alpha, alpha_full)
            acc_scr[...] = acc_scr[...] * alpha_full + pv

        @pl.when(j == nb - 1)
        def _():
            lo = _iota((1, LANES), 1) < FOX_DH
            l_full = jnp.where(lo, l_scr[0], l_scr[1])
            m_full = jnp.where(lo, m_scr[0], m_scr[1])
            o_ref[...] = acc_scr[...] / l_full
            lse_ref[...] = m_full + jnp.log(l_full)

    qs = pl.BlockSpec((tb, LANES), lambda h, i, j: (i, h))
    ks = pl.BlockSpec((tb, LANES), lambda h, i, j: (jnp.minimum(j, i), h))
    crs = pl.BlockSpec((None, SUBLANES, tb), lambda h, i, j: (h, 0, jnp.minimum(j, i)))
    return pl.pallas_call(
        body, name=name, grid=(HP, nb, nb),
        in_specs=[qs, ks, ks, qs, crs], out_specs=[qs, qs],
        out_shape=[jax.ShapeDtypeStruct((S_len, W), F32), jax.ShapeDtypeStruct((S_len, W), F32)],
        scratch_shapes=[pltpu.VMEM((2, tb, 1), F32), pltpu.VMEM((2, tb, 1), F32), pltpu.VMEM((tb, LANES), F32)],
        compiler_params=_params("parallel", "parallel", "arbitrary"),
    )(qn, kn, vb, c_b, c_rowp)


def _fox_attn_bwd(qn, kn, vb, c_b, c_rowp, o, lse_b, do, *, tb, name):
    S_len, W = qn.shape
    HP = W // LANES
    tb = min(tb, S_len)
    nb = S_len // tb
    scale = FOX_DH ** -0.5

    def body(q_ref, k_ref, v_ref, cb_ref, cr_ref, o_ref, lse_ref, do_ref, dq_ref, dk_ref, dv_ref, dcr_ref, dct_ref,
             dk_scr, dv_scr, dc_scr):
        j, i = pl.program_id(1), pl.program_id(2)

        @pl.when(jnp.logical_and(j == 0, i == 0))
        def _():
            dq_ref[...] = jnp.zeros(dq_ref.shape, F32)
            dct_ref[...] = jnp.zeros(dct_ref.shape, F32)

        @pl.when(i == 0)
        def _():
            dk_scr[...] = jnp.zeros(dk_scr.shape, F32)
            dv_scr[...] = jnp.zeros(dv_scr.shape, F32)
            dc_scr[...] = jnp.zeros(dc_scr.shape, F32)

        @pl.when(i >= j)
        def _():
            q, k, v = q_ref[...], k_ref[...], v_ref[...]
            dof = do_ref[...]
            dob = dof.astype(BF16)
            doo = dof * o_ref[...]
            lo = _iota((1, LANES), 1) < FOX_DH
            causal = (j * tb + _iota((tb, tb), 1)) <= (i * tb + _iota((tb, tb), 0))
            dq_part = jnp.zeros((tb, LANES), F32)
            dct_part = jnp.zeros((tb, LANES), F32)
            for hh in range(2):
                msk = lo if hh == 0 else jnp.logical_not(lo)
                qh = jnp.where(msk, q, jnp.zeros_like(q))
                kh = jnp.where(msk, k, jnp.zeros_like(k))
                doh = jnp.where(msk, dob, jnp.zeros_like(dob))
                s = lax.dot_general(qh, k, _DIMS["nt"], preferred_element_type=F32) * scale
                ct = cb_ref[:, hh * FOX_DH:hh * FOX_DH + 1]
                cs = cr_ref[hh:hh + 1, :]
                lse = lse_ref[:, hh * FOX_DH:hh * FOX_DH + 1]
                p = jnp.exp(jnp.where(causal, s + (ct - cs), -jnp.inf) - lse)
                pb = p.astype(BF16)
                dv_scr[...] += lax.dot_general(pb, doh, _DIMS["tn"], preferred_element_type=F32)
                dp = lax.dot_general(doh, v, _DIMS["nt"], preferred_element_type=F32)
                delta = jnp.sum(jnp.where(msk, doo, 0.0), -1, keepdims=True)
                ds = p * (dp - delta)
                dsb = ds.astype(BF16)
                dq_part = dq_part + lax.dot_general(dsb, kh, _DIMS["nn"], preferred_element_type=F32)
                dk_scr[...] += lax.dot_general(dsb, qh, _DIMS["tn"], preferred_element_type=F32)
                dc_scr[hh:hh + 1, :] -= jnp.sum(ds, axis=0, keepdims=True)
                dct_part = jnp.where(msk, jnp.sum(ds, -1, keepdims=True), dct_part)
            rows = pl.ds(pl.multiple_of(i * tb, tb), tb)
            dq_ref[rows, :] += dq_part * scale
            dct_ref[rows, :] += dct_part

        @pl.when(i == nb - 1)
        def _():
            dk_ref[...] = dk_scr[...] * scale
            dv_ref[...] = dv_scr[...]
            dcr_ref[...] = dc_scr[...]

    qs = pl.BlockSpec((tb, LANES), lambda h, j, i: (jnp.maximum(i, j), h))
    ks = pl.BlockSpec((tb, LANES), lambda h, j, i: (j, h))
    crs = pl.BlockSpec((None, SUBLANES, tb), lambda h, j, i: (h, 0, j))
    return pl.pallas_call(
        body, name=name, grid=(HP, nb, nb),
        in_specs=[qs, ks, ks, qs, crs, qs, qs, qs],
        out_specs=[pl.BlockSpec((S_len, LANES), lambda h, j, i: (0, h)), ks, ks, crs,
                   pl.BlockSpec((S_len, LANES), lambda h, j, i: (0, h))],
        out_shape=[jax.ShapeDtypeStruct((S_len, W), F32)] * 3 + [jax.ShapeDtypeStruct((HP, SUBLANES, S_len), F32),
                                                                 jax.ShapeDtypeStruct((S_len, W), F32)],
        scratch_shapes=[pltpu.VMEM((tb, LANES), F32), pltpu.VMEM((tb, LANES), F32), pltpu.VMEM((SUBLANES, tb), F32)],
        compiler_params=_params("parallel", "arbitrary", "arbitrary"),
    )(qn, kn, vb, c_b, c_rowp, o, lse_b, do)


def _fox_post_tile(o, z):
    return o * _silu(z)


def _fox_post_fwd(o, h_main, *, tr, name):
    W = o.shape[1]

    def body(rows, consts, orows, oaccs, scr, step, blk):
        orows[0][...] = _fox_post_tile(rows[0][...], rows[1][...]).astype(BF16)

    z = (h_main, lambda tr_, n: pl.BlockSpec((tr_, W), lambda i: (i, 3)))
    return _rowcall(body, [o, z], [], [(W, BF16)], [], tr=tr, name=name)[0]


def _fox_post_bwd(o, h_main, d_og, *, tr, name):
    W = o.shape[1]

    def body(rows, consts, orows, oaccs, scr, step, blk):
        _, vjp = jax.vjp(_fox_post_tile, rows[0][...], rows[1][...])
        d_o, d_z = vjp(rows[2][...])
        orows[0][...] = d_o
        orows[1][...] = d_z.astype(BF16)

    z = (h_main, lambda tr_, n: pl.BlockSpec((tr_, W), lambda i: (i, 3)))
    return _rowcall(body, [o, z, d_og], [], [(W, F32), (W, BF16)], [], tr=tr, name=name)


MESH_IDS = pl.DeviceIdType.MESH
N_CHIPS = 4
N_DEV = 8
_ANY = pl.BlockSpec(memory_space=pl.ANY)


def _xy_exchange(src, *, gather, name):
    shape = (N_CHIPS,) + tuple(src.shape[-2:])

    def body(src_ref, out_ref, send_sems, recv_sems, local_sem):
        x, y, c = lax.axis_index("x"), lax.axis_index("y"), lax.axis_index("c")
        me = 2 * x + y
        peers = [(1 - x, y), (x, 1 - y), (1 - x, 1 - y)]

        def outgoing(px, py):
            return src_ref if gather else src_ref.at[2 * px + py]

        mine = pltpu.make_async_copy(outgoing(x, y), out_ref.at[me], local_sem)
        mine.start()
        copies = []
        for j, (px, py) in enumerate(peers):
            cp = pltpu.make_async_remote_copy(
                src_ref=outgoing(px, py), dst_ref=out_ref.at[me],
                send_sem=send_sems.at[j], recv_sem=recv_sems.at[j],
                device_id=(px, py, c), device_id_type=MESH_IDS)
            cp.start()
            copies.append(cp)
        for j, (px, py) in enumerate(peers):
            pltpu.make_async_remote_copy(
                src_ref=outgoing(px, py), dst_ref=out_ref.at[2 * px + py],
                send_sem=send_sems.at[j], recv_sem=recv_sems.at[j],
                device_id=(px, py, c), device_id_type=MESH_IDS).wait_recv()
        for cp in copies:
            cp.wait_send()
        mine.wait()

    return pl.pallas_call(
        body, name=name, in_specs=[_ANY], out_specs=_ANY,
        out_shape=jax.ShapeDtypeStruct(shape, src.dtype),
        scratch_shapes=[pltpu.SemaphoreType.DMA((3,)), pltpu.SemaphoreType.DMA((3,)), pltpu.SemaphoreType.DMA],
    )(src)


def _c_swap(src, *, name):
    def body(src_ref, out_ref, send_sem, recv_sem):
        x, y, c = lax.axis_index("x"), lax.axis_index("y"), lax.axis_index("c")
        cp = pltpu.make_async_remote_copy(
            src_ref=src_ref, dst_ref=out_ref, send_sem=send_sem, recv_sem=recv_sem,
            device_id=(x, y, 1 - c), device_id_type=MESH_IDS)
        cp.start()
        cp.wait()

    return pl.pallas_call(
        body, name=name, in_specs=[_ANY], out_specs=_ANY,
        out_shape=jax.ShapeDtypeStruct(src.shape, src.dtype),
        scratch_shapes=[pltpu.SemaphoreType.DMA, pltpu.SemaphoreType.DMA],
    )(src)


def _all_gather8(blk, *, name):
    m_per, n = blk.shape

    def body(x_ref, out_ref, send_sems, recv_sems, local_sem):
        x, y, c = lax.axis_index("x"), lax.axis_index("y"), lax.axis_index("c")
        me, sibling = (x, y, c), (x, y, 1 - c)
        chips = [(1 - x, y), (x, 1 - y), (1 - x, 1 - y)]

        def rows(px, py, pc):
            return out_ref.at[pl.ds((4 * px + 2 * py + pc) * m_per, m_per), :]

        def copy(k, block, to, src=None):
            return pltpu.make_async_remote_copy(
                src_ref=rows(*block) if src is None else src, dst_ref=rows(*block),
                send_sem=send_sems.at[k], recv_sem=recv_sems.at[k], device_id=to, device_id_type=MESH_IDS)

        mine = pltpu.make_async_copy(x_ref, rows(*me), local_sem)
        mine.start()
        first = [copy(0, me, sibling, src=x_ref)]
        first += [copy(1 + j, me, (*chip, c), src=x_ref) for j, chip in enumerate(chips)]
        for cp in first:
            cp.start()
        passed = [copy(4 + j, (*chip, c), sibling) for j, chip in enumerate(chips)]
        for j, chip in enumerate(chips):
            copy(1 + j, (*chip, c), me).wait_recv()
            passed[j].start()
        copy(0, sibling, me).wait_recv()
        for j, chip in enumerate(chips):
            copy(4 + j, (*chip, 1 - c), me).wait_recv()
        for cp in first + passed:
            cp.wait_send()
        mine.wait()

    return pl.pallas_call(
        body, name=name,
        out_shape=jax.ShapeDtypeStruct((N_DEV * m_per, n), blk.dtype),
        in_specs=[pl.BlockSpec(memory_space=pltpu.VMEM)], out_specs=pl.BlockSpec(memory_space=pltpu.VMEM),
        scratch_shapes=[pltpu.SemaphoreType.DMA((7,)), pltpu.SemaphoreType.DMA((7,)), pltpu.SemaphoreType.DMA],
    )(blk)


def _sum_slots(parts, *, tr, name):
    n, R, _ = parts.shape
    pack = 2 * SUBLANES
    tr = max(t for t in range(pack, min(tr, R) + 1, pack) if R % t == 0) if R % pack == 0 else R

    def body(p_ref, o_ref):
        tot = p_ref[0].astype(F32)
        for s in range(1, n):
            tot = tot + p_ref[s].astype(F32)
        o_ref[...] = tot

    return pl.pallas_call(
        body, name=name, grid=(R // tr,),
        in_specs=[pl.BlockSpec((n, tr, LANES), lambda i: (0, i, 0))], out_specs=pl.BlockSpec((tr, LANES), lambda i: (i, 0)),
        out_shape=jax.ShapeDtypeStruct((R, LANES), F32), compiler_params=_params("parallel"),
    )(parts)


def _adamw(w, g_parts, m, v, *, name):
    shape = w.shape
    as2d = lambda a: a.reshape(-1, shape[-1])
    w2, m2, v2 = as2d(w), as2d(m), as2d(v)
    gs = [as2d(g) for g in g_parts]
    R, C = w2.shape
    tr = R
    while tr * C * 4 > (1 << 20) and tr % 2 == 0 and (tr // 2) % SUBLANES == 0:
        tr //= 2
    ng = len(gs)

    def body(*refs):
        w_ref, m_ref, v_ref = refs[:3]
        g_refs = refs[3:3 + ng]
        go_ref, d_ref, mo_ref, vo_ref = refs[3 + ng:]
        g = g_refs[0][...]
        for r in g_refs[1:]:
            g = g + r[...]
        mn = ADAM_B1 * m_ref[...] + (1.0 - ADAM_B1) * g
        vn = ADAM_B2 * v_ref[...] + (1.0 - ADAM_B2) * jnp.square(g)
        m_hat = mn / (1.0 - ADAM_B1 ** ADAM_STEP)
        v_hat = vn / (1.0 - ADAM_B2 ** ADAM_STEP)
        go_ref[...] = g
        d_ref[...] = -ADAM_LR * (m_hat / (jnp.sqrt(v_hat) + ADAM_EPS) + ADAM_WD * w_ref[...])
        mo_ref[...] = mn
        vo_ref[...] = vn

    spec = pl.BlockSpec((tr, C), lambda i: (i, 0))
    outs = pl.pallas_call(
        body, name=name, grid=(R // tr,), in_specs=[spec] * (3 + ng), out_specs=[spec] * 4,
        out_shape=[jax.ShapeDtypeStruct((R, C), F32)] * 4, compiler_params=_params("parallel"),
    )(w2, m2, v2, *gs)
    return tuple(o.reshape(shape) for o in outs)


TR = 256
ATTN_TILE = 512


def _mm_nn(a, b, name, **kw):
    return _matmul(a, b, tm=512, tn=1024, tk=1024, name=name, **kw)


def _mm_nt(a, b, name, **kw):
    return _matmul(a, b, tb=True, tm=512, tn=1024, tk=1024, name=name, **kw)


def _mm_tn(a, b, name, **kw):
    return _matmul(a, b, ta=True, tm=1024, tn=1024, tk=512, name=name, **kw)


def _c_rows(c, Hf):
    S_len = c.shape[0]
    ct = c[:, :Hf].T.reshape(Hf // 2, 2, S_len)
    return jnp.pad(ct, ((0, 0), (0, SUBLANES - 2), (0, 0)))


def _local_step(x, p, target, wts):
    L = wts["ln_g"].shape[0]
    alpha = (2 * L) ** 0.25
    Hg = wts["gdn_a_log"].shape[1]
    Hf = wts["fox_b_f"].shape[1]
    Wg_ = Hg * GDN_DK
    Wf_ = Hf * FOX_DH
    saved = []
    for i in range(L):
        j = i // 2
        sv = {"x": x}
        if i % 2 == 0:
            w_in = wts["gdn_w_in"][j]
            wm, ws = w_in[:, :4 * Wg_], _pad_lanes(w_in[:, 4 * Wg_:])
            cw8 = jnp.pad(wts["gdn_conv_w"][j], ((0, SUBLANES - GDN_CONV), (0, 0)))
            alog = _pad_lanes(_row(wts["gdn_a_log"][j]), offset=Hg)
            dtb = _pad_lanes(_row(wts["gdn_dt_bias"][j]), offset=Hg)
            hm = _mm_nn(x, wm, f"gdn{j}_in_main")
            hs = _mm_nn(x, ws, f"gdn{j}_in_small")
            q, k, v, bg = _gdn_pre_fwd(hm, hs, cw8, alog, dtb, Hg, tr=TR, name=f"gdn{j}_pre")
            o, states = _gdn_rule_fwd(q, k, v, bg, Hg, name=f"gdn{j}_rule")
            on = _gdn_post_fwd(o, hm, wts["gdn_norm_g"][j], Hg, tr=TR, name=f"gdn{j}_post")
            y = _mm_nn(on, wts["gdn_w_out"][j], f"gdn{j}_out")
            sv.update(wm=wm, ws=ws, cw8=cw8, alog=alog, dtb=dtb, hm=hm, hs=hs, q=q, k=k, v=v, bg=bg, o=o,
                      states=states, on=on)
        else:
            w_in = wts["fox_w_in"][j]
            wm, ws = w_in[:, :4 * Wf_], _pad_lanes(w_in[:, 4 * Wf_:])
            gq2 = _row(jnp.tile(wts["fox_q_norm_g"][j], 2))
            gk2 = _row(jnp.tile(wts["fox_k_norm_g"][j], 2))
            bf = _pad_lanes(_row(wts["fox_b_f"][j]))
            hm = _mm_nn(x, wm, f"fox{j}_in_main")
            hs = _mm_nn(x, ws, f"fox{j}_in_small")
            qn, kn, vb, c, c_b = _fox_pre_fwd(hm, hs, gq2, gk2, bf, Hf, tr=TR, name=f"fox{j}_pre")
            c_rowp = _c_rows(c, Hf)
            o, lse_b = _fox_attn_fwd(qn, kn, vb, c_b, c_rowp, tb=ATTN_TILE, name=f"fox{j}_attn")
            on = _fox_post_fwd(o, hm, tr=TR, name=f"fox{j}_post")
            y = _mm_nn(on, wts["fox_w_out"][j], f"fox{j}_out")
            sv.update(wm=wm, ws=ws, gq2=gq2, gk2=gk2, bf=bf, hm=hm, hs=hs, qn=qn, kn=kn, vb=vb, c_b=c_b,
                      c_rowp=c_rowp, o=o, lse_b=lse_b, on=on)
        x_ln = _ln_fwd(x, y, wts["ln_g"][i], wts["ln_b"][i], alpha, tr=TR, name=f"ln{i}")
        gp = _mm_nn(x_ln, wts["ple_w_gate"][i], f"ple{i}_gate")
        pp = _mm_nn(p[i], wts["ple_w_proj"][i], f"ple{i}_proj")
        x_out = _ple_fwd(x_ln, gp, pp, tr=TR, name=f"ple{i}_mix")
        sv.update(y=y, x_ln=x_ln, gp=gp, pp=pp)
        saved.append(sv)
        x = x_out

    loss_row, dx = _loss_fwd_bwd(x, target, tr=TR, name="loss")

    g = {n: [None] * wts[n].shape[0] for n in wts}
    for i in reversed(range(L)):
        j = i // 2
        sv = saved[i]
        d_pre, d_pp = _ple_bwd(dx, sv["gp"], sv["pp"], tr=TR, name=f"ple{i}_mix_bwd")
        g["ple_w_gate"][i] = _mm_tn(sv["x_ln"], d_pre, f"ple{i}_gate_dw")
        g["ple_w_proj"][i] = _mm_tn(p[i], d_pp, f"ple{i}_proj_dw")
        t = _mm_nt(d_pre, wts["ple_w_gate"][i], f"ple{i}_gate_dx")
        du, g["ln_g"][i], g["ln_b"][i] = _ln_bwd(sv["x"], sv["y"], wts["ln_g"][i], wts["ln_b"][i], dx, t, alpha,
                                                 tr=TR, name=f"ln{i}_bwd")
        if i % 2 == 0:
            g["gdn_w_out"][j] = _mm_tn(sv["on"], du, f"gdn{j}_out_dw")
            d_on = _mm_nt(du, wts["gdn_w_out"][j], f"gdn{j}_out_dx")
            d_o, d_z, d_ng = _gdn_post_bwd(sv["o"], sv["hm"], wts["gdn_norm_g"][j], d_on, Hg, tr=TR, name=f"gdn{j}_post_bwd")
            dq, dk, dv, dbg = _gdn_rule_bwd(sv["q"], sv["k"], sv["v"], sv["bg"], sv["states"], d_o, Hg, name=f"gdn{j}_rule_bwd")
            d_hm, d_hs, d_cw, d_al, d_dtb = _gdn_pre_bwd(sv["hm"], sv["hs"], sv["cw8"], sv["alog"], sv["dtb"],
                                                         dq, dk, dv, dbg, d_z, Hg, tr=TR, name=f"gdn{j}_pre_bwd")
            g["gdn_norm_g"][j] = d_ng[0]
            g["gdn_conv_w"][j] = d_cw[:GDN_CONV]
            g["gdn_a_log"][j] = d_al[0, Hg:2 * Hg]
            g["gdn_dt_bias"][j] = d_dtb[0, Hg:2 * Hg]
            wname, nsmall, Wd = "gdn_w_in", 2 * Hg, Wg_
        else:
            g["fox_w_out"][j] = _mm_tn(sv["on"], du, f"fox{j}_out_dw")
            d_og = _mm_nt(du, wts["fox_w_out"][j], f"fox{j}_out_dx")
            d_o, d_z = _fox_post_bwd(sv["o"], sv["hm"], d_og, tr=TR, name=f"fox{j}_post_bwd")
            dqn, dkn, dvv, dcr, dct = _fox_attn_bwd(sv["qn"], sv["kn"], sv["vb"], sv["c_b"], sv["c_rowp"], sv["o"], sv["lse_b"],
                                               d_o, tb=ATTN_TILE, name=f"fox{j}_attn_bwd")
            dc = _pad_lanes(dcr[:, :2, :].reshape(Hf, -1).T + dct[:, ::FOX_DH])
            d_hm, d_hs, d_gq, d_gk, d_bf = _fox_pre_bwd(sv["hm"], sv["hs"], sv["gq2"], sv["gk2"], sv["bf"],
                                                        dqn, dkn, dvv, d_z, dc, Hf, tr=TR, name=f"fox{j}_pre_bwd")
            g["fox_q_norm_g"][j] = d_gq[0, :FOX_DH] + d_gq[0, FOX_DH:]
            g["fox_k_norm_g"][j] = d_gk[0, :FOX_DH] + d_gk[0, FOX_DH:]
            g["fox_b_f"][j] = d_bf[0, :Hf]
            wname, nsmall, Wd = "fox_w_in", Hf, Wf_
        dwm = _mm_tn(sv["x"], d_hm, f"{wname}{j}_main_dw")
        dws = _mm_tn(sv["x"], d_hs, f"{wname}{j}_small_dw")
        g[wname][j] = jnp.concatenate([dwm, dws[:, :nsmall]], axis=1)
        t1 = _mm_nt(d_hs, sv["ws"], f"{wname}{j}_small_dx", add=du, add_scale=alpha)
        dx = _mm_nt(d_hm, sv["wm"], f"{wname}{j}_main_dx", add=t1)
        del Wd
    grads = {n: jnp.stack(v) for n, v in g.items()}
    return loss_row, dx, grads


_SHARDED = (("ple_w_gate", 1), ("ple_w_proj", 2), ("gdn_w_in", 2), ("gdn_conv_w", 2), ("gdn_w_out", 1),
            ("fox_w_in", 2), ("fox_w_out", 1))
_REPLICATED = ("ln_g", "ln_b", "gdn_a_log", "gdn_dt_bias", "gdn_norm_g", "fox_b_f", "fox_q_norm_g", "fox_k_norm_g")
_EXACT = ("gdn_conv_w",)
_ORDER = ("ln_g", "ln_b", "ple_w_gate", "ple_w_proj", "gdn_w_in", "gdn_conv_w", "gdn_a_log", "gdn_dt_bias",
          "gdn_norm_g", "gdn_w_out", "fox_w_in", "fox_b_f", "fox_q_norm_g", "fox_k_norm_g", "fox_w_out")


def _as_rows(a):
    return a.reshape(-1, LANES)


def _gather_weights(local):
    parts = []
    for name, _ in _SHARDED:
        w = local[name]
        wb = lax.bitcast_convert_type(w, BF16) if name in _EXACT else w.astype(BF16)
        parts.append(_as_rows(wb))
    packed = jnp.concatenate(parts, axis=0)
    got = _xy_exchange(packed, gather=True, name="gather_weights")
    full, r0 = {}, 0
    for (name, axis), part in zip(_SHARDED, parts):
        nrow = part.shape[0]
        seg = got[:, r0:r0 + nrow]
        r0 += nrow
        shp = local[name].shape
        if name in _EXACT:
            blocks = lax.bitcast_convert_type(seg.reshape((N_CHIPS,) + shp + (2,)), F32)
        else:
            blocks = seg.reshape((N_CHIPS,) + shp)
        full[name] = jnp.concatenate([blocks[s] for s in range(N_CHIPS)], axis=axis)
    return full


def _reduce_sharded(grads):
    per_owner = []
    sizes = []
    for s in range(N_CHIPS):
        parts = []
        for name, axis in _SHARDED:
            gfull = grads[name]
            n = gfull.shape[axis] // N_CHIPS
            blk = lax.slice_in_dim(gfull, s * n, (s + 1) * n, axis=axis)
            parts.append(_as_rows(blk.astype(BF16)))
        sizes = [q.shape[0] for q in parts]
        per_owner.append(jnp.concatenate(parts, axis=0))
    packed = jnp.stack(per_owner)
    got = _xy_exchange(packed, gather=False, name="exchange_grads")
    mine = _sum_slots(got, tr=4096, name="sum_grads")
    other = _c_swap(mine, name="swap_grads")
    out, r0 = {}, 0
    for (name, axis), nrow in zip(_SHARDED, sizes):
        shp = list(grads[name].shape)
        shp[axis] //= N_CHIPS
        out[name] = (mine[r0:r0 + nrow].reshape(shp), other[r0:r0 + nrow].reshape(shp))
        r0 += nrow
    return out


def _reduce_replicated(grads, loss_part):
    rows = [_pad_lanes(jnp.reshape(loss_part, (1, 1)))]
    for name in _REPLICATED:
        gr = grads[name]
        rows.append(_as_rows(gr) if gr.shape[-1] % LANES == 0 else _pad_lanes(gr))
    sizes = [r.shape[0] for r in rows]
    blk = jnp.concatenate(rows, axis=0)
    nrow = blk.shape[0]
    npad = -nrow % SUBLANES
    blk = jnp.pad(blk, ((0, npad), (0, 0)))
    allb = _all_gather8(blk, name="gather_small_grads").reshape(N_DEV, nrow + npad, LANES)
    tot = _sum_slots(allb, tr=nrow + npad, name="sum_small_grads")
    out, r0 = {}, sizes[0]
    loss = tot[0, 0]
    for name, n in zip(_REPLICATED, sizes[1:]):
        gr = grads[name]
        seg = tot[r0:r0 + n]
        out[name] = seg.reshape(gr.shape) if gr.shape[-1] % LANES == 0 else seg[:, :gr.shape[-1]]
        r0 += n
    return loss, out


def kernel(x, p, ln_g, ln_b, ple_w_gate, ple_w_proj, gdn_w_in, gdn_conv_w, gdn_a_log, gdn_dt_bias, gdn_norm_g, gdn_w_out, fox_w_in, fox_b_f, fox_q_norm_g, fox_k_norm_g, fox_w_out, loss_target, m_ln_g, m_ln_b, m_ple_w_gate, m_ple_w_proj, m_gdn_w_in, m_gdn_conv_w, m_gdn_a_log, m_gdn_dt_bias, m_gdn_norm_g, m_gdn_w_out, m_fox_w_in, m_fox_b_f, m_fox_q_norm_g, m_fox_k_norm_g, m_fox_w_out, v_ln_g, v_ln_b, v_ple_w_gate, v_ple_w_proj, v_gdn_w_in, v_gdn_conv_w, v_gdn_a_log, v_gdn_dt_bias, v_gdn_norm_g, v_gdn_w_out, v_fox_w_in, v_fox_b_f, v_fox_q_norm_g, v_fox_k_norm_g, v_fox_w_out):
    local = dict(ln_g=ln_g, ln_b=ln_b, ple_w_gate=ple_w_gate, ple_w_proj=ple_w_proj, gdn_w_in=gdn_w_in,
                 gdn_conv_w=gdn_conv_w, gdn_a_log=gdn_a_log, gdn_dt_bias=gdn_dt_bias, gdn_norm_g=gdn_norm_g,
                 gdn_w_out=gdn_w_out, fox_w_in=fox_w_in, fox_b_f=fox_b_f, fox_q_norm_g=fox_q_norm_g,
                 fox_k_norm_g=fox_k_norm_g, fox_w_out=fox_w_out)
    mom_m = dict(ln_g=m_ln_g, ln_b=m_ln_b, ple_w_gate=m_ple_w_gate, ple_w_proj=m_ple_w_proj, gdn_w_in=m_gdn_w_in,
                 gdn_conv_w=m_gdn_conv_w, gdn_a_log=m_gdn_a_log, gdn_dt_bias=m_gdn_dt_bias, gdn_norm_g=m_gdn_norm_g,
                 gdn_w_out=m_gdn_w_out, fox_w_in=m_fox_w_in, fox_b_f=m_fox_b_f, fox_q_norm_g=m_fox_q_norm_g,
                 fox_k_norm_g=m_fox_k_norm_g, fox_w_out=m_fox_w_out)
    mom_v = dict(ln_g=v_ln_g, ln_b=v_ln_b, ple_w_gate=v_ple_w_gate, ple_w_proj=v_ple_w_proj, gdn_w_in=v_gdn_w_in,
                 gdn_conv_w=v_gdn_conv_w, gdn_a_log=v_gdn_a_log, gdn_dt_bias=v_gdn_dt_bias, gdn_norm_g=v_gdn_norm_g,
                 gdn_w_out=v_gdn_w_out, fox_w_in=v_fox_w_in, fox_b_f=v_fox_b_f, fox_q_norm_g=v_fox_q_norm_g,
                 fox_k_norm_g=v_fox_k_norm_g, fox_w_out=v_fox_w_out)

    wts = dict(_gather_weights(local))
    for name in _REPLICATED:
        wts[name] = local[name]
    loss_row, dx, grads = _local_step(x[0], p[:, 0], loss_target[0], wts)
    loss, small = _reduce_replicated(grads, jnp.sum(loss_row))
    big = _reduce_sharded(grads)

    outs = {}
    for name in _ORDER:
        parts = list(big[name]) if name in big else [small[name]]
        outs[name] = _adamw(local[name], parts, mom_m[name], mom_v[name], name=f"adamw_{name}")
    return (loss, dx[None], *[outs[n][0] for n in _ORDER], *[outs[n][1] for n in _ORDER],
            *[outs[n][2] for n in _ORDER], *[outs[n][3] for n in _ORDER])
```

```python
import functools

import jax
import jax.numpy as jnp
from jax import lax
from jax.experimental import pallas as pl
from jax.experimental.pallas import tpu as pltpu

F32 = jnp.float32
BF16 = jnp.bfloat16

LANES = 128
SUBLANES = 8
VMEM_LIMIT_BYTES = 56 * 1024 * 1024

GDN_DK = 128
GDN_CHUNK = 64
GDN_CONV = 4
FOX_DH = 64
LN_EPS = 1e-5
RMS_EPS = 1e-6

ADAM_LR = 0.001
ADAM_B1 = 0.9
ADAM_B2 = 0.999
ADAM_EPS = 1e-08
ADAM_WD = 0.01
ADAM_STEP = 10

HI = lax.Precision.HIGHEST
_DIMS = {"nn": (((1,), (0,)), ((), ())), "nt": (((1,), (1,)), ((), ())), "tn": (((0,), (0,)), ((), ()))}


def _params(*sem):
    return pltpu.CompilerParams(dimension_semantics=sem, vmem_limit_bytes=VMEM_LIMIT_BYTES)


def _raw_mm(a, b, form, hi):
    if hi:
        return lax.dot_general(a.astype(F32), b.astype(F32), _DIMS[form], precision=HI, preferred_element_type=F32)
    return lax.dot_general(a.astype(BF16), b.astype(BF16), _DIMS[form], preferred_element_type=F32)


@functools.partial(jax.custom_vjp, nondiff_argnums=(2, 3))
def _mm(a, b, form, hi):
    return _raw_mm(a, b, form, hi)


def _mm_fwd(a, b, form, hi):
    return _raw_mm(a, b, form, hi), (a, b)


def _mm_bwd(form, hi, res, g):
    a, b = res
    if form == "nn":
        return _mm(g, b, "nt", hi), _mm(a, g, "tn", hi)
    if form == "nt":
        return _mm(g, b, "nn", hi), _mm(g, a, "tn", hi)
    return _mm(b, g, "nt", hi), _mm(a, g, "nn", hi)


_mm.defvjp(_mm_fwd, _mm_bwd)


def _silu(x):
    return x * jax.nn.sigmoid(x)


def _softplus(x):
    return jnp.maximum(x, 0.0) + jnp.log1p(jnp.exp(-jnp.abs(x)))


def _iota(shape, dim):
    return lax.broadcasted_iota(jnp.int32, shape, dim)


def _matmul(a, b, *, ta=False, tb=False, out_dtype=F32, add=None, add_scale=1.0, tm=512, tn=512, tk=512, name):
    if ta:
        K, M = a.shape
    else:
        M, K = a.shape
    if tb:
        N, K2 = b.shape
    else:
        K2, N = b.shape
    assert K == K2, (a.shape, b.shape, ta, tb)
    tm, tn, tk = min(tm, M), min(tn, N), min(tk, K)
    assert M % tm == 0 and N % tn == 0 and K % tk == 0, (M, N, K, tm, tn, tk)
    nk = K // tk
    form = ("t" if ta else "n") + ("t" if tb else "n")
    dims = (((0 if ta else 1,), (1 if tb else 0,)), ((), ()))
    del form
    a_spec = pl.BlockSpec((tk, tm), lambda i, j, k: (k, i)) if ta else pl.BlockSpec((tm, tk), lambda i, j, k: (i, k))
    b_spec = pl.BlockSpec((tn, tk), lambda i, j, k: (j, k)) if tb else pl.BlockSpec((tk, tn), lambda i, j, k: (k, j))
    o_spec = pl.BlockSpec((tm, tn), lambda i, j, k: (i, j))
    has_add = add is not None

    def body(*refs):
        if has_add:
            a_ref, b_ref, add_ref, o_ref, acc_ref = refs
        else:
            a_ref, b_ref, o_ref, acc_ref = refs
        k = pl.program_id(2)
        part = lax.dot_general(a_ref[...].astype(BF16), b_ref[...].astype(BF16), dims, preferred_element_type=F32)

        def finish(total):
            if has_add:
                total = total + add_scale * add_ref[...].astype(F32)
            o_ref[...] = total.astype(o_ref.dtype)

        if nk == 1:
            finish(part)
        else:
            @pl.when(k == 0)
            def _():
                acc_ref[...] = part

            @pl.when(jnp.logical_and(k > 0, k < nk - 1))
            def _():
                acc_ref[...] += part

            @pl.when(k == nk - 1)
            def _():
                finish(acc_ref[...] + part)

    in_specs = [a_spec, b_spec] + ([o_spec] if has_add else [])
    args = (a, b) + ((add,) if has_add else ())
    return pl.pallas_call(
        body, name=name, grid=(M // tm, N // tn, nk),
        in_specs=in_specs, out_specs=o_spec,
        out_shape=jax.ShapeDtypeStruct((M, N), out_dtype),
        scratch_shapes=[pltpu.VMEM((tm, tn), F32)],
        compiler_params=_params("parallel", "parallel", "arbitrary"),
    )(*args)


def _rowcall(body_fn, rows, consts, out_rows, out_accs, *, tr, name, reverse=False, scratch=()):
    def arr_spec(r):
        return r if isinstance(r, tuple) else (r, None)

    S = arr_spec(rows[0])[0].shape[0]
    tr = min(tr, S)
    assert S % tr == 0
    n = S // tr
    ridx = (lambda i: (n - 1 - i, 0)) if reverse else (lambda i: (i, 0))
    in_specs, args = [], []
    for r in rows:
        arr, spec = arr_spec(r)
        args.append(arr)
        in_specs.append(spec(tr, n) if spec is not None else pl.BlockSpec((tr, arr.shape[1]), ridx))
    for c in consts:
        args.append(c)
        in_specs.append(pl.BlockSpec(c.shape, lambda i: (0, 0)))
    out_specs, out_shape = [], []
    for (ncol, dt) in out_rows:
        out_specs.append(pl.BlockSpec((tr, ncol), ridx))
        out_shape.append(jax.ShapeDtypeStruct((S, ncol), dt))
    for shp in out_accs:
        out_specs.append(pl.BlockSpec(shp, lambda i: (0, 0)))
        out_shape.append(jax.ShapeDtypeStruct(shp, F32))
    nr, nc, no, na = len(rows), len(consts), len(out_rows), len(out_accs)

    def kernel(*refs):
        row_refs = refs[:nr]
        const_refs = refs[nr:nr + nc]
        orow_refs = refs[nr + nc:nr + nc + no]
        oacc_refs = refs[nr + nc + no:nr + nc + no + na]
        scr = refs[nr + nc + no + na:]
        step = pl.program_id(0)
        blk = (n - 1 - step) if reverse else step

        @pl.when(step == 0)
        def _():
            for acc in oacc_refs:
                acc[...] = jnp.zeros(acc.shape, F32)

        body_fn(row_refs, const_refs, orow_refs, oacc_refs, scr, step, blk)

    outs = pl.pallas_call(
        kernel, name=name, grid=(n,), in_specs=in_specs, out_specs=out_specs, out_shape=out_shape,
        scratch_shapes=list(scratch), compiler_params=_params("arbitrary"),
    )(*args)
    return outs


def _row(v):
    return v.astype(F32).reshape(1, -1)


def _pad_lanes(v, width=LANES, offset=0):
    pad = [(0, 0)] * (v.ndim - 1) + [(offset, width - offset - v.shape[-1])]
    return jnp.pad(v, pad)


def _ln_tile(x, y, g, b, alpha):
    u = alpha * x + y
    mu = jnp.mean(u, -1, keepdims=True)
    d = u - mu
    var = jnp.mean(d * d, -1, keepdims=True)
    return d * lax.rsqrt(var + LN_EPS) * g + b


def _ln_fwd(x, y, g, b, alpha, *, tr, name):
    D = x.shape[1]

    def body(rows, consts, orows, oaccs, scr, step, blk):
        orows[0][...] = _ln_tile(rows[0][...], rows[1][...], consts[0][...], consts[1][...], alpha)

    return _rowcall(body, [x, y], [_row(g), _row(b)], [(D, F32)], [], tr=tr, name=name)[0]


def _ln_bwd(x, y, g, b, dxo, t, alpha, *, tr, name):
    D = x.shape[1]

    def body(rows, consts, orows, oaccs, scr, step, blk):
        xv, yv = rows[0][...], rows[1][...]
        ct = rows[2][...] + rows[3][...]
        _, vjp = jax.vjp(lambda yy, gg, bb: _ln_tile(xv, yy, gg, bb, alpha), yv, consts[0][...], consts[1][...])
        du, dg, db = vjp(ct)
        orows[0][...] = du
        oaccs[0][...] += dg
        oaccs[1][...] += db

    du, dg, db = _rowcall(body, [x, y, dxo, t], [_row(g), _row(b)], [(D, F32)], [(1, D), (1, D)], tr=tr, name=name)
    return du, dg[0], db[0]


def _ple_fwd(x_ln, gp, pp, *, tr, name):
    D = x_ln.shape[1]

    def body(rows, consts, orows, oaccs, scr, step, blk):
        orows[0][...] = rows[0][...] + jax.nn.sigmoid(rows[1][...]) * rows[2][...]

    return _rowcall(body, [x_ln, gp, pp], [], [(D, F32)], [], tr=tr, name=name)[0]


def _ple_bwd(dxo, gp, pp, *, tr, name):
    D = dxo.shape[1]

    def body(rows, consts, orows, oaccs, scr, step, blk):
        d = rows[0][...]
        s = jax.nn.sigmoid(rows[1][...])
        orows[0][...] = (d * rows[2][...] * s * (1.0 - s)).astype(BF16)
        orows[1][...] = (d * s).astype(BF16)

    return _rowcall(body, [dxo, gp, pp], [], [(D, BF16), (D, BF16)], [], tr=tr, name=name)


def _loss_fwd_bwd(xf, target, *, tr, name):
    D = xf.shape[1]

    def body(rows, consts, orows, oaccs, scr, step, blk):
        err = rows[0][...] - rows[1][...]
        orows[0][...] = err * (1.0 / D)
        part = jnp.sum(err * err, axis=0, keepdims=True) * (0.5 / D)
        oaccs[0][...] += part

    dx, lrow = _rowcall(body, [xf, target], [], [(D, F32)], [(1, D)], tr=tr, name=name)
    return lrow, dx


def _gdn_qk_tile(c):
    y = _silu(c)
    return y * lax.rsqrt(jnp.sum(y * y, -1, keepdims=True) + RMS_EPS)


def _make_bg_fn(H):
    def fn(hs, alog, dtb):
        lane = _iota((1, LANES), 1)
        beta = jax.nn.sigmoid(hs)
        g = -jnp.exp(alog) * _softplus(hs + dtb)
        return jnp.where(lane < H, beta, jnp.where(lane < 2 * H, g, 0.0))
    return fn


def _halo_spec(ncol):
    def make(tr, n):
        per = tr // SUBLANES
        return pl.BlockSpec((SUBLANES, ncol), lambda i: (jnp.maximum(i * per - 1, 0), 0))
    return make


def _halo_spec_rev(ncol):
    def make(tr, n):
        per = tr // SUBLANES
        return pl.BlockSpec((SUBLANES, ncol), lambda i: (jnp.maximum((n - 1 - i) * per - 1, 0), 0))
    return make


def _gdn_pre_fwd(h_main, h_small, conv_w8, alog_row, dtb_row, H, *, tr, name):
    W = H * GDN_DK
    C3 = 3 * W
    bg_fn = _make_bg_fn(H)

    def body(rows, consts, orows, oaccs, scr, step, blk):
        main_ref, halo_ref, hs_ref = rows
        w_ref, alog_ref, dtb_ref = consts
        q_ref, k_ref, v_ref, bg_ref = orows
        xs = scr[0]
        trr = main_ref.shape[0]
        xs[pl.ds(SUBLANES, trr), :] = main_ref[...]
        xs[pl.ds(0, SUBLANES), :] = jnp.where(blk > 0, halo_ref[...], 0.0)
        for s in range(C3 // LANES):
            ls = slice(s * LANES, (s + 1) * LANES)
            c = jnp.zeros((trr, LANES), F32)
            for j in range(GDN_CONV):
                c = c + w_ref[GDN_CONV - 1 - j:GDN_CONV - j, ls] * xs[pl.ds(SUBLANES - j, trr), ls]
            if s < 2 * H:
                out = _gdn_qk_tile(c)
                (q_ref if s < H else k_ref)[:, (s % H) * LANES:(s % H + 1) * LANES] = out
            else:
                v_ref[:, (s - 2 * H) * LANES:(s - 2 * H + 1) * LANES] = _silu(c)
        bg_ref[...] = bg_fn(hs_ref[...], alog_ref[...], dtb_ref[...])

    main = (h_main, lambda tr_, n: pl.BlockSpec((tr_, C3), lambda i: (i, 0)))
    halo = (h_main, _halo_spec(C3))
    trr = min(tr, h_main.shape[0])
    return _rowcall(body, [main, halo, h_small], [conv_w8, alog_row, dtb_row],
                    [(W, F32), (W, F32), (W, F32), (LANES, F32)], [], tr=tr, name=name,
                    scratch=[pltpu.VMEM((trr + SUBLANES, C3), F32)])


def _gdn_pre_bwd(h_main, h_small, conv_w8, alog_row, dtb_row, dq, dk, dv, dbg, dz, H, *, tr, name):
    W = H * GDN_DK
    C3 = 3 * W
    bg_fn = _make_bg_fn(H)

    def body(rows, consts, orows, oaccs, scr, step, blk):
        main_ref, halo_ref, hs_ref, dq_ref, dk_ref, dv_ref, dbg_ref, dz_ref = rows
        w_ref, alog_ref, dtb_ref = consts
        dmain_ref, dhs_ref = orows
        dw_ref, dalog_ref, ddtb_ref = oaccs
        xs, dcs = scr
        trr = main_ref.shape[0]
        xs[pl.ds(SUBLANES, trr), :] = main_ref[...]
        xs[pl.ds(0, SUBLANES), :] = jnp.where(blk > 0, halo_ref[...], 0.0)

        @pl.when(step == 0)
        def _():
            dcs[pl.ds(trr, SUBLANES), :] = jnp.zeros((SUBLANES, C3), F32)

        for s in range(C3 // LANES):
            ls = slice(s * LANES, (s + 1) * LANES)
            c = jnp.zeros((trr, LANES), F32)
            for j in range(GDN_CONV):
                c = c + w_ref[GDN_CONV - 1 - j:GDN_CONV - j, ls] * xs[pl.ds(SUBLANES - j, trr), ls]
            if s < 2 * H:
                src = dq_ref if s < H else dk_ref
                ct = src[:, (s % H) * LANES:(s % H + 1) * LANES]
                _, vjp = jax.vjp(_gdn_qk_tile, c)
            else:
                ct = dv_ref[:, (s - 2 * H) * LANES:(s - 2 * H + 1) * LANES]
                _, vjp = jax.vjp(_silu, c)
            dcs[pl.ds(0, trr), ls] = vjp(ct)[0]
        for s in range(C3 // LANES):
            ls = slice(s * LANES, (s + 1) * LANES)
            dx = jnp.zeros((trr, LANES), F32)
            dc0 = dcs[pl.ds(0, trr), ls]
            for j in range(GDN_CONV):
                wrow = w_ref[GDN_CONV - 1 - j:GDN_CONV - j, ls]
                dx = dx + wrow * dcs[pl.ds(j, trr), ls]
                dw_ref[GDN_CONV - 1 - j:GDN_CONV - j, ls] += jnp.sum(dc0 * xs[pl.ds(SUBLANES - j, trr), ls], axis=0, keepdims=True)
            dmain_ref[:, ls] = dx.astype(BF16)
        dmain_ref[:, C3:] = dz_ref[...]
        dcs[pl.ds(trr, SUBLANES), :] = dcs[pl.ds(0, SUBLANES), :]
        _, vjp = jax.vjp(bg_fn, hs_ref[...], alog_ref[...], dtb_ref[...])
        dhs, dalog, ddtb = vjp(dbg_ref[...])
        dhs_ref[...] = dhs.astype(BF16)
        dalog_ref[...] += dalog
        ddtb_ref[...] += ddtb

    trr = min(tr, h_main.shape[0])
    main = (h_main, lambda tr_, n: pl.BlockSpec((tr_, C3), lambda i: (n - 1 - i, 0)))
    halo = (h_main, _halo_spec_rev(C3))
    return _rowcall(body, [main, halo, h_small, dq, dk, dv, dbg, dz], [conv_w8, alog_row, dtb_row],
                    [(4 * W, BF16), (LANES, BF16)], [(SUBLANES, C3), (1, LANES), (1, LANES)],
                    tr=tr, name=name, reverse=True,
                    scratch=[pltpu.VMEM((trr + SUBLANES, C3), F32), pltpu.VMEM((trr + SUBLANES, C3), F32)])


def _gdn_chunk(qs, ks, vs, betas, gs, Ss):
    C, dk = qs[0].shape
    dv = vs[0].shape[1]
    ri, ci = _iota((C, C), 0), _iota((C, C), 1)
    causal, strict = ri >= ci, ri > ci
    tril = causal.astype(F32)
    eye = (ri == ci).astype(F32)
    lane0 = (_iota((1, LANES), 1) == 0).astype(F32)
    e0 = jnp.ones((C, 1), F32) * lane0
    ones_cc = jnp.ones((C, C), F32)
    ones_kc = jnp.ones((dk, C), F32)

    def each(f, *lists):
        return [f(*a) for a in zip(*lists)]

    G = each(lambda g: g * jnp.ones((1, LANES), F32), gs)
    gcB = each(lambda x: _mm(tril, x, "nn", True), G)
    gc = each(lambda x: jnp.sum(x * lane0, -1, keepdims=True), gcB)
    gc_row = each(lambda x: _mm(e0, x, "nt", True), gcB)
    decay = each(lambda a, b: jnp.where(causal, jnp.exp(jnp.where(causal, a - b, 0.0)), 0.0), gc, gc_row)
    kb = each(lambda k, b: k * b, ks, betas)
    kk = each(lambda a, k: _mm(a, k, "nt", False), kb, ks)
    L = each(lambda a, d: jnp.where(strict, a * d, 0.0), kk, decay)
    X = each(lambda l: eye - l, L)
    P = each(lambda l: _mm(l, l, "nn", True), L)
    n_sq = max(1, (C - 1).bit_length() - 1)
    for it in range(n_sq):
        XP = each(lambda x, p_: _mm(x, p_, "nn", True), X, P)
        if it < n_sq - 1:
            P = each(lambda p_: _mm(p_, p_, "nn", True), P)
        X = each(lambda x, y: x + y, X, XP)
    egc = each(jnp.exp, gc)
    u = each(lambda x, v, b: _mm(x, v * b, "nn", True), X, vs, betas)
    w = each(lambda x, a, e: _mm(x, a * e, "nn", True), X, kb, egc)
    qsc = each(lambda q: q * (dk ** -0.5), qs)
    qk = each(lambda q, k: _mm(q, k, "nt", False), qsc, ks)
    A = each(lambda a, d: jnp.where(causal, a * d, 0.0), qk, decay)
    q_dec = each(lambda q, e: q * e, qsc, egc)
    glB = each(lambda x: _mm(ones_cc, x, "nn", True), G)
    gl = each(lambda x: jnp.sum(x * lane0, -1, keepdims=True), glB)
    k_dec = each(lambda k, a, b: k * jnp.exp(a - b), ks, gl, gc)
    glS = each(lambda g: _mm(ones_kc, g * jnp.ones((1, dv), F32), "nn", True), gs)
    wS = each(lambda a, s: _mm(a, s, "nn", False), w, Ss)
    qS = each(lambda a, s: _mm(a, s, "nn", False), q_dec, Ss)
    v_new = each(lambda a, b: a - b, u, wS)
    Av = each(lambda a, b: _mm(a, b, "nn", False), A, v_new)
    kv = each(lambda a, b: _mm(a, b, "tn", False), k_dec, v_new)
    o = each(lambda a, b: a + b, qS, Av)
    S_new = each(lambda s, e, x: s * jnp.exp(e) + x, Ss, glS, kv)
    return o, S_new


def _gdn_rule_fwd(q, k, v, bg, H, *, name):
    S_len = q.shape[0]
    C = min(GDN_CHUNK, S_len)
    N = S_len // C
    dk = dv = GDN_DK

    def body(q_ref, k_ref, v_ref, bg_ref, o_ref, st_ref, s_scr):
        n = pl.program_id(0)

        @pl.when(n == 0)
        def _():
            s_scr[...] = jnp.zeros(s_scr.shape, F32)

        bgt = bg_ref[...]
        sl = [slice(h * dk, (h + 1) * dk) for h in range(H)]
        Ss = [s_scr[h] for h in range(H)]
        for h in range(H):
            st_ref[h] = Ss[h]
        os_, S_new = _gdn_chunk([q_ref[:, s] for s in sl], [k_ref[:, s] for s in sl], [v_ref[:, s] for s in sl],
                                [bgt[:, h:h + 1] for h in range(H)], [bgt[:, H + h:H + h + 1] for h in range(H)], Ss)
        for h in range(H):
            o_ref[:, sl[h]] = os_[h]
            s_scr[h] = S_new[h]

    rows = pl.BlockSpec((C, H * dk), lambda n: (n, 0))
    return pl.pallas_call(
        body, name=name, grid=(N,),
        in_specs=[rows, rows, rows, pl.BlockSpec((C, LANES), lambda n: (n, 0))],
        out_specs=[rows, pl.BlockSpec((H, dk, dv), lambda n: (n, 0, 0))],
        out_shape=[jax.ShapeDtypeStruct((S_len, H * dv), F32), jax.ShapeDtypeStruct((N * H, dk, dv), F32)],
        scratch_shapes=[pltpu.VMEM((H, dk, dv), F32)],
        compiler_params=_params("arbitrary"),
    )(q, k, v, bg)


def _gdn_rule_bwd(q, k, v, bg, states, do, H, *, name):
    S_len = q.shape[0]
    C = min(GDN_CHUNK, S_len)
    N = S_len // C
    dk = dv = GDN_DK

    def body(q_ref, k_ref, v_ref, bg_ref, st_ref, do_ref, dq_ref, dk_ref, dv_ref, dbg_ref, ds_scr):
        step = pl.program_id(0)

        @pl.when(step == 0)
        def _():
            ds_scr[...] = jnp.zeros(ds_scr.shape, F32)

        bgt = bg_ref[...]
        lane = _iota((1, LANES), 1)
        dbg = jnp.zeros((C, LANES), F32)
        sl = [slice(h * dk, (h + 1) * dk) for h in range(H)]
        _, vjp = jax.vjp(_gdn_chunk, [q_ref[:, s] for s in sl], [k_ref[:, s] for s in sl], [v_ref[:, s] for s in sl],
                         [bgt[:, h:h + 1] for h in range(H)], [bgt[:, H + h:H + h + 1] for h in range(H)],
                         [st_ref[h] for h in range(H)])
        dq, dkk, dvv, dbeta, dg, dS = vjp(([do_ref[:, s] for s in sl], [ds_scr[h] for h in range(H)]))
        for h in range(H):
            dq_ref[:, sl[h]] = dq[h]
            dk_ref[:, sl[h]] = dkk[h]
            dv_ref[:, sl[h]] = dvv[h]
            dbg = dbg + jnp.where(lane == h, dbeta[h], 0.0) + jnp.where(lane == h + H, dg[h], 0.0)
            ds_scr[h] = dS[h]
        dbg_ref[...] = dbg

    rows = pl.BlockSpec((C, H * dk), lambda s: (N - 1 - s, 0))
    bgs = pl.BlockSpec((C, LANES), lambda s: (N - 1 - s, 0))
    return pl.pallas_call(
        body, name=name, grid=(N,),
        in_specs=[rows, rows, rows, bgs, pl.BlockSpec((H, dk, dv), lambda s: (N - 1 - s, 0, 0)), rows],
        out_specs=[rows, rows, rows, bgs],
        out_shape=[jax.ShapeDtypeStruct((S_len, H * dk), F32)] * 3 + [jax.ShapeDtypeStruct((S_len, LANES), F32)],
        scratch_shapes=[pltpu.VMEM((H, dk, dv), F32)],
        compiler_params=_params("arbitrary"),
    )(q, k, v, bg, states, do)


def _gdn_post_tile(o, z, g):
    return o * lax.rsqrt(jnp.mean(o * o, -1, keepdims=True) + RMS_EPS) * g * _silu(z)


def _gdn_post_fwd(o, h_main, norm_g, H, *, tr, name):
    W = H * GDN_DK

    def body(rows, consts, orows, oaccs, scr, step, blk):
        for h in range(H):
            ls = slice(h * LANES, (h + 1) * LANES)
            orows[0][:, ls] = _gdn_post_tile(rows[0][:, ls], rows[1][:, ls], consts[0][...]).astype(BF16)

    z = (h_main, lambda tr_, n: pl.BlockSpec((tr_, W), lambda i: (i, 3)))
    return _rowcall(body, [o, z], [_row(norm_g)], [(W, BF16)], [], tr=tr, name=name)[0]


def _gdn_post_bwd(o, h_main, norm_g, d_on, H, *, tr, name):
    W = H * GDN_DK

    def body(rows, consts, orows, oaccs, scr, step, blk):
        for h in range(H):
            ls = slice(h * LANES, (h + 1) * LANES)
            _, vjp = jax.vjp(_gdn_post_tile, rows[0][:, ls], rows[1][:, ls], consts[0][...])
            d_o, d_z, d_g = vjp(rows[2][:, ls])
            orows[0][:, ls] = d_o
            orows[1][:, ls] = d_z.astype(BF16)
            oaccs[0][...] += d_g

    z = (h_main, lambda tr_, n: pl.BlockSpec((tr_, W), lambda i: (i, 3)))
    return _rowcall(body, [o, z, d_on], [_row(norm_g)], [(W, F32), (W, BF16)], [(1, LANES)], tr=tr, name=name)


def _seg_ones():
    ri, ci = _iota((LANES, LANES), 0), _iota((LANES, LANES), 1)
    return ((ri < FOX_DH) == (ci < FOX_DH)).astype(F32)


def _fox_qk_tile(x, g2):
    ms = _mm(x * x, _seg_ones(), "nn", True) * (1.0 / FOX_DH)
    return x * lax.rsqrt(ms + RMS_EPS) * g2


def _make_lf_fn(Hf):
    def fn(hs, bf):
        lane = _iota((1, LANES), 1)
        return jnp.where(lane < Hf, -_softplus(-(hs + bf)), 0.0)
    return fn


def _fox_pre_fwd(h_main, h_small, gq2, gk2, bf_row, Hf, *, tr, name):
    W = Hf * FOX_DH
    lf_fn = _make_lf_fn(Hf)

    def body(rows, consts, orows, oaccs, scr, step, blk):
        qk_ref, v_ref, hs_ref = rows
        gq_ref, gk_ref, bf_ref = consts
        qn_ref, kn_ref, vb_ref, c_ref, cb_ref = orows
        carry = scr[0]
        trr = qk_ref.shape[0]

        @pl.when(step == 0)
        def _():
            carry[...] = jnp.zeros(carry.shape, F32)

        for s in range(W // LANES):
            ls = slice(s * LANES, (s + 1) * LANES)
            qn_ref[:, ls] = _fox_qk_tile(qk_ref[:, ls], gq_ref[...]).astype(BF16)
            kn_ref[:, ls] = _fox_qk_tile(qk_ref[:, W + s * LANES:W + (s + 1) * LANES], gk_ref[...]).astype(BF16)
        vb_ref[...] = v_ref[...].astype(BF16)
        lf = lf_fn(hs_ref[...], bf_ref[...])
        tril = (_iota((trr, trr), 0) >= _iota((trr, trr), 1)).astype(F32)
        c = _raw_mm(tril, lf, "nn", True) + carry[0:1, :]
        c_ref[...] = c
        carry[0:1, :] = c[trr - 1:trr, :]
        col = _iota((LANES, 2 * W), 1)
        parity = (col >= W).astype(jnp.int32)
        slab = jnp.right_shift(col - parity * W, 7)
        expand = (2 * slab + parity == _iota((LANES, 2 * W), 0)).astype(F32)
        cb_ref[...] = _raw_mm(c, expand, "nn", True)

    qk = (h_main, lambda tr_, n: pl.BlockSpec((tr_, 2 * W), lambda i: (i, 0)))
    vv = (h_main, lambda tr_, n: pl.BlockSpec((tr_, W), lambda i: (i, 2)))
    return _rowcall(body, [qk, vv, h_small], [gq2, gk2, bf_row],
                    [(W, BF16), (W, BF16), (W, BF16), (LANES, F32), (2 * W, F32)], [], tr=tr, name=name,
                    scratch=[pltpu.VMEM((SUBLANES, LANES), F32)])


def _fox_pre_bwd(h_main, h_small, gq2, gk2, bf_row, dqn, dkn, dvv, dz, dc, Hf, *, tr, name):
    W = Hf * FOX_DH
    lf_fn = _make_lf_fn(Hf)

    def body(rows, consts, orows, oaccs, scr, step, blk):
        qk_ref, hs_ref, dqn_ref, dkn_ref, dvv_ref, dz_ref, dc_ref = rows
        gq_ref, gk_ref, bf_ref = consts
        dmain_ref, dhs_ref = orows
        dgq_ref, dgk_ref, dbf_ref = oaccs
        carry = scr[0]
        trr = qk_ref.shape[0]

        @pl.when(step == 0)
        def _():
            carry[...] = jnp.zeros(carry.shape, F32)

        for s in range(W // LANES):
            ls = slice(s * LANES, (s + 1) * LANES)
            lk = slice(W + s * LANES, W + (s + 1) * LANES)
            _, vjp = jax.vjp(_fox_qk_tile, qk_ref[:, ls], gq_ref[...])
            dx, dg = vjp(dqn_ref[:, ls])
            dmain_ref[:, ls] = dx.astype(BF16)
            dgq_ref[...] += dg
            _, vjp = jax.vjp(_fox_qk_tile, qk_ref[:, lk], gk_ref[...])
            dx, dg = vjp(dkn_ref[:, ls])
            dmain_ref[:, lk] = dx.astype(BF16)
            dgk_ref[...] += dg
        dmain_ref[:, 2 * W:3 * W] = dvv_ref[...].astype(BF16)
        dmain_ref[:, 3 * W:] = dz_ref[...]
        dcv = dc_ref[...]
        triu = (_iota((trr, trr), 0) <= _iota((trr, trr), 1)).astype(F32)
        dlf = _raw_mm(triu, dcv, "nn", True) + carry[0:1, :]
        carry[0:1, :] = dlf[0:1, :]
        _, vjp = jax.vjp(lf_fn, hs_ref[...], bf_ref[...])
        dhs, dbf = vjp(dlf)
        dhs_ref[...] = dhs.astype(BF16)
        dbf_ref[...] += dbf

    qk = (h_main, lambda tr_, n: pl.BlockSpec((tr_, 2 * W), lambda i: (n - 1 - i, 0)))
    return _rowcall(body, [qk, h_small, dqn, dkn, dvv, dz, dc], [gq2, gk2, bf_row],
                    [(4 * W, BF16), (LANES, BF16)], [(1, LANES), (1, LANES), (1, LANES)],
                    tr=tr, name=name, reverse=True, scratch=[pltpu.VMEM((SUBLANES, LANES), F32)])


def _fox_attn_fwd(qn, kn, vb, c_b, c_rowp, *, tb, name):
    S_len, W = qn.shape
    HP = W // LANES
    tb = min(tb, S_len)
    nb = S_len // tb
    scale = FOX_DH ** -0.5
    rb, cb = min(ATTN_ROWS, tb), min(ATTN_COLS, tb)
    nblk = 2 * (tb // rb)

    steps = [(i, j) for i in range(nb) for j in range(i + 1)]
    ti = jnp.asarray([s[0] for s in steps], jnp.int32)
    tj = jnp.asarray([s[1] for s in steps], jnp.int32)

    def body(ti_ref, tj_ref, q_ref, k_ref, v_ref, cb0_ref, cb1_ref, cr_ref, o_ref, lse_ref,
             m_scr, l_scr, acc_scr, s_scr, p_scr, a_scr):
        t = pl.program_id(1)
        i, j = ti_ref[t], tj_ref[t]

        @pl.when(j == 0)
        def _():
            m_scr[...] = jnp.full(m_scr.shape, -jnp.inf, F32)
            l_scr[...] = jnp.zeros(l_scr.shape, F32)
            acc_scr[...] = jnp.zeros(acc_scr.shape, F32)

        def compute(diag):
            lo = _iota((1, LANES), 1) < FOX_DH
            v = v_ref[...]
            ones = jnp.ones((tb, LANES), BF16)
            blocks = [(hh, r) for hh in range(2) for r in range(tb // rb)]
            masks = [lo, jnp.logical_not(lo)]

            def visible(r):
                return ((r + 1) * rb - 1) // LANES + 1 if diag else tb // LANES

            for b, (hh, r) in enumerate(blocks):
                rows = slice(r * rb, (r + 1) * rb)
                qr = q_ref[rows, :]
                qh = jnp.where(masks[hh], qr * scale, jnp.zeros_like(qr))
                ctb = (cb0_ref if hh == 0 else cb1_ref)[rows, :]
                mx = None
                for c in range(tb // cb):
                    if c * cb // LANES >= visible(r):
                        continue
                    s2 = lax.dot_general(qh, k_ref[c * cb:(c + 1) * cb, :], _DIMS["nt"], preferred_element_type=F32)
                    for piece in range(c * cb // LANES, min((c + 1) * cb // LANES, visible(r))):
                        cols = slice(piece * LANES, (piece + 1) * LANES)
                        s = s2[:, piece * LANES - c * cb:(piece + 1) * LANES - c * cb] + ctb - cr_ref[hh:hh + 1, cols]
                        if diag and (piece + 1) * LANES - 1 > r * rb:
                            keep = piece * LANES + _iota((rb, LANES), 1) <= r * rb + _iota((rb, LANES), 0)
                            s = jnp.where(keep, s, -jnp.inf)
                        s_scr[b, :, cols] = s
                        mx = s if mx is None else jnp.maximum(mx, s)
                m_prev = m_scr[hh, rows, :]
                m_new = jnp.maximum(m_prev, jnp.broadcast_to(jnp.max(mx, -1, keepdims=True), (rb, LANES)))
                a_scr[b] = jnp.exp(m_prev - m_new)
                m_scr[hh, rows, :] = m_new
            for b, (hh, r) in enumerate(blocks):
                m_new = m_scr[hh, r * rb:(r + 1) * rb, :]
                for piece in range(visible(r)):
                    cols = slice(piece * LANES, (piece + 1) * LANES)
                    p_scr[b, :, cols] = jnp.exp(s_scr[b, :, cols] - m_new).astype(BF16)
            for b, (hh, r) in enumerate(blocks):
                rows = slice(r * rb, (r + 1) * rb)
                nkv = visible(r) * LANES
                pb = p_scr[b, :, :nkv]
                vh = jnp.where(masks[hh], v[:nkv], jnp.zeros_like(v[:nkv]))
                psum = lax.dot_general(pb, ones[:nkv], _DIMS["nn"], preferred_element_type=F32)
                pv = lax.dot_general(pb, vh, _DIMS["nn"], preferred_element_type=F32)
                alpha = a_scr[b]
                l_scr[hh, rows, :] = alpha * l_scr[hh, rows, :] + psum
                acc = acc_scr[rows, :]
                acc_scr[rows, :] = jnp.where(masks[hh], acc * alpha + pv, acc)

        @pl.when(j < i)
        def _():
            compute(False)

        @pl.when(j == i)
        def _():
            compute(True)
            lo = _iota((1, LANES), 1) < FOX_DH
            o_ref[...] = acc_scr[...] / jnp.where(lo, l_scr[0], l_scr[1])
            lse_ref[...] = m_scr[...] + jnp.log(l_scr[...])

    qs = pl.BlockSpec((tb, LANES), lambda h, t, ti_, tj_: (ti_[t], h))
    qs1 = pl.BlockSpec((tb, LANES), lambda h, t, ti_, tj_: (ti_[t], HP + h))
    ks = pl.BlockSpec((tb, LANES), lambda h, t, ti_, tj_: (tj_[t], h))
    crs = pl.BlockSpec((None, SUBLANES, tb), lambda h, t, ti_, tj_: (h, 0, tj_[t]))
    return pl.pallas_call(
        body, name=name,
        grid_spec=pltpu.PrefetchScalarGridSpec(
            num_scalar_prefetch=2, grid=(HP, len(steps)),
            in_specs=[qs, ks, ks, qs, qs1, crs],
            out_specs=[qs, pl.BlockSpec((2, tb, LANES), lambda h, t, ti_, tj_: (0, ti_[t], h))],
            scratch_shapes=[pltpu.VMEM((2, tb, LANES), F32), pltpu.VMEM((2, tb, LANES), F32),
                            pltpu.VMEM((tb, LANES), F32), pltpu.VMEM((nblk, rb, tb), F32),
                            pltpu.VMEM((nblk, rb, tb), BF16), pltpu.VMEM((nblk, rb, LANES), F32)]),
        out_shape=[jax.ShapeDtypeStruct((S_len, W), F32), jax.ShapeDtypeStruct((2, S_len, W), F32)],
        compiler_params=_params("parallel", "arbitrary"),
    )(ti, tj, qn, kn, vb, c_b, c_b, c_rowp)


def _fox_attn_bwd(qn, kn, vb, c_b, c_rowp, lse_b, delta_b, do, *, tb, name):
    S_len, W = qn.shape
    HP = W // LANES
    tb = min(tb, S_len)
    nb = S_len // tb
    scale = FOX_DH ** -0.5
    rb, cb = min(ATTN_ROWS, tb), min(ATTN_COLS, tb)

    steps = [(j, i) for j in range(nb) for i in range(j, nb)]
    tj = jnp.asarray([s[0] for s in steps], jnp.int32)
    ti = jnp.asarray([s[1] for s in steps], jnp.int32)

    def body(tj_ref, ti_ref, q_ref, k_ref, v_ref, cb0_ref, cb1_ref, cr_ref, lse_ref, dl0_ref, dl1_ref, do_ref,
             dq_ref, dk_ref, dv_ref, dcr_ref, dct_ref, dk_scr, dv_scr, dc_scr, p_scr, ds_scr):
        t = pl.program_id(1)
        j, i = tj_ref[t], ti_ref[t]

        @pl.when(t == 0)
        def _():
            dq_ref[...] = jnp.zeros(dq_ref.shape, F32)
            dct_ref[...] = jnp.zeros(dct_ref.shape, F32)

        @pl.when(i == j)
        def _():
            dk_scr[...] = jnp.zeros(dk_scr.shape, F32)
            dv_scr[...] = jnp.zeros(dv_scr.shape, F32)
            dc_scr[...] = jnp.zeros(dc_scr.shape, F32)

        def compute(diag):
            lo = _iota((1, LANES), 1) < FOX_DH
            masks = [lo, jnp.logical_not(lo)]
            row0 = pl.multiple_of(i * tb, tb)
            npiece = tb // LANES
            colsum = [[None] * npiece for _ in range(2)]

            def visible(r):
                return ((r + 1) * rb - 1) // LANES + 1 if diag else npiece

            for hh in range(2):
                for r in range(tb // rb):
                    rows = slice(r * rb, (r + 1) * rb)
                    qr = q_ref[rows, :]
                    qh = jnp.where(masks[hh], qr * scale, jnp.zeros_like(qr))
                    doh = jnp.where(masks[hh], do_ref[rows, :], 0.0).astype(BF16)
                    bq = (cb0_ref if hh == 0 else cb1_ref)[rows, :] - lse_ref[hh, rows, :]
                    dlt = (dl0_ref if hh == 0 else dl1_ref)[rows, :]
                    rsum = None
                    for c in range(tb // cb):
                        first, last = c * cb // LANES, min((c + 1) * cb // LANES, visible(r))
                        for piece in range(max(first, last), (c + 1) * cb // LANES):
                            cols = slice(piece * LANES, (piece + 1) * LANES)
                            p_scr[hh, rows, cols] = jnp.zeros((rb, LANES), BF16)
                            ds_scr[hh, rows, cols] = jnp.zeros((rb, LANES), BF16)
                        if first >= last:
                            continue
                        s2 = lax.dot_general(qh, k_ref[c * cb:(c + 1) * cb, :], _DIMS["nt"], preferred_element_type=F32)
                        dp2 = lax.dot_general(doh, v_ref[c * cb:(c + 1) * cb, :], _DIMS["nt"], preferred_element_type=F32)
                        for piece in range(first, last):
                            cols = slice(piece * LANES, (piece + 1) * LANES)
                            sub = slice(piece * LANES - c * cb, (piece + 1) * LANES - c * cb)
                            s = s2[:, sub] + bq - cr_ref[hh:hh + 1, cols]
                            if diag and (piece + 1) * LANES - 1 > r * rb:
                                keep = piece * LANES + _iota((rb, LANES), 1) <= r * rb + _iota((rb, LANES), 0)
                                s = jnp.where(keep, s, -jnp.inf)
                            p = jnp.exp(s)
                            ds = p * (dp2[:, sub] - dlt)
                            p_scr[hh, rows, cols] = p.astype(BF16)
                            ds_scr[hh, rows, cols] = ds.astype(BF16)
                            rsum = ds if rsum is None else rsum + ds
                            csum = jnp.sum(ds, axis=0, keepdims=True)
                            colsum[hh][piece] = csum if colsum[hh][piece] is None else colsum[hh][piece] + csum
                    grow = pl.ds(row0 + r * rb, rb)
                    dct_ref[grow, :] += jnp.where(masks[hh], jnp.sum(rsum, -1, keepdims=True), 0.0)
            k = k_ref[...]
            qf = q_ref[...]
            dof = do_ref[...]
            dq_part = jnp.zeros((tb, LANES), F32)
            for hh in range(2):
                kh = jnp.where(masks[hh], k, jnp.zeros_like(k))
                qhf = jnp.where(masks[hh], qf * scale, jnp.zeros_like(qf))
                dohf = jnp.where(masks[hh], dof, 0.0).astype(BF16)
                dv_scr[...] += lax.dot_general(p_scr[hh], dohf, _DIMS["tn"], preferred_element_type=F32)
                dk_scr[...] += lax.dot_general(ds_scr[hh], qhf, _DIMS["tn"], preferred_element_type=F32)
                dq_part = dq_part + lax.dot_general(ds_scr[hh], kh, _DIMS["nn"], preferred_element_type=F32)
                for piece in range(npiece):
                    if colsum[hh][piece] is not None:
                        dc_scr[hh:hh + 1, piece * LANES:(piece + 1) * LANES] -= colsum[hh][piece]
            dq_ref[pl.ds(row0, tb), :] += dq_part * scale

        @pl.when(i > j)
        def _():
            compute(False)

        @pl.when(i == j)
        def _():
            compute(True)

        @pl.when(i == nb - 1)
        def _():
            dk_ref[...] = dk_scr[...]
            dv_ref[...] = dv_scr[...]
            dcr_ref[...] = dc_scr[...]

    qs = pl.BlockSpec((tb, LANES), lambda h, t, tj_, ti_: (ti_[t], h))
    qs1 = pl.BlockSpec((tb, LANES), lambda h, t, tj_, ti_: (ti_[t], HP + h))
    ks = pl.BlockSpec((tb, LANES), lambda h, t, tj_, ti_: (tj_[t], h))
    crs = pl.BlockSpec((None, SUBLANES, tb), lambda h, t, tj_, ti_: (h, 0, tj_[t]))
    whole = pl.BlockSpec((S_len, LANES), lambda h, t, tj_, ti_: (0, h))
    return pl.pallas_call(
        body, name=name,
        grid_spec=pltpu.PrefetchScalarGridSpec(
            num_scalar_prefetch=2, grid=(HP, len(steps)),
            in_specs=[qs, ks, ks, qs, qs1, crs, pl.BlockSpec((2, tb, LANES), lambda h, t, tj_, ti_: (0, ti_[t], h)),
                      qs, qs1, qs],
            out_specs=[whole, ks, ks, crs, whole],
            scratch_shapes=[pltpu.VMEM((tb, LANES), F32), pltpu.VMEM((tb, LANES), F32),
                            pltpu.VMEM((SUBLANES, tb), F32), pltpu.VMEM((2, tb, tb), BF16),
                            pltpu.VMEM((2, tb, tb), BF16)]),
        out_shape=[jax.ShapeDtypeStruct((S_len, W), F32)] * 3 + [jax.ShapeDtypeStruct((HP, SUBLANES, S_len), F32),
                                                                 jax.ShapeDtypeStruct((S_len, W), F32)],
        compiler_params=_params("parallel", "arbitrary"),
    )(tj, ti, qn, kn, vb, c_b, c_b, c_rowp, lse_b, delta_b, delta_b, do)


def _fox_post_tile(o, z):
    return o * _silu(z)


def _fox_post_fwd(o, h_main, *, tr, name):
    W = o.shape[1]

    def body(rows, consts, orows, oaccs, scr, step, blk):
        orows[0][...] = _fox_post_tile(rows[0][...], rows[1][...]).astype(BF16)

    z = (h_main, lambda tr_, n: pl.BlockSpec((tr_, W), lambda i: (i, 3)))
    return _rowcall(body, [o, z], [], [(W, BF16)], [], tr=tr, name=name)[0]


def _fox_post_bwd(o, h_main, d_og, *, tr, name):
    W = o.shape[1]

    def body(rows, consts, orows, oaccs, scr, step, blk):
        _, vjp = jax.vjp(_fox_post_tile, rows[0][...], rows[1][...])
        d_o, d_z = vjp(rows[2][...])
        orows[0][...] = d_o
        orows[1][...] = d_z.astype(BF16)
        lo_rows = (_iota((LANES, LANES), 0) < FOX_DH)
        for s in range(W // LANES):
            ls = slice(s * LANES, (s + 1) * LANES)
            prod = d_o[:, ls] * rows[0][:, ls]
            orows[2][:, ls] = _raw_mm(prod, lo_rows.astype(F32), "nn", True)
            orows[2][:, W + s * LANES:W + (s + 1) * LANES] = _raw_mm(prod, jnp.logical_not(lo_rows).astype(F32), "nn", True)

    z = (h_main, lambda tr_, n: pl.BlockSpec((tr_, W), lambda i: (i, 3)))
    return _rowcall(body, [o, z, d_og], [], [(W, F32), (W, BF16), (2 * W, F32)], [], tr=tr, name=name)


MESH_IDS = pl.DeviceIdType.MESH
N_CHIPS = 4
N_DEV = 8
_ANY = pl.BlockSpec(memory_space=pl.ANY)


def _xy_exchange(src, *, gather, name):
    shape = (N_CHIPS,) + tuple(src.shape[-2:])

    def body(src_ref, out_ref, send_sems, recv_sems, local_sem):
        x, y, c = lax.axis_index("x"), lax.axis_index("y"), lax.axis_index("c")
        me = 2 * x + y
        peers = [(1 - x, y), (x, 1 - y), (1 - x, 1 - y)]

        def outgoing(px, py):
            return src_ref if gather else src_ref.at[2 * px + py]

        mine = pltpu.make_async_copy(outgoing(x, y), out_ref.at[me], local_sem)
        mine.start()
        copies = []
        for j, (px, py) in enumerate(peers):
            cp = pltpu.make_async_remote_copy(
                src_ref=outgoing(px, py), dst_ref=out_ref.at[me],
                send_sem=send_sems.at[j], recv_sem=recv_sems.at[j],
                device_id=(px, py, c), device_id_type=MESH_IDS)
            cp.start()
            copies.append(cp)
        for j, (px, py) in enumerate(peers):
            pltpu.make_async_remote_copy(
                src_ref=outgoing(px, py), dst_ref=out_ref.at[2 * px + py],
                send_sem=send_sems.at[j], recv_sem=recv_sems.at[j],
                device_id=(px, py, c), device_id_type=MESH_IDS).wait_recv()
        for cp in copies:
            cp.wait_send()
        mine.wait()

    return pl.pallas_call(
        body, name=name, in_specs=[_ANY], out_specs=_ANY,
        out_shape=jax.ShapeDtypeStruct(shape, src.dtype),
        scratch_shapes=[pltpu.SemaphoreType.DMA((3,)), pltpu.SemaphoreType.DMA((3,)), pltpu.SemaphoreType.DMA],
    )(src)


def _c_swap(src, *, name):
    def body(src_ref, out_ref, send_sem, recv_sem):
        x, y, c = lax.axis_index("x"), lax.axis_index("y"), lax.axis_index("c")
        cp = pltpu.make_async_remote_copy(
            src_ref=src_ref, dst_ref=out_ref, send_sem=send_sem, recv_sem=recv_sem,
            device_id=(x, y, 1 - c), device_id_type=MESH_IDS)
        cp.start()
        cp.wait()

    return pl.pallas_call(
        body, name=name, in_specs=[_ANY], out_specs=_ANY,
        out_shape=jax.ShapeDtypeStruct(src.shape, src.dtype),
        scratch_shapes=[pltpu.SemaphoreType.DMA, pltpu.SemaphoreType.DMA],
    )(src)


def _all_gather8(blk, *, name):
    m_per, n = blk.shape

    def body(x_ref, out_ref, send_sems, recv_sems, local_sem):
        x, y, c = lax.axis_index("x"), lax.axis_index("y"), lax.axis_index("c")
        me, sibling = (x, y, c), (x, y, 1 - c)
        chips = [(1 - x, y), (x, 1 - y), (1 - x, 1 - y)]

        def rows(px, py, pc):
            return out_ref.at[pl.ds((4 * px + 2 * py + pc) * m_per, m_per), :]

        def copy(k, block, to, src=None):
            return pltpu.make_async_remote_copy(
                src_ref=rows(*block) if src is None else src, dst_ref=rows(*block),
                send_sem=send_sems.at[k], recv_sem=recv_sems.at[k], device_id=to, device_id_type=MESH_IDS)

        mine = pltpu.make_async_copy(x_ref, rows(*me), local_sem)
        mine.start()
        first = [copy(0, me, sibling, src=x_ref)]
        first += [copy(1 + j, me, (*chip, c), src=x_ref) for j, chip in enumerate(chips)]
        for cp in first:
            cp.start()
        passed = [copy(4 + j, (*chip, c), sibling) for j, chip in enumerate(chips)]
        for j, chip in enumerate(chips):
            copy(1 + j, (*chip, c), me).wait_recv()
            passed[j].start()
        copy(0, sibling, me).wait_recv()
        for j, chip in enumerate(chips):
            copy(4 + j, (*chip, 1 - c), me).wait_recv()
        for cp in first + passed:
            cp.wait_send()
        mine.wait()

    return pl.pallas_call(
        body, name=name,
        out_shape=jax.ShapeDtypeStruct((N_DEV * m_per, n), blk.dtype),
        in_specs=[pl.BlockSpec(memory_space=pltpu.VMEM)], out_specs=pl.BlockSpec(memory_space=pltpu.VMEM),
        scratch_shapes=[pltpu.SemaphoreType.DMA((7,)), pltpu.SemaphoreType.DMA((7,)), pltpu.SemaphoreType.DMA],
    )(blk)


def _sum_slots(parts, *, tr, name):
    n, R, _ = parts.shape
    pack = 2 * SUBLANES
    tr = max(t for t in range(pack, min(tr, R) + 1, pack) if R % t == 0) if R % pack == 0 else R

    def body(p_ref, o_ref):
        tot = p_ref[0].astype(F32)
        for s in range(1, n):
            tot = tot + p_ref[s].astype(F32)
        o_ref[...] = tot

    return pl.pallas_call(
        body, name=name, grid=(R // tr,),
        in_specs=[pl.BlockSpec((n, tr, LANES), lambda i: (0, i, 0))], out_specs=pl.BlockSpec((tr, LANES), lambda i: (i, 0)),
        out_shape=jax.ShapeDtypeStruct((R, LANES), F32), compiler_params=_params("parallel"),
    )(parts)


def _adamw(w, g_parts, m, v, *, name):
    shape = w.shape
    as2d = lambda a: a.reshape(-1, shape[-1])
    w2, m2, v2 = as2d(w), as2d(m), as2d(v)
    gs = [as2d(g) for g in g_parts]
    R, C = w2.shape
    tr = R
    while tr * C * 4 > (1 << 20) and tr % 2 == 0 and (tr // 2) % SUBLANES == 0:
        tr //= 2
    ng = len(gs)

    def body(*refs):
        w_ref, m_ref, v_ref = refs[:3]
        g_refs = refs[3:3 + ng]
        go_ref, d_ref, mo_ref, vo_ref = refs[3 + ng:]
        g = g_refs[0][...]
        for r in g_refs[1:]:
            g = g + r[...]
        mn = ADAM_B1 * m_ref[...] + (1.0 - ADAM_B1) * g
        vn = ADAM_B2 * v_ref[...] + (1.0 - ADAM_B2) * jnp.square(g)
        m_hat = mn / (1.0 - ADAM_B1 ** ADAM_STEP)
        v_hat = vn / (1.0 - ADAM_B2 ** ADAM_STEP)
        go_ref[...] = g
        d_ref[...] = -ADAM_LR * (m_hat / (jnp.sqrt(v_hat) + ADAM_EPS) + ADAM_WD * w_ref[...])
        mo_ref[...] = mn
        vo_ref[...] = vn

    spec = pl.BlockSpec((tr, C), lambda i: (i, 0))
    outs = pl.pallas_call(
        body, name=name, grid=(R // tr,), in_specs=[spec] * (3 + ng), out_specs=[spec] * 4,
        out_shape=[jax.ShapeDtypeStruct((R, C), F32)] * 4, compiler_params=_params("parallel"),
    )(w2, m2, v2, *gs)
    return tuple(o.reshape(shape) for o in outs)


TR = 256
ATTN_TILE = 512
ATTN_ROWS = 128
ATTN_COLS = 256


def _mm_nn(a, b, name, **kw):
    return _matmul(a, b, tm=512, tn=1024, tk=1024, name=name, **kw)


def _mm_nt(a, b, name, **kw):
    return _matmul(a, b, tb=True, tm=512, tn=1024, tk=1024, name=name, **kw)


def _mm_tn(a, b, name, **kw):
    return _matmul(a, b, ta=True, tm=1024, tn=1024, tk=512, name=name, **kw)


def _c_rows(c, Hf):
    S_len = c.shape[0]
    ct = c[:, :Hf].T.reshape(Hf // 2, 2, S_len)
    return jnp.pad(ct, ((0, 0), (0, SUBLANES - 2), (0, 0)))


def _local_step(x, p, target, wts):
    L = wts["ln_g"].shape[0]
    alpha = (2 * L) ** 0.25
    Hg = wts["gdn_a_log"].shape[1]
    Hf = wts["fox_b_f"].shape[1]
    Wg_ = Hg * GDN_DK
    Wf_ = Hf * FOX_DH
    saved = []
    for i in range(L):
        j = i // 2
        sv = {"x": x}
        if i % 2 == 0:
            w_in = wts["gdn_w_in"][j]
            wm, ws = w_in[:, :4 * Wg_], _pad_lanes(w_in[:, 4 * Wg_:])
            cw8 = jnp.pad(wts["gdn_conv_w"][j], ((0, SUBLANES - GDN_CONV), (0, 0)))
            alog = _pad_lanes(_row(wts["gdn_a_log"][j]), offset=Hg)
            dtb = _pad_lanes(_row(wts["gdn_dt_bias"][j]), offset=Hg)
            hm = _mm_nn(x, wm, f"gdn{j}_in_main")
            hs = _mm_nn(x, ws, f"gdn{j}_in_small")
            q, k, v, bg = _gdn_pre_fwd(hm, hs, cw8, alog, dtb, Hg, tr=TR, name=f"gdn{j}_pre")
            o, states = _gdn_rule_fwd(q, k, v, bg, Hg, name=f"gdn{j}_rule")
            on = _gdn_post_fwd(o, hm, wts["gdn_norm_g"][j], Hg, tr=TR, name=f"gdn{j}_post")
            y = _mm_nn(on, wts["gdn_w_out"][j], f"gdn{j}_out")
            sv.update(wm=wm, ws=ws, cw8=cw8, alog=alog, dtb=dtb, hm=hm, hs=hs, q=q, k=k, v=v, bg=bg, o=o,
                      states=states, on=on)
        else:
            w_in = wts["fox_w_in"][j]
            wm, ws = w_in[:, :4 * Wf_], _pad_lanes(w_in[:, 4 * Wf_:])
            gq2 = _row(jnp.tile(wts["fox_q_norm_g"][j], 2))
            gk2 = _row(jnp.tile(wts["fox_k_norm_g"][j], 2))
            bf = _pad_lanes(_row(wts["fox_b_f"][j]))
            hm = _mm_nn(x, wm, f"fox{j}_in_main")
            hs = _mm_nn(x, ws, f"fox{j}_in_small")
            qn, kn, vb, c, c_b = _fox_pre_fwd(hm, hs, gq2, gk2, bf, Hf, tr=TR, name=f"fox{j}_pre")
            c_rowp = _c_rows(c, Hf)
            o, lse_b = _fox_attn_fwd(qn, kn, vb, c_b, c_rowp, tb=ATTN_TILE, name=f"fox{j}_attn")
            on = _fox_post_fwd(o, hm, tr=TR, name=f"fox{j}_post")
            y = _mm_nn(on, wts["fox_w_out"][j], f"fox{j}_out")
            sv.update(wm=wm, ws=ws, gq2=gq2, gk2=gk2, bf=bf, hm=hm, hs=hs, qn=qn, kn=kn, vb=vb, c_b=c_b,
                      c_rowp=c_rowp, o=o, lse_b=lse_b, on=on)
        x_ln = _ln_fwd(x, y, wts["ln_g"][i], wts["ln_b"][i], alpha, tr=TR, name=f"ln{i}")
        gp = _mm_nn(x_ln, wts["ple_w_gate"][i], f"ple{i}_gate")
        pp = _mm_nn(p[i], wts["ple_w_proj"][i], f"ple{i}_proj")
        x_out = _ple_fwd(x_ln, gp, pp, tr=TR, name=f"ple{i}_mix")
        sv.update(y=y, x_ln=x_ln, gp=gp, pp=pp)
        saved.append(sv)
        x = x_out

    loss_row, dx = _loss_fwd_bwd(x, target, tr=TR, name="loss")

    g = {n: [None] * wts[n].shape[0] for n in wts}
    for i in reversed(range(L)):
        j = i // 2
        sv = saved[i]
        d_pre, d_pp = _ple_bwd(dx, sv["gp"], sv["pp"], tr=TR, name=f"ple{i}_mix_bwd")
        g["ple_w_gate"][i] = _mm_tn(sv["x_ln"], d_pre, f"ple{i}_gate_dw")
        g["ple_w_proj"][i] = _mm_tn(p[i], d_pp, f"ple{i}_proj_dw")
        t = _mm_nt(d_pre, wts["ple_w_gate"][i], f"ple{i}_gate_dx")
        du, g["ln_g"][i], g["ln_b"][i] = _ln_bwd(sv["x"], sv["y"], wts["ln_g"][i], wts["ln_b"][i], dx, t, alpha,
                                                 tr=TR, name=f"ln{i}_bwd")
        if i % 2 == 0:
            g["gdn_w_out"][j] = _mm_tn(sv["on"], du, f"gdn{j}_out_dw")
            d_on = _mm_nt(du, wts["gdn_w_out"][j], f"gdn{j}_out_dx")
            d_o, d_z, d_ng = _gdn_post_bwd(sv["o"], sv["hm"], wts["gdn_norm_g"][j], d_on, Hg, tr=TR, name=f"gdn{j}_post_bwd")
            dq, dk, dv, dbg = _gdn_rule_bwd(sv["q"], sv["k"], sv["v"], sv["bg"], sv["states"], d_o, Hg, name=f"gdn{j}_rule_bwd")
            d_hm, d_hs, d_cw, d_al, d_dtb = _gdn_pre_bwd(sv["hm"], sv["hs"], sv["cw8"], sv["alog"], sv["dtb"],
                                                         dq, dk, dv, dbg, d_z, Hg, tr=TR, name=f"gdn{j}_pre_bwd")
            g["gdn_norm_g"][j] = d_ng[0]
            g["gdn_conv_w"][j] = d_cw[:GDN_CONV]
            g["gdn_a_log"][j] = d_al[0, Hg:2 * Hg]
            g["gdn_dt_bias"][j] = d_dtb[0, Hg:2 * Hg]
            wname, nsmall, Wd = "gdn_w_in", 2 * Hg, Wg_
        else:
            g["fox_w_out"][j] = _mm_tn(sv["on"], du, f"fox{j}_out_dw")
            d_og = _mm_nt(du, wts["fox_w_out"][j], f"fox{j}_out_dx")
            d_o, d_z, delta_b = _fox_post_bwd(sv["o"], sv["hm"], d_og, tr=TR, name=f"fox{j}_post_bwd")
            dqn, dkn, dvv, dcr, dct = _fox_attn_bwd(sv["qn"], sv["kn"], sv["vb"], sv["c_b"], sv["c_rowp"], sv["lse_b"],
                                                    delta_b, d_o, tb=ATTN_TILE, name=f"fox{j}_attn_bwd")
            dc = _pad_lanes(dcr[:, :2, :].reshape(Hf, -1).T + dct[:, ::FOX_DH])
            d_hm, d_hs, d_gq, d_gk, d_bf = _fox_pre_bwd(sv["hm"], sv["hs"], sv["gq2"], sv["gk2"], sv["bf"],
                                                        dqn, dkn, dvv, d_z, dc, Hf, tr=TR, name=f"fox{j}_pre_bwd")
            g["fox_q_norm_g"][j] = d_gq[0, :FOX_DH] + d_gq[0, FOX_DH:]
            g["fox_k_norm_g"][j] = d_gk[0, :FOX_DH] + d_gk[0, FOX_DH:]
            g["fox_b_f"][j] = d_bf[0, :Hf]
            wname, nsmall, Wd = "fox_w_in", Hf, Wf_
        dwm = _mm_tn(sv["x"], d_hm, f"{wname}{j}_main_dw")
        dws = _mm_tn(sv["x"], d_hs, f"{wname}{j}_small_dw")
        g[wname][j] = jnp.concatenate([dwm, dws[:, :nsmall]], axis=1)
        t1 = _mm_nt(d_hs, sv["ws"], f"{wname}{j}_small_dx", add=du, add_scale=alpha)
        dx = _mm_nt(d_hm, sv["wm"], f"{wname}{j}_main_dx", add=t1)
        del Wd
    grads = {n: jnp.stack(v) for n, v in g.items()}
    return loss_row, dx, grads


_SHARDED = (("ple_w_gate", 1), ("ple_w_proj", 2), ("gdn_w_in", 2), ("gdn_conv_w", 2), ("gdn_w_out", 1),
            ("fox_w_in", 2), ("fox_w_out", 1))
_REPLICATED = ("ln_g", "ln_b", "gdn_a_log", "gdn_dt_bias", "gdn_norm_g", "fox_b_f", "fox_q_norm_g", "fox_k_norm_g")
_EXACT = ("gdn_conv_w",)
_ORDER = ("ln_g", "ln_b", "ple_w_gate", "ple_w_proj", "gdn_w_in", "gdn_conv_w", "gdn_a_log", "gdn_dt_bias",
          "gdn_norm_g", "gdn_w_out", "fox_w_in", "fox_b_f", "fox_q_norm_g", "fox_k_norm_g", "fox_w_out")


def _as_rows(a):
    return a.reshape(-1, LANES)


def _gather_weights(local):
    parts = []
    for name, _ in _SHARDED:
        w = local[name]
        wb = lax.bitcast_convert_type(w, BF16) if name in _EXACT else w.astype(BF16)
        parts.append(_as_rows(wb))
    packed = jnp.concatenate(parts, axis=0)
    got = _xy_exchange(packed, gather=True, name="gather_weights")
    full, r0 = {}, 0
    for (name, axis), part in zip(_SHARDED, parts):
        nrow = part.shape[0]
        seg = got[:, r0:r0 + nrow]
        r0 += nrow
        shp = local[name].shape
        if name in _EXACT:
            blocks = lax.bitcast_convert_type(seg.reshape((N_CHIPS,) + shp + (2,)), F32)
        else:
            blocks = seg.reshape((N_CHIPS,) + shp)
        full[name] = jnp.concatenate([blocks[s] for s in range(N_CHIPS)], axis=axis)
    return full


def _reduce_sharded(grads):
    per_owner = []
    sizes = []
    for s in range(N_CHIPS):
        parts = []
        for name, axis in _SHARDED:
            gfull = grads[name]
            n = gfull.shape[axis] // N_CHIPS
            blk = lax.slice_in_dim(gfull, s * n, (s + 1) * n, axis=axis)
            parts.append(_as_rows(blk.astype(BF16)))
        sizes = [q.shape[0] for q in parts]
        per_owner.append(jnp.concatenate(parts, axis=0))
    packed = jnp.stack(per_owner)
    got = _xy_exchange(packed, gather=False, name="exchange_grads")
    mine = _sum_slots(got, tr=4096, name="sum_grads")
    other = _c_swap(mine, name="swap_grads")
    out, r0 = {}, 0
    for (name, axis), nrow in zip(_SHARDED, sizes):
        shp = list(grads[name].shape)
        shp[axis] //= N_CHIPS
        out[name] = (mine[r0:r0 + nrow].reshape(shp), other[r0:r0 + nrow].reshape(shp))
        r0 += nrow
    return out


def _reduce_replicated(grads, loss_part):
    rows = [_pad_lanes(jnp.reshape(loss_part, (1, 1)))]
    for name in _REPLICATED:
        gr = grads[name]
        rows.append(_as_rows(gr) if gr.shape[-1] % LANES == 0 else _pad_lanes(gr))
    sizes = [r.shape[0] for r in rows]
    blk = jnp.concatenate(rows, axis=0)
    nrow = blk.shape[0]
    npad = -nrow % SUBLANES
    blk = jnp.pad(blk, ((0, npad), (0, 0)))
    allb = _all_gather8(blk, name="gather_small_grads").reshape(N_DEV, nrow + npad, LANES)
    tot = _sum_slots(allb, tr=nrow + npad, name="sum_small_grads")
    out, r0 = {}, sizes[0]
    loss = tot[0, 0]
    for name, n in zip(_REPLICATED, sizes[1:]):
        gr = grads[name]
        seg = tot[r0:r0 + n]
        out[name] = seg.reshape(gr.shape) if gr.shape[-1] % LANES == 0 else seg[:, :gr.shape[-1]]
        r0 += n
    return loss, out


def kernel(x, p, ln_g, ln_b, ple_w_gate, ple_w_proj, gdn_w_in, gdn_conv_w, gdn_a_log, gdn_dt_bias, gdn_norm_g, gdn_w_out, fox_w_in, fox_b_f, fox_q_norm_g, fox_k_norm_g, fox_w_out, loss_target, m_ln_g, m_ln_b, m_ple_w_gate, m_ple_w_proj, m_gdn_w_in, m_gdn_conv_w, m_gdn_a_log, m_gdn_dt_bias, m_gdn_norm_g, m_gdn_w_out, m_fox_w_in, m_fox_b_f, m_fox_q_norm_g, m_fox_k_norm_g, m_fox_w_out, v_ln_g, v_ln_b, v_ple_w_gate, v_ple_w_proj, v_gdn_w_in, v_gdn_conv_w, v_gdn_a_log, v_gdn_dt_bias, v_gdn_norm_g, v_gdn_w_out, v_fox_w_in, v_fox_b_f, v_fox_q_norm_g, v_fox_k_norm_g, v_fox_w_out):
    local = dict(ln_g=ln_g, ln_b=ln_b, ple_w_gate=ple_w_gate, ple_w_proj=ple_w_proj, gdn_w_in=gdn_w_in,
                 gdn_conv_w=gdn_conv_w, gdn_a_log=gdn_a_log, gdn_dt_bias=gdn_dt_bias, gdn_norm_g=gdn_norm_g,
                 gdn_w_out=gdn_w_out, fox_w_in=fox_w_in, fox_b_f=fox_b_f, fox_q_norm_g=fox_q_norm_g,
                 fox_k_norm_g=fox_k_norm_g, fox_w_out=fox_w_out)
    mom_m = dict(ln_g=m_ln_g, ln_b=m_ln_b, ple_w_gate=m_ple_w_gate, ple_w_proj=m_ple_w_proj, gdn_w_in=m_gdn_w_in,
                 gdn_conv_w=m_gdn_conv_w, gdn_a_log=m_gdn_a_log, gdn_dt_bias=m_gdn_dt_bias, gdn_norm_g=m_gdn_norm_g,
                 gdn_w_out=m_gdn_w_out, fox_w_in=m_fox_w_in, fox_b_f=m_fox_b_f, fox_q_norm_g=m_fox_q_norm_g,
                 fox_k_norm_g=m_fox_k_norm_g, fox_w_out=m_fox_w_out)
    mom_v = dict(ln_g=v_ln_g, ln_b=v_ln_b, ple_w_gate=v_ple_w_gate, ple_w_proj=v_ple_w_proj, gdn_w_in=v_gdn_w_in,
                 gdn_conv_w=v_gdn_conv_w, gdn_a_log=v_gdn_a_log, gdn_dt_bias=v_gdn_dt_bias, gdn_norm_g=v_gdn_norm_g,
                 gdn_w_out=v_gdn_w_out, fox_w_in=v_fox_w_in, fox_b_f=v_fox_b_f, fox_q_norm_g=v_fox_q_norm_g,
                 fox_k_norm_g=v_fox_k_norm_g, fox_w_out=v_fox_w_out)

    wts = dict(_gather_weights(local))
    for name in _REPLICATED:
        wts[name] = local[name]
    loss_row, dx, grads = _local_step(x[0], p[:, 0], loss_target[0], wts)
    loss, small = _reduce_replicated(grads, jnp.sum(loss_row))
    big = _reduce_sharded(grads)

    outs = {}
    for name in _ORDER:
        parts = list(big[name]) if name in big else [small[name]]
        outs[name] = _adamw(local[name], parts, mom_m[name], mom_v[name], name=f"adamw_{name}")
    return (loss, dx[None], *[outs[n][0] for n in _ORDER], *[outs[n][1] for n in _ORDER],
            *[outs[n][2] for n in _ORDER], *[outs[n][3] for n in _ORDER])
```

```python
import functools

import jax
import jax.numpy as jnp
from jax import lax
from jax.experimental import pallas as pl
from jax.experimental.pallas import tpu as pltpu

F32 = jnp.float32
BF16 = jnp.bfloat16

LANES = 128
SUBLANES = 8
VMEM_LIMIT_BYTES = 56 * 1024 * 1024

GDN_DK = 128
GDN_CHUNK = 64
GDN_CONV = 4
FOX_DH = 64
LN_EPS = 1e-5
RMS_EPS = 1e-6

ADAM_LR = 0.001
ADAM_B1 = 0.9
ADAM_B2 = 0.999
ADAM_EPS = 1e-08
ADAM_WD = 0.01
ADAM_STEP = 10

_DIMS = {"nn": (((1,), (0,)), ((), ())), "nt": (((1,), (1,)), ((), ())), "tn": (((0,), (0,)), ((), ()))}


def _params(*sem):
    return pltpu.CompilerParams(dimension_semantics=sem, vmem_limit_bytes=VMEM_LIMIT_BYTES)


def _split(a, terms):
    out = []
    rest = a.astype(F32)
    for t in range(terms):
        piece = rest.astype(BF16)
        out.append(piece)
        if t + 1 < terms:
            rest = rest - piece.astype(F32)
    return out


def _raw_mm(a, b, form, mode):
    dot = lambda x, y: lax.dot_general(x, y, _DIMS[form], preferred_element_type=F32)
    if mode == "b":
        return dot(a.astype(BF16), b.astype(BF16))
    if mode == "x3":
        (ah, al), (bh, bl) = _split(a, 2), _split(b, 2)
        return dot(ah, bh) + (dot(ah, bl) + dot(al, bh))
    if mode == "ca":
        ac = a.astype(BF16)
        b1, b2, b3 = _split(b, 3)
        return dot(ac, b1) + (dot(ac, b2) + dot(ac, b3))
    assert mode == "cb", mode
    bc = b.astype(BF16)
    a1, a2, a3 = _split(a, 3)
    return dot(a1, bc) + (dot(a2, bc) + dot(a3, bc))


@functools.partial(jax.custom_vjp, nondiff_argnums=(2, 3))
def _mm(a, b, form, mode):
    return _raw_mm(a, b, form, mode)


def _mm_fwd(a, b, form, mode):
    return _raw_mm(a, b, form, mode), (a, b)


def _mm_bwd(form, mode, res, g):
    a, b = res
    flip = {"b": "b", "x3": "x3", "ca": "cb", "cb": "ca"}[mode]
    if form == "nn":
        da, db = (lambda: _mm(g, b, "nt", mode)), (lambda: _mm(a, g, "tn", mode))
    elif form == "nt":
        da, db = (lambda: _mm(g, b, "nn", mode)), (lambda: _mm(g, a, "tn", flip))
    else:
        da, db = (lambda: _mm(b, g, "nt", flip)), (lambda: _mm(a, g, "nn", mode))
    return (jnp.zeros_like(a) if mode == "ca" else da()), (jnp.zeros_like(b) if mode == "cb" else db())


_mm.defvjp(_mm_fwd, _mm_bwd)


@jax.custom_vjp
def _tri_inv(Ls):
    C = Ls[0].shape[0]
    eye = (_iota((C, C), 0) == _iota((C, C), 1)).astype(F32)
    X = [eye - L for L in Ls]
    P = [_raw_mm(L, L, "nn", "x3") for L in Ls]
    n_sq = max(1, (C - 1).bit_length() - 1)
    for it in range(n_sq):
        XP = [_raw_mm(x, p, "nn", "x3") for x, p in zip(X, P)]
        if it < n_sq - 1:
            P = [_raw_mm(p, p, "nn", "x3") for p in P]
        X = [x + xp for x, xp in zip(X, XP)]
    return X


def _tri_inv_fwd(Ls):
    Ts = _tri_inv(Ls)
    return Ts, Ts


def _tri_inv_bwd(Ts, dTs):
    Ms = [_raw_mm(dT, T, "nt", "x3") for dT, T in zip(dTs, Ts)]
    return ([-_raw_mm(T, M, "tn", "x3") for T, M in zip(Ts, Ms)],)


_tri_inv.defvjp(_tri_inv_fwd, _tri_inv_bwd)


def _silu(x):
    return x * jax.nn.sigmoid(x)


def _softplus(x):
    return jnp.maximum(x, 0.0) + jnp.log1p(jnp.exp(-jnp.abs(x)))


def _iota(shape, dim):
    return lax.broadcasted_iota(jnp.int32, shape, dim)


def _matmul(a, b, *, ta=False, tb=False, out_dtype=F32, add=None, add_scale=1.0, tm=512, tn=512, tk=512, name):
    if ta:
        K, M = a.shape
    else:
        M, K = a.shape
    if tb:
        N, K2 = b.shape
    else:
        K2, N = b.shape
    assert K == K2, (a.shape, b.shape, ta, tb)
    tm, tn, tk = min(tm, M), min(tn, N), min(tk, K)
    assert M % tm == 0 and N % tn == 0 and K % tk == 0, (M, N, K, tm, tn, tk)
    nk = K // tk
    form = ("t" if ta else "n") + ("t" if tb else "n")
    dims = (((0 if ta else 1,), (1 if tb else 0,)), ((), ()))
    del form
    a_spec = pl.BlockSpec((tk, tm), lambda i, j, k: (k, i)) if ta else pl.BlockSpec((tm, tk), lambda i, j, k: (i, k))
    b_spec = pl.BlockSpec((tn, tk), lambda i, j, k: (j, k)) if tb else pl.BlockSpec((tk, tn), lambda i, j, k: (k, j))
    o_spec = pl.BlockSpec((tm, tn), lambda i, j, k: (i, j))
    has_add = add is not None

    def body(*refs):
        if has_add:
            a_ref, b_ref, add_ref, o_ref, acc_ref = refs
        else:
            a_ref, b_ref, o_ref, acc_ref = refs
        k = pl.program_id(2)
        part = lax.dot_general(a_ref[...].astype(BF16), b_ref[...].astype(BF16), dims, preferred_element_type=F32)

        def finish(total):
            if has_add:
                total = total + add_scale * add_ref[...].astype(F32)
            o_ref[...] = total.astype(o_ref.dtype)

        if nk == 1:
            finish(part)
        else:
            @pl.when(k == 0)
            def _():
                acc_ref[...] = part

            @pl.when(jnp.logical_and(k > 0, k < nk - 1))
            def _():
                acc_ref[...] += part

            @pl.when(k == nk - 1)
            def _():
                finish(acc_ref[...] + part)

    in_specs = [a_spec, b_spec] + ([o_spec] if has_add else [])
    args = (a, b) + ((add,) if has_add else ())
    return pl.pallas_call(
        body, name=name, grid=(M // tm, N // tn, nk),
        in_specs=in_specs, out_specs=o_spec,
        out_shape=jax.ShapeDtypeStruct((M, N), out_dtype),
        scratch_shapes=[pltpu.VMEM((tm, tn), F32)],
        compiler_params=_params("parallel", "parallel", "arbitrary"),
    )(*args)


def _rowcall(body_fn, rows, consts, out_rows, out_accs, *, tr, name, reverse=False, scratch=()):
    def arr_spec(r):
        return r if isinstance(r, tuple) else (r, None)

    S = arr_spec(rows[0])[0].shape[0]
    tr = min(tr, S)
    assert S % tr == 0
    n = S // tr
    ridx = (lambda i: (n - 1 - i, 0)) if reverse else (lambda i: (i, 0))
    in_specs, args = [], []
    for r in rows:
        arr, spec = arr_spec(r)
        args.append(arr)
        in_specs.append(spec(tr, n) if spec is not None else pl.BlockSpec((tr, arr.shape[1]), ridx))
    for c in consts:
        args.append(c)
        in_specs.append(pl.BlockSpec(c.shape, lambda i: (0, 0)))
    out_specs, out_shape = [], []
    for (ncol, dt) in out_rows:
        out_specs.append(pl.BlockSpec((tr, ncol), ridx))
        out_shape.append(jax.ShapeDtypeStruct((S, ncol), dt))
    for shp in out_accs:
        out_specs.append(pl.BlockSpec(shp, lambda i: (0, 0)))
        out_shape.append(jax.ShapeDtypeStruct(shp, F32))
    nr, nc, no, na = len(rows), len(consts), len(out_rows), len(out_accs)

    def kernel(*refs):
        row_refs = refs[:nr]
        const_refs = refs[nr:nr + nc]
        orow_refs = refs[nr + nc:nr + nc + no]
        oacc_refs = refs[nr + nc + no:nr + nc + no + na]
        scr = refs[nr + nc + no + na:]
        step = pl.program_id(0)
        blk = (n - 1 - step) if reverse else step

        @pl.when(step == 0)
        def _():
            for acc in oacc_refs:
                acc[...] = jnp.zeros(acc.shape, F32)

        body_fn(row_refs, const_refs, orow_refs, oacc_refs, scr, step, blk)

    outs = pl.pallas_call(
        kernel, name=name, grid=(n,), in_specs=in_specs, out_specs=out_specs, out_shape=out_shape,
        scratch_shapes=list(scratch), compiler_params=_params("arbitrary"),
    )(*args)
    return outs


def _row(v):
    return v.astype(F32).reshape(1, -1)


def _pad_lanes(v, width=LANES, offset=0):
    pad = [(0, 0)] * (v.ndim - 1) + [(offset, width - offset - v.shape[-1])]
    return jnp.pad(v, pad)


def _ln_tile(x, y, g, b, alpha):
    u = alpha * x + y
    mu = jnp.mean(u, -1, keepdims=True)
    d = u - mu
    var = jnp.mean(d * d, -1, keepdims=True)
    return d * lax.rsqrt(var + LN_EPS) * g + b


def _ln_fwd(x, y, g, b, alpha, *, tr, name):
    D = x.shape[1]

    def body(rows, consts, orows, oaccs, scr, step, blk):
        orows[0][...] = _ln_tile(rows[0][...], rows[1][...], consts[0][...], consts[1][...], alpha)

    return _rowcall(body, [x, y], [_row(g), _row(b)], [(D, F32)], [], tr=tr, name=name)[0]


def _ln_bwd(x, y, g, b, dxo, t, alpha, *, tr, name):
    D = x.shape[1]

    def body(rows, consts, orows, oaccs, scr, step, blk):
        xv, yv = rows[0][...], rows[1][...]
        ct = rows[2][...] + rows[3][...]
        _, vjp = jax.vjp(lambda yy, gg, bb: _ln_tile(xv, yy, gg, bb, alpha), yv, consts[0][...], consts[1][...])
        du, dg, db = vjp(ct)
        orows[0][...] = du
        oaccs[0][...] += dg
        oaccs[1][...] += db

    du, dg, db = _rowcall(body, [x, y, dxo, t], [_row(g), _row(b)], [(D, F32)], [(1, D), (1, D)], tr=tr, name=name)
    return du, dg[0], db[0]


def _ple_fwd(x_ln, gp, pp, *, tr, name):
    D = x_ln.shape[1]

    def body(rows, consts, orows, oaccs, scr, step, blk):
        orows[0][...] = rows[0][...] + jax.nn.sigmoid(rows[1][...]) * rows[2][...]

    return _rowcall(body, [x_ln, gp, pp], [], [(D, F32)], [], tr=tr, name=name)[0]


def _ple_bwd(dxo, gp, pp, *, tr, name):
    D = dxo.shape[1]

    def body(rows, consts, orows, oaccs, scr, step, blk):
        d = rows[0][...]
        s = jax.nn.sigmoid(rows[1][...])
        orows[0][...] = (d * rows[2][...] * s * (1.0 - s)).astype(BF16)
        orows[1][...] = (d * s).astype(BF16)

    return _rowcall(body, [dxo, gp, pp], [], [(D, BF16), (D, BF16)], [], tr=tr, name=name)


def _loss_fwd_bwd(xf, target, *, tr, name):
    D = xf.shape[1]

    def body(rows, consts, orows, oaccs, scr, step, blk):
        err = rows[0][...] - rows[1][...]
        orows[0][...] = err * (1.0 / D)
        part = jnp.sum(err * err, axis=0, keepdims=True) * (0.5 / D)
        oaccs[0][...] += part

    dx, lrow = _rowcall(body, [xf, target], [], [(D, F32)], [(1, D)], tr=tr, name=name)
    return lrow, dx


def _gdn_qk_tile(c):
    y = _silu(c)
    return y * lax.rsqrt(jnp.sum(y * y, -1, keepdims=True) + RMS_EPS)


def _make_bg_fn(H):
    def fn(hs, alog, dtb):
        lane = _iota((1, LANES), 1)
        beta = jax.nn.sigmoid(hs)
        g = -jnp.exp(alog) * _softplus(hs + dtb)
        return jnp.where(lane < H, beta, jnp.where(lane < 2 * H, g, 0.0))
    return fn


def _halo_spec(ncol):
    def make(tr, n):
        per = tr // SUBLANES
        return pl.BlockSpec((SUBLANES, ncol), lambda i: (jnp.maximum(i * per - 1, 0), 0))
    return make


def _halo_spec_rev(ncol):
    def make(tr, n):
        per = tr // SUBLANES
        return pl.BlockSpec((SUBLANES, ncol), lambda i: (jnp.maximum((n - 1 - i) * per - 1, 0), 0))
    return make


def _gdn_pre_fwd(h_main, h_small, conv_w8, alog_row, dtb_row, H, *, tr, name):
    W = H * GDN_DK
    C3 = 3 * W
    bg_fn = _make_bg_fn(H)

    def body(rows, consts, orows, oaccs, scr, step, blk):
        main_ref, halo_ref, hs_ref = rows
        w_ref, alog_ref, dtb_ref = consts
        q_ref, k_ref, v_ref, bg_ref = orows
        xs = scr[0]
        trr = main_ref.shape[0]
        xs[pl.ds(SUBLANES, trr), :] = main_ref[...]
        xs[pl.ds(0, SUBLANES), :] = jnp.where(blk > 0, halo_ref[...], 0.0)
        for s in range(C3 // LANES):
            ls = slice(s * LANES, (s + 1) * LANES)
            c = jnp.zeros((trr, LANES), F32)
            for j in range(GDN_CONV):
                c = c + w_ref[GDN_CONV - 1 - j:GDN_CONV - j, ls] * xs[pl.ds(SUBLANES - j, trr), ls]
            if s < 2 * H:
                out = _gdn_qk_tile(c)
                (q_ref if s < H else k_ref)[:, (s % H) * LANES:(s % H + 1) * LANES] = out
            else:
                v_ref[:, (s - 2 * H) * LANES:(s - 2 * H + 1) * LANES] = _silu(c)
        bg_ref[...] = bg_fn(hs_ref[...], alog_ref[...], dtb_ref[...])

    main = (h_main, lambda tr_, n: pl.BlockSpec((tr_, C3), lambda i: (i, 0)))
    halo = (h_main, _halo_spec(C3))
    trr = min(tr, h_main.shape[0])
    return _rowcall(body, [main, halo, h_small], [conv_w8, alog_row, dtb_row],
                    [(W, F32), (W, F32), (W, F32), (LANES, F32)], [], tr=tr, name=name,
                    scratch=[pltpu.VMEM((trr + SUBLANES, C3), F32)])


def _gdn_pre_bwd(h_main, h_small, conv_w8, alog_row, dtb_row, dq, dk, dv, dbg, dz, H, *, tr, name):
    W = H * GDN_DK
    C3 = 3 * W
    bg_fn = _make_bg_fn(H)

    def body(rows, consts, orows, oaccs, scr, step, blk):
        main_ref, halo_ref, hs_ref, dq_ref, dk_ref, dv_ref, dbg_ref, dz_ref = rows
        w_ref, alog_ref, dtb_ref = consts
        dmain_ref, dhs_ref = orows
        dw_ref, dalog_ref, ddtb_ref = oaccs
        xs, dcs = scr
        trr = main_ref.shape[0]
        xs[pl.ds(SUBLANES, trr), :] = main_ref[...]
        xs[pl.ds(0, SUBLANES), :] = jnp.where(blk > 0, halo_ref[...], 0.0)

        @pl.when(step == 0)
        def _():
            dcs[pl.ds(trr, SUBLANES), :] = jnp.zeros((SUBLANES, C3), F32)

        for s in range(C3 // LANES):
            ls = slice(s * LANES, (s + 1) * LANES)
            c = jnp.zeros((trr, LANES), F32)
            for j in range(GDN_CONV):
                c = c + w_ref[GDN_CONV - 1 - j:GDN_CONV - j, ls] * xs[pl.ds(SUBLANES - j, trr), ls]
            if s < 2 * H:
                src = dq_ref if s < H else dk_ref
                ct = src[:, (s % H) * LANES:(s % H + 1) * LANES]
                _, vjp = jax.vjp(_gdn_qk_tile, c)
            else:
                ct = dv_ref[:, (s - 2 * H) * LANES:(s - 2 * H + 1) * LANES]
                _, vjp = jax.vjp(_silu, c)
            dcs[pl.ds(0, trr), ls] = vjp(ct)[0]
        for s in range(C3 // LANES):
            ls = slice(s * LANES, (s + 1) * LANES)
            dx = jnp.zeros((trr, LANES), F32)
            dc0 = dcs[pl.ds(0, trr), ls]
            for j in range(GDN_CONV):
                wrow = w_ref[GDN_CONV - 1 - j:GDN_CONV - j, ls]
                dx = dx + wrow * dcs[pl.ds(j, trr), ls]
                dw_ref[GDN_CONV - 1 - j:GDN_CONV - j, ls] += jnp.sum(dc0 * xs[pl.ds(SUBLANES - j, trr), ls], axis=0, keepdims=True)
            dmain_ref[:, ls] = dx.astype(BF16)
        dmain_ref[:, C3:] = dz_ref[...]
        dcs[pl.ds(trr, SUBLANES), :] = dcs[pl.ds(0, SUBLANES), :]
        _, vjp = jax.vjp(bg_fn, hs_ref[...], alog_ref[...], dtb_ref[...])
        dhs, dalog, ddtb = vjp(dbg_ref[...])
        dhs_ref[...] = dhs.astype(BF16)
        dalog_ref[...] += dalog
        ddtb_ref[...] += ddtb

    trr = min(tr, h_main.shape[0])
    main = (h_main, lambda tr_, n: pl.BlockSpec((tr_, C3), lambda i: (n - 1 - i, 0)))
    halo = (h_main, _halo_spec_rev(C3))
    return _rowcall(body, [main, halo, h_small, dq, dk, dv, dbg, dz], [conv_w8, alog_row, dtb_row],
                    [(4 * W, BF16), (LANES, BF16)], [(SUBLANES, C3), (1, LANES), (1, LANES)],
                    tr=tr, name=name, reverse=True,
                    scratch=[pltpu.VMEM((trr + SUBLANES, C3), F32), pltpu.VMEM((trr + SUBLANES, C3), F32)])


def _gdn_chunk(qs, ks, vs, betas, gs, Ss):
    C, dk = qs[0].shape
    dv = vs[0].shape[1]
    ri, ci = _iota((C, C), 0), _iota((C, C), 1)
    causal, strict = ri >= ci, ri > ci
    tril = causal.astype(F32)
    lane0 = (_iota((1, LANES), 1) == 0).astype(F32)
    e0 = jnp.ones((C, 1), F32) * lane0
    last = (_iota((C, 1), 0) == C - 1).astype(F32)

    def each(f, *lists):
        return [f(*a) for a in zip(*lists)]

    G = each(lambda g: g * jnp.ones((1, LANES), F32), gs)
    gcB = each(lambda x: _mm(tril, x, "nn", "ca"), G)
    gc = each(lambda x: jnp.sum(x * lane0, -1, keepdims=True), gcB)
    gc_row = each(lambda x: _mm(e0, x, "nt", "ca"), gcB)
    decay = each(lambda a, b: jnp.where(causal, jnp.exp(jnp.where(causal, a - b, 0.0)), 0.0), gc, gc_row)
    kb = each(lambda k, b: k * b, ks, betas)
    kk = each(lambda a, k: _mm(a, k, "nt", "b"), kb, ks)
    L = each(lambda a, d: jnp.where(strict, a * d, 0.0), kk, decay)
    X = _tri_inv(L)
    egc = each(jnp.exp, gc)
    u = each(lambda x, v, b: _mm(x, v * b, "nn", "x3"), X, vs, betas)
    w = each(lambda x, a, e: _mm(x, a * e, "nn", "x3"), X, kb, egc)
    qsc = each(lambda q: q * (dk ** -0.5), qs)
    qk = each(lambda q, k: _mm(q, k, "nt", "b"), qsc, ks)
    A = each(lambda a, d: jnp.where(causal, a * d, 0.0), qk, decay)
    q_dec = each(lambda q, e: q * e, qsc, egc)
    gl = each(lambda x: jnp.sum(x * last, keepdims=True), gc)
    k_dec = each(lambda k, a, b: k * jnp.exp(a - b), ks, gl, gc)
    wS = each(lambda a, s: _mm(a, s, "nn", "b"), w, Ss)
    qS = each(lambda a, s: _mm(a, s, "nn", "b"), q_dec, Ss)
    v_new = each(lambda a, b: a - b, u, wS)
    Av = each(lambda a, b: _mm(a, b, "nn", "b"), A, v_new)
    kv = each(lambda a, b: _mm(a, b, "tn", "b"), k_dec, v_new)
    o = each(lambda a, b: a + b, qS, Av)
    S_new = each(lambda s, e, x: s * jnp.exp(e) + x, Ss, gl, kv)
    return o, S_new


def _gdn_rule_fwd(q, k, v, bg, H, *, name):
    S_len = q.shape[0]
    C = min(GDN_CHUNK, S_len)
    N = S_len // C
    dk = dv = GDN_DK

    def body(q_ref, k_ref, v_ref, bg_ref, o_ref, st_ref, s_scr):
        n = pl.program_id(0)

        @pl.when(n == 0)
        def _():
            s_scr[...] = jnp.zeros(s_scr.shape, F32)

        bgt = bg_ref[...]
        sl = [slice(h * dk, (h + 1) * dk) for h in range(H)]
        Ss = [s_scr[h] for h in range(H)]
        for h in range(H):
            st_ref[h] = Ss[h]
        os_, S_new = _gdn_chunk([q_ref[:, s] for s in sl], [k_ref[:, s] for s in sl], [v_ref[:, s] for s in sl],
                                [bgt[:, h:h + 1] for h in range(H)], [bgt[:, H + h:H + h + 1] for h in range(H)], Ss)
        for h in range(H):
            o_ref[:, sl[h]] = os_[h]
            s_scr[h] = S_new[h]

    rows = pl.BlockSpec((C, H * dk), lambda n: (n, 0))
    return pl.pallas_call(
        body, name=name, grid=(N,),
        in_specs=[rows, rows, rows, pl.BlockSpec((C, LANES), lambda n: (n, 0))],
        out_specs=[rows, pl.BlockSpec((H, dk, dv), lambda n: (n, 0, 0))],
        out_shape=[jax.ShapeDtypeStruct((S_len, H * dv), F32), jax.ShapeDtypeStruct((N * H, dk, dv), F32)],
        scratch_shapes=[pltpu.VMEM((H, dk, dv), F32)],
        compiler_params=_params("arbitrary"),
    )(q, k, v, bg)


def _gdn_rule_bwd(q, k, v, bg, states, do, H, *, name):
    S_len = q.shape[0]
    C = min(GDN_CHUNK, S_len)
    N = S_len // C
    dk = dv = GDN_DK

    def body(q_ref, k_ref, v_ref, bg_ref, st_ref, do_ref, dq_ref, dk_ref, dv_ref, dbg_ref, ds_scr):
        step = pl.program_id(0)

        @pl.when(step == 0)
        def _():
            ds_scr[...] = jnp.zeros(ds_scr.shape, F32)

        bgt = bg_ref[...]
        lane = _iota((1, LANES), 1)
        dbg = jnp.zeros((C, LANES), F32)
        sl = [slice(h * dk, (h + 1) * dk) for h in range(H)]
        _, vjp = jax.vjp(_gdn_chunk, [q_ref[:, s] for s in sl], [k_ref[:, s] for s in sl], [v_ref[:, s] for s in sl],
                         [bgt[:, h:h + 1] for h in range(H)], [bgt[:, H + h:H + h + 1] for h in range(H)],
                         [st_ref[h] for h in range(H)])
        dq, dkk, dvv, dbeta, dg, dS = vjp(([do_ref[:, s] for s in sl], [ds_scr[h] for h in range(H)]))
        for h in range(H):
            dq_ref[:, sl[h]] = dq[h]
            dk_ref[:, sl[h]] = dkk[h]
            dv_ref[:, sl[h]] = dvv[h]
            dbg = dbg + jnp.where(lane == h, dbeta[h], 0.0) + jnp.where(lane == h + H, dg[h], 0.0)
            ds_scr[h] = dS[h]
        dbg_ref[...] = dbg

    rows = pl.BlockSpec((C, H * dk), lambda s: (N - 1 - s, 0))
    bgs = pl.BlockSpec((C, LANES), lambda s: (N - 1 - s, 0))
    return pl.pallas_call(
        body, name=name, grid=(N,),
        in_specs=[rows, rows, rows, bgs, pl.BlockSpec((H, dk, dv), lambda s: (N - 1 - s, 0, 0)), rows],
        out_specs=[rows, rows, rows, bgs],
        out_shape=[jax.ShapeDtypeStruct((S_len, H * dk), F32)] * 3 + [jax.ShapeDtypeStruct((S_len, LANES), F32)],
        scratch_shapes=[pltpu.VMEM((H, dk, dv), F32)],
        compiler_params=_params("arbitrary"),
    )(q, k, v, bg, states, do)


def _gdn_post_tile(o, z, g):
    return o * lax.rsqrt(jnp.mean(o * o, -1, keepdims=True) + RMS_EPS) * g * _silu(z)


def _gdn_post_fwd(o, h_main, norm_g, H, *, tr, name):
    W = H * GDN_DK

    def body(rows, consts, orows, oaccs, scr, step, blk):
        for h in range(H):
            ls = slice(h * LANES, (h + 1) * LANES)
            orows[0][:, ls] = _gdn_post_tile(rows[0][:, ls], rows[1][:, ls], consts[0][...]).astype(BF16)

    z = (h_main, lambda tr_, n: pl.BlockSpec((tr_, W), lambda i: (i, 3)))
    return _rowcall(body, [o, z], [_row(norm_g)], [(W, BF16)], [], tr=tr, name=name)[0]


def _gdn_post_bwd(o, h_main, norm_g, d_on, H, *, tr, name):
    W = H * GDN_DK

    def body(rows, consts, orows, oaccs, scr, step, blk):
        for h in range(H):
            ls = slice(h * LANES, (h + 1) * LANES)
            _, vjp = jax.vjp(_gdn_post_tile, rows[0][:, ls], rows[1][:, ls], consts[0][...])
            d_o, d_z, d_g = vjp(rows[2][:, ls])
            orows[0][:, ls] = d_o
            orows[1][:, ls] = d_z.astype(BF16)
            oaccs[0][...] += d_g

    z = (h_main, lambda tr_, n: pl.BlockSpec((tr_, W), lambda i: (i, 3)))
    return _rowcall(body, [o, z, d_on], [_row(norm_g)], [(W, F32), (W, BF16)], [(1, LANES)], tr=tr, name=name)


def _seg_ones():
    ri, ci = _iota((LANES, LANES), 0), _iota((LANES, LANES), 1)
    return ((ri < FOX_DH) == (ci < FOX_DH)).astype(F32)


def _fox_qk_tile(x, g2):
    ms = _mm(x * x, _seg_ones(), "nn", "cb") * (1.0 / FOX_DH)
    return x * lax.rsqrt(ms + RMS_EPS) * g2


def _make_lf_fn(Hf):
    def fn(hs, bf):
        lane = _iota((1, LANES), 1)
        return jnp.where(lane < Hf, -_softplus(-(hs + bf)), 0.0)
    return fn


def _fox_pre_fwd(h_main, h_small, gq2, gk2, bf_row, Hf, *, tr, name):
    W = Hf * FOX_DH
    lf_fn = _make_lf_fn(Hf)

    def body(rows, consts, orows, oaccs, scr, step, blk):
        qk_ref, v_ref, hs_ref = rows
        gq_ref, gk_ref, bf_ref = consts
        qn_ref, kn_ref, vb_ref, c_ref, cb_ref = orows
        carry = scr[0]
        trr = qk_ref.shape[0]

        @pl.when(step == 0)
        def _():
            carry[...] = jnp.zeros(carry.shape, F32)

        for s in range(W // LANES):
            ls = slice(s * LANES, (s + 1) * LANES)
            qn_ref[:, ls] = _fox_qk_tile(qk_ref[:, ls], gq_ref[...]).astype(BF16)
            kn_ref[:, ls] = _fox_qk_tile(qk_ref[:, W + s * LANES:W + (s + 1) * LANES], gk_ref[...]).astype(BF16)
        vb_ref[...] = v_ref[...].astype(BF16)
        lf = lf_fn(hs_ref[...], bf_ref[...])
        tril = (_iota((trr, trr), 0) >= _iota((trr, trr), 1)).astype(F32)
        c = _raw_mm(tril, lf, "nn", "ca") + carry[0:1, :]
        c_ref[...] = c
        carry[0:1, :] = c[trr - 1:trr, :]
        col = _iota((LANES, 2 * W), 1)
        parity = (col >= W).astype(jnp.int32)
        slab = jnp.right_shift(col - parity * W, 7)
        expand = (2 * slab + parity == _iota((LANES, 2 * W), 0)).astype(F32)
        cb_ref[...] = _raw_mm(c, expand, "nn", "cb")

    qk = (h_main, lambda tr_, n: pl.BlockSpec((tr_, 2 * W), lambda i: (i, 0)))
    vv = (h_main, lambda tr_, n: pl.BlockSpec((tr_, W), lambda i: (i, 2)))
    return _rowcall(body, [qk, vv, h_small], [gq2, gk2, bf_row],
                    [(W, BF16), (W, BF16), (W, BF16), (LANES, F32), (2 * W, F32)], [], tr=tr, name=name,
                    scratch=[pltpu.VMEM((SUBLANES, LANES), F32)])


def _fox_pre_bwd(h_main, h_small, gq2, gk2, bf_row, dqn, dkn, dvv, dz, dc, Hf, *, tr, name):
    W = Hf * FOX_DH
    lf_fn = _make_lf_fn(Hf)

    def body(rows, consts, orows, oaccs, scr, step, blk):
        qk_ref, hs_ref, dqn_ref, dkn_ref, dvv_ref, dz_ref, dc_ref = rows
        gq_ref, gk_ref, bf_ref = consts
        dmain_ref, dhs_ref = orows
        dgq_ref, dgk_ref, dbf_ref = oaccs
        carry = scr[0]
        trr = qk_ref.shape[0]

        @pl.when(step == 0)
        def _():
            carry[...] = jnp.zeros(carry.shape, F32)

        for s in range(W // LANES):
            ls = slice(s * LANES, (s + 1) * LANES)
            lk = slice(W + s * LANES, W + (s + 1) * LANES)
            _, vjp = jax.vjp(_fox_qk_tile, qk_ref[:, ls], gq_ref[...])
            dx, dg = vjp(dqn_ref[:, ls])
            dmain_ref[:, ls] = dx.astype(BF16)
            dgq_ref[...] += dg
            _, vjp = jax.vjp(_fox_qk_tile, qk_ref[:, lk], gk_ref[...])
            dx, dg = vjp(dkn_ref[:, ls])
            dmain_ref[:, lk] = dx.astype(BF16)
            dgk_ref[...] += dg
        dmain_ref[:, 2 * W:3 * W] = dvv_ref[...].astype(BF16)
        dmain_ref[:, 3 * W:] = dz_ref[...]
        dcv = dc_ref[...]
        triu = (_iota((trr, trr), 0) <= _iota((trr, trr), 1)).astype(F32)
        dlf = _raw_mm(triu, dcv, "nn", "ca") + carry[0:1, :]
        carry[0:1, :] = dlf[0:1, :]
        _, vjp = jax.vjp(lf_fn, hs_ref[...], bf_ref[...])
        dhs, dbf = vjp(dlf)
        dhs_ref[...] = dhs.astype(BF16)
        dbf_ref[...] += dbf

    qk = (h_main, lambda tr_, n: pl.BlockSpec((tr_, 2 * W), lambda i: (n - 1 - i, 0)))
    return _rowcall(body, [qk, h_small, dqn, dkn, dvv, dz, dc], [gq2, gk2, bf_row],
                    [(4 * W, BF16), (LANES, BF16)], [(1, LANES), (1, LANES), (1, LANES)],
                    tr=tr, name=name, reverse=True, scratch=[pltpu.VMEM((SUBLANES, LANES), F32)])


def _fox_attn_fwd(qn, kn, vb, c_b, c_rowp, *, tb, name):
    S_len, W = qn.shape
    HP = W // LANES
    tb = min(tb, S_len)
    nb = S_len // tb
    scale = FOX_DH ** -0.5
    rb, cb = min(ATTN_ROWS, tb), min(ATTN_COLS, tb)
    nblk = 2 * (tb // rb)

    steps = [(i, j) for i in range(nb) for j in range(i + 1)]
    ti = jnp.asarray([s[0] for s in steps], jnp.int32)
    tj = jnp.asarray([s[1] for s in steps], jnp.int32)

    def body(ti_ref, tj_ref, q_ref, k_ref, v_ref, cb0_ref, cb1_ref, cr_ref, o_ref, lse_ref,
             m_scr, l_scr, acc_scr, s_scr, p_scr, a_scr):
        t = pl.program_id(1)
        i, j = ti_ref[t], tj_ref[t]

        @pl.when(j == 0)
        def _():
            m_scr[...] = jnp.full(m_scr.shape, -jnp.inf, F32)
            l_scr[...] = jnp.zeros(l_scr.shape, F32)
            acc_scr[...] = jnp.zeros(acc_scr.shape, F32)

        def compute(diag):
            lo = _iota((1, LANES), 1) < FOX_DH
            v = v_ref[...]
            ones = jnp.ones((tb, LANES), BF16)
            blocks = [(hh, r) for hh in range(2) for r in range(tb // rb)]
            masks = [lo, jnp.logical_not(lo)]

            def visible(r):
                return ((r + 1) * rb - 1) // LANES + 1 if diag else tb // LANES

            for b, (hh, r) in enumerate(blocks):
                rows = slice(r * rb, (r + 1) * rb)
                qr = q_ref[rows, :]
                qh = jnp.where(masks[hh], qr * scale, jnp.zeros_like(qr))
                ctb = (cb0_ref if hh == 0 else cb1_ref)[rows, :]
                mx = None
                for c in range(tb // cb):
                    if c * cb // LANES >= visible(r):
                        continue
                    s2 = lax.dot_general(qh, k_ref[c * cb:(c + 1) * cb, :], _DIMS["nt"], preferred_element_type=F32)
                    for piece in range(c * cb // LANES, min((c + 1) * cb // LANES, visible(r))):
                        cols = slice(piece * LANES, (piece + 1) * LANES)
                        s = s2[:, piece * LANES - c * cb:(piece + 1) * LANES - c * cb] + ctb - cr_ref[hh:hh + 1, cols]
                        if diag and (piece + 1) * LANES - 1 > r * rb:
                            keep = piece * LANES + _iota((rb, LANES), 1) <= r * rb + _iota((rb, LANES), 0)
                            s = jnp.where(keep, s, -jnp.inf)
                        s_scr[b, :, cols] = s
                        mx = s if mx is None else jnp.maximum(mx, s)
                m_prev = m_scr[hh, rows, :]
                m_new = jnp.maximum(m_prev, jnp.broadcast_to(jnp.max(mx, -1, keepdims=True), (rb, LANES)))
                a_scr[b] = jnp.exp(m_prev - m_new)
                m_scr[hh, rows, :] = m_new
            for b, (hh, r) in enumerate(blocks):
                m_new = m_scr[hh, r * rb:(r + 1) * rb, :]
                for piece in range(visible(r)):
                    cols = slice(piece * LANES, (piece + 1) * LANES)
                    p_scr[b, :, cols] = jnp.exp(s_scr[b, :, cols] - m_new).astype(BF16)
            for b, (hh, r) in enumerate(blocks):
                rows = slice(r * rb, (r + 1) * rb)
                nkv = visible(r) * LANES
                pb = p_scr[b, :, :nkv]
                vh = jnp.where(masks[hh], v[:nkv], jnp.zeros_like(v[:nkv]))
                psum = lax.dot_general(pb, ones[:nkv], _DIMS["nn"], preferred_element_type=F32)
                pv = lax.dot_general(pb, vh, _DIMS["nn"], preferred_element_type=F32)
                alpha = a_scr[b]
                l_scr[hh, rows, :] = alpha * l_scr[hh, rows, :] + psum
                acc = acc_scr[rows, :]
                acc_scr[rows, :] = jnp.where(masks[hh], acc * alpha + pv, acc)

        @pl.when(j < i)
        def _():
            compute(False)

        @pl.when(j == i)
        def _():
            compute(True)
            lo = _iota((1, LANES), 1) < FOX_DH
            o_ref[...] = acc_scr[...] / jnp.where(lo, l_scr[0], l_scr[1])
            lse_ref[...] = m_scr[...] + jnp.log(l_scr[...])

    qs = pl.BlockSpec((tb, LANES), lambda h, t, ti_, tj_: (ti_[t], h))
    qs1 = pl.BlockSpec((tb, LANES), lambda h, t, ti_, tj_: (ti_[t], HP + h))
    ks = pl.BlockSpec((tb, LANES), lambda h, t, ti_, tj_: (tj_[t], h))
    crs = pl.BlockSpec((None, SUBLANES, tb), lambda h, t, ti_, tj_: (h, 0, tj_[t]))
    return pl.pallas_call(
        body, name=name,
        grid_spec=pltpu.PrefetchScalarGridSpec(
            num_scalar_prefetch=2, grid=(HP, len(steps)),
            in_specs=[qs, ks, ks, qs, qs1, crs],
            out_specs=[qs, pl.BlockSpec((2, tb, LANES), lambda h, t, ti_, tj_: (0, ti_[t], h))],
            scratch_shapes=[pltpu.VMEM((2, tb, LANES), F32), pltpu.VMEM((2, tb, LANES), F32),
                            pltpu.VMEM((tb, LANES), F32), pltpu.VMEM((nblk, rb, tb), F32),
                            pltpu.VMEM((nblk, rb, tb), BF16), pltpu.VMEM((nblk, rb, LANES), F32)]),
        out_shape=[jax.ShapeDtypeStruct((S_len, W), F32), jax.ShapeDtypeStruct((2, S_len, W), F32)],
        compiler_params=_params("parallel", "arbitrary"),
    )(ti, tj, qn, kn, vb, c_b, c_b, c_rowp)


def _fox_attn_bwd(qn, kn, vb, c_b, c_rowp, lse_b, delta_b, do, *, tb, name):
    S_len, W = qn.shape
    HP = W // LANES
    tb = min(tb, S_len)
    nb = S_len // tb
    scale = FOX_DH ** -0.5
    rb, cb = min(ATTN_ROWS, tb), min(ATTN_COLS, tb)

    steps = [(j, i) for j in range(nb) for i in range(j, nb)]
    tj = jnp.asarray([s[0] for s in steps], jnp.int32)
    ti = jnp.asarray([s[1] for s in steps], jnp.int32)

    def body(tj_ref, ti_ref, q_ref, k_ref, v_ref, cb0_ref, cb1_ref, cr_ref, lse_ref, dl0_ref, dl1_ref, do_ref,
             dq_ref, dk_ref, dv_ref, dcr_ref, dct_ref, dk_scr, dv_scr, dc_scr, p_scr, ds_scr):
        t = pl.program_id(1)
        j, i = tj_ref[t], ti_ref[t]

        @pl.when(t == 0)
        def _():
            dq_ref[...] = jnp.zeros(dq_ref.shape, F32)
            dct_ref[...] = jnp.zeros(dct_ref.shape, F32)

        @pl.when(i == j)
        def _():
            dk_scr[...] = jnp.zeros(dk_scr.shape, F32)
            dv_scr[...] = jnp.zeros(dv_scr.shape, F32)
            dc_scr[...] = jnp.zeros(dc_scr.shape, F32)

        def compute(diag):
            lo = _iota((1, LANES), 1) < FOX_DH
            masks = [lo, jnp.logical_not(lo)]
            row0 = pl.multiple_of(i * tb, tb)
            npiece = tb // LANES
            colsum = [[None] * npiece for _ in range(2)]

            def visible(r):
                return ((r + 1) * rb - 1) // LANES + 1 if diag else npiece

            for hh in range(2):
                for r in range(tb // rb):
                    rows = slice(r * rb, (r + 1) * rb)
                    qr = q_ref[rows, :]
                    qh = jnp.where(masks[hh], qr * scale, jnp.zeros_like(qr))
                    doh = jnp.where(masks[hh], do_ref[rows, :], 0.0).astype(BF16)
                    bq = (cb0_ref if hh == 0 else cb1_ref)[rows, :] - lse_ref[hh, rows, :]
                    dlt = (dl0_ref if hh == 0 else dl1_ref)[rows, :]
                    rsum = None
                    for c in range(tb // cb):
                        first, last = c * cb // LANES, min((c + 1) * cb // LANES, visible(r))
                        for piece in range(max(first, last), (c + 1) * cb // LANES):
                            cols = slice(piece * LANES, (piece + 1) * LANES)
                            p_scr[hh, rows, cols] = jnp.zeros((rb, LANES), BF16)
                            ds_scr[hh, rows, cols] = jnp.zeros((rb, LANES), BF16)
                        if first >= last:
                            continue
                        s2 = lax.dot_general(qh, k_ref[c * cb:(c + 1) * cb, :], _DIMS["nt"], preferred_element_type=F32)
                        dp2 = lax.dot_general(doh, v_ref[c * cb:(c + 1) * cb, :], _DIMS["nt"], preferred_element_type=F32)
                        for piece in range(first, last):
                            cols = slice(piece * LANES, (piece + 1) * LANES)
                            sub = slice(piece * LANES - c * cb, (piece + 1) * LANES - c * cb)
                            s = s2[:, sub] + bq - cr_ref[hh:hh + 1, cols]
                            if diag and (piece + 1) * LANES - 1 > r * rb:
                                keep = piece * LANES + _iota((rb, LANES), 1) <= r * rb + _iota((rb, LANES), 0)
                                s = jnp.where(keep, s, -jnp.inf)
                            p = jnp.exp(s)
                            ds = p * (dp2[:, sub] - dlt)
                            p_scr[hh, rows, cols] = p.astype(BF16)
                            ds_scr[hh, rows, cols] = ds.astype(BF16)
                            rsum = ds if rsum is None else rsum + ds
                            csum = jnp.sum(ds, axis=0, keepdims=True)
                            colsum[hh][piece] = csum if colsum[hh][piece] is None else colsum[hh][piece] + csum
                    grow = pl.ds(row0 + r * rb, rb)
                    dct_ref[grow, :] += jnp.where(masks[hh], jnp.sum(rsum, -1, keepdims=True), 0.0)
            k = k_ref[...]
            qf = q_ref[...]
            dof = do_ref[...]
            dq_part = jnp.zeros((tb, LANES), F32)
            for hh in range(2):
                kh = jnp.where(masks[hh], k, jnp.zeros_like(k))
                qhf = jnp.where(masks[hh], qf * scale, jnp.zeros_like(qf))
                dohf = jnp.where(masks[hh], dof, 0.0).astype(BF16)
                dv_scr[...] += lax.dot_general(p_scr[hh], dohf, _DIMS["tn"], preferred_element_type=F32)
                dk_scr[...] += lax.dot_general(ds_scr[hh], qhf, _DIMS["tn"], preferred_element_type=F32)
                dq_part = dq_part + lax.dot_general(ds_scr[hh], kh, _DIMS["nn"], preferred_element_type=F32)
                for piece in range(npiece):
                    if colsum[hh][piece] is not None:
                        dc_scr[hh:hh + 1, piece * LANES:(piece + 1) * LANES] -= colsum[hh][piece]
            dq_ref[pl.ds(row0, tb), :] += dq_part * scale

        @pl.when(i > j)
        def _():
            compute(False)

        @pl.when(i == j)
        def _():
            compute(True)

        @pl.when(i == nb - 1)
        def _():
            dk_ref[...] = dk_scr[...]
            dv_ref[...] = dv_scr[...]
            dcr_ref[...] = dc_scr[...]

    qs = pl.BlockSpec((tb, LANES), lambda h, t, tj_, ti_: (ti_[t], h))
    qs1 = pl.BlockSpec((tb, LANES), lambda h, t, tj_, ti_: (ti_[t], HP + h))
    ks = pl.BlockSpec((tb, LANES), lambda h, t, tj_, ti_: (tj_[t], h))
    crs = pl.BlockSpec((None, SUBLANES, tb), lambda h, t, tj_, ti_: (h, 0, tj_[t]))
    whole = pl.BlockSpec((S_len, LANES), lambda h, t, tj_, ti_: (0, h))
    return pl.pallas_call(
        body, name=name,
        grid_spec=pltpu.PrefetchScalarGridSpec(
            num_scalar_prefetch=2, grid=(HP, len(steps)),
            in_specs=[qs, ks, ks, qs, qs1, crs, pl.BlockSpec((2, tb, LANES), lambda h, t, tj_, ti_: (0, ti_[t], h)),
                      qs, qs1, qs],
            out_specs=[whole, ks, ks, crs, whole],
            scratch_shapes=[pltpu.VMEM((tb, LANES), F32), pltpu.VMEM((tb, LANES), F32),
                            pltpu.VMEM((SUBLANES, tb), F32), pltpu.VMEM((2, tb, tb), BF16),
                            pltpu.VMEM((2, tb, tb), BF16)]),
        out_shape=[jax.ShapeDtypeStruct((S_len, W), F32)] * 3 + [jax.ShapeDtypeStruct((HP, SUBLANES, S_len), F32),
                                                                 jax.ShapeDtypeStruct((S_len, W), F32)],
        compiler_params=_params("parallel", "arbitrary"),
    )(tj, ti, qn, kn, vb, c_b, c_b, c_rowp, lse_b, delta_b, delta_b, do)


def _fox_post_tile(o, z):
    return o * _silu(z)


def _fox_post_fwd(o, h_main, *, tr, name):
    W = o.shape[1]

    def body(rows, consts, orows, oaccs, scr, step, blk):
        orows[0][...] = _fox_post_tile(rows[0][...], rows[1][...]).astype(BF16)

    z = (h_main, lambda tr_, n: pl.BlockSpec((tr_, W), lambda i: (i, 3)))
    return _rowcall(body, [o, z], [], [(W, BF16)], [], tr=tr, name=name)[0]


def _fox_post_bwd(o, h_main, d_og, *, tr, name):
    W = o.shape[1]

    def body(rows, consts, orows, oaccs, scr, step, blk):
        _, vjp = jax.vjp(_fox_post_tile, rows[0][...], rows[1][...])
        d_o, d_z = vjp(rows[2][...])
        orows[0][...] = d_o
        orows[1][...] = d_z.astype(BF16)
        lo_rows = (_iota((LANES, LANES), 0) < FOX_DH)
        for s in range(W // LANES):
            ls = slice(s * LANES, (s + 1) * LANES)
            prod = d_o[:, ls] * rows[0][:, ls]
            orows[2][:, ls] = _raw_mm(prod, lo_rows.astype(F32), "nn", "cb")
            orows[2][:, W + s * LANES:W + (s + 1) * LANES] = _raw_mm(prod, jnp.logical_not(lo_rows).astype(F32), "nn", "cb")

    z = (h_main, lambda tr_, n: pl.BlockSpec((tr_, W), lambda i: (i, 3)))
    return _rowcall(body, [o, z, d_og], [], [(W, F32), (W, BF16), (2 * W, F32)], [], tr=tr, name=name)


MESH_IDS = pl.DeviceIdType.MESH
N_CHIPS = 4
N_DEV = 8
_ANY = pl.BlockSpec(memory_space=pl.ANY)


def _xy_exchange(src, *, gather, name):
    shape = (N_CHIPS,) + tuple(src.shape[-2:])

    def body(src_ref, out_ref, send_sems, recv_sems, local_sem):
        x, y, c = lax.axis_index("x"), lax.axis_index("y"), lax.axis_index("c")
        me = 2 * x + y
        peers = [(1 - x, y), (x, 1 - y), (1 - x, 1 - y)]

        def outgoing(px, py):
            return src_ref if gather else src_ref.at[2 * px + py]

        mine = pltpu.make_async_copy(outgoing(x, y), out_ref.at[me], local_sem)
        mine.start()
        copies = []
        for j, (px, py) in enumerate(peers):
            cp = pltpu.make_async_remote_copy(
                src_ref=outgoing(px, py), dst_ref=out_ref.at[me],
                send_sem=send_sems.at[j], recv_sem=recv_sems.at[j],
                device_id=(px, py, c), device_id_type=MESH_IDS)
            cp.start()
            copies.append(cp)
        for j, (px, py) in enumerate(peers):
            pltpu.make_async_remote_copy(
                src_ref=outgoing(px, py), dst_ref=out_ref.at[2 * px + py],
                send_sem=send_sems.at[j], recv_sem=recv_sems.at[j],
                device_id=(px, py, c), device_id_type=MESH_IDS).wait_recv()
        for cp in copies:
            cp.wait_send()
        mine.wait()

    return pl.pallas_call(
        body, name=name, in_specs=[_ANY], out_specs=_ANY,
        out_shape=jax.ShapeDtypeStruct(shape, src.dtype),
        scratch_shapes=[pltpu.SemaphoreType.DMA((3,)), pltpu.SemaphoreType.DMA((3,)), pltpu.SemaphoreType.DMA],
    )(src)


def _c_swap(src, *, name):
    def body(src_ref, out_ref, send_sem, recv_sem):
        x, y, c = lax.axis_index("x"), lax.axis_index("y"), lax.axis_index("c")
        cp = pltpu.make_async_remote_copy(
            src_ref=src_ref, dst_ref=out_ref, send_sem=send_sem, recv_sem=recv_sem,
            device_id=(x, y, 1 - c), device_id_type=MESH_IDS)
        cp.start()
        cp.wait()

    return pl.pallas_call(
        body, name=name, in_specs=[_ANY], out_specs=_ANY,
        out_shape=jax.ShapeDtypeStruct(src.shape, src.dtype),
        scratch_shapes=[pltpu.SemaphoreType.DMA, pltpu.SemaphoreType.DMA],
    )(src)


def _all_gather8(blk, *, name):
    m_per, n = blk.shape

    def body(x_ref, out_ref, send_sems, recv_sems, local_sem):
        x, y, c = lax.axis_index("x"), lax.axis_index("y"), lax.axis_index("c")
        me, sibling = (x, y, c), (x, y, 1 - c)
        chips = [(1 - x, y), (x, 1 - y), (1 - x, 1 - y)]

        def rows(px, py, pc):
            return out_ref.at[pl.ds((4 * px + 2 * py + pc) * m_per, m_per), :]

        def copy(k, block, to, src=None):
            return pltpu.make_async_remote_copy(
                src_ref=rows(*block) if src is None else src, dst_ref=rows(*block),
                send_sem=send_sems.at[k], recv_sem=recv_sems.at[k], device_id=to, device_id_type=MESH_IDS)

        mine = pltpu.make_async_copy(x_ref, rows(*me), local_sem)
        mine.start()
        first = [copy(0, me, sibling, src=x_ref)]
        first += [copy(1 + j, me, (*chip, c), src=x_ref) for j, chip in enumerate(chips)]
        for cp in first:
            cp.start()
        passed = [copy(4 + j, (*chip, c), sibling) for j, chip in enumerate(chips)]
        for j, chip in enumerate(chips):
            copy(1 + j, (*chip, c), me).wait_recv()
            passed[j].start()
        copy(0, sibling, me).wait_recv()
        for j, chip in enumerate(chips):
            copy(4 + j, (*chip, 1 - c), me).wait_recv()
        for cp in first + passed:
            cp.wait_send()
        mine.wait()

    return pl.pallas_call(
        body, name=name,
        out_shape=jax.ShapeDtypeStruct((N_DEV * m_per, n), blk.dtype),
        in_specs=[pl.BlockSpec(memory_space=pltpu.VMEM)], out_specs=pl.BlockSpec(memory_space=pltpu.VMEM),
        scratch_shapes=[pltpu.SemaphoreType.DMA((7,)), pltpu.SemaphoreType.DMA((7,)), pltpu.SemaphoreType.DMA],
    )(blk)


def _sum_slots(parts, *, tr, name):
    n, R, _ = parts.shape
    pack = 2 * SUBLANES
    tr = max(t for t in range(pack, min(tr, R) + 1, pack) if R % t == 0) if R % pack == 0 else R

    def body(p_ref, o_ref):
        tot = p_ref[0].astype(F32)
        for s in range(1, n):
            tot = tot + p_ref[s].astype(F32)
        o_ref[...] = tot

    return pl.pallas_call(
        body, name=name, grid=(R // tr,),
        in_specs=[pl.BlockSpec((n, tr, LANES), lambda i: (0, i, 0))], out_specs=pl.BlockSpec((tr, LANES), lambda i: (i, 0)),
        out_shape=jax.ShapeDtypeStruct((R, LANES), F32), compiler_params=_params("parallel"),
    )(parts)


def _adamw(w, g_parts, m, v, *, name):
    shape = w.shape
    as2d = lambda a: a.reshape(-1, shape[-1])
    w2, m2, v2 = as2d(w), as2d(m), as2d(v)
    gs = [as2d(g) for g in g_parts]
    R, C = w2.shape
    tr = R
    while tr * C * 4 > (1 << 20) and tr % 2 == 0 and (tr // 2) % SUBLANES == 0:
        tr //= 2
    ng = len(gs)

    def body(*refs):
        w_ref, m_ref, v_ref = refs[:3]
        g_refs = refs[3:3 + ng]
        go_ref, d_ref, mo_ref, vo_ref = refs[3 + ng:]
        g = g_refs[0][...]
        for r in g_refs[1:]:
            g = g + r[...]
        mn = ADAM_B1 * m_ref[...] + (1.0 - ADAM_B1) * g
        vn = ADAM_B2 * v_ref[...] + (1.0 - ADAM_B2) * jnp.square(g)
        m_hat = mn / (1.0 - ADAM_B1 ** ADAM_STEP)
        v_hat = vn / (1.0 - ADAM_B2 ** ADAM_STEP)
        go_ref[...] = g
        d_ref[...] = -ADAM_LR * (m_hat / (jnp.sqrt(v_hat) + ADAM_EPS) + ADAM_WD * w_ref[...])
        mo_ref[...] = mn
        vo_ref[...] = vn

    spec = pl.BlockSpec((tr, C), lambda i: (i, 0))
    outs = pl.pallas_call(
        body, name=name, grid=(R // tr,), in_specs=[spec] * (3 + ng), out_specs=[spec] * 4,
        out_shape=[jax.ShapeDtypeStruct((R, C), F32)] * 4, compiler_params=_params("parallel"),
    )(w2, m2, v2, *gs)
    return tuple(o.reshape(shape) for o in outs)


TR = 256
ATTN_TILE = 1024
ATTN_ROWS = 128
ATTN_COLS = 256


def _mm_nn(a, b, name, **kw):
    return _matmul(a, b, tm=512, tn=1024, tk=1024, name=name, **kw)


def _mm_nt(a, b, name, **kw):
    return _matmul(a, b, tb=True, tm=512, tn=1024, tk=1024, name=name, **kw)


def _mm_tn(a, b, name, **kw):
    return _matmul(a, b, ta=True, tm=1024, tn=1024, tk=512, name=name, **kw)


def _c_rows(c, Hf):
    S_len = c.shape[0]
    ct = c[:, :Hf].T.reshape(Hf // 2, 2, S_len)
    return jnp.pad(ct, ((0, 0), (0, SUBLANES - 2), (0, 0)))


def _local_step(x, p, target, wts):
    L = wts["ln_g"].shape[0]
    alpha = (2 * L) ** 0.25
    Hg = wts["gdn_a_log"].shape[1]
    Hf = wts["fox_b_f"].shape[1]
    Wg_ = Hg * GDN_DK
    Wf_ = Hf * FOX_DH
    saved = []
    for i in range(L):
        j = i // 2
        sv = {"x": x}
        if i % 2 == 0:
            w_in = wts["gdn_w_in"][j]
            wm, ws = w_in[:, :4 * Wg_], _pad_lanes(w_in[:, 4 * Wg_:])
            cw8 = jnp.pad(wts["gdn_conv_w"][j], ((0, SUBLANES - GDN_CONV), (0, 0)))
            alog = _pad_lanes(_row(wts["gdn_a_log"][j]), offset=Hg)
            dtb = _pad_lanes(_row(wts["gdn_dt_bias"][j]), offset=Hg)
            hm = _mm_nn(x, wm, f"gdn{j}_in_main")
            hs = _mm_nn(x, ws, f"gdn{j}_in_small")
            q, k, v, bg = _gdn_pre_fwd(hm, hs, cw8, alog, dtb, Hg, tr=TR, name=f"gdn{j}_pre")
            o, states = _gdn_rule_fwd(q, k, v, bg, Hg, name=f"gdn{j}_rule")
            on = _gdn_post_fwd(o, hm, wts["gdn_norm_g"][j], Hg, tr=TR, name=f"gdn{j}_post")
            y = _mm_nn(on, wts["gdn_w_out"][j], f"gdn{j}_out")
            sv.update(wm=wm, ws=ws, cw8=cw8, alog=alog, dtb=dtb, hm=hm, hs=hs, q=q, k=k, v=v, bg=bg, o=o,
                      states=states, on=on)
        else:
            w_in = wts["fox_w_in"][j]
            wm, ws = w_in[:, :4 * Wf_], _pad_lanes(w_in[:, 4 * Wf_:])
            gq2 = _row(jnp.tile(wts["fox_q_norm_g"][j], 2))
            gk2 = _row(jnp.tile(wts["fox_k_norm_g"][j], 2))
            bf = _pad_lanes(_row(wts["fox_b_f"][j]))
            hm = _mm_nn(x, wm, f"fox{j}_in_main")
            hs = _mm_nn(x, ws, f"fox{j}_in_small")
            qn, kn, vb, c, c_b = _fox_pre_fwd(hm, hs, gq2, gk2, bf, Hf, tr=TR, name=f"fox{j}_pre")
            c_rowp = _c_rows(c, Hf)
            o, lse_b = _fox_attn_fwd(qn, kn, vb, c_b, c_rowp, tb=ATTN_TILE, name=f"fox{j}_attn")
            on = _fox_post_fwd(o, hm, tr=TR, name=f"fox{j}_post")
            y = _mm_nn(on, wts["fox_w_out"][j], f"fox{j}_out")
            sv.update(wm=wm, ws=ws, gq2=gq2, gk2=gk2, bf=bf, hm=hm, hs=hs, qn=qn, kn=kn, vb=vb, c_b=c_b,
                      c_rowp=c_rowp, o=o, lse_b=lse_b, on=on)
        x_ln = _ln_fwd(x, y, wts["ln_g"][i], wts["ln_b"][i], alpha, tr=TR, name=f"ln{i}")
        gp = _mm_nn(x_ln, wts["ple_w_gate"][i], f"ple{i}_gate")
        pp = _mm_nn(p[i], wts["ple_w_proj"][i], f"ple{i}_proj")
        x_out = _ple_fwd(x_ln, gp, pp, tr=TR, name=f"ple{i}_mix")
        sv.update(y=y, x_ln=x_ln, gp=gp, pp=pp)
        saved.append(sv)
        x = x_out

    loss_row, dx = _loss_fwd_bwd(x, target, tr=TR, name="loss")

    g = {n: [None] * wts[n].shape[0] for n in wts}
    for i in reversed(range(L)):
        j = i // 2
        sv = saved[i]
        d_pre, d_pp = _ple_bwd(dx, sv["gp"], sv["pp"], tr=TR, name=f"ple{i}_mix_bwd")
        g["ple_w_gate"][i] = _mm_tn(sv["x_ln"], d_pre, f"ple{i}_gate_dw")
        g["ple_w_proj"][i] = _mm_tn(p[i], d_pp, f"ple{i}_proj_dw")
        t = _mm_nt(d_pre, wts["ple_w_gate"][i], f"ple{i}_gate_dx")
        du, g["ln_g"][i], g["ln_b"][i] = _ln_bwd(sv["x"], sv["y"], wts["ln_g"][i], wts["ln_b"][i], dx, t, alpha,
                                                 tr=TR, name=f"ln{i}_bwd")
        if i % 2 == 0:
            g["gdn_w_out"][j] = _mm_tn(sv["on"], du, f"gdn{j}_out_dw")
            d_on = _mm_nt(du, wts["gdn_w_out"][j], f"gdn{j}_out_dx")
            d_o, d_z, d_ng = _gdn_post_bwd(sv["o"], sv["hm"], wts["gdn_norm_g"][j], d_on, Hg, tr=TR, name=f"gdn{j}_post_bwd")
            dq, dk, dv, dbg = _gdn_rule_bwd(sv["q"], sv["k"], sv["v"], sv["bg"], sv["states"], d_o, Hg, name=f"gdn{j}_rule_bwd")
            d_hm, d_hs, d_cw, d_al, d_dtb = _gdn_pre_bwd(sv["hm"], sv["hs"], sv["cw8"], sv["alog"], sv["dtb"],
                                                         dq, dk, dv, dbg, d_z, Hg, tr=TR, name=f"gdn{j}_pre_bwd")
            g["gdn_norm_g"][j] = d_ng[0]
            g["gdn_conv_w"][j] = d_cw[:GDN_CONV]
            g["gdn_a_log"][j] = d_al[0, Hg:2 * Hg]
            g["gdn_dt_bias"][j] = d_dtb[0, Hg:2 * Hg]
            wname, nsmall, Wd = "gdn_w_in", 2 * Hg, Wg_
        else:
            g["fox_w_out"][j] = _mm_tn(sv["on"], du, f"fox{j}_out_dw")
            d_og = _mm_nt(du, wts["fox_w_out"][j], f"fox{j}_out_dx")
            d_o, d_z, delta_b = _fox_post_bwd(sv["o"], sv["hm"], d_og, tr=TR, name=f"fox{j}_post_bwd")
            dqn, dkn, dvv, dcr, dct = _fox_attn_bwd(sv["qn"], sv["kn"], sv["vb"], sv["c_b"], sv["c_rowp"], sv["lse_b"],
                                                    delta_b, d_o, tb=ATTN_TILE, name=f"fox{j}_attn_bwd")
            dc = _pad_lanes(dcr[:, :2, :].reshape(Hf, -1).T + dct[:, ::FOX_DH])
            d_hm, d_hs, d_gq, d_gk, d_bf = _fox_pre_bwd(sv["hm"], sv["hs"], sv["gq2"], sv["gk2"], sv["bf"],
                                                        dqn, dkn, dvv, d_z, dc, Hf, tr=TR, name=f"fox{j}_pre_bwd")
            g["fox_q_norm_g"][j] = d_gq[0, :FOX_DH] + d_gq[0, FOX_DH:]
            g["fox_k_norm_g"][j] = d_gk[0, :FOX_DH] + d_gk[0, FOX_DH:]
            g["fox_b_f"][j] = d_bf[0, :Hf]
            wname, nsmall, Wd = "fox_w_in", Hf, Wf_
        dwm = _mm_tn(sv["x"], d_hm, f"{wname}{j}_main_dw")
        dws = _mm_tn(sv["x"], d_hs, f"{wname}{j}_small_dw")
        g[wname][j] = jnp.concatenate([dwm, dws[:, :nsmall]], axis=1)
        t1 = _mm_nt(d_hs, sv["ws"], f"{wname}{j}_small_dx", add=du, add_scale=alpha)
        dx = _mm_nt(d_hm, sv["wm"], f"{wname}{j}_main_dx", add=t1)
        del Wd
    grads = {n: jnp.stack(v) for n, v in g.items()}
    return loss_row, dx, grads


_SHARDED = (("ple_w_gate", 1), ("ple_w_proj", 2), ("gdn_w_in", 2), ("gdn_conv_w", 2), ("gdn_w_out", 1),
            ("fox_w_in", 2), ("fox_w_out", 1))
_REPLICATED = ("ln_g", "ln_b", "gdn_a_log", "gdn_dt_bias", "gdn_norm_g", "fox_b_f", "fox_q_norm_g", "fox_k_norm_g")
_EXACT = ("gdn_conv_w",)
_ORDER = ("ln_g", "ln_b", "ple_w_gate", "ple_w_proj", "gdn_w_in", "gdn_conv_w", "gdn_a_log", "gdn_dt_bias",
          "gdn_norm_g", "gdn_w_out", "fox_w_in", "fox_b_f", "fox_q_norm_g", "fox_k_norm_g", "fox_w_out")


def _as_rows(a):
    return a.reshape(-1, LANES)


def _gather_weights(local):
    parts = []
    for name, _ in _SHARDED:
        w = local[name]
        wb = lax.bitcast_convert_type(w, BF16) if name in _EXACT else w.astype(BF16)
        parts.append(_as_rows(wb))
    packed = jnp.concatenate(parts, axis=0)
    got = _xy_exchange(packed, gather=True, name="gather_weights")
    full, r0 = {}, 0
    for (name, axis), part in zip(_SHARDED, parts):
        nrow = part.shape[0]
        seg = got[:, r0:r0 + nrow]
        r0 += nrow
        shp = local[name].shape
        if name in _EXACT:
            blocks = lax.bitcast_convert_type(seg.reshape((N_CHIPS,) + shp + (2,)), F32)
        else:
            blocks = seg.reshape((N_CHIPS,) + shp)
        full[name] = jnp.concatenate([blocks[s] for s in range(N_CHIPS)], axis=axis)
    return full


def _reduce_sharded(grads):
    per_owner = []
    sizes = []
    for s in range(N_CHIPS):
        parts = []
        for name, axis in _SHARDED:
            gfull = grads[name]
            n = gfull.shape[axis] // N_CHIPS
            blk = lax.slice_in_dim(gfull, s * n, (s + 1) * n, axis=axis)
            parts.append(_as_rows(blk.astype(BF16)))
        sizes = [q.shape[0] for q in parts]
        per_owner.append(jnp.concatenate(parts, axis=0))
    packed = jnp.stack(per_owner)
    got = _xy_exchange(packed, gather=False, name="exchange_grads")
    mine = _sum_slots(got, tr=4096, name="sum_grads")
    other = _c_swap(mine, name="swap_grads")
    out, r0 = {}, 0
    for (name, axis), nrow in zip(_SHARDED, sizes):
        shp = list(grads[name].shape)
        shp[axis] //= N_CHIPS
        out[name] = (mine[r0:r0 + nrow].reshape(shp), other[r0:r0 + nrow].reshape(shp))
        r0 += nrow
    return out


def _reduce_replicated(grads, loss_part):
    rows = [_pad_lanes(jnp.reshape(loss_part, (1, 1)))]
    for name in _REPLICATED:
        gr = grads[name]
        rows.append(_as_rows(gr) if gr.shape[-1] % LANES == 0 else _pad_lanes(gr))
    sizes = [r.shape[0] for r in rows]
    blk = jnp.concatenate(rows, axis=0)
    nrow = blk.shape[0]
    npad = -nrow % SUBLANES
    blk = jnp.pad(blk, ((0, npad), (0, 0)))
    allb = _all_gather8(blk, name="gather_small_grads").reshape(N_DEV, nrow + npad, LANES)
    tot = _sum_slots(allb, tr=nrow + npad, name="sum_small_grads")
    out, r0 = {}, sizes[0]
    loss = tot[0, 0]
    for name, n in zip(_REPLICATED, sizes[1:]):
        gr = grads[name]
        seg = tot[r0:r0 + n]
        out[name] = seg.reshape(gr.shape) if gr.shape[-1] % LANES == 0 else seg[:, :gr.shape[-1]]
        r0 += n
    return loss, out


def kernel(x, p, ln_g, ln_b, ple_w_gate, ple_w_proj, gdn_w_in, gdn_conv_w, gdn_a_log, gdn_dt_bias, gdn_norm_g, gdn_w_out, fox_w_in, fox_b_f, fox_q_norm_g, fox_k_norm_g, fox_w_out, loss_target, m_ln_g, m_ln_b, m_ple_w_gate, m_ple_w_proj, m_gdn_w_in, m_gdn_conv_w, m_gdn_a_log, m_gdn_dt_bias, m_gdn_norm_g, m_gdn_w_out, m_fox_w_in, m_fox_b_f, m_fox_q_norm_g, m_fox_k_norm_g, m_fox_w_out, v_ln_g, v_ln_b, v_ple_w_gate, v_ple_w_proj, v_gdn_w_in, v_gdn_conv_w, v_gdn_a_log, v_gdn_dt_bias, v_gdn_norm_g, v_gdn_w_out, v_fox_w_in, v_fox_b_f, v_fox_q_norm_g, v_fox_k_norm_g, v_fox_w_out):
    local = dict(ln_g=ln_g, ln_b=ln_b, ple_w_gate=ple_w_gate, ple_w_proj=ple_w_proj, gdn_w_in=gdn_w_in,
                 gdn_conv_w=gdn_conv_w, gdn_a_log=gdn_a_log, gdn_dt_bias=gdn_dt_bias, gdn_norm_g=gdn_norm_g,
                 gdn_w_out=gdn_w_out, fox_w_in=fox_w_in, fox_b_f=fox_b_f, fox_q_norm_g=fox_q_norm_g,
                 fox_k_norm_g=fox_k_norm_g, fox_w_out=fox_w_out)
    mom_m = dict(ln_g=m_ln_g, ln_b=m_ln_b, ple_w_gate=m_ple_w_gate, ple_w_proj=m_ple_w_proj, gdn_w_in=m_gdn_w_in,
                 gdn_conv_w=m_gdn_conv_w, gdn_a_log=m_gdn_a_log, gdn_dt_bias=m_gdn_dt_bias, gdn_norm_g=m_gdn_norm_g,
                 gdn_w_out=m_gdn_w_out, fox_w_in=m_fox_w_in, fox_b_f=m_fox_b_f, fox_q_norm_g=m_fox_q_norm_g,
                 fox_k_norm_g=m_fox_k_norm_g, fox_w_out=m_fox_w_out)
    mom_v = dict(ln_g=v_ln_g, ln_b=v_ln_b, ple_w_gate=v_ple_w_gate, ple_w_proj=v_ple_w_proj, gdn_w_in=v_gdn_w_in,
                 gdn_conv_w=v_gdn_conv_w, gdn_a_log=v_gdn_a_log, gdn_dt_bias=v_gdn_dt_bias, gdn_norm_g=v_gdn_norm_g,
                 gdn_w_out=v_gdn_w_out, fox_w_in=v_fox_w_in, fox_b_f=v_fox_b_f, fox_q_norm_g=v_fox_q_norm_g,
                 fox_k_norm_g=v_fox_k_norm_g, fox_w_out=v_fox_w_out)

    wts = dict(_gather_weights(local))
    for name in _REPLICATED:
        wts[name] = local[name]
    loss_row, dx, grads = _local_step(x[0], p[:, 0], loss_target[0], wts)
    loss, small = _reduce_replicated(grads, jnp.sum(loss_row))
    big = _reduce_sharded(grads)

    outs = {}
    for name in _ORDER:
        parts = list(big[name]) if name in big else [small[name]]
        outs[name] = _adamw(local[name], parts, mom_m[name], mom_v[name], name=f"adamw_{name}")
    return (loss, dx[None], *[outs[n][0] for n in _ORDER], *[outs[n][1] for n in _ORDER],
            *[outs[n][2] for n in _ORDER], *[outs[n][3] for n in _ORDER])
```

```python
import functools

import jax
import jax.numpy as jnp
from jax import lax
from jax.experimental import pallas as pl
from jax.experimental.pallas import tpu as pltpu

F32 = jnp.float32
BF16 = jnp.bfloat16

LANES = 128
SUBLANES = 8
VMEM_LIMIT_BYTES = 56 * 1024 * 1024

GDN_DK = 128
GDN_CHUNK = 64
GDN_CONV = 4
FOX_DH = 64
LN_EPS = 1e-5
RMS_EPS = 1e-6

ADAM_LR = 0.001
ADAM_B1 = 0.9
ADAM_B2 = 0.999
ADAM_EPS = 1e-08
ADAM_WD = 0.01
ADAM_STEP = 10

_DIMS = {"nn": (((1,), (0,)), ((), ())), "nt": (((1,), (1,)), ((), ())), "tn": (((0,), (0,)), ((), ()))}


def _params(*sem):
    return pltpu.CompilerParams(dimension_semantics=sem, vmem_limit_bytes=VMEM_LIMIT_BYTES)


def _split(a, terms):
    out = []
    rest = a.astype(F32)
    for t in range(terms):
        piece = rest.astype(BF16)
        out.append(piece)
        if t + 1 < terms:
            rest = rest - piece.astype(F32)
    return out


def _raw_mm(a, b, form, mode):
    dot = lambda x, y: lax.dot_general(x, y, _DIMS[form], preferred_element_type=F32)
    if mode == "b":
        return dot(a.astype(BF16), b.astype(BF16))
    if mode == "x3":
        (ah, al), (bh, bl) = _split(a, 2), _split(b, 2)
        return dot(ah, bh) + (dot(ah, bl) + dot(al, bh))
    if mode == "ca":
        ac = a.astype(BF16)
        b1, b2, b3 = _split(b, 3)
        return dot(ac, b1) + (dot(ac, b2) + dot(ac, b3))
    assert mode == "cb", mode
    bc = b.astype(BF16)
    a1, a2, a3 = _split(a, 3)
    return dot(a1, bc) + (dot(a2, bc) + dot(a3, bc))


@functools.partial(jax.custom_vjp, nondiff_argnums=(2, 3))
def _mm(a, b, form, mode):
    return _raw_mm(a, b, form, mode)


def _mm_fwd(a, b, form, mode):
    return _raw_mm(a, b, form, mode), (a, b)


def _mm_bwd(form, mode, res, g):
    a, b = res
    flip = {"b": "b", "x3": "x3", "ca": "cb", "cb": "ca"}[mode]
    if form == "nn":
        da, db = (lambda: _mm(g, b, "nt", mode)), (lambda: _mm(a, g, "tn", mode))
    elif form == "nt":
        da, db = (lambda: _mm(g, b, "nn", mode)), (lambda: _mm(g, a, "tn", flip))
    else:
        da, db = (lambda: _mm(b, g, "nt", flip)), (lambda: _mm(a, g, "nn", mode))
    return (jnp.zeros_like(a) if mode == "ca" else da()), (jnp.zeros_like(b) if mode == "cb" else db())


_mm.defvjp(_mm_fwd, _mm_bwd)


@jax.custom_vjp
def _tri_inv(Ls):
    C = Ls[0].shape[0]
    eye = (_iota((C, C), 0) == _iota((C, C), 1)).astype(F32)
    X = [eye - L for L in Ls]
    P = [_raw_mm(L, L, "nn", "x3") for L in Ls]
    n_sq = max(1, (C - 1).bit_length() - 1)
    for it in range(n_sq):
        XP = [_raw_mm(x, p, "nn", "x3") for x, p in zip(X, P)]
        if it < n_sq - 1:
            P = [_raw_mm(p, p, "nn", "x3") for p in P]
        X = [x + xp for x, xp in zip(X, XP)]
    return X


def _tri_inv_fwd(Ls):
    Ts = _tri_inv(Ls)
    return Ts, Ts


def _tri_inv_bwd(Ts, dTs):
    Ms = [_raw_mm(dT, T, "nt", "x3") for dT, T in zip(dTs, Ts)]
    return ([-_raw_mm(T, M, "tn", "x3") for T, M in zip(Ts, Ms)],)


_tri_inv.defvjp(_tri_inv_fwd, _tri_inv_bwd)


def _silu(x):
    return x * jax.nn.sigmoid(x)


def _softplus(x):
    return jnp.maximum(x, 0.0) + jnp.log1p(jnp.exp(-jnp.abs(x)))


def _iota(shape, dim):
    return lax.broadcasted_iota(jnp.int32, shape, dim)


def _matmul(a, b, *, ta=False, tb=False, out_dtype=F32, add=None, add_scale=1.0, tm=512, tn=512, tk=512, name):
    if ta:
        K, M = a.shape
    else:
        M, K = a.shape
    if tb:
        N, K2 = b.shape
    else:
        K2, N = b.shape
    assert K == K2, (a.shape, b.shape, ta, tb)
    tm, tn, tk = min(tm, M), min(tn, N), min(tk, K)
    assert M % tm == 0 and N % tn == 0 and K % tk == 0, (M, N, K, tm, tn, tk)
    nk = K // tk
    form = ("t" if ta else "n") + ("t" if tb else "n")
    dims = (((0 if ta else 1,), (1 if tb else 0,)), ((), ()))
    del form
    a_spec = pl.BlockSpec((tk, tm), lambda i, j, k: (k, i)) if ta else pl.BlockSpec((tm, tk), lambda i, j, k: (i, k))
    b_spec = pl.BlockSpec((tn, tk), lambda i, j, k: (j, k)) if tb else pl.BlockSpec((tk, tn), lambda i, j, k: (k, j))
    o_spec = pl.BlockSpec((tm, tn), lambda i, j, k: (i, j))
    has_add = add is not None

    def body(*refs):
        a_ref, b_ref = refs[:2]
        add_ref = refs[2] if has_add else None
        o_ref = refs[3] if has_add else refs[2]
        acc_ref = refs[-1] if nk > 1 else None
        k = pl.program_id(2)
        part = lax.dot_general(a_ref[...].astype(BF16), b_ref[...].astype(BF16), dims, preferred_element_type=F32)

        def finish(total):
            if has_add:
                total = total + add_scale * add_ref[...].astype(F32)
            o_ref[...] = total.astype(o_ref.dtype)

        if nk == 1:
            finish(part)
        else:
            @pl.when(k == 0)
            def _():
                acc_ref[...] = part

            @pl.when(jnp.logical_and(k > 0, k < nk - 1))
            def _():
                acc_ref[...] += part

            @pl.when(k == nk - 1)
            def _():
                finish(acc_ref[...] + part)

    in_specs = [a_spec, b_spec] + ([o_spec] if has_add else [])
    args = (a, b) + ((add,) if has_add else ())
    return pl.pallas_call(
        body, name=name, grid=(M // tm, N // tn, nk),
        in_specs=in_specs, out_specs=o_spec,
        out_shape=jax.ShapeDtypeStruct((M, N), out_dtype),
        scratch_shapes=[pltpu.VMEM((tm, tn), F32)] if nk > 1 else [],
        compiler_params=_params("parallel", "parallel", "arbitrary"),
    )(*args)


def _rowcall(body_fn, rows, consts, out_rows, out_accs, *, tr, name, reverse=False, scratch=()):
    def arr_spec(r):
        return r if isinstance(r, tuple) else (r, None)

    S = arr_spec(rows[0])[0].shape[0]
    tr = min(tr, S)
    assert S % tr == 0
    n = S // tr
    ridx = (lambda i: (n - 1 - i, 0)) if reverse else (lambda i: (i, 0))
    in_specs, args = [], []
    for r in rows:
        arr, spec = arr_spec(r)
        args.append(arr)
        in_specs.append(spec(tr, n) if spec is not None else pl.BlockSpec((tr, arr.shape[1]), ridx))
    for c in consts:
        args.append(c)
        in_specs.append(pl.BlockSpec(c.shape, lambda i: (0, 0)))
    out_specs, out_shape = [], []
    for (ncol, dt) in out_rows:
        out_specs.append(pl.BlockSpec((tr, ncol), ridx))
        out_shape.append(jax.ShapeDtypeStruct((S, ncol), dt))
    for shp in out_accs:
        out_specs.append(pl.BlockSpec(shp, lambda i: (0, 0)))
        out_shape.append(jax.ShapeDtypeStruct(shp, F32))
    nr, nc, no, na = len(rows), len(consts), len(out_rows), len(out_accs)

    def kernel(*refs):
        row_refs = refs[:nr]
        const_refs = refs[nr:nr + nc]
        orow_refs = refs[nr + nc:nr + nc + no]
        oacc_refs = refs[nr + nc + no:nr + nc + no + na]
        scr = refs[nr + nc + no + na:]
        step = pl.program_id(0)
        blk = (n - 1 - step) if reverse else step

        @pl.when(step == 0)
        def _():
            for acc in oacc_refs:
                acc[...] = jnp.zeros(acc.shape, F32)

        body_fn(row_refs, const_refs, orow_refs, oacc_refs, scr, step, blk)

    outs = pl.pallas_call(
        kernel, name=name, grid=(n,), in_specs=in_specs, out_specs=out_specs, out_shape=out_shape,
        scratch_shapes=list(scratch), compiler_params=_params("arbitrary"),
    )(*args)
    return outs


def _row(v):
    return v.astype(F32).reshape(1, -1)


def _pad_lanes(v, width=LANES, offset=0):
    pad = [(0, 0)] * (v.ndim - 1) + [(offset, width - offset - v.shape[-1])]
    return jnp.pad(v, pad)


def _ln_tile(x, y, g, b, alpha):
    u = alpha * x + y
    mu = jnp.mean(u, -1, keepdims=True)
    d = u - mu
    var = jnp.mean(d * d, -1, keepdims=True)
    return d * lax.rsqrt(var + LN_EPS) * g + b


def _ln_fwd(x, y, g, b, alpha, *, tr, name):
    D = x.shape[1]

    def body(rows, consts, orows, oaccs, scr, step, blk):
        orows[0][...] = _ln_tile(rows[0][...], rows[1][...], consts[0][...], consts[1][...], alpha)

    return _rowcall(body, [x, y], [_row(g), _row(b)], [(D, F32)], [], tr=tr, name=name)[0]


def _ln_bwd(x, y, g, b, dxo, t, alpha, *, tr, name):
    D = x.shape[1]

    def body(rows, consts, orows, oaccs, scr, step, blk):
        xv, yv = rows[0][...], rows[1][...]
        ct = rows[2][...] + rows[3][...]
        _, vjp = jax.vjp(lambda yy, gg, bb: _ln_tile(xv, yy, gg, bb, alpha), yv, consts[0][...], consts[1][...])
        du, dg, db = vjp(ct)
        orows[0][...] = du
        oaccs[0][...] += dg
        oaccs[1][...] += db

    du, dg, db = _rowcall(body, [x, y, dxo, t], [_row(g), _row(b)], [(D, F32)], [(1, D), (1, D)], tr=tr, name=name)
    return du, dg[0], db[0]


def _ple_fwd(x_ln, gp, pp, *, tr, name):
    D = x_ln.shape[1]

    def body(rows, consts, orows, oaccs, scr, step, blk):
        orows[0][...] = rows[0][...] + jax.nn.sigmoid(rows[1][...]) * rows[2][...]

    return _rowcall(body, [x_ln, gp, pp], [], [(D, F32)], [], tr=tr, name=name)[0]


def _ple_bwd(dxo, gp, pp, *, tr, name):
    D = dxo.shape[1]

    def body(rows, consts, orows, oaccs, scr, step, blk):
        d = rows[0][...]
        s = jax.nn.sigmoid(rows[1][...])
        orows[0][...] = (d * rows[2][...] * s * (1.0 - s)).astype(BF16)
        orows[1][...] = (d * s).astype(BF16)

    return _rowcall(body, [dxo, gp, pp], [], [(D, BF16), (D, BF16)], [], tr=tr, name=name)


def _loss_fwd_bwd(xf, target, *, tr, name):
    D = xf.shape[1]

    def body(rows, consts, orows, oaccs, scr, step, blk):
        err = rows[0][...] - rows[1][...]
        orows[0][...] = err * (1.0 / D)
        part = jnp.sum(err * err, axis=0, keepdims=True) * (0.5 / D)
        oaccs[0][...] += part

    dx, lrow = _rowcall(body, [xf, target], [], [(D, F32)], [(1, D)], tr=tr, name=name)
    return lrow, dx


def _gdn_qk_tile(c):
    y = _silu(c)
    return y * lax.rsqrt(jnp.sum(y * y, -1, keepdims=True) + RMS_EPS)


def _make_bg_fn(H):
    def fn(hs, alog, dtb):
        lane = _iota((1, LANES), 1)
        beta = jax.nn.sigmoid(hs)
        g = -jnp.exp(alog) * _softplus(hs + dtb)
        return jnp.where(lane < H, beta, jnp.where(lane < 2 * H, g, 0.0))
    return fn


def _halo_spec(ncol):
    def make(tr, n):
        per = tr // SUBLANES
        return pl.BlockSpec((SUBLANES, ncol), lambda i: (jnp.maximum(i * per - 1, 0), 0))
    return make


def _halo_spec_rev(ncol):
    def make(tr, n):
        per = tr // SUBLANES
        return pl.BlockSpec((SUBLANES, ncol), lambda i: (jnp.maximum((n - 1 - i) * per - 1, 0), 0))
    return make


def _gdn_pre_fwd(h_main, h_small, conv_w8, alog_row, dtb_row, H, *, tr, name):
    W = H * GDN_DK
    C3 = 3 * W
    bg_fn = _make_bg_fn(H)

    def body(rows, consts, orows, oaccs, scr, step, blk):
        main_ref, halo_ref, hs_ref = rows
        w_ref, alog_ref, dtb_ref = consts
        q_ref, k_ref, v_ref, bg_ref = orows
        xs = scr[0]
        trr = main_ref.shape[0]
        xs[pl.ds(SUBLANES, trr), :] = main_ref[...]
        xs[pl.ds(0, SUBLANES), :] = jnp.where(blk > 0, halo_ref[...], 0.0)
        for s in range(C3 // LANES):
            ls = slice(s * LANES, (s + 1) * LANES)
            c = jnp.zeros((trr, LANES), F32)
            for j in range(GDN_CONV):
                c = c + w_ref[GDN_CONV - 1 - j:GDN_CONV - j, ls] * xs[pl.ds(SUBLANES - j, trr), ls]
            if s < 2 * H:
                out = _gdn_qk_tile(c)
                (q_ref if s < H else k_ref)[:, (s % H) * LANES:(s % H + 1) * LANES] = out
            else:
                v_ref[:, (s - 2 * H) * LANES:(s - 2 * H + 1) * LANES] = _silu(c)
        bg_ref[...] = bg_fn(hs_ref[...], alog_ref[...], dtb_ref[...])

    main = (h_main, lambda tr_, n: pl.BlockSpec((tr_, C3), lambda i: (i, 0)))
    halo = (h_main, _halo_spec(C3))
    trr = min(tr, h_main.shape[0])
    return _rowcall(body, [main, halo, h_small], [conv_w8, alog_row, dtb_row],
                    [(W, F32), (W, F32), (W, F32), (LANES, F32)], [], tr=tr, name=name,
                    scratch=[pltpu.VMEM((trr + SUBLANES, C3), F32)])


def _gdn_pre_bwd(h_main, h_small, conv_w8, alog_row, dtb_row, dq, dk, dv, dbg, dz, H, *, tr, name):
    W = H * GDN_DK
    C3 = 3 * W
    bg_fn = _make_bg_fn(H)

    def body(rows, consts, orows, oaccs, scr, step, blk):
        main_ref, halo_ref, hs_ref, dq_ref, dk_ref, dv_ref, dbg_ref, dz_ref = rows
        w_ref, alog_ref, dtb_ref = consts
        dmain_ref, dhs_ref = orows
        dw_ref, dalog_ref, ddtb_ref = oaccs
        xs, dcs = scr
        trr = main_ref.shape[0]
        xs[pl.ds(SUBLANES, trr), :] = main_ref[...]
        xs[pl.ds(0, SUBLANES), :] = jnp.where(blk > 0, halo_ref[...], 0.0)

        @pl.when(step == 0)
        def _():
            dcs[pl.ds(trr, SUBLANES), :] = jnp.zeros((SUBLANES, C3), F32)

        for s in range(C3 // LANES):
            ls = slice(s * LANES, (s + 1) * LANES)
            c = jnp.zeros((trr, LANES), F32)
            for j in range(GDN_CONV):
                c = c + w_ref[GDN_CONV - 1 - j:GDN_CONV - j, ls] * xs[pl.ds(SUBLANES - j, trr), ls]
            if s < 2 * H:
                src = dq_ref if s < H else dk_ref
                ct = src[:, (s % H) * LANES:(s % H + 1) * LANES]
                _, vjp = jax.vjp(_gdn_qk_tile, c)
            else:
                ct = dv_ref[:, (s - 2 * H) * LANES:(s - 2 * H + 1) * LANES]
                _, vjp = jax.vjp(_silu, c)
            dcs[pl.ds(0, trr), ls] = vjp(ct)[0]
        for s in range(C3 // LANES):
            ls = slice(s * LANES, (s + 1) * LANES)
            dx = jnp.zeros((trr, LANES), F32)
            dc0 = dcs[pl.ds(0, trr), ls]
            for j in range(GDN_CONV):
                wrow = w_ref[GDN_CONV - 1 - j:GDN_CONV - j, ls]
                dx = dx + wrow * dcs[pl.ds(j, trr), ls]
                dw_ref[GDN_CONV - 1 - j:GDN_CONV - j, ls] += jnp.sum(dc0 * xs[pl.ds(SUBLANES - j, trr), ls], axis=0, keepdims=True)
            dmain_ref[:, ls] = dx.astype(BF16)
        dmain_ref[:, C3:] = dz_ref[...]
        dcs[pl.ds(trr, SUBLANES), :] = dcs[pl.ds(0, SUBLANES), :]
        _, vjp = jax.vjp(bg_fn, hs_ref[...], alog_ref[...], dtb_ref[...])
        dhs, dalog, ddtb = vjp(dbg_ref[...])
        dhs_ref[...] = dhs.astype(BF16)
        dalog_ref[...] += dalog
        ddtb_ref[...] += ddtb

    trr = min(tr, h_main.shape[0])
    main = (h_main, lambda tr_, n: pl.BlockSpec((tr_, C3), lambda i: (n - 1 - i, 0)))
    halo = (h_main, _halo_spec_rev(C3))
    return _rowcall(body, [main, halo, h_small, dq, dk, dv, dbg, dz], [conv_w8, alog_row, dtb_row],
                    [(4 * W, BF16), (LANES, BF16)], [(SUBLANES, C3), (1, LANES), (1, LANES)],
                    tr=tr, name=name, reverse=True,
                    scratch=[pltpu.VMEM((trr + SUBLANES, C3), F32), pltpu.VMEM((trr + SUBLANES, C3), F32)])


def _gdn_chunk(qs, ks, vs, betas, gs, Ss):
    C, dk = qs[0].shape
    dv = vs[0].shape[1]
    ri, ci = _iota((C, C), 0), _iota((C, C), 1)
    causal, strict = ri >= ci, ri > ci
    tril = causal.astype(F32)
    lane0 = (_iota((1, LANES), 1) == 0).astype(F32)
    e0 = jnp.ones((C, 1), F32) * lane0
    last = (_iota((C, 1), 0) == C - 1).astype(F32)

    def each(f, *lists):
        return [f(*a) for a in zip(*lists)]

    G = each(lambda g: g * jnp.ones((1, LANES), F32), gs)
    gcB = each(lambda x: _mm(tril, x, "nn", "ca"), G)
    gc = each(lambda x: jnp.sum(x * lane0, -1, keepdims=True), gcB)
    gc_row = each(lambda x: _mm(e0, x, "nt", "ca"), gcB)
    decay = each(lambda a, b: jnp.where(causal, jnp.exp(jnp.where(causal, a - b, 0.0)), 0.0), gc, gc_row)
    kb = each(lambda k, b: k * b, ks, betas)
    kk = each(lambda a, k: _mm(a, k, "nt", "b"), kb, ks)
    L = each(lambda a, d: jnp.where(strict, a * d, 0.0), kk, decay)
    X = _tri_inv(L)
    egc = each(jnp.exp, gc)
    u = each(lambda x, v, b: _mm(x, v * b, "nn", "x3"), X, vs, betas)
    w = each(lambda x, a, e: _mm(x, a * e, "nn", "x3"), X, kb, egc)
    qsc = each(lambda q: q * (dk ** -0.5), qs)
    qk = each(lambda q, k: _mm(q, k, "nt", "b"), qsc, ks)
    A = each(lambda a, d: jnp.where(causal, a * d, 0.0), qk, decay)
    q_dec = each(lambda q, e: q * e, qsc, egc)
    gl = each(lambda x: jnp.sum(x * last, keepdims=True), gc)
    k_dec = each(lambda k, a, b: k * jnp.exp(a - b), ks, gl, gc)
    wS = each(lambda a, s: _mm(a, s, "nn", "b"), w, Ss)
    qS = each(lambda a, s: _mm(a, s, "nn", "b"), q_dec, Ss)
    v_new = each(lambda a, b: a - b, u, wS)
    Av = each(lambda a, b: _mm(a, b, "nn", "b"), A, v_new)
    kv = each(lambda a, b: _mm(a, b, "tn", "b"), k_dec, v_new)
    o = each(lambda a, b: a + b, qS, Av)
    S_new = each(lambda s, e, x: s * jnp.exp(e) + x, Ss, gl, kv)
    return o, S_new


def _gdn_rule_fwd(q, k, v, bg, H, *, name):
    S_len = q.shape[0]
    C = min(GDN_CHUNK, S_len)
    N = S_len // C
    dk = dv = GDN_DK

    def body(q_ref, k_ref, v_ref, bg_ref, o_ref, st_ref, s_scr):
        n = pl.program_id(0)

        @pl.when(n == 0)
        def _():
            s_scr[...] = jnp.zeros(s_scr.shape, F32)

        bgt = bg_ref[...]
        sl = [slice(h * dk, (h + 1) * dk) for h in range(H)]
        Ss = [s_scr[h] for h in range(H)]
        for h in range(H):
            st_ref[h] = Ss[h]
        os_, S_new = _gdn_chunk([q_ref[:, s] for s in sl], [k_ref[:, s] for s in sl], [v_ref[:, s] for s in sl],
                                [bgt[:, h:h + 1] for h in range(H)], [bgt[:, H + h:H + h + 1] for h in range(H)], Ss)
        for h in range(H):
            o_ref[:, sl[h]] = os_[h]
            s_scr[h] = S_new[h]

    rows = pl.BlockSpec((C, H * dk), lambda n: (n, 0))
    return pl.pallas_call(
        body, name=name, grid=(N,),
        in_specs=[rows, rows, rows, pl.BlockSpec((C, LANES), lambda n: (n, 0))],
        out_specs=[rows, pl.BlockSpec((H, dk, dv), lambda n: (n, 0, 0))],
        out_shape=[jax.ShapeDtypeStruct((S_len, H * dv), F32), jax.ShapeDtypeStruct((N * H, dk, dv), F32)],
        scratch_shapes=[pltpu.VMEM((H, dk, dv), F32)],
        compiler_params=_params("arbitrary"),
    )(q, k, v, bg)


def _gdn_rule_bwd(q, k, v, bg, states, do, H, *, name):
    S_len = q.shape[0]
    C = min(GDN_CHUNK, S_len)
    N = S_len // C
    dk = dv = GDN_DK

    def body(q_ref, k_ref, v_ref, bg_ref, st_ref, do_ref, dq_ref, dk_ref, dv_ref, dbg_ref, ds_scr):
        step = pl.program_id(0)

        @pl.when(step == 0)
        def _():
            ds_scr[...] = jnp.zeros(ds_scr.shape, F32)

        bgt = bg_ref[...]
        lane = _iota((1, LANES), 1)
        dbg = jnp.zeros((C, LANES), F32)
        sl = [slice(h * dk, (h + 1) * dk) for h in range(H)]
        _, vjp = jax.vjp(_gdn_chunk, [q_ref[:, s] for s in sl], [k_ref[:, s] for s in sl], [v_ref[:, s] for s in sl],
                         [bgt[:, h:h + 1] for h in range(H)], [bgt[:, H + h:H + h + 1] for h in range(H)],
                         [st_ref[h] for h in range(H)])
        dq, dkk, dvv, dbeta, dg, dS = vjp(([do_ref[:, s] for s in sl], [ds_scr[h] for h in range(H)]))
        for h in range(H):
            dq_ref[:, sl[h]] = dq[h]
            dk_ref[:, sl[h]] = dkk[h]
            dv_ref[:, sl[h]] = dvv[h]
            dbg = dbg + jnp.where(lane == h, dbeta[h], 0.0) + jnp.where(lane == h + H, dg[h], 0.0)
            ds_scr[h] = dS[h]
        dbg_ref[...] = dbg

    rows = pl.BlockSpec((C, H * dk), lambda s: (N - 1 - s, 0))
    bgs = pl.BlockSpec((C, LANES), lambda s: (N - 1 - s, 0))
    return pl.pallas_call(
        body, name=name, grid=(N,),
        in_specs=[rows, rows, rows, bgs, pl.BlockSpec((H, dk, dv), lambda s: (N - 1 - s, 0, 0)), rows],
        out_specs=[rows, rows, rows, bgs],
        out_shape=[jax.ShapeDtypeStruct((S_len, H * dk), F32)] * 3 + [jax.ShapeDtypeStruct((S_len, LANES), F32)],
        scratch_shapes=[pltpu.VMEM((H, dk, dv), F32)],
        compiler_params=_params("arbitrary"),
    )(q, k, v, bg, states, do)


def _gdn_post_tile(o, z, g):
    return o * lax.rsqrt(jnp.mean(o * o, -1, keepdims=True) + RMS_EPS) * g * _silu(z)


def _gdn_post_fwd(o, h_main, norm_g, H, *, tr, name):
    W = H * GDN_DK

    def body(rows, consts, orows, oaccs, scr, step, blk):
        for h in range(H):
            ls = slice(h * LANES, (h + 1) * LANES)
            orows[0][:, ls] = _gdn_post_tile(rows[0][:, ls], rows[1][:, ls], consts[0][...]).astype(BF16)

    z = (h_main, lambda tr_, n: pl.BlockSpec((tr_, W), lambda i: (i, 3)))
    return _rowcall(body, [o, z], [_row(norm_g)], [(W, BF16)], [], tr=tr, name=name)[0]


def _gdn_post_bwd(o, h_main, norm_g, d_on, H, *, tr, name):
    W = H * GDN_DK

    def body(rows, consts, orows, oaccs, scr, step, blk):
        for h in range(H):
            ls = slice(h * LANES, (h + 1) * LANES)
            _, vjp = jax.vjp(_gdn_post_tile, rows[0][:, ls], rows[1][:, ls], consts[0][...])
            d_o, d_z, d_g = vjp(rows[2][:, ls])
            orows[0][:, ls] = d_o
            orows[1][:, ls] = d_z.astype(BF16)
            oaccs[0][...] += d_g

    z = (h_main, lambda tr_, n: pl.BlockSpec((tr_, W), lambda i: (i, 3)))
    return _rowcall(body, [o, z, d_on], [_row(norm_g)], [(W, F32), (W, BF16)], [(1, LANES)], tr=tr, name=name)


def _seg_ones():
    ri, ci = _iota((LANES, LANES), 0), _iota((LANES, LANES), 1)
    return ((ri < FOX_DH) == (ci < FOX_DH)).astype(F32)


def _fox_qk_tile(x, g2):
    ms = _mm(x * x, _seg_ones(), "nn", "cb") * (1.0 / FOX_DH)
    return x * lax.rsqrt(ms + RMS_EPS) * g2


def _make_lf_fn(Hf):
    def fn(hs, bf):
        lane = _iota((1, LANES), 1)
        return jnp.where(lane < Hf, -_softplus(-(hs + bf)), 0.0)
    return fn


def _fox_pre_fwd(h_main, h_small, gq2, gk2, bf_row, Hf, *, tr, name):
    W = Hf * FOX_DH
    lf_fn = _make_lf_fn(Hf)

    def body(rows, consts, orows, oaccs, scr, step, blk):
        qk_ref, v_ref, hs_ref = rows
        gq_ref, gk_ref, bf_ref = consts
        qn_ref, kn_ref, vb_ref, c_ref, cb_ref = orows
        carry = scr[0]
        trr = qk_ref.shape[0]

        @pl.when(step == 0)
        def _():
            carry[...] = jnp.zeros(carry.shape, F32)

        for s in range(W // LANES):
            ls = slice(s * LANES, (s + 1) * LANES)
            qn_ref[:, ls] = _fox_qk_tile(qk_ref[:, ls], gq_ref[...]).astype(BF16)
            kn_ref[:, ls] = _fox_qk_tile(qk_ref[:, W + s * LANES:W + (s + 1) * LANES], gk_ref[...]).astype(BF16)
        vb_ref[...] = v_ref[...].astype(BF16)
        lf = lf_fn(hs_ref[...], bf_ref[...])
        tril = (_iota((trr, trr), 0) >= _iota((trr, trr), 1)).astype(F32)
        c = _raw_mm(tril, lf, "nn", "ca") + carry[0:1, :]
        c_ref[...] = c
        carry[0:1, :] = c[trr - 1:trr, :]
        col = _iota((LANES, 2 * W), 1)
        parity = (col >= W).astype(jnp.int32)
        slab = jnp.right_shift(col - parity * W, 7)
        expand = (2 * slab + parity == _iota((LANES, 2 * W), 0)).astype(F32)
        cb_ref[...] = _raw_mm(c, expand, "nn", "cb")

    qk = (h_main, lambda tr_, n: pl.BlockSpec((tr_, 2 * W), lambda i: (i, 0)))
    vv = (h_main, lambda tr_, n: pl.BlockSpec((tr_, W), lambda i: (i, 2)))
    return _rowcall(body, [qk, vv, h_small], [gq2, gk2, bf_row],
                    [(W, BF16), (W, BF16), (W, BF16), (LANES, F32), (2 * W, F32)], [], tr=tr, name=name,
                    scratch=[pltpu.VMEM((SUBLANES, LANES), F32)])


def _fox_pre_bwd(h_main, h_small, gq2, gk2, bf_row, dqn, dkn, dvv, dz, dc, Hf, *, tr, name):
    W = Hf * FOX_DH
    lf_fn = _make_lf_fn(Hf)

    def body(rows, consts, orows, oaccs, scr, step, blk):
        qk_ref, hs_ref, dqn_ref, dkn_ref, dvv_ref, dz_ref, dc_ref = rows
        gq_ref, gk_ref, bf_ref = consts
        dmain_ref, dhs_ref = orows
        dgq_ref, dgk_ref, dbf_ref = oaccs
        carry = scr[0]
        trr = qk_ref.shape[0]

        @pl.when(step == 0)
        def _():
            carry[...] = jnp.zeros(carry.shape, F32)

        for s in range(W // LANES):
            ls = slice(s * LANES, (s + 1) * LANES)
            lk = slice(W + s * LANES, W + (s + 1) * LANES)
            _, vjp = jax.vjp(_fox_qk_tile, qk_ref[:, ls], gq_ref[...])
            dx, dg = vjp(dqn_ref[:, ls])
            dmain_ref[:, ls] = dx.astype(BF16)
            dgq_ref[...] += dg
            _, vjp = jax.vjp(_fox_qk_tile, qk_ref[:, lk], gk_ref[...])
            dx, dg = vjp(dkn_ref[:, ls])
            dmain_ref[:, lk] = dx.astype(BF16)
            dgk_ref[...] += dg
        dmain_ref[:, 2 * W:3 * W] = dvv_ref[...].astype(BF16)
        dmain_ref[:, 3 * W:] = dz_ref[...]
        dcv = dc_ref[...]
        triu = (_iota((trr, trr), 0) <= _iota((trr, trr), 1)).astype(F32)
        dlf = _raw_mm(triu, dcv, "nn", "ca") + carry[0:1, :]
        carry[0:1, :] = dlf[0:1, :]
        _, vjp = jax.vjp(lf_fn, hs_ref[...], bf_ref[...])
        dhs, dbf = vjp(dlf)
        dhs_ref[...] = dhs.astype(BF16)
        dbf_ref[...] += dbf

    qk = (h_main, lambda tr_, n: pl.BlockSpec((tr_, 2 * W), lambda i: (n - 1 - i, 0)))
    return _rowcall(body, [qk, h_small, dqn, dkn, dvv, dz, dc], [gq2, gk2, bf_row],
                    [(4 * W, BF16), (LANES, BF16)], [(1, LANES), (1, LANES), (1, LANES)],
                    tr=tr, name=name, reverse=True, scratch=[pltpu.VMEM((SUBLANES, LANES), F32)])


def _fox_attn_fwd(qn, kn, vb, c_b, c_rowp, *, tb, name):
    S_len, W = qn.shape
    HP = W // LANES
    tb = min(tb, S_len)
    nb = S_len // tb
    scale = FOX_DH ** -0.5
    rb, cb = min(ATTN_ROWS, tb), min(ATTN_COLS, tb)
    nblk = 2 * (tb // rb)

    steps = [(i, j) for i in range(nb) for j in range(i + 1)]
    ti = jnp.asarray([s[0] for s in steps], jnp.int32)
    tj = jnp.asarray([s[1] for s in steps], jnp.int32)

    def body(ti_ref, tj_ref, q_ref, k_ref, v_ref, cb0_ref, cb1_ref, cr_ref, o_ref, lse_ref,
             m_scr, l_scr, acc_scr, s_scr, p_scr, a_scr):
        t = pl.program_id(1)
        i, j = ti_ref[t], tj_ref[t]

        @pl.when(j == 0)
        def _():
            m_scr[...] = jnp.full(m_scr.shape, -jnp.inf, F32)
            l_scr[...] = jnp.zeros(l_scr.shape, F32)
            acc_scr[...] = jnp.zeros(acc_scr.shape, F32)

        def compute(diag):
            lo = _iota((1, LANES), 1) < FOX_DH
            v = v_ref[...]
            lane = _iota((1, LANES), 1)
            blocks = [(hh, r) for hh in range(2) for r in range(tb // rb)]
            masks = [lo, jnp.logical_not(lo)]

            def visible(r):
                return ((r + 1) * rb - 1) // LANES + 1 if diag else tb // LANES

            for b, (hh, r) in enumerate(blocks):
                rows = slice(r * rb, (r + 1) * rb)
                qr = q_ref[rows, :]
                qh = jnp.where(masks[hh], qr * scale, jnp.zeros_like(qr))
                ctb = (cb0_ref if hh == 0 else cb1_ref)[rows, :]
                mx = None
                for c in range(tb // cb):
                    if c * cb // LANES >= visible(r):
                        continue
                    s2 = lax.dot_general(qh, k_ref[c * cb:(c + 1) * cb, :], _DIMS["nt"], preferred_element_type=F32)
                    for piece in range(c * cb // LANES, min((c + 1) * cb // LANES, visible(r))):
                        cols = slice(piece * LANES, (piece + 1) * LANES)
                        s = s2[:, piece * LANES - c * cb:(piece + 1) * LANES - c * cb] + ctb - cr_ref[hh:hh + 1, cols]
                        if diag and (piece + 1) * LANES - 1 > r * rb:
                            keep = piece * LANES + _iota((rb, LANES), 1) <= r * rb + _iota((rb, LANES), 0)
                            s = jnp.where(keep, s, -jnp.inf)
                        s_scr[b, :, cols] = s
                        mx = s if mx is None else jnp.maximum(mx, s)
                m_prev = m_scr[hh, rows, :]
                m_new = jnp.maximum(m_prev, jnp.broadcast_to(jnp.max(mx, -1, keepdims=True), (rb, LANES)))
                a_scr[b] = jnp.exp(m_prev - m_new)
                m_scr[hh, rows, :] = m_new
            for b, (hh, r) in enumerate(blocks):
                m_new = m_scr[hh, r * rb:(r + 1) * rb, :]
                for piece in range(visible(r)):
                    cols = slice(piece * LANES, (piece + 1) * LANES)
                    p_scr[b, :, cols] = jnp.exp(s_scr[b, :, cols] - m_new).astype(BF16)
            for b, (hh, r) in enumerate(blocks):
                rows = slice(r * rb, (r + 1) * rb)
                nkv = visible(r) * LANES
                pb = p_scr[b, :, :nkv]
                spare = (1 - hh) * FOX_DH
                vh = jnp.where(masks[hh], v[:nkv], (lane == spare).astype(BF16))
                pv = lax.dot_general(pb, vh, _DIMS["nn"], preferred_element_type=F32)
                psum = jnp.broadcast_to(pv[:, spare:spare + 1], (rb, LANES))
                alpha = a_scr[b]
                l_scr[hh, rows, :] = alpha * l_scr[hh, rows, :] + psum
                acc = acc_scr[rows, :]
                acc_scr[rows, :] = jnp.where(masks[hh], acc * alpha + pv, acc)

        @pl.when(j < i)
        def _():
            compute(False)

        @pl.when(j == i)
        def _():
            compute(True)
            lo = _iota((1, LANES), 1) < FOX_DH
            o_ref[...] = acc_scr[...] / jnp.where(lo, l_scr[0], l_scr[1])
            lse_ref[...] = m_scr[...] + jnp.log(l_scr[...])

    qs = pl.BlockSpec((tb, LANES), lambda h, t, ti_, tj_: (ti_[t], h))
    qs1 = pl.BlockSpec((tb, LANES), lambda h, t, ti_, tj_: (ti_[t], HP + h))
    ks = pl.BlockSpec((tb, LANES), lambda h, t, ti_, tj_: (tj_[t], h))
    crs = pl.BlockSpec((None, SUBLANES, tb), lambda h, t, ti_, tj_: (h, 0, tj_[t]))
    return pl.pallas_call(
        body, name=name,
        grid_spec=pltpu.PrefetchScalarGridSpec(
            num_scalar_prefetch=2, grid=(HP, len(steps)),
            in_specs=[qs, ks, ks, qs, qs1, crs],
            out_specs=[qs, pl.BlockSpec((2, tb, LANES), lambda h, t, ti_, tj_: (0, ti_[t], h))],
            scratch_shapes=[pltpu.VMEM((2, tb, LANES), F32), pltpu.VMEM((2, tb, LANES), F32),
                            pltpu.VMEM((tb, LANES), F32), pltpu.VMEM((nblk, rb, tb), F32),
                            pltpu.VMEM((nblk, rb, tb), BF16), pltpu.VMEM((nblk, rb, LANES), F32)]),
        out_shape=[jax.ShapeDtypeStruct((S_len, W), F32), jax.ShapeDtypeStruct((2, S_len, W), F32)],
        compiler_params=_params("parallel", "arbitrary"),
    )(ti, tj, qn, kn, vb, c_b, c_b, c_rowp)


def _fox_attn_bwd(qn, kn, vb, c_b, c_rowp, lse_b, delta_b, do, *, tb, name):
    S_len, W = qn.shape
    HP = W // LANES
    tb = min(tb, S_len)
    nb = S_len // tb
    scale = FOX_DH ** -0.5
    rb, cb = min(ATTN_ROWS, tb), min(ATTN_COLS, tb)

    steps = [(j, i) for j in range(nb) for i in range(j, nb)]
    tj = jnp.asarray([s[0] for s in steps], jnp.int32)
    ti = jnp.asarray([s[1] for s in steps], jnp.int32)

    def body(tj_ref, ti_ref, q_ref, k_ref, v_ref, cb0_ref, cb1_ref, cr_ref, lse_ref, dl0_ref, dl1_ref, do_ref,
             dq_ref, dk_ref, dv_ref, dcr_ref, dct_ref, dk_scr, dv_scr, dc_scr, p_scr, ds_scr):
        t = pl.program_id(1)
        j, i = tj_ref[t], ti_ref[t]

        @pl.when(t == 0)
        def _():
            dq_ref[...] = jnp.zeros(dq_ref.shape, F32)
            dct_ref[...] = jnp.zeros(dct_ref.shape, F32)

        @pl.when(i == j)
        def _():
            dk_scr[...] = jnp.zeros(dk_scr.shape, F32)
            dv_scr[...] = jnp.zeros(dv_scr.shape, F32)
            dc_scr[...] = jnp.zeros(dc_scr.shape, F32)

        def compute(diag):
            lo = _iota((1, LANES), 1) < FOX_DH
            masks = [lo, jnp.logical_not(lo)]
            row0 = pl.multiple_of(i * tb, tb)
            npiece = tb // LANES
            colsum = [[None] * npiece for _ in range(2)]

            def visible(r):
                return ((r + 1) * rb - 1) // LANES + 1 if diag else npiece

            for hh in range(2):
                for r in range(tb // rb):
                    rows = slice(r * rb, (r + 1) * rb)
                    qr = q_ref[rows, :]
                    qh = jnp.where(masks[hh], qr * scale, jnp.zeros_like(qr))
                    doh = jnp.where(masks[hh], do_ref[rows, :], 0.0).astype(BF16)
                    bq = (cb0_ref if hh == 0 else cb1_ref)[rows, :] - lse_ref[hh, rows, :]
                    dlt = (dl0_ref if hh == 0 else dl1_ref)[rows, :]
                    rsum = None
                    for c in range(tb // cb):
                        first, last = c * cb // LANES, min((c + 1) * cb // LANES, visible(r))
                        for piece in range(max(first, last), (c + 1) * cb // LANES):
                            cols = slice(piece * LANES, (piece + 1) * LANES)
                            p_scr[hh, rows, cols] = jnp.zeros((rb, LANES), BF16)
                            ds_scr[hh, rows, cols] = jnp.zeros((rb, LANES), BF16)
                        if first >= last:
                            continue
                        s2 = lax.dot_general(qh, k_ref[c * cb:(c + 1) * cb, :], _DIMS["nt"], preferred_element_type=F32)
                        dp2 = lax.dot_general(doh, v_ref[c * cb:(c + 1) * cb, :], _DIMS["nt"], preferred_element_type=F32)
                        for piece in range(first, last):
                            cols = slice(piece * LANES, (piece + 1) * LANES)
                            sub = slice(piece * LANES - c * cb, (piece + 1) * LANES - c * cb)
                            s = s2[:, sub] + bq - cr_ref[hh:hh + 1, cols]
                            if diag and (piece + 1) * LANES - 1 > r * rb:
                                keep = piece * LANES + _iota((rb, LANES), 1) <= r * rb + _iota((rb, LANES), 0)
                                s = jnp.where(keep, s, -jnp.inf)
                            p = jnp.exp(s)
                            ds = p * (dp2[:, sub] - dlt)
                            p_scr[hh, rows, cols] = p.astype(BF16)
                            ds_scr[hh, rows, cols] = ds.astype(BF16)
                            rsum = ds if rsum is None else rsum + ds
                            csum = jnp.sum(ds, axis=0, keepdims=True)
                            colsum[hh][piece] = csum if colsum[hh][piece] is None else colsum[hh][piece] + csum
                    grow = pl.ds(row0 + r * rb, rb)
                    dct_ref[grow, :] += jnp.where(masks[hh], jnp.sum(rsum, -1, keepdims=True), 0.0)
            k = k_ref[...]
            qf = q_ref[...]
            dof = do_ref[...]
            dq_part = jnp.zeros((tb, LANES), F32)
            for hh in range(2):
                kh = jnp.where(masks[hh], k, jnp.zeros_like(k))
                qhf = jnp.where(masks[hh], qf * scale, jnp.zeros_like(qf))
                dohf = jnp.where(masks[hh], dof, 0.0).astype(BF16)
                dv_scr[...] += lax.dot_general(p_scr[hh], dohf, _DIMS["tn"], preferred_element_type=F32)
                dk_scr[...] += lax.dot_general(ds_scr[hh], qhf, _DIMS["tn"], preferred_element_type=F32)
                dq_part = dq_part + lax.dot_general(ds_scr[hh], kh, _DIMS["nn"], preferred_element_type=F32)
                for piece in range(npiece):
                    if colsum[hh][piece] is not None:
                        dc_scr[hh:hh + 1, piece * LANES:(piece + 1) * LANES] -= colsum[hh][piece]
            dq_ref[pl.ds(row0, tb), :] += dq_part * scale

        @pl.when(i > j)
        def _():
            compute(False)

        @pl.when(i == j)
        def _():
            compute(True)

        @pl.when(i == nb - 1)
        def _():
            dk_ref[...] = dk_scr[...]
            dv_ref[...] = dv_scr[...]
            dcr_ref[...] = dc_scr[...]

    qs = pl.BlockSpec((tb, LANES), lambda h, t, tj_, ti_: (ti_[t], h))
    qs1 = pl.BlockSpec((tb, LANES), lambda h, t, tj_, ti_: (ti_[t], HP + h))
    ks = pl.BlockSpec((tb, LANES), lambda h, t, tj_, ti_: (tj_[t], h))
    crs = pl.BlockSpec((None, SUBLANES, tb), lambda h, t, tj_, ti_: (h, 0, tj_[t]))
    whole = pl.BlockSpec((S_len, LANES), lambda h, t, tj_, ti_: (0, h))
    return pl.pallas_call(
        body, name=name,
        grid_spec=pltpu.PrefetchScalarGridSpec(
            num_scalar_prefetch=2, grid=(HP, len(steps)),
            in_specs=[qs, ks, ks, qs, qs1, crs, pl.BlockSpec((2, tb, LANES), lambda h, t, tj_, ti_: (0, ti_[t], h)),
                      qs, qs1, qs],
            out_specs=[whole, ks, ks, crs, whole],
            scratch_shapes=[pltpu.VMEM((tb, LANES), F32), pltpu.VMEM((tb, LANES), F32),
                            pltpu.VMEM((SUBLANES, tb), F32), pltpu.VMEM((2, tb, tb), BF16),
                            pltpu.VMEM((2, tb, tb), BF16)]),
        out_shape=[jax.ShapeDtypeStruct((S_len, W), F32)] * 3 + [jax.ShapeDtypeStruct((HP, SUBLANES, S_len), F32),
                                                                 jax.ShapeDtypeStruct((S_len, W), F32)],
        compiler_params=_params("parallel", "arbitrary"),
    )(tj, ti, qn, kn, vb, c_b, c_b, c_rowp, lse_b, delta_b, delta_b, do)


def _fox_post_tile(o, z):
    return o * _silu(z)


def _fox_post_fwd(o, h_main, *, tr, name):
    W = o.shape[1]

    def body(rows, consts, orows, oaccs, scr, step, blk):
        orows[0][...] = _fox_post_tile(rows[0][...], rows[1][...]).astype(BF16)

    z = (h_main, lambda tr_, n: pl.BlockSpec((tr_, W), lambda i: (i, 3)))
    return _rowcall(body, [o, z], [], [(W, BF16)], [], tr=tr, name=name)[0]


def _fox_post_bwd(o, h_main, d_og, *, tr, name):
    W = o.shape[1]

    def body(rows, consts, orows, oaccs, scr, step, blk):
        _, vjp = jax.vjp(_fox_post_tile, rows[0][...], rows[1][...])
        d_o, d_z = vjp(rows[2][...])
        orows[0][...] = d_o
        orows[1][...] = d_z.astype(BF16)
        lo_rows = (_iota((LANES, LANES), 0) < FOX_DH)
        for s in range(W // LANES):
            ls = slice(s * LANES, (s + 1) * LANES)
            prod = d_o[:, ls] * rows[0][:, ls]
            orows[2][:, ls] = _raw_mm(prod, lo_rows.astype(F32), "nn", "cb")
            orows[2][:, W + s * LANES:W + (s + 1) * LANES] = _raw_mm(prod, jnp.logical_not(lo_rows).astype(F32), "nn", "cb")

    z = (h_main, lambda tr_, n: pl.BlockSpec((tr_, W), lambda i: (i, 3)))
    return _rowcall(body, [o, z, d_og], [], [(W, F32), (W, BF16), (2 * W, F32)], [], tr=tr, name=name)


MESH_IDS = pl.DeviceIdType.MESH
N_CHIPS = 4
N_DEV = 8
_ANY = pl.BlockSpec(memory_space=pl.ANY)


def _xy_exchange(src, *, gather, name):
    shape = (N_CHIPS,) + tuple(src.shape[-2:])

    def body(src_ref, out_ref, send_sems, recv_sems, local_sem):
        x, y, c = lax.axis_index("x"), lax.axis_index("y"), lax.axis_index("c")
        me = 2 * x + y
        peers = [(1 - x, y), (x, 1 - y), (1 - x, 1 - y)]

        def outgoing(px, py):
            return src_ref if gather else src_ref.at[2 * px + py]

        mine = pltpu.make_async_copy(outgoing(x, y), out_ref.at[me], local_sem)
        mine.start()
        copies = []
        for j, (px, py) in enumerate(peers):
            cp = pltpu.make_async_remote_copy(
                src_ref=outgoing(px, py), dst_ref=out_ref.at[me],
                send_sem=send_sems.at[j], recv_sem=recv_sems.at[j],
                device_id=(px, py, c), device_id_type=MESH_IDS)
            cp.start()
            copies.append(cp)
        for j, (px, py) in enumerate(peers):
            pltpu.make_async_remote_copy(
                src_ref=outgoing(px, py), dst_ref=out_ref.at[2 * px + py],
                send_sem=send_sems.at[j], recv_sem=recv_sems.at[j],
                device_id=(px, py, c), device_id_type=MESH_IDS).wait_recv()
        for cp in copies:
            cp.wait_send()
        mine.wait()

    return pl.pallas_call(
        body, name=name, in_specs=[_ANY], out_specs=_ANY,
        out_shape=jax.ShapeDtypeStruct(shape, src.dtype),
        scratch_shapes=[pltpu.SemaphoreType.DMA((3,)), pltpu.SemaphoreType.DMA((3,)), pltpu.SemaphoreType.DMA],
    )(src)


def _c_swap(src, *, name):
    def body(src_ref, out_ref, send_sem, recv_sem):
        x, y, c = lax.axis_index("x"), lax.axis_index("y"), lax.axis_index("c")
        cp = pltpu.make_async_remote_copy(
            src_ref=src_ref, dst_ref=out_ref, send_sem=send_sem, recv_sem=recv_sem,
            device_id=(x, y, 1 - c), device_id_type=MESH_IDS)
        cp.start()
        cp.wait()

    return pl.pallas_call(
        body, name=name, in_specs=[_ANY], out_specs=_ANY,
        out_shape=jax.ShapeDtypeStruct(src.shape, src.dtype),
        scratch_shapes=[pltpu.SemaphoreType.DMA, pltpu.SemaphoreType.DMA],
    )(src)


def _all_gather8(blk, *, name):
    m_per, n = blk.shape

    def body(x_ref, out_ref, send_sems, recv_sems, local_sem):
        x, y, c = lax.axis_index("x"), lax.axis_index("y"), lax.axis_index("c")
        me, sibling = (x, y, c), (x, y, 1 - c)
        chips = [(1 - x, y), (x, 1 - y), (1 - x, 1 - y)]

        def rows(px, py, pc):
            return out_ref.at[pl.ds((4 * px + 2 * py + pc) * m_per, m_per), :]

        def copy(k, block, to, src=None):
            return pltpu.make_async_remote_copy(
                src_ref=rows(*block) if src is None else src, dst_ref=rows(*block),
                send_sem=send_sems.at[k], recv_sem=recv_sems.at[k], device_id=to, device_id_type=MESH_IDS)

        mine = pltpu.make_async_copy(x_ref, rows(*me), local_sem)
        mine.start()
        first = [copy(0, me, sibling, src=x_ref)]
        first += [copy(1 + j, me, (*chip, c), src=x_ref) for j, chip in enumerate(chips)]
        for cp in first:
            cp.start()
        passed = [copy(4 + j, (*chip, c), sibling) for j, chip in enumerate(chips)]
        for j, chip in enumerate(chips):
            copy(1 + j, (*chip, c), me).wait_recv()
            passed[j].start()
        copy(0, sibling, me).wait_recv()
        for j, chip in enumerate(chips):
            copy(4 + j, (*chip, 1 - c), me).wait_recv()
        for cp in first + passed:
            cp.wait_send()
        mine.wait()

    return pl.pallas_call(
        body, name=name,
        out_shape=jax.ShapeDtypeStruct((N_DEV * m_per, n), blk.dtype),
        in_specs=[pl.BlockSpec(memory_space=pltpu.VMEM)], out_specs=pl.BlockSpec(memory_space=pltpu.VMEM),
        scratch_shapes=[pltpu.SemaphoreType.DMA((7,)), pltpu.SemaphoreType.DMA((7,)), pltpu.SemaphoreType.DMA],
    )(blk)


def _sum_slots(parts, *, tr, name):
    n, R, _ = parts.shape
    pack = 2 * SUBLANES
    tr = max(t for t in range(pack, min(tr, R) + 1, pack) if R % t == 0) if R % pack == 0 else R

    def body(p_ref, o_ref):
        tot = p_ref[0].astype(F32)
        for s in range(1, n):
            tot = tot + p_ref[s].astype(F32)
        o_ref[...] = tot

    return pl.pallas_call(
        body, name=name, grid=(R // tr,),
        in_specs=[pl.BlockSpec((n, tr, LANES), lambda i: (0, i, 0))], out_specs=pl.BlockSpec((tr, LANES), lambda i: (i, 0)),
        out_shape=jax.ShapeDtypeStruct((R, LANES), F32), compiler_params=_params("parallel"),
    )(parts)


def _adamw(w, g_parts, m, v, *, name):
    shape = w.shape
    as2d = lambda a: a.reshape(-1, shape[-1])
    w2, m2, v2 = as2d(w), as2d(m), as2d(v)
    gs = [as2d(g) for g in g_parts]
    R, C = w2.shape
    tr = R
    while tr * C * 4 > (1 << 20) and tr % 2 == 0 and (tr // 2) % SUBLANES == 0:
        tr //= 2
    ng = len(gs)

    def body(*refs):
        w_ref, m_ref, v_ref = refs[:3]
        g_refs = refs[3:3 + ng]
        go_ref, d_ref, mo_ref, vo_ref = refs[3 + ng:]
        g = g_refs[0][...]
        for r in g_refs[1:]:
            g = g + r[...]
        mn = ADAM_B1 * m_ref[...] + (1.0 - ADAM_B1) * g
        vn = ADAM_B2 * v_ref[...] + (1.0 - ADAM_B2) * jnp.square(g)
        m_hat = mn / (1.0 - ADAM_B1 ** ADAM_STEP)
        v_hat = vn / (1.0 - ADAM_B2 ** ADAM_STEP)
        go_ref[...] = g
        d_ref[...] = -ADAM_LR * (m_hat / (jnp.sqrt(v_hat) + ADAM_EPS) + ADAM_WD * w_ref[...])
        mo_ref[...] = mn
        vo_ref[...] = vn

    spec = pl.BlockSpec((tr, C), lambda i: (i, 0))
    outs = pl.pallas_call(
        body, name=name, grid=(R // tr,), in_specs=[spec] * (3 + ng), out_specs=[spec] * 4,
        out_shape=[jax.ShapeDtypeStruct((R, C), F32)] * 4, compiler_params=_params("parallel"),
    )(w2, m2, v2, *gs)
    return tuple(o.reshape(shape) for o in outs)


TR = 256
ATTN_TILE = 1024
ATTN_ROWS = 128
ATTN_COLS = 256


def _mm_nn(a, b, name, **kw):
    return _matmul(a, b, tm=2048, tn=1024, tk=1024, name=name, **kw)


def _mm_nt(a, b, name, **kw):
    return _matmul(a, b, tb=True, tm=1024, tn=1024, tk=1024, name=name, **kw)


def _mm_tn(a, b, name, **kw):
    return _matmul(a, b, ta=True, tm=1024, tn=2048, tk=512, name=name, **kw)


def _c_rows(c, Hf):
    S_len = c.shape[0]
    ct = c[:, :Hf].T.reshape(Hf // 2, 2, S_len)
    return jnp.pad(ct, ((0, 0), (0, SUBLANES - 2), (0, 0)))


def _local_step(x, p, target, wts):
    L = wts["ln_g"].shape[0]
    alpha = (2 * L) ** 0.25
    Hg = wts["gdn_a_log"].shape[1]
    Hf = wts["fox_b_f"].shape[1]
    Wg_ = Hg * GDN_DK
    Wf_ = Hf * FOX_DH
    saved = []
    for i in range(L):
        j = i // 2
        sv = {"x": x}
        if i % 2 == 0:
            w_in = wts["gdn_w_in"][j]
            wm, ws = w_in[:, :4 * Wg_], _pad_lanes(w_in[:, 4 * Wg_:])
            cw8 = jnp.pad(wts["gdn_conv_w"][j], ((0, SUBLANES - GDN_CONV), (0, 0)))
            alog = _pad_lanes(_row(wts["gdn_a_log"][j]), offset=Hg)
            dtb = _pad_lanes(_row(wts["gdn_dt_bias"][j]), offset=Hg)
            hm = _mm_nn(x, wm, f"gdn{j}_in_main")
            hs = _mm_nn(x, ws, f"gdn{j}_in_small")
            q, k, v, bg = _gdn_pre_fwd(hm, hs, cw8, alog, dtb, Hg, tr=TR, name=f"gdn{j}_pre")
            o, states = _gdn_rule_fwd(q, k, v, bg, Hg, name=f"gdn{j}_rule")
            on = _gdn_post_fwd(o, hm, wts["gdn_norm_g"][j], Hg, tr=TR, name=f"gdn{j}_post")
            y = _mm_nn(on, wts["gdn_w_out"][j], f"gdn{j}_out")
            sv.update(wm=wm, ws=ws, cw8=cw8, alog=alog, dtb=dtb, hm=hm, hs=hs, q=q, k=k, v=v, bg=bg, o=o,
                      states=states, on=on)
        else:
            w_in = wts["fox_w_in"][j]
            wm, ws = w_in[:, :4 * Wf_], _pad_lanes(w_in[:, 4 * Wf_:])
            gq2 = _row(jnp.tile(wts["fox_q_norm_g"][j], 2))
            gk2 = _row(jnp.tile(wts["fox_k_norm_g"][j], 2))
            bf = _pad_lanes(_row(wts["fox_b_f"][j]))
            hm = _mm_nn(x, wm, f"fox{j}_in_main")
            hs = _mm_nn(x, ws, f"fox{j}_in_small")
            qn, kn, vb, c, c_b = _fox_pre_fwd(hm, hs, gq2, gk2, bf, Hf, tr=TR, name=f"fox{j}_pre")
            c_rowp = _c_rows(c, Hf)
            o, lse_b = _fox_attn_fwd(qn, kn, vb, c_b, c_rowp, tb=ATTN_TILE, name=f"fox{j}_attn")
            on = _fox_post_fwd(o, hm, tr=TR, name=f"fox{j}_post")
            y = _mm_nn(on, wts["fox_w_out"][j], f"fox{j}_out")
            sv.update(wm=wm, ws=ws, gq2=gq2, gk2=gk2, bf=bf, hm=hm, hs=hs, qn=qn, kn=kn, vb=vb, c_b=c_b,
                      c_rowp=c_rowp, o=o, lse_b=lse_b, on=on)
        x_ln = _ln_fwd(x, y, wts["ln_g"][i], wts["ln_b"][i], alpha, tr=TR, name=f"ln{i}")
        gp = _mm_nn(x_ln, wts["ple_w_gate"][i], f"ple{i}_gate")
        pp = _mm_nn(p[i], wts["ple_w_proj"][i], f"ple{i}_proj")
        x_out = _ple_fwd(x_ln, gp, pp, tr=TR, name=f"ple{i}_mix")
        sv.update(y=y, x_ln=x_ln, gp=gp, pp=pp)
        saved.append(sv)
        x = x_out

    loss_row, dx = _loss_fwd_bwd(x, target, tr=TR, name="loss")

    g = {n: [None] * wts[n].shape[0] for n in wts}
    for i in reversed(range(L)):
        j = i // 2
        sv = saved[i]
        d_pre, d_pp = _ple_bwd(dx, sv["gp"], sv["pp"], tr=TR, name=f"ple{i}_mix_bwd")
        g["ple_w_gate"][i] = _mm_tn(sv["x_ln"], d_pre, f"ple{i}_gate_dw")
        g["ple_w_proj"][i] = _mm_tn(p[i], d_pp, f"ple{i}_proj_dw")
        t = _mm_nt(d_pre, wts["ple_w_gate"][i], f"ple{i}_gate_dx")
        du, g["ln_g"][i], g["ln_b"][i] = _ln_bwd(sv["x"], sv["y"], wts["ln_g"][i], wts["ln_b"][i], dx, t, alpha,
                                                 tr=TR, name=f"ln{i}_bwd")
        if i % 2 == 0:
            g["gdn_w_out"][j] = _mm_tn(sv["on"], du, f"gdn{j}_out_dw")
            d_on = _mm_nt(du, wts["gdn_w_out"][j], f"gdn{j}_out_dx")
            d_o, d_z, d_ng = _gdn_post_bwd(sv["o"], sv["hm"], wts["gdn_norm_g"][j], d_on, Hg, tr=TR, name=f"gdn{j}_post_bwd")
            dq, dk, dv, dbg = _gdn_rule_bwd(sv["q"], sv["k"], sv["v"], sv["bg"], sv["states"], d_o, Hg, name=f"gdn{j}_rule_bwd")
            d_hm, d_hs, d_cw, d_al, d_dtb = _gdn_pre_bwd(sv["hm"], sv["hs"], sv["cw8"], sv["alog"], sv["dtb"],
                                                         dq, dk, dv, dbg, d_z, Hg, tr=TR, name=f"gdn{j}_pre_bwd")
            g["gdn_norm_g"][j] = d_ng[0]
            g["gdn_conv_w"][j] = d_cw[:GDN_CONV]
            g["gdn_a_log"][j] = d_al[0, Hg:2 * Hg]
            g["gdn_dt_bias"][j] = d_dtb[0, Hg:2 * Hg]
            wname, nsmall, Wd = "gdn_w_in", 2 * Hg, Wg_
        else:
            g["fox_w_out"][j] = _mm_tn(sv["on"], du, f"fox{j}_out_dw")
            d_og = _mm_nt(du, wts["fox_w_out"][j], f"fox{j}_out_dx")
            d_o, d_z, delta_b = _fox_post_bwd(sv["o"], sv["hm"], d_og, tr=TR, name=f"fox{j}_post_bwd")
            dqn, dkn, dvv, dcr, dct = _fox_attn_bwd(sv["qn"], sv["kn"], sv["vb"], sv["c_b"], sv["c_rowp"], sv["lse_b"],
                                                    delta_b, d_o, tb=ATTN_TILE, name=f"fox{j}_attn_bwd")
            dc = _pad_lanes(dcr[:, :2, :].reshape(Hf, -1).T + dct[:, ::FOX_DH])
            d_hm, d_hs, d_gq, d_gk, d_bf = _fox_pre_bwd(sv["hm"], sv["hs"], sv["gq2"], sv["gk2"], sv["bf"],
                                                        dqn, dkn, dvv, d_z, dc, Hf, tr=TR, name=f"fox{j}_pre_bwd")
            g["fox_q_norm_g"][j] = d_gq[0, :FOX_DH] + d_gq[0, FOX_DH:]
            g["fox_k_norm_g"][j] = d_gk[0, :FOX_DH] + d_gk[0, FOX_DH:]
            g["fox_b_f"][j] = d_bf[0, :Hf]
            wname, nsmall, Wd = "fox_w_in", Hf, Wf_
        dwm = _mm_tn(sv["x"], d_hm, f"{wname}{j}_main_dw")
        dws = _mm_tn(sv["x"], d_hs, f"{wname}{j}_small_dw")
        g[wname][j] = jnp.concatenate([dwm, dws[:, :nsmall]], axis=1)
        t1 = _mm_nt(d_hs, sv["ws"], f"{wname}{j}_small_dx", add=du, add_scale=alpha)
        dx = _mm_nt(d_hm, sv["wm"], f"{wname}{j}_main_dx", add=t1)
        del Wd
    grads = {n: jnp.stack(v) for n, v in g.items()}
    return loss_row, dx, grads


_SHARDED = (("ple_w_gate", 1), ("ple_w_proj", 2), ("gdn_w_in", 2), ("gdn_conv_w", 2), ("gdn_w_out", 1),
            ("fox_w_in", 2), ("fox_w_out", 1))
_REPLICATED = ("ln_g", "ln_b", "gdn_a_log", "gdn_dt_bias", "gdn_norm_g", "fox_b_f", "fox_q_norm_g", "fox_k_norm_g")
_EXACT = ("gdn_conv_w",)
_ORDER = ("ln_g", "ln_b", "ple_w_gate", "ple_w_proj", "gdn_w_in", "gdn_conv_w", "gdn_a_log", "gdn_dt_bias",
          "gdn_norm_g", "gdn_w_out", "fox_w_in", "fox_b_f", "fox_q_norm_g", "fox_k_norm_g", "fox_w_out")


def _as_rows(a):
    return a.reshape(-1, LANES)


def _gather_weights(local):
    parts = []
    for name, _ in _SHARDED:
        w = local[name]
        wb = lax.bitcast_convert_type(w, BF16) if name in _EXACT else w.astype(BF16)
        parts.append(_as_rows(wb))
    packed = jnp.concatenate(parts, axis=0)
    got = _xy_exchange(packed, gather=True, name="gather_weights")
    full, r0 = {}, 0
    for (name, axis), part in zip(_SHARDED, parts):
        nrow = part.shape[0]
        seg = got[:, r0:r0 + nrow]
        r0 += nrow
        shp = local[name].shape
        if name in _EXACT:
            blocks = lax.bitcast_convert_type(seg.reshape((N_CHIPS,) + shp + (2,)), F32)
        else:
            blocks = seg.reshape((N_CHIPS,) + shp)
        full[name] = jnp.concatenate([blocks[s] for s in range(N_CHIPS)], axis=axis)
    return full


def _reduce_sharded(grads):
    per_owner = []
    sizes = []
    for s in range(N_CHIPS):
        parts = []
        for name, axis in _SHARDED:
            gfull = grads[name]
            n = gfull.shape[axis] // N_CHIPS
            blk = lax.slice_in_dim(gfull, s * n, (s + 1) * n, axis=axis)
            parts.append(_as_rows(blk.astype(BF16)))
        sizes = [q.shape[0] for q in parts]
        per_owner.append(jnp.concatenate(parts, axis=0))
    packed = jnp.stack(per_owner)
    got = _xy_exchange(packed, gather=False, name="exchange_grads")
    mine = _sum_slots(got, tr=4096, name="sum_grads")
    other = _c_swap(mine, name="swap_grads")
    out, r0 = {}, 0
    for (name, axis), nrow in zip(_SHARDED, sizes):
        shp = list(grads[name].shape)
        shp[axis] //= N_CHIPS
        out[name] = (mine[r0:r0 + nrow].reshape(shp), other[r0:r0 + nrow].reshape(shp))
        r0 += nrow
    return out


def _reduce_replicated(grads, loss_part):
    rows = [_pad_lanes(jnp.reshape(loss_part, (1, 1)))]
    for name in _REPLICATED:
        gr = grads[name]
        rows.append(_as_rows(gr) if gr.shape[-1] % LANES == 0 else _pad_lanes(gr))
    sizes = [r.shape[0] for r in rows]
    blk = jnp.concatenate(rows, axis=0)
    nrow = blk.shape[0]
    npad = -nrow % SUBLANES
    blk = jnp.pad(blk, ((0, npad), (0, 0)))
    allb = _all_gather8(blk, name="gather_small_grads").reshape(N_DEV, nrow + npad, LANES)
    tot = _sum_slots(allb, tr=nrow + npad, name="sum_small_grads")
    out, r0 = {}, sizes[0]
    loss = tot[0, 0]
    for name, n in zip(_REPLICATED, sizes[1:]):
        gr = grads[name]
        seg = tot[r0:r0 + n]
        out[name] = seg.reshape(gr.shape) if gr.shape[-1] % LANES == 0 else seg[:, :gr.shape[-1]]
        r0 += n
    return loss, out


def kernel(x, p, ln_g, ln_b, ple_w_gate, ple_w_proj, gdn_w_in, gdn_conv_w, gdn_a_log, gdn_dt_bias, gdn_norm_g, gdn_w_out, fox_w_in, fox_b_f, fox_q_norm_g, fox_k_norm_g, fox_w_out, loss_target, m_ln_g, m_ln_b, m_ple_w_gate, m_ple_w_proj, m_gdn_w_in, m_gdn_conv_w, m_gdn_a_log, m_gdn_dt_bias, m_gdn_norm_g, m_gdn_w_out, m_fox_w_in, m_fox_b_f, m_fox_q_norm_g, m_fox_k_norm_g, m_fox_w_out, v_ln_g, v_ln_b, v_ple_w_gate, v_ple_w_proj, v_gdn_w_in, v_gdn_conv_w, v_gdn_a_log, v_gdn_dt_bias, v_gdn_norm_g, v_gdn_w_out, v_fox_w_in, v_fox_b_f, v_fox_q_norm_g, v_fox_k_norm_g, v_fox_w_out):
    local = dict(ln_g=ln_g, ln_b=ln_b, ple_w_gate=ple_w_gate, ple_w_proj=ple_w_proj, gdn_w_in=gdn_w_in,
                 gdn_conv_w=gdn_conv_w, gdn_a_log=gdn_a_log, gdn_dt_bias=gdn_dt_bias, gdn_norm_g=gdn_norm_g,
                 gdn_w_out=gdn_w_out, fox_w_in=fox_w_in, fox_b_f=fox_b_f, fox_q_norm_g=fox_q_norm_g,
                 fox_k_norm_g=fox_k_norm_g, fox_w_out=fox_w_out)
    mom_m = dict(ln_g=m_ln_g, ln_b=m_ln_b, ple_w_gate=m_ple_w_gate, ple_w_proj=m_ple_w_proj, gdn_w_in=m_gdn_w_in,
                 gdn_conv_w=m_gdn_conv_w, gdn_a_log=m_gdn_a_log, gdn_dt_bias=m_gdn_dt_bias, gdn_norm_g=m_gdn_norm_g,
                 gdn_w_out=m_gdn_w_out, fox_w_in=m_fox_w_in, fox_b_f=m_fox_b_f, fox_q_norm_g=m_fox_q_norm_g,
                 fox_k_norm_g=m_fox_k_norm_g, fox_w_out=m_fox_w_out)
    mom_v = dict(ln_g=v_ln_g, ln_b=v_ln_b, ple_w_gate=v_ple_w_gate, ple_w_proj=v_ple_w_proj, gdn_w_in=v_gdn_w_in,
                 gdn_conv_w=v_gdn_conv_w, gdn_a_log=v_gdn_a_log, gdn_dt_bias=v_gdn_dt_bias, gdn_norm_g=v_gdn_norm_g,
                 gdn_w_out=v_gdn_w_out, fox_w_in=v_fox_w_in, fox_b_f=v_fox_b_f, fox_q_norm_g=v_fox_q_norm_g,
                 fox_k_norm_g=v_fox_k_norm_g, fox_w_out=v_fox_w_out)

    wts = dict(_gather_weights(local))
    for name in _REPLICATED:
        wts[name] = local[name]
    loss_row, dx, grads = _local_step(x[0], p[:, 0], loss_target[0], wts)
    loss, small = _reduce_replicated(grads, jnp.sum(loss_row))
    big = _reduce_sharded(grads)

    outs = {}
    for name in _ORDER:
        parts = list(big[name]) if name in big else [small[name]]
        outs[name] = _adamw(local[name], parts, mom_m[name], mom_v[name], name=f"adamw_{name}")
    return (loss, dx[None], *[outs[n][0] for n in _ORDER], *[outs[n][1] for n in _ORDER],
            *[outs[n][2] for n in _ORDER], *[outs[n][3] for n in _ORDER])
```

```python
import functools

import jax
import jax.numpy as jnp
from jax import lax
from jax.experimental import pallas as pl
from jax.experimental.pallas import tpu as pltpu

F32 = jnp.float32
BF16 = jnp.bfloat16

LANES = 128
SUBLANES = 8
VMEM_LIMIT_BYTES = 56 * 1024 * 1024

GDN_DK = 128
GDN_CHUNK = 64
GDN_CONV = 4
FOX_DH = 64
LN_EPS = 1e-5
RMS_EPS = 1e-6

ADAM_LR = 0.001
ADAM_B1 = 0.9
ADAM_B2 = 0.999
ADAM_EPS = 1e-08
ADAM_WD = 0.01
ADAM_STEP = 10

_DIMS = {"nn": (((1,), (0,)), ((), ())), "nt": (((1,), (1,)), ((), ())), "tn": (((0,), (0,)), ((), ()))}


def _params(*sem):
    return pltpu.CompilerParams(dimension_semantics=sem, vmem_limit_bytes=VMEM_LIMIT_BYTES)


def _split(a, terms):
    out = []
    rest = a.astype(F32)
    for t in range(terms):
        piece = rest.astype(BF16)
        out.append(piece)
        if t + 1 < terms:
            rest = rest - piece.astype(F32)
    return out


def _raw_mm(a, b, form, mode):
    dot = lambda x, y: lax.dot_general(x, y, _DIMS[form], preferred_element_type=F32)
    if mode == "b":
        return dot(a.astype(BF16), b.astype(BF16))
    if mode == "x3":
        (ah, al), (bh, bl) = _split(a, 2), _split(b, 2)
        return dot(ah, bh) + (dot(ah, bl) + dot(al, bh))
    if mode == "ca":
        ac = a.astype(BF16)
        b1, b2, b3 = _split(b, 3)
        return dot(ac, b1) + (dot(ac, b2) + dot(ac, b3))
    assert mode == "cb", mode
    bc = b.astype(BF16)
    a1, a2, a3 = _split(a, 3)
    return dot(a1, bc) + (dot(a2, bc) + dot(a3, bc))


@functools.partial(jax.custom_vjp, nondiff_argnums=(2, 3))
def _mm(a, b, form, mode):
    return _raw_mm(a, b, form, mode)


def _mm_fwd(a, b, form, mode):
    return _raw_mm(a, b, form, mode), (a, b)


def _mm_bwd(form, mode, res, g):
    a, b = res
    flip = {"b": "b", "x3": "x3", "ca": "cb", "cb": "ca"}[mode]
    if form == "nn":
        da, db = (lambda: _mm(g, b, "nt", mode)), (lambda: _mm(a, g, "tn", mode))
    elif form == "nt":
        da, db = (lambda: _mm(g, b, "nn", mode)), (lambda: _mm(g, a, "tn", flip))
    else:
        da, db = (lambda: _mm(b, g, "nt", flip)), (lambda: _mm(a, g, "nn", mode))
    return (jnp.zeros_like(a) if mode == "ca" else da()), (jnp.zeros_like(b) if mode == "cb" else db())


_mm.defvjp(_mm_fwd, _mm_bwd)


@jax.custom_vjp
def _tri_inv(Ls):
    C = Ls[0].shape[0]
    eye = (_iota((C, C), 0) == _iota((C, C), 1)).astype(F32)
    X = [eye - L for L in Ls]
    P = [_raw_mm(L, L, "nn", "x3") for L in Ls]
    n_sq = max(1, (C - 1).bit_length() - 1)
    for it in range(n_sq):
        XP = [_raw_mm(x, p, "nn", "x3") for x, p in zip(X, P)]
        if it < n_sq - 1:
            P = [_raw_mm(p, p, "nn", "x3") for p in P]
        X = [x + xp for x, xp in zip(X, XP)]
    return X


def _tri_inv_fwd(Ls):
    Ts = _tri_inv(Ls)
    return Ts, Ts


def _tri_inv_bwd(Ts, dTs):
    Ms = [_raw_mm(dT, T, "nt", "x3") for dT, T in zip(dTs, Ts)]
    return ([-_raw_mm(T, M, "tn", "x3") for T, M in zip(Ts, Ms)],)


_tri_inv.defvjp(_tri_inv_fwd, _tri_inv_bwd)


@jax.custom_vjp
def _tri_inv_known(Ls, Ts):
    return Ts


def _tri_inv_known_fwd(Ls, Ts):
    return Ts, Ts


def _tri_inv_known_bwd(Ts, dTs):
    return _tri_inv_bwd(Ts, dTs)[0], [jnp.zeros_like(T) for T in Ts]


_tri_inv_known.defvjp(_tri_inv_known_fwd, _tri_inv_known_bwd)


def _silu(x):
    return x * jax.nn.sigmoid(x)


def _softplus(x):
    return jnp.maximum(x, 0.0) + jnp.log1p(jnp.exp(-jnp.abs(x)))


def _iota(shape, dim):
    return lax.broadcasted_iota(jnp.int32, shape, dim)


def _matmul(a, b, *, ta=False, tb=False, out_dtype=F32, add=None, add_scale=1.0, tm=512, tn=512, tk=512, name):
    if ta:
        K, M = a.shape
    else:
        M, K = a.shape
    if tb:
        N, K2 = b.shape
    else:
        K2, N = b.shape
    assert K == K2, (a.shape, b.shape, ta, tb)
    tm, tn, tk = min(tm, M), min(tn, N), min(tk, K)
    assert M % tm == 0 and N % tn == 0 and K % tk == 0, (M, N, K, tm, tn, tk)
    nk = K // tk
    form = ("t" if ta else "n") + ("t" if tb else "n")
    dims = (((0 if ta else 1,), (1 if tb else 0,)), ((), ()))
    del form
    a_spec = pl.BlockSpec((tk, tm), lambda i, j, k: (k, i)) if ta else pl.BlockSpec((tm, tk), lambda i, j, k: (i, k))
    b_spec = pl.BlockSpec((tn, tk), lambda i, j, k: (j, k)) if tb else pl.BlockSpec((tk, tn), lambda i, j, k: (k, j))
    o_spec = pl.BlockSpec((tm, tn), lambda i, j, k: (i, j))
    has_add = add is not None

    def body(*refs):
        a_ref, b_ref = refs[:2]
        add_ref = refs[2] if has_add else None
        o_ref = refs[3] if has_add else refs[2]
        acc_ref = refs[-1] if nk > 1 else None
        k = pl.program_id(2)
        part = lax.dot_general(a_ref[...].astype(BF16), b_ref[...].astype(BF16), dims, preferred_element_type=F32)

        def finish(total):
            if has_add:
                total = total + add_scale * add_ref[...].astype(F32)
            o_ref[...] = total.astype(o_ref.dtype)

        if nk == 1:
            finish(part)
        else:
            @pl.when(k == 0)
            def _():
                acc_ref[...] = part

            @pl.when(jnp.logical_and(k > 0, k < nk - 1))
            def _():
                acc_ref[...] += part

            @pl.when(k == nk - 1)
            def _():
                finish(acc_ref[...] + part)

    in_specs = [a_spec, b_spec] + ([o_spec] if has_add else [])
    args = (a, b) + ((add,) if has_add else ())
    return pl.pallas_call(
        body, name=name, grid=(M // tm, N // tn, nk),
        in_specs=in_specs, out_specs=o_spec,
        out_shape=jax.ShapeDtypeStruct((M, N), out_dtype),
        scratch_shapes=[pltpu.VMEM((tm, tn), F32)] if nk > 1 else [],
        compiler_params=_params("parallel", "parallel", "arbitrary"),
    )(*args)


def _rowcall(body_fn, rows, consts, out_rows, out_accs, *, tr, name, reverse=False, scratch=()):
    def arr_spec(r):
        return r if isinstance(r, tuple) else (r, None)

    S = arr_spec(rows[0])[0].shape[0]
    tr = min(tr, S)
    assert S % tr == 0
    n = S // tr
    ridx = (lambda i: (n - 1 - i, 0)) if reverse else (lambda i: (i, 0))
    in_specs, args = [], []
    for r in rows:
        arr, spec = arr_spec(r)
        args.append(arr)
        in_specs.append(spec(tr, n) if spec is not None else pl.BlockSpec((tr, arr.shape[1]), ridx))
    for c in consts:
        args.append(c)
        in_specs.append(pl.BlockSpec(c.shape, lambda i: (0, 0)))
    out_specs, out_shape = [], []
    for (ncol, dt) in out_rows:
        out_specs.append(pl.BlockSpec((tr, ncol), ridx))
        out_shape.append(jax.ShapeDtypeStruct((S, ncol), dt))
    for shp in out_accs:
        out_specs.append(pl.BlockSpec(shp, lambda i: (0, 0)))
        out_shape.append(jax.ShapeDtypeStruct(shp, F32))
    nr, nc, no, na = len(rows), len(consts), len(out_rows), len(out_accs)

    def kernel(*refs):
        row_refs = refs[:nr]
        const_refs = refs[nr:nr + nc]
        orow_refs = refs[nr + nc:nr + nc + no]
        oacc_refs = refs[nr + nc + no:nr + nc + no + na]
        scr = refs[nr + nc + no + na:]
        step = pl.program_id(0)
        blk = (n - 1 - step) if reverse else step

        @pl.when(step == 0)
        def _():
            for acc in oacc_refs:
                acc[...] = jnp.zeros(acc.shape, F32)

        body_fn(row_refs, const_refs, orow_refs, oacc_refs, scr, step, blk)

    outs = pl.pallas_call(
        kernel, name=name, grid=(n,), in_specs=in_specs, out_specs=out_specs, out_shape=out_shape,
        scratch_shapes=list(scratch), compiler_params=_params("arbitrary"),
    )(*args)
    return outs


def _row(v):
    return v.astype(F32).reshape(1, -1)


def _pad_lanes(v, width=LANES, offset=0):
    pad = [(0, 0)] * (v.ndim - 1) + [(offset, width - offset - v.shape[-1])]
    return jnp.pad(v, pad)


def _ln_tile(x, y, g, b, alpha):
    u = alpha * x + y
    mu = jnp.mean(u, -1, keepdims=True)
    d = u - mu
    var = jnp.mean(d * d, -1, keepdims=True)
    return d * lax.rsqrt(var + LN_EPS) * g + b


def _ln_fwd(x, y, g, b, alpha, *, tr, name):
    D = x.shape[1]

    def body(rows, consts, orows, oaccs, scr, step, blk):
        orows[0][...] = _ln_tile(rows[0][...], rows[1][...], consts[0][...], consts[1][...], alpha)

    return _rowcall(body, [x, y], [_row(g), _row(b)], [(D, F32)], [], tr=tr, name=name)[0]


def _ln_bwd(x, y, g, b, dxo, t, alpha, *, tr, name):
    D = x.shape[1]

    def body(rows, consts, orows, oaccs, scr, step, blk):
        xv, yv = rows[0][...], rows[1][...]
        ct = rows[2][...] + rows[3][...]
        _, vjp = jax.vjp(lambda yy, gg, bb: _ln_tile(xv, yy, gg, bb, alpha), yv, consts[0][...], consts[1][...])
        du, dg, db = vjp(ct)
        orows[0][...] = du
        oaccs[0][...] += dg
        oaccs[1][...] += db

    du, dg, db = _rowcall(body, [x, y, dxo, t], [_row(g), _row(b)], [(D, F32)], [(1, D), (1, D)], tr=tr, name=name)
    return du, dg[0], db[0]


def _ple_fwd(x_ln, gp, pp, *, tr, name):
    D = x_ln.shape[1]

    def body(rows, consts, orows, oaccs, scr, step, blk):
        orows[0][...] = rows[0][...] + jax.nn.sigmoid(rows[1][...]) * rows[2][...]

    return _rowcall(body, [x_ln, gp, pp], [], [(D, F32)], [], tr=tr, name=name)[0]


def _ple_bwd(dxo, gp, pp, *, tr, name):
    D = dxo.shape[1]

    def body(rows, consts, orows, oaccs, scr, step, blk):
        d = rows[0][...]
        s = jax.nn.sigmoid(rows[1][...])
        orows[0][...] = (d * rows[2][...] * s * (1.0 - s)).astype(BF16)
        orows[1][...] = (d * s).astype(BF16)

    return _rowcall(body, [dxo, gp, pp], [], [(D, BF16), (D, BF16)], [], tr=tr, name=name)


def _loss_fwd_bwd(xf, target, *, tr, name):
    D = xf.shape[1]

    def body(rows, consts, orows, oaccs, scr, step, blk):
        err = rows[0][...] - rows[1][...]
        orows[0][...] = err * (1.0 / D)
        part = jnp.sum(err * err, axis=0, keepdims=True) * (0.5 / D)
        oaccs[0][...] += part

    dx, lrow = _rowcall(body, [xf, target], [], [(D, F32)], [(1, D)], tr=tr, name=name)
    return lrow, dx


def _gdn_qk_tile(c):
    y = _silu(c)
    return y * lax.rsqrt(jnp.sum(y * y, -1, keepdims=True) + RMS_EPS)


def _make_bg_fn(H):
    def fn(hs, alog, dtb):
        lane = _iota((1, LANES), 1)
        beta = jax.nn.sigmoid(hs)
        g = -jnp.exp(alog) * _softplus(hs + dtb)
        return jnp.where(lane < H, beta, jnp.where(lane < 2 * H, g, 0.0))
    return fn


def _halo_spec(ncol):
    def make(tr, n):
        per = tr // SUBLANES
        return pl.BlockSpec((SUBLANES, ncol), lambda i: (jnp.maximum(i * per - 1, 0), 0))
    return make


def _halo_spec_rev(ncol):
    def make(tr, n):
        per = tr // SUBLANES
        return pl.BlockSpec((SUBLANES, ncol), lambda i: (jnp.maximum((n - 1 - i) * per - 1, 0), 0))
    return make


def _gdn_pre_fwd(h_main, h_small, conv_w8, alog_row, dtb_row, H, *, tr, name):
    W = H * GDN_DK
    C3 = 3 * W
    bg_fn = _make_bg_fn(H)

    def body(rows, consts, orows, oaccs, scr, step, blk):
        main_ref, halo_ref, hs_ref = rows
        w_ref, alog_ref, dtb_ref = consts
        q_ref, k_ref, v_ref, bg_ref = orows
        xs = scr[0]
        trr = main_ref.shape[0]
        xs[pl.ds(SUBLANES, trr), :] = main_ref[...]
        xs[pl.ds(0, SUBLANES), :] = jnp.where(blk > 0, halo_ref[...], 0.0)
        for s in range(C3 // LANES):
            ls = slice(s * LANES, (s + 1) * LANES)
            c = jnp.zeros((trr, LANES), F32)
            for j in range(GDN_CONV):
                c = c + w_ref[GDN_CONV - 1 - j:GDN_CONV - j, ls] * xs[pl.ds(SUBLANES - j, trr), ls]
            if s < 2 * H:
                out = _gdn_qk_tile(c)
                (q_ref if s < H else k_ref)[:, (s % H) * LANES:(s % H + 1) * LANES] = out
            else:
                v_ref[:, (s - 2 * H) * LANES:(s - 2 * H + 1) * LANES] = _silu(c)
        bg_ref[...] = bg_fn(hs_ref[...], alog_ref[...], dtb_ref[...])

    main = (h_main, lambda tr_, n: pl.BlockSpec((tr_, C3), lambda i: (i, 0)))
    halo = (h_main, _halo_spec(C3))
    trr = min(tr, h_main.shape[0])
    return _rowcall(body, [main, halo, h_small], [conv_w8, alog_row, dtb_row],
                    [(W, F32), (W, F32), (W, F32), (LANES, F32)], [], tr=tr, name=name,
                    scratch=[pltpu.VMEM((trr + SUBLANES, C3), F32)])


def _gdn_pre_bwd(h_main, h_small, conv_w8, alog_row, dtb_row, dq, dk, dv, dbg, dz, H, *, tr, name):
    W = H * GDN_DK
    C3 = 3 * W
    bg_fn = _make_bg_fn(H)

    def body(rows, consts, orows, oaccs, scr, step, blk):
        main_ref, halo_ref, hs_ref, dq_ref, dk_ref, dv_ref, dbg_ref, dz_ref = rows
        w_ref, alog_ref, dtb_ref = consts
        dmain_ref, dhs_ref = orows
        dw_ref, dalog_ref, ddtb_ref = oaccs
        xs, dcs = scr
        trr = main_ref.shape[0]
        xs[pl.ds(SUBLANES, trr), :] = main_ref[...]
        xs[pl.ds(0, SUBLANES), :] = jnp.where(blk > 0, halo_ref[...], 0.0)

        @pl.when(step == 0)
        def _():
            dcs[pl.ds(trr, SUBLANES), :] = jnp.zeros((SUBLANES, C3), F32)

        for s in range(C3 // LANES):
            ls = slice(s * LANES, (s + 1) * LANES)
            c = jnp.zeros((trr, LANES), F32)
            for j in range(GDN_CONV):
                c = c + w_ref[GDN_CONV - 1 - j:GDN_CONV - j, ls] * xs[pl.ds(SUBLANES - j, trr), ls]
            if s < 2 * H:
                src = dq_ref if s < H else dk_ref
                ct = src[:, (s % H) * LANES:(s % H + 1) * LANES]
                _, vjp = jax.vjp(_gdn_qk_tile, c)
            else:
                ct = dv_ref[:, (s - 2 * H) * LANES:(s - 2 * H + 1) * LANES]
                _, vjp = jax.vjp(_silu, c)
            dcs[pl.ds(0, trr), ls] = vjp(ct)[0]
        for s in range(C3 // LANES):
            ls = slice(s * LANES, (s + 1) * LANES)
            dx = jnp.zeros((trr, LANES), F32)
            dc0 = dcs[pl.ds(0, trr), ls]
            for j in range(GDN_CONV):
                wrow = w_ref[GDN_CONV - 1 - j:GDN_CONV - j, ls]
                dx = dx + wrow * dcs[pl.ds(j, trr), ls]
                dw_ref[GDN_CONV - 1 - j:GDN_CONV - j, ls] += jnp.sum(dc0 * xs[pl.ds(SUBLANES - j, trr), ls], axis=0, keepdims=True)
            dmain_ref[:, ls] = dx.astype(BF16)
        dmain_ref[:, C3:] = dz_ref[...]
        dcs[pl.ds(trr, SUBLANES), :] = dcs[pl.ds(0, SUBLANES), :]
        _, vjp = jax.vjp(bg_fn, hs_ref[...], alog_ref[...], dtb_ref[...])
        dhs, dalog, ddtb = vjp(dbg_ref[...])
        dhs_ref[...] = dhs.astype(BF16)
        dalog_ref[...] += dalog
        ddtb_ref[...] += ddtb

    trr = min(tr, h_main.shape[0])
    main = (h_main, lambda tr_, n: pl.BlockSpec((tr_, C3), lambda i: (n - 1 - i, 0)))
    halo = (h_main, _halo_spec_rev(C3))
    return _rowcall(body, [main, halo, h_small, dq, dk, dv, dbg, dz], [conv_w8, alog_row, dtb_row],
                    [(4 * W, BF16), (LANES, BF16)], [(SUBLANES, C3), (1, LANES), (1, LANES)],
                    tr=tr, name=name, reverse=True,
                    scratch=[pltpu.VMEM((trr + SUBLANES, C3), F32), pltpu.VMEM((trr + SUBLANES, C3), F32)])


def _gdn_chunk(qs, ks, vs, betas, gs, Ss, Ts=None, with_inverse=False):
    C, dk = qs[0].shape
    dv = vs[0].shape[1]
    ri, ci = _iota((C, C), 0), _iota((C, C), 1)
    causal, strict = ri >= ci, ri > ci
    tril = causal.astype(F32)
    lane0 = (_iota((1, LANES), 1) == 0).astype(F32)
    e0 = jnp.ones((C, 1), F32) * lane0
    last = (_iota((C, 1), 0) == C - 1).astype(F32)

    def each(f, *lists):
        return [f(*a) for a in zip(*lists)]

    G = each(lambda g: g * jnp.ones((1, LANES), F32), gs)
    gcB = each(lambda x: _mm(tril, x, "nn", "ca"), G)
    gc = each(lambda x: jnp.sum(x * lane0, -1, keepdims=True), gcB)
    gc_row = each(lambda x: _mm(e0, x, "nt", "ca"), gcB)
    decay = each(lambda a, b: jnp.where(causal, jnp.exp(jnp.where(causal, a - b, 0.0)), 0.0), gc, gc_row)
    kb = each(lambda k, b: k * b, ks, betas)
    kk = each(lambda a, k: _mm(a, k, "nt", "b"), kb, ks)
    L = each(lambda a, d: jnp.where(strict, a * d, 0.0), kk, decay)
    X = _tri_inv(L) if Ts is None else _tri_inv_known(L, Ts)
    egc = each(jnp.exp, gc)
    u = each(lambda x, v, b: _mm(x, v * b, "nn", "x3"), X, vs, betas)
    w = each(lambda x, a, e: _mm(x, a * e, "nn", "x3"), X, kb, egc)
    qsc = each(lambda q: q * (dk ** -0.5), qs)
    qk = each(lambda q, k: _mm(q, k, "nt", "b"), qsc, ks)
    A = each(lambda a, d: jnp.where(causal, a * d, 0.0), qk, decay)
    q_dec = each(lambda q, e: q * e, qsc, egc)
    gl = each(lambda x: jnp.sum(x * last, keepdims=True), gc)
    k_dec = each(lambda k, a, b: k * jnp.exp(a - b), ks, gl, gc)
    wS = each(lambda a, s: _mm(a, s, "nn", "b"), w, Ss)
    qS = each(lambda a, s: _mm(a, s, "nn", "b"), q_dec, Ss)
    v_new = each(lambda a, b: a - b, u, wS)
    Av = each(lambda a, b: _mm(a, b, "nn", "b"), A, v_new)
    kv = each(lambda a, b: _mm(a, b, "tn", "b"), k_dec, v_new)
    o = each(lambda a, b: a + b, qS, Av)
    S_new = each(lambda s, e, x: s * jnp.exp(e) + x, Ss, gl, kv)
    return (o, S_new, X) if with_inverse else (o, S_new)


def _gdn_rule_fwd(q, k, v, bg, H, *, name, carry=None):
    S_len = q.shape[0]
    C = min(GDN_CHUNK, S_len)
    N = S_len // C
    dk = dv = GDN_DK

    def body(*refs):
        if carry is None:
            q_ref, k_ref, v_ref, bg_ref, o_ref, st_ref, ti_ref, s_scr = refs
        else:
            q_ref, k_ref, v_ref, bg_ref, src_ref, o_ref, st_ref, ti_ref, got_ref, s_scr = refs[:10]
            xy = (src_ref, got_ref) + tuple(refs[10:]) + (carry[1],)
        n = pl.program_id(0)

        @pl.when(n == 0)
        def _():
            s_scr[...] = jnp.zeros(s_scr.shape, F32)
            if carry is not None:
                _xy_start(*xy)

        bgt = bg_ref[...]
        sl = [slice(h * dk, (h + 1) * dk) for h in range(H)]
        Ss = [s_scr[h] for h in range(H)]
        for h in range(H):
            st_ref[h] = Ss[h]
        os_, S_new, Ts = _gdn_chunk([q_ref[:, s] for s in sl], [k_ref[:, s] for s in sl], [v_ref[:, s] for s in sl],
                                    [bgt[:, h:h + 1] for h in range(H)], [bgt[:, H + h:H + h + 1] for h in range(H)],
                                    Ss, with_inverse=True)
        for h in range(H):
            o_ref[:, sl[h]] = os_[h]
            s_scr[h] = S_new[h]
            ti_ref[h] = Ts[h]

        if carry is not None:
            @pl.when(n == N - 1)
            def _():
                _xy_wait(*xy)

    rows = pl.BlockSpec((C, H * dk), lambda n: (n, 0))
    extra = carry is not None
    return pl.pallas_call(
        body, name=name, grid=(N,),
        in_specs=[rows, rows, rows, pl.BlockSpec((C, LANES), lambda n: (n, 0))] + ([_ANY] if extra else []),
        out_specs=[rows, pl.BlockSpec((H, dk, dv), lambda n: (n, 0, 0)), pl.BlockSpec((H, C, C), lambda n: (n, 0, 0))]
        + ([_ANY] if extra else []),
        out_shape=[jax.ShapeDtypeStruct((S_len, H * dv), F32), jax.ShapeDtypeStruct((N * H, dk, dv), F32),
                   jax.ShapeDtypeStruct((N * H, C, C), F32)] + ([_xy_out_shape(carry[0])] if extra else []),
        scratch_shapes=[pltpu.VMEM((H, dk, dv), F32)] + (_xy_sems() if extra else []),
        compiler_params=_params("arbitrary"),
    )(q, k, v, bg, *((carry[0],) if extra else ()))


def _gdn_rule_bwd(q, k, v, bg, states, tinv, do, H, *, name, carry=None):
    S_len = q.shape[0]
    C = min(GDN_CHUNK, S_len)
    N = S_len // C
    dk = dv = GDN_DK

    def body(*refs):
        if carry is None:
            q_ref, k_ref, v_ref, bg_ref, st_ref, ti_ref, do_ref, dq_ref, dk_ref, dv_ref, dbg_ref, ds_scr = refs
        else:
            (q_ref, k_ref, v_ref, bg_ref, st_ref, ti_ref, do_ref, src_ref,
             dq_ref, dk_ref, dv_ref, dbg_ref, got_ref, ds_scr) = refs[:14]
            xy = (src_ref, got_ref) + tuple(refs[14:]) + (carry[1],)
        step = pl.program_id(0)

        @pl.when(step == 0)
        def _():
            ds_scr[...] = jnp.zeros(ds_scr.shape, F32)
            if carry is not None:
                _xy_start(*xy)

        bgt = bg_ref[...]
        lane = _iota((1, LANES), 1)
        dbg = jnp.zeros((C, LANES), F32)
        sl = [slice(h * dk, (h + 1) * dk) for h in range(H)]
        Ts = [ti_ref[h] for h in range(H)]
        _, vjp = jax.vjp(lambda *a: _gdn_chunk(*a, Ts=Ts),
                         [q_ref[:, s] for s in sl], [k_ref[:, s] for s in sl], [v_ref[:, s] for s in sl],
                         [bgt[:, h:h + 1] for h in range(H)], [bgt[:, H + h:H + h + 1] for h in range(H)],
                         [st_ref[h] for h in range(H)])
        dq, dkk, dvv, dbeta, dg, dS = vjp(([do_ref[:, s] for s in sl], [ds_scr[h] for h in range(H)]))
        for h in range(H):
            dq_ref[:, sl[h]] = dq[h]
            dk_ref[:, sl[h]] = dkk[h]
            dv_ref[:, sl[h]] = dvv[h]
            dbg = dbg + jnp.where(lane == h, dbeta[h], 0.0) + jnp.where(lane == h + H, dg[h], 0.0)
            ds_scr[h] = dS[h]
        dbg_ref[...] = dbg

        if carry is not None:
            @pl.when(step == N - 1)
            def _():
                _xy_wait(*xy)

    rows = pl.BlockSpec((C, H * dk), lambda s: (N - 1 - s, 0))
    bgs = pl.BlockSpec((C, LANES), lambda s: (N - 1 - s, 0))
    extra = carry is not None
    return pl.pallas_call(
        body, name=name, grid=(N,),
        in_specs=[rows, rows, rows, bgs, pl.BlockSpec((H, dk, dv), lambda s: (N - 1 - s, 0, 0)),
                  pl.BlockSpec((H, C, C), lambda s: (N - 1 - s, 0, 0)), rows] + ([_ANY] if extra else []),
        out_specs=[rows, rows, rows, bgs] + ([_ANY] if extra else []),
        out_shape=[jax.ShapeDtypeStruct((S_len, H * dk), F32)] * 3 + [jax.ShapeDtypeStruct((S_len, LANES), F32)]
        + ([_xy_out_shape(carry[0])] if extra else []),
        scratch_shapes=[pltpu.VMEM((H, dk, dv), F32)] + (_xy_sems() if extra else []),
        compiler_params=_params("arbitrary"),
    )(q, k, v, bg, states, tinv, do, *((carry[0],) if extra else ()))


def _gdn_post_tile(o, z, g):
    return o * lax.rsqrt(jnp.mean(o * o, -1, keepdims=True) + RMS_EPS) * g * _silu(z)


def _gdn_post_fwd(o, h_main, norm_g, H, *, tr, name):
    W = H * GDN_DK

    def body(rows, consts, orows, oaccs, scr, step, blk):
        for h in range(H):
            ls = slice(h * LANES, (h + 1) * LANES)
            orows[0][:, ls] = _gdn_post_tile(rows[0][:, ls], rows[1][:, ls], consts[0][...]).astype(BF16)

    z = (h_main, lambda tr_, n: pl.BlockSpec((tr_, W), lambda i: (i, 3)))
    return _rowcall(body, [o, z], [_row(norm_g)], [(W, BF16)], [], tr=tr, name=name)[0]


def _gdn_post_bwd(o, h_main, norm_g, d_on, H, *, tr, name):
    W = H * GDN_DK

    def body(rows, consts, orows, oaccs, scr, step, blk):
        for h in range(H):
            ls = slice(h * LANES, (h + 1) * LANES)
            _, vjp = jax.vjp(_gdn_post_tile, rows[0][:, ls], rows[1][:, ls], consts[0][...])
            d_o, d_z, d_g = vjp(rows[2][:, ls])
            orows[0][:, ls] = d_o
            orows[1][:, ls] = d_z.astype(BF16)
            oaccs[0][...] += d_g

    z = (h_main, lambda tr_, n: pl.BlockSpec((tr_, W), lambda i: (i, 3)))
    return _rowcall(body, [o, z, d_on], [_row(norm_g)], [(W, F32), (W, BF16)], [(1, LANES)], tr=tr, name=name)


def _seg_ones():
    ri, ci = _iota((LANES, LANES), 0), _iota((LANES, LANES), 1)
    return ((ri < FOX_DH) == (ci < FOX_DH)).astype(F32)


def _fox_qk_tile(x, g2):
    ms = _mm(x * x, _seg_ones(), "nn", "cb") * (1.0 / FOX_DH)
    return x * lax.rsqrt(ms + RMS_EPS) * g2


def _make_lf_fn(Hf):
    def fn(hs, bf):
        lane = _iota((1, LANES), 1)
        return jnp.where(lane < Hf, -_softplus(-(hs + bf)), 0.0)
    return fn


def _fox_pre_fwd(h_main, h_small, gq2, gk2, bf_row, Hf, *, tr, name):
    W = Hf * FOX_DH
    lf_fn = _make_lf_fn(Hf)

    def body(rows, consts, orows, oaccs, scr, step, blk):
        qk_ref, v_ref, hs_ref = rows
        gq_ref, gk_ref, bf_ref = consts
        qn_ref, kn_ref, vb_ref, c_ref, cb_ref = orows
        carry = scr[0]
        trr = qk_ref.shape[0]

        @pl.when(step == 0)
        def _():
            carry[...] = jnp.zeros(carry.shape, F32)

        for s in range(W // LANES):
            ls = slice(s * LANES, (s + 1) * LANES)
            qn_ref[:, ls] = _fox_qk_tile(qk_ref[:, ls], gq_ref[...]).astype(BF16)
            kn_ref[:, ls] = _fox_qk_tile(qk_ref[:, W + s * LANES:W + (s + 1) * LANES], gk_ref[...]).astype(BF16)
        vb_ref[...] = v_ref[...].astype(BF16)
        lf = lf_fn(hs_ref[...], bf_ref[...])
        tril = (_iota((trr, trr), 0) >= _iota((trr, trr), 1)).astype(F32)
        c = _raw_mm(tril, lf, "nn", "ca") + carry[0:1, :]
        c_ref[...] = c
        carry[0:1, :] = c[trr - 1:trr, :]
        col = _iota((LANES, 2 * W), 1)
        parity = (col >= W).astype(jnp.int32)
        slab = jnp.right_shift(col - parity * W, 7)
        expand = (2 * slab + parity == _iota((LANES, 2 * W), 0)).astype(F32)
        cb_ref[...] = _raw_mm(c, expand, "nn", "cb")

    qk = (h_main, lambda tr_, n: pl.BlockSpec((tr_, 2 * W), lambda i: (i, 0)))
    vv = (h_main, lambda tr_, n: pl.BlockSpec((tr_, W), lambda i: (i, 2)))
    return _rowcall(body, [qk, vv, h_small], [gq2, gk2, bf_row],
                    [(W, BF16), (W, BF16), (W, BF16), (LANES, F32), (2 * W, F32)], [], tr=tr, name=name,
                    scratch=[pltpu.VMEM((SUBLANES, LANES), F32)])


def _fox_pre_bwd(h_main, h_small, gq2, gk2, bf_row, dqn, dkn, dvv, dz, dc, Hf, *, tr, name):
    W = Hf * FOX_DH
    lf_fn = _make_lf_fn(Hf)

    def body(rows, consts, orows, oaccs, scr, step, blk):
        qk_ref, hs_ref, dqn_ref, dkn_ref, dvv_ref, dz_ref, dc_ref = rows
        gq_ref, gk_ref, bf_ref = consts
        dmain_ref, dhs_ref = orows
        dgq_ref, dgk_ref, dbf_ref = oaccs
        carry = scr[0]
        trr = qk_ref.shape[0]

        @pl.when(step == 0)
        def _():
            carry[...] = jnp.zeros(carry.shape, F32)

        for s in range(W // LANES):
            ls = slice(s * LANES, (s + 1) * LANES)
            lk = slice(W + s * LANES, W + (s + 1) * LANES)
            _, vjp = jax.vjp(_fox_qk_tile, qk_ref[:, ls], gq_ref[...])
            dx, dg = vjp(dqn_ref[:, ls])
            dmain_ref[:, ls] = dx.astype(BF16)
            dgq_ref[...] += dg
            _, vjp = jax.vjp(_fox_qk_tile, qk_ref[:, lk], gk_ref[...])
            dx, dg = vjp(dkn_ref[:, ls])
            dmain_ref[:, lk] = dx.astype(BF16)
            dgk_ref[...] += dg
        dmain_ref[:, 2 * W:3 * W] = dvv_ref[...].astype(BF16)
        dmain_ref[:, 3 * W:] = dz_ref[...]
        dcv = dc_ref[...]
        triu = (_iota((trr, trr), 0) <= _iota((trr, trr), 1)).astype(F32)
        dlf = _raw_mm(triu, dcv, "nn", "ca") + carry[0:1, :]
        carry[0:1, :] = dlf[0:1, :]
        _, vjp = jax.vjp(lf_fn, hs_ref[...], bf_ref[...])
        dhs, dbf = vjp(dlf)
        dhs_ref[...] = dhs.astype(BF16)
        dbf_ref[...] += dbf

    qk = (h_main, lambda tr_, n: pl.BlockSpec((tr_, 2 * W), lambda i: (n - 1 - i, 0)))
    return _rowcall(body, [qk, h_small, dqn, dkn, dvv, dz, dc], [gq2, gk2, bf_row],
                    [(4 * W, BF16), (LANES, BF16)], [(1, LANES), (1, LANES), (1, LANES)],
                    tr=tr, name=name, reverse=True, scratch=[pltpu.VMEM((SUBLANES, LANES), F32)])


def _fox_attn_fwd(qn, kn, vb, c_b, c_rowp, *, tb, name):
    S_len, W = qn.shape
    HP = W // LANES
    tb = min(tb, S_len)
    nb = S_len // tb
    scale = FOX_DH ** -0.5
    rb, cb = min(ATTN_ROWS, tb), min(ATTN_COLS, tb)
    nblk = 2 * (tb // rb)

    steps = [(i, j) for i in range(nb) for j in range(i + 1)]
    ti = jnp.asarray([s[0] for s in steps], jnp.int32)
    tj = jnp.asarray([s[1] for s in steps], jnp.int32)

    def body(ti_ref, tj_ref, q_ref, k_ref, v_ref, cb0_ref, cb1_ref, cr_ref, o_ref, lse_ref,
             m_scr, l_scr, acc_scr, s_scr, p_scr, a_scr):
        t = pl.program_id(1)
        i, j = ti_ref[t], tj_ref[t]

        @pl.when(j == 0)
        def _():
            m_scr[...] = jnp.full(m_scr.shape, -jnp.inf, F32)
            l_scr[...] = jnp.zeros(l_scr.shape, F32)
            acc_scr[...] = jnp.zeros(acc_scr.shape, F32)

        def compute(diag):
            lo = _iota((1, LANES), 1) < FOX_DH
            v = v_ref[...]
            lane = _iota((1, LANES), 1)
            blocks = [(hh, r) for hh in range(2) for r in range(tb // rb)]
            masks = [lo, jnp.logical_not(lo)]

            def visible(r):
                return ((r + 1) * rb - 1) // LANES + 1 if diag else tb // LANES

            for b, (hh, r) in enumerate(blocks):
                rows = slice(r * rb, (r + 1) * rb)
                qr = q_ref[rows, :]
                qh = jnp.where(masks[hh], qr * scale, jnp.zeros_like(qr))
                ctb = (cb0_ref if hh == 0 else cb1_ref)[rows, :]
                mx = None
                for c in range(tb // cb):
                    if c * cb // LANES >= visible(r):
                        continue
                    s2 = lax.dot_general(qh, k_ref[c * cb:(c + 1) * cb, :], _DIMS["nt"], preferred_element_type=F32)
                    for piece in range(c * cb // LANES, min((c + 1) * cb // LANES, visible(r))):
                        cols = slice(piece * LANES, (piece + 1) * LANES)
                        s = s2[:, piece * LANES - c * cb:(piece + 1) * LANES - c * cb] + ctb - cr_ref[hh:hh + 1, cols]
                        if diag and (piece + 1) * LANES - 1 > r * rb:
                            keep = piece * LANES + _iota((rb, LANES), 1) <= r * rb + _iota((rb, LANES), 0)
                            s = jnp.where(keep, s, -jnp.inf)
                        s_scr[b, :, cols] = s
                        mx = s if mx is None else jnp.maximum(mx, s)
                m_prev = m_scr[hh, rows, :]
                m_new = jnp.maximum(m_prev, jnp.broadcast_to(jnp.max(mx, -1, keepdims=True), (rb, LANES)))
                a_scr[b] = jnp.exp(m_prev - m_new)
                m_scr[hh, rows, :] = m_new
            for b, (hh, r) in enumerate(blocks):
                m_new = m_scr[hh, r * rb:(r + 1) * rb, :]
                for piece in range(visible(r)):
                    cols = slice(piece * LANES, (piece + 1) * LANES)
                    p_scr[b, :, cols] = jnp.exp(s_scr[b, :, cols] - m_new).astype(BF16)
            for b, (hh, r) in enumerate(blocks):
                rows = slice(r * rb, (r + 1) * rb)
                nkv = visible(r) * LANES
                pb = p_scr[b, :, :nkv]
                spare = (1 - hh) * FOX_DH
                vh = jnp.where(masks[hh], v[:nkv], (lane == spare).astype(BF16))
                pv = lax.dot_general(pb, vh, _DIMS["nn"], preferred_element_type=F32)
                psum = jnp.broadcast_to(pv[:, spare:spare + 1], (rb, LANES))
                alpha = a_scr[b]
                l_scr[hh, rows, :] = alpha * l_scr[hh, rows, :] + psum
                acc = acc_scr[rows, :]
                acc_scr[rows, :] = jnp.where(masks[hh], acc * alpha + pv, acc)

        @pl.when(j < i)
        def _():
            compute(False)

        @pl.when(j == i)
        def _():
            compute(True)
            lo = _iota((1, LANES), 1) < FOX_DH
            o_ref[...] = acc_scr[...] / jnp.where(lo, l_scr[0], l_scr[1])
            lse_ref[...] = m_scr[...] + jnp.log(l_scr[...])

    qs = pl.BlockSpec((tb, LANES), lambda h, t, ti_, tj_: (ti_[t], h))
    qs1 = pl.BlockSpec((tb, LANES), lambda h, t, ti_, tj_: (ti_[t], HP + h))
    ks = pl.BlockSpec((tb, LANES), lambda h, t, ti_, tj_: (tj_[t], h))
    crs = pl.BlockSpec((None, SUBLANES, tb), lambda h, t, ti_, tj_: (h, 0, tj_[t]))
    return pl.pallas_call(
        body, name=name,
        grid_spec=pltpu.PrefetchScalarGridSpec(
            num_scalar_prefetch=2, grid=(HP, len(steps)),
            in_specs=[qs, ks, ks, qs, qs1, crs],
            out_specs=[qs, pl.BlockSpec((2, tb, LANES), lambda h, t, ti_, tj_: (0, ti_[t], h))],
            scratch_shapes=[pltpu.VMEM((2, tb, LANES), F32), pltpu.VMEM((2, tb, LANES), F32),
                            pltpu.VMEM((tb, LANES), F32), pltpu.VMEM((nblk, rb, tb), F32),
                            pltpu.VMEM((nblk, rb, tb), BF16), pltpu.VMEM((nblk, rb, LANES), F32)]),
        out_shape=[jax.ShapeDtypeStruct((S_len, W), F32), jax.ShapeDtypeStruct((2, S_len, W), F32)],
        compiler_params=_params("parallel", "arbitrary"),
    )(ti, tj, qn, kn, vb, c_b, c_b, c_rowp)


def _fox_attn_bwd(qn, kn, vb, c_b, c_rowp, lse_b, delta_b, do, *, tb, name):
    S_len, W = qn.shape
    HP = W // LANES
    tb = min(tb, S_len)
    nb = S_len // tb
    scale = FOX_DH ** -0.5
    rb, cb = min(ATTN_ROWS, tb), min(ATTN_COLS, tb)

    steps = [(j, i) for j in range(nb) for i in range(j, nb)]
    tj = jnp.asarray([s[0] for s in steps], jnp.int32)
    ti = jnp.asarray([s[1] for s in steps], jnp.int32)

    def body(tj_ref, ti_ref, q_ref, k_ref, v_ref, cb0_ref, cb1_ref, cr_ref, lse_ref, dl0_ref, dl1_ref, do_ref,
             dq_ref, dk_ref, dv_ref, dcr_ref, dct_ref, dk_scr, dv_scr, dc_scr, p_scr, ds_scr):
        t = pl.program_id(1)
        j, i = tj_ref[t], ti_ref[t]

        @pl.when(t == 0)
        def _():
            dq_ref[...] = jnp.zeros(dq_ref.shape, F32)
            dct_ref[...] = jnp.zeros(dct_ref.shape, F32)

        @pl.when(i == j)
        def _():
            dk_scr[...] = jnp.zeros(dk_scr.shape, F32)
            dv_scr[...] = jnp.zeros(dv_scr.shape, F32)
            dc_scr[...] = jnp.zeros(dc_scr.shape, F32)

        def compute(diag):
            lo = _iota((1, LANES), 1) < FOX_DH
            masks = [lo, jnp.logical_not(lo)]
            row0 = pl.multiple_of(i * tb, tb)
            npiece = tb // LANES
            colsum = [[None] * npiece for _ in range(2)]

            def visible(r):
                return ((r + 1) * rb - 1) // LANES + 1 if diag else npiece

            for hh in range(2):
                for r in range(tb // rb):
                    rows = slice(r * rb, (r + 1) * rb)
                    qr = q_ref[rows, :]
                    qh = jnp.where(masks[hh], qr * scale, jnp.zeros_like(qr))
                    doh = jnp.where(masks[hh], do_ref[rows, :], 0.0).astype(BF16)
                    bq = (cb0_ref if hh == 0 else cb1_ref)[rows, :] - lse_ref[hh, rows, :]
                    dlt = (dl0_ref if hh == 0 else dl1_ref)[rows, :]
                    rsum = None
                    for c in range(tb // cb):
                        first, last = c * cb // LANES, min((c + 1) * cb // LANES, visible(r))
                        for piece in range(max(first, last), (c + 1) * cb // LANES):
                            cols = slice(piece * LANES, (piece + 1) * LANES)
                            p_scr[hh, rows, cols] = jnp.zeros((rb, LANES), BF16)
                            ds_scr[hh, rows, cols] = jnp.zeros((rb, LANES), BF16)
                        if first >= last:
                            continue
                        s2 = lax.dot_general(qh, k_ref[c * cb:(c + 1) * cb, :], _DIMS["nt"], preferred_element_type=F32)
                        dp2 = lax.dot_general(doh, v_ref[c * cb:(c + 1) * cb, :], _DIMS["nt"], preferred_element_type=F32)
                        for piece in range(first, last):
                            cols = slice(piece * LANES, (piece + 1) * LANES)
                            sub = slice(piece * LANES - c * cb, (piece + 1) * LANES - c * cb)
                            s = s2[:, sub] + bq - cr_ref[hh:hh + 1, cols]
                            if diag and (piece + 1) * LANES - 1 > r * rb:
                                keep = piece * LANES + _iota((rb, LANES), 1) <= r * rb + _iota((rb, LANES), 0)
                                s = jnp.where(keep, s, -jnp.inf)
                            p = jnp.exp(s)
                            ds = p * (dp2[:, sub] - dlt)
                            p_scr[hh, rows, cols] = p.astype(BF16)
                            ds_scr[hh, rows, cols] = ds.astype(BF16)
                            rsum = ds if rsum is None else rsum + ds
                            csum = jnp.sum(ds, axis=0, keepdims=True)
                            colsum[hh][piece] = csum if colsum[hh][piece] is None else colsum[hh][piece] + csum
                    grow = pl.ds(row0 + r * rb, rb)
                    dct_ref[grow, :] += jnp.where(masks[hh], jnp.sum(rsum, -1, keepdims=True), 0.0)
            k = k_ref[...]
            qf = q_ref[...]
            dof = do_ref[...]
            dq_part = jnp.zeros((tb, LANES), F32)
            for hh in range(2):
                kh = jnp.where(masks[hh], k, jnp.zeros_like(k))
                qhf = jnp.where(masks[hh], qf * scale, jnp.zeros_like(qf))
                dohf = jnp.where(masks[hh], dof, 0.0).astype(BF16)
                dv_scr[...] += lax.dot_general(p_scr[hh], dohf, _DIMS["tn"], preferred_element_type=F32)
                dk_scr[...] += lax.dot_general(ds_scr[hh], qhf, _DIMS["tn"], preferred_element_type=F32)
                dq_part = dq_part + lax.dot_general(ds_scr[hh], kh, _DIMS["nn"], preferred_element_type=F32)
                for piece in range(npiece):
                    if colsum[hh][piece] is not None:
                        dc_scr[hh:hh + 1, piece * LANES:(piece + 1) * LANES] -= colsum[hh][piece]
            dq_ref[pl.ds(row0, tb), :] += dq_part * scale

        @pl.when(i > j)
        def _():
            compute(False)

        @pl.when(i == j)
        def _():
            compute(True)

        @pl.when(i == nb - 1)
        def _():
            dk_ref[...] = dk_scr[...]
            dv_ref[...] = dv_scr[...]
            dcr_ref[...] = dc_scr[...]

    qs = pl.BlockSpec((tb, LANES), lambda h, t, tj_, ti_: (ti_[t], h))
    qs1 = pl.BlockSpec((tb, LANES), lambda h, t, tj_, ti_: (ti_[t], HP + h))
    ks = pl.BlockSpec((tb, LANES), lambda h, t, tj_, ti_: (tj_[t], h))
    crs = pl.BlockSpec((None, SUBLANES, tb), lambda h, t, tj_, ti_: (h, 0, tj_[t]))
    whole = pl.BlockSpec((S_len, LANES), lambda h, t, tj_, ti_: (0, h))
    return pl.pallas_call(
        body, name=name,
        grid_spec=pltpu.PrefetchScalarGridSpec(
            num_scalar_prefetch=2, grid=(HP, len(steps)),
            in_specs=[qs, ks, ks, qs, qs1, crs, pl.BlockSpec((2, tb, LANES), lambda h, t, tj_, ti_: (0, ti_[t], h)),
                      qs, qs1, qs],
            out_specs=[whole, ks, ks, crs, whole],
            scratch_shapes=[pltpu.VMEM((tb, LANES), F32), pltpu.VMEM((tb, LANES), F32),
                            pltpu.VMEM((SUBLANES, tb), F32), pltpu.VMEM((2, tb, tb), BF16),
                            pltpu.VMEM((2, tb, tb), BF16)]),
        out_shape=[jax.ShapeDtypeStruct((S_len, W), F32)] * 3 + [jax.ShapeDtypeStruct((HP, SUBLANES, S_len), F32),
                                                                 jax.ShapeDtypeStruct((S_len, W), F32)],
        compiler_params=_params("parallel", "arbitrary"),
    )(tj, ti, qn, kn, vb, c_b, c_b, c_rowp, lse_b, delta_b, delta_b, do)


def _fox_post_tile(o, z):
    return o * _silu(z)


def _fox_post_fwd(o, h_main, *, tr, name):
    W = o.shape[1]

    def body(rows, consts, orows, oaccs, scr, step, blk):
        orows[0][...] = _fox_post_tile(rows[0][...], rows[1][...]).astype(BF16)

    z = (h_main, lambda tr_, n: pl.BlockSpec((tr_, W), lambda i: (i, 3)))
    return _rowcall(body, [o, z], [], [(W, BF16)], [], tr=tr, name=name)[0]


def _fox_post_bwd(o, h_main, d_og, *, tr, name):
    W = o.shape[1]

    def body(rows, consts, orows, oaccs, scr, step, blk):
        _, vjp = jax.vjp(_fox_post_tile, rows[0][...], rows[1][...])
        d_o, d_z = vjp(rows[2][...])
        orows[0][...] = d_o
        orows[1][...] = d_z.astype(BF16)
        lo_rows = (_iota((LANES, LANES), 0) < FOX_DH)
        for s in range(W // LANES):
            ls = slice(s * LANES, (s + 1) * LANES)
            prod = d_o[:, ls] * rows[0][:, ls]
            orows[2][:, ls] = _raw_mm(prod, lo_rows.astype(F32), "nn", "cb")
            orows[2][:, W + s * LANES:W + (s + 1) * LANES] = _raw_mm(prod, jnp.logical_not(lo_rows).astype(F32), "nn", "cb")

    z = (h_main, lambda tr_, n: pl.BlockSpec((tr_, W), lambda i: (i, 3)))
    return _rowcall(body, [o, z, d_og], [], [(W, F32), (W, BF16), (2 * W, F32)], [], tr=tr, name=name)


MESH_IDS = pl.DeviceIdType.MESH
N_CHIPS = 4
N_DEV = 8
_ANY = pl.BlockSpec(memory_space=pl.ANY)


def _xy_exchange(src, *, gather, name):
    def body(src_ref, out_ref, send_sems, recv_sems, local_sem):
        _xy_start(src_ref, out_ref, send_sems, recv_sems, local_sem, gather)
        _xy_wait(src_ref, out_ref, send_sems, recv_sems, local_sem, gather)

    return pl.pallas_call(
        body, name=name, in_specs=[_ANY], out_specs=_ANY,
        out_shape=_xy_out_shape(src), scratch_shapes=_xy_sems(),
    )(src)


def _xy_out_shape(src):
    return jax.ShapeDtypeStruct((N_CHIPS,) + tuple(src.shape[-2:]), src.dtype)


def _xy_sems():
    return [pltpu.SemaphoreType.DMA((N_CHIPS - 1,)), pltpu.SemaphoreType.DMA((N_CHIPS - 1,)), pltpu.SemaphoreType.DMA]


def _xy_copies(src_ref, out_ref, send_sems, recv_sems, local_sem, gather, with_arrivals=True):
    x, y, c = lax.axis_index("x"), lax.axis_index("y"), lax.axis_index("c")
    me = 2 * x + y
    peers = [(1 - x, y), (x, 1 - y), (1 - x, 1 - y)]

    def outgoing(px, py):
        return src_ref if gather else src_ref.at[2 * px + py]

    def copy(j, px, py, slot):
        return pltpu.make_async_remote_copy(
            src_ref=outgoing(px, py), dst_ref=out_ref.at[slot], send_sem=send_sems.at[j], recv_sem=recv_sems.at[j],
            device_id=(px, py, c), device_id_type=MESH_IDS)

    mine = pltpu.make_async_copy(outgoing(x, y), out_ref.at[me], local_sem)
    sends = [copy(j, px, py, me) for j, (px, py) in enumerate(peers)]
    arrivals = [copy(j, px, py, 2 * px + py) for j, (px, py) in enumerate(peers)] if with_arrivals else []
    return mine, sends, arrivals


def _xy_start(src_ref, out_ref, send_sems, recv_sems, local_sem, gather):
    mine, sends, _ = _xy_copies(src_ref, out_ref, send_sems, recv_sems, local_sem, gather, with_arrivals=False)
    mine.start()
    for cp in sends:
        cp.start()


def _xy_wait(src_ref, out_ref, send_sems, recv_sems, local_sem, gather):
    mine, sends, arrivals = _xy_copies(src_ref, out_ref, send_sems, recv_sems, local_sem, gather)
    for cp in arrivals:
        cp.wait_recv()
    for cp in sends:
        cp.wait_send()
    mine.wait()


def _c_swap(src, *, name):
    def body(src_ref, out_ref, send_sem, recv_sem):
        x, y, c = lax.axis_index("x"), lax.axis_index("y"), lax.axis_index("c")
        cp = pltpu.make_async_remote_copy(
            src_ref=src_ref, dst_ref=out_ref, send_sem=send_sem, recv_sem=recv_sem,
            device_id=(x, y, 1 - c), device_id_type=MESH_IDS)
        cp.start()
        cp.wait()

    return pl.pallas_call(
        body, name=name, in_specs=[_ANY], out_specs=_ANY,
        out_shape=jax.ShapeDtypeStruct(src.shape, src.dtype),
        scratch_shapes=[pltpu.SemaphoreType.DMA, pltpu.SemaphoreType.DMA],
    )(src)


def _all_gather8(blk, *, name):
    m_per, n = blk.shape

    def body(x_ref, out_ref, send_sems, recv_sems, local_sem):
        x, y, c = lax.axis_index("x"), lax.axis_index("y"), lax.axis_index("c")
        me, sibling = (x, y, c), (x, y, 1 - c)
        chips = [(1 - x, y), (x, 1 - y), (1 - x, 1 - y)]

        def rows(px, py, pc):
            return out_ref.at[pl.ds((4 * px + 2 * py + pc) * m_per, m_per), :]

        def copy(k, block, to, src=None):
            return pltpu.make_async_remote_copy(
                src_ref=rows(*block) if src is None else src, dst_ref=rows(*block),
                send_sem=send_sems.at[k], recv_sem=recv_sems.at[k], device_id=to, device_id_type=MESH_IDS)

        mine = pltpu.make_async_copy(x_ref, rows(*me), local_sem)
        mine.start()
        first = [copy(0, me, sibling, src=x_ref)]
        first += [copy(1 + j, me, (*chip, c), src=x_ref) for j, chip in enumerate(chips)]
        for cp in first:
            cp.start()
        passed = [copy(4 + j, (*chip, c), sibling) for j, chip in enumerate(chips)]
        for j, chip in enumerate(chips):
            copy(1 + j, (*chip, c), me).wait_recv()
            passed[j].start()
        copy(0, sibling, me).wait_recv()
        for j, chip in enumerate(chips):
            copy(4 + j, (*chip, 1 - c), me).wait_recv()
        for cp in first + passed:
            cp.wait_send()
        mine.wait()

    return pl.pallas_call(
        body, name=name,
        out_shape=jax.ShapeDtypeStruct((N_DEV * m_per, n), blk.dtype),
        in_specs=[pl.BlockSpec(memory_space=pltpu.VMEM)], out_specs=pl.BlockSpec(memory_space=pltpu.VMEM),
        scratch_shapes=[pltpu.SemaphoreType.DMA((7,)), pltpu.SemaphoreType.DMA((7,)), pltpu.SemaphoreType.DMA],
    )(blk)


def _sum_slots(parts, *, tr, name):
    n, R, _ = parts.shape
    pack = 2 * SUBLANES
    tr = max(t for t in range(pack, min(tr, R) + 1, pack) if R % t == 0) if R % pack == 0 else R

    def body(p_ref, o_ref):
        tot = p_ref[0].astype(F32)
        for s in range(1, n):
            tot = tot + p_ref[s].astype(F32)
        o_ref[...] = tot

    return pl.pallas_call(
        body, name=name, grid=(R // tr,),
        in_specs=[pl.BlockSpec((n, tr, LANES), lambda i: (0, i, 0))], out_specs=pl.BlockSpec((tr, LANES), lambda i: (i, 0)),
        out_shape=jax.ShapeDtypeStruct((R, LANES), F32), compiler_params=_params("parallel"),
    )(parts)


def _adamw(w, g_parts, m, v, *, name):
    shape = w.shape
    as2d = lambda a: a.reshape(-1, shape[-1])
    w2, m2, v2 = as2d(w), as2d(m), as2d(v)
    gs = [as2d(g) for g in g_parts]
    R, C = w2.shape
    tr = R
    while tr * C * 4 > (1 << 20) and tr % 2 == 0 and (tr // 2) % SUBLANES == 0:
        tr //= 2
    ng = len(gs)

    def body(*refs):
        w_ref, m_ref, v_ref = refs[:3]
        g_refs = refs[3:3 + ng]
        go_ref, d_ref, mo_ref, vo_ref = refs[3 + ng:]
        g = g_refs[0][...]
        for r in g_refs[1:]:
            g = g + r[...]
        mn = ADAM_B1 * m_ref[...] + (1.0 - ADAM_B1) * g
        vn = ADAM_B2 * v_ref[...] + (1.0 - ADAM_B2) * jnp.square(g)
        m_hat = mn / (1.0 - ADAM_B1 ** ADAM_STEP)
        v_hat = vn / (1.0 - ADAM_B2 ** ADAM_STEP)
        go_ref[...] = g
        d_ref[...] = -ADAM_LR * (m_hat / (jnp.sqrt(v_hat) + ADAM_EPS) + ADAM_WD * w_ref[...])
        mo_ref[...] = mn
        vo_ref[...] = vn

    spec = pl.BlockSpec((tr, C), lambda i: (i, 0))
    outs = pl.pallas_call(
        body, name=name, grid=(R // tr,), in_specs=[spec] * (3 + ng), out_specs=[spec] * 4,
        out_shape=[jax.ShapeDtypeStruct((R, C), F32)] * 4, compiler_params=_params("parallel"),
    )(w2, m2, v2, *gs)
    return tuple(o.reshape(shape) for o in outs)


TR = 256
ATTN_TILE = 1024
ATTN_ROWS = 128
ATTN_COLS = 256


def _mm_nn(a, b, name, **kw):
    return _matmul(a, b, tm=2048, tn=1024, tk=1024, name=name, **kw)


def _mm_nt(a, b, name, **kw):
    return _matmul(a, b, tb=True, tm=1024, tn=1024, tk=1024, name=name, **kw)


def _mm_tn(a, b, name, **kw):
    return _matmul(a, b, ta=True, tm=1024, tn=2048, tk=512, name=name, **kw)


def _c_rows(c, Hf):
    S_len = c.shape[0]
    ct = c[:, :Hf].T.reshape(Hf // 2, 2, S_len)
    return jnp.pad(ct, ((0, 0), (0, SUBLANES - 2), (0, 0)))


def _local_step(x, p, target, wts, late_weights=None, early_grads=None):
    L = len(wts["ln_g"])
    alpha = (2 * L) ** 0.25
    Hg = wts["gdn_a_log"][0].shape[-1]
    Hf = wts["fox_b_f"][0].shape[-1]
    Wg_ = Hg * GDN_DK
    Wf_ = Hf * FOX_DH
    saved = []
    for i in range(L):
        j = i // 2
        sv = {"x": x}
        if i % 2 == 0:
            w_in = wts["gdn_w_in"][j]
            wm, ws = w_in[:, :4 * Wg_], _pad_lanes(w_in[:, 4 * Wg_:])
            cw8 = jnp.pad(wts["gdn_conv_w"][j], ((0, SUBLANES - GDN_CONV), (0, 0)))
            alog = _pad_lanes(_row(wts["gdn_a_log"][j]), offset=Hg)
            dtb = _pad_lanes(_row(wts["gdn_dt_bias"][j]), offset=Hg)
            hm = _mm_nn(x, wm, f"gdn{j}_in_main")
            hs = _mm_nn(x, ws, f"gdn{j}_in_small")
            q, k, v, bg = _gdn_pre_fwd(hm, hs, cw8, alog, dtb, Hg, tr=TR, name=f"gdn{j}_pre")
            if i == 0 and late_weights is not None:
                o, states, tinv, got = _gdn_rule_fwd(q, k, v, bg, Hg, name=f"gdn{j}_rule", carry=(late_weights[0], True))
                for (wname_, idx), arr in late_weights[1](got).items():
                    wts[wname_][idx] = arr
            else:
                o, states, tinv = _gdn_rule_fwd(q, k, v, bg, Hg, name=f"gdn{j}_rule")
            on = _gdn_post_fwd(o, hm, wts["gdn_norm_g"][j], Hg, tr=TR, name=f"gdn{j}_post")
            y = _mm_nn(on, wts["gdn_w_out"][j], f"gdn{j}_out")
            sv.update(wm=wm, ws=ws, cw8=cw8, alog=alog, dtb=dtb, hm=hm, hs=hs, q=q, k=k, v=v, bg=bg, o=o,
                      states=states, tinv=tinv, on=on)
        else:
            w_in = wts["fox_w_in"][j]
            wm, ws = w_in[:, :4 * Wf_], _pad_lanes(w_in[:, 4 * Wf_:])
            gq2 = _row(jnp.tile(wts["fox_q_norm_g"][j], 2))
            gk2 = _row(jnp.tile(wts["fox_k_norm_g"][j], 2))
            bf = _pad_lanes(_row(wts["fox_b_f"][j]))
            hm = _mm_nn(x, wm, f"fox{j}_in_main")
            hs = _mm_nn(x, ws, f"fox{j}_in_small")
            qn, kn, vb, c, c_b = _fox_pre_fwd(hm, hs, gq2, gk2, bf, Hf, tr=TR, name=f"fox{j}_pre")
            c_rowp = _c_rows(c, Hf)
            o, lse_b = _fox_attn_fwd(qn, kn, vb, c_b, c_rowp, tb=ATTN_TILE, name=f"fox{j}_attn")
            on = _fox_post_fwd(o, hm, tr=TR, name=f"fox{j}_post")
            y = _mm_nn(on, wts["fox_w_out"][j], f"fox{j}_out")
            sv.update(wm=wm, ws=ws, gq2=gq2, gk2=gk2, bf=bf, hm=hm, hs=hs, qn=qn, kn=kn, vb=vb, c_b=c_b,
                      c_rowp=c_rowp, o=o, lse_b=lse_b, on=on)
        x_ln = _ln_fwd(x, y, wts["ln_g"][i], wts["ln_b"][i], alpha, tr=TR, name=f"ln{i}")
        gp = _mm_nn(x_ln, wts["ple_w_gate"][i], f"ple{i}_gate")
        pp = _mm_nn(p[i], wts["ple_w_proj"][i], f"ple{i}_proj")
        x_out = _ple_fwd(x_ln, gp, pp, tr=TR, name=f"ple{i}_mix")
        sv.update(y=y, x_ln=x_ln, gp=gp, pp=pp)
        saved.append(sv)
        x = x_out

    loss_row, dx = _loss_fwd_bwd(x, target, tr=TR, name="loss")

    g = {n: [None] * len(wts[n]) for n in wts}
    got_early = None
    for i in reversed(range(L)):
        j = i // 2
        sv = saved[i]
        d_pre, d_pp = _ple_bwd(dx, sv["gp"], sv["pp"], tr=TR, name=f"ple{i}_mix_bwd")
        g["ple_w_gate"][i] = _mm_tn(sv["x_ln"], d_pre, f"ple{i}_gate_dw")
        g["ple_w_proj"][i] = _mm_tn(p[i], d_pp, f"ple{i}_proj_dw")
        t = _mm_nt(d_pre, wts["ple_w_gate"][i], f"ple{i}_gate_dx")
        du, g["ln_g"][i], g["ln_b"][i] = _ln_bwd(sv["x"], sv["y"], wts["ln_g"][i], wts["ln_b"][i], dx, t, alpha,
                                                 tr=TR, name=f"ln{i}_bwd")
        if i % 2 == 0:
            g["gdn_w_out"][j] = _mm_tn(sv["on"], du, f"gdn{j}_out_dw")
            d_on = _mm_nt(du, wts["gdn_w_out"][j], f"gdn{j}_out_dx")
            d_o, d_z, d_ng = _gdn_post_bwd(sv["o"], sv["hm"], wts["gdn_norm_g"][j], d_on, Hg, tr=TR, name=f"gdn{j}_post_bwd")
            rule_args = (sv["q"], sv["k"], sv["v"], sv["bg"], sv["states"], sv["tinv"], d_o, Hg)
            if i == 0 and early_grads is not None:
                dq, dk, dv, dbg, got_early = _gdn_rule_bwd(*rule_args, name=f"gdn{j}_rule_bwd",
                                                           carry=(early_grads(g), False))
            else:
                dq, dk, dv, dbg = _gdn_rule_bwd(*rule_args, name=f"gdn{j}_rule_bwd")
            d_hm, d_hs, d_cw, d_al, d_dtb = _gdn_pre_bwd(sv["hm"], sv["hs"], sv["cw8"], sv["alog"], sv["dtb"],
                                                         dq, dk, dv, dbg, d_z, Hg, tr=TR, name=f"gdn{j}_pre_bwd")
            g["gdn_norm_g"][j] = d_ng[0]
            g["gdn_conv_w"][j] = d_cw[:GDN_CONV]
            g["gdn_a_log"][j] = d_al[0, Hg:2 * Hg]
            g["gdn_dt_bias"][j] = d_dtb[0, Hg:2 * Hg]
            wname, nsmall = "gdn_w_in", 2 * Hg
        else:
            g["fox_w_out"][j] = _mm_tn(sv["on"], du, f"fox{j}_out_dw")
            d_og = _mm_nt(du, wts["fox_w_out"][j], f"fox{j}_out_dx")
            d_o, d_z, delta_b = _fox_post_bwd(sv["o"], sv["hm"], d_og, tr=TR, name=f"fox{j}_post_bwd")
            dqn, dkn, dvv, dcr, dct = _fox_attn_bwd(sv["qn"], sv["kn"], sv["vb"], sv["c_b"], sv["c_rowp"], sv["lse_b"],
                                                    delta_b, d_o, tb=ATTN_TILE, name=f"fox{j}_attn_bwd")
            dc = _pad_lanes(dcr[:, :2, :].reshape(Hf, -1).T + dct[:, ::FOX_DH])
            d_hm, d_hs, d_gq, d_gk, d_bf = _fox_pre_bwd(sv["hm"], sv["hs"], sv["gq2"], sv["gk2"], sv["bf"],
                                                        dqn, dkn, dvv, d_z, dc, Hf, tr=TR, name=f"fox{j}_pre_bwd")
            g["fox_q_norm_g"][j] = d_gq[0, :FOX_DH] + d_gq[0, FOX_DH:]
            g["fox_k_norm_g"][j] = d_gk[0, :FOX_DH] + d_gk[0, FOX_DH:]
            g["fox_b_f"][j] = d_bf[0, :Hf]
            wname, nsmall = "fox_w_in", Hf
        dwm = _mm_tn(sv["x"], d_hm, f"{wname}{j}_main_dw")
        dws = _mm_tn(sv["x"], d_hs, f"{wname}{j}_small_dw")
        g[wname][j] = jnp.concatenate([dwm, dws[:, :nsmall]], axis=1)
        t1 = _mm_nt(d_hs, sv["ws"], f"{wname}{j}_small_dx", add=du, add_scale=alpha)
        dx = _mm_nt(d_hm, sv["wm"], f"{wname}{j}_main_dx", add=t1)
    return loss_row, dx, g, got_early


_SHARDED = (("ple_w_gate", 1), ("ple_w_proj", 2), ("gdn_w_in", 2), ("gdn_conv_w", 2), ("gdn_w_out", 1),
            ("fox_w_in", 2), ("fox_w_out", 1))
_SHARD_AXIS = dict(_SHARDED)
_REPLICATED = ("ln_g", "ln_b", "gdn_a_log", "gdn_dt_bias", "gdn_norm_g", "fox_b_f", "fox_q_norm_g", "fox_k_norm_g")
_EXACT = ("gdn_conv_w",)
_ORDER = ("ln_g", "ln_b", "ple_w_gate", "ple_w_proj", "gdn_w_in", "gdn_conv_w", "gdn_a_log", "gdn_dt_bias",
          "gdn_norm_g", "gdn_w_out", "fox_w_in", "fox_b_f", "fox_q_norm_g", "fox_k_norm_g", "fox_w_out")
_FIRST_WEIGHTS = (("gdn_w_in", 0), ("gdn_conv_w", 0))
_LAST_GRADS = (("ple_w_gate", 0), ("ple_w_proj", 0), ("gdn_w_in", 0), ("gdn_conv_w", 0), ("gdn_w_out", 0))
PACK_ROWS = 2 * SUBLANES


def _as_rows(a):
    rows = a.reshape(-1, LANES)
    return jnp.pad(rows, ((0, -rows.shape[0] % PACK_ROWS), (0, 0)))


def _n_elements(shape):
    n = 1
    for d in shape:
        n *= d
    return n


def _pack_weights(local, items):
    parts = []
    for name, idx in items:
        w = local[name][idx]
        parts.append(_as_rows(lax.bitcast_convert_type(w, BF16) if name in _EXACT else w.astype(BF16)))
    return jnp.concatenate(parts, axis=0), [q.shape[0] for q in parts]


def _unpack_weights(got, local, items, sizes):
    out, r0 = {}, 0
    for (name, idx), nrow in zip(items, sizes):
        shp = tuple(local[name].shape[1:])
        n_el = _n_elements(shp) * (2 if name in _EXACT else 1)
        seg = got[:, r0:r0 + nrow].reshape(N_CHIPS, -1)[:, :n_el]
        r0 += nrow
        if name in _EXACT:
            blocks = lax.bitcast_convert_type(seg.reshape((N_CHIPS,) + shp + (2,)), F32)
        else:
            blocks = seg.reshape((N_CHIPS,) + shp)
        out[(name, idx)] = jnp.concatenate([blocks[s] for s in range(N_CHIPS)], axis=_SHARD_AXIS[name] - 1)
    return out


def _pack_grads(g, items):
    per_owner, sizes = [], []
    for s in range(N_CHIPS):
        parts = []
        for name, idx in items:
            gfull = g[name][idx]
            axis = _SHARD_AXIS[name] - 1
            n = gfull.shape[axis] // N_CHIPS
            parts.append(_as_rows(lax.slice_in_dim(gfull, s * n, (s + 1) * n, axis=axis).astype(BF16)))
        sizes = [q.shape[0] for q in parts]
        per_owner.append(jnp.concatenate(parts, axis=0))
    return jnp.stack(per_owner), sizes


def _unpack_grads(flat, local, items, sizes):
    out, r0 = {}, 0
    for (name, idx), nrow in zip(items, sizes):
        shp = tuple(local[name].shape[1:])
        out[(name, idx)] = flat[r0:r0 + nrow].reshape(-1)[:_n_elements(shp)].reshape(shp)
        r0 += nrow
    return out


def _reduce_replicated(grads, loss_part):
    rows = [_pad_lanes(jnp.reshape(loss_part, (1, 1)))]
    for name in _REPLICATED:
        gr = grads[name]
        rows.append(gr.reshape(-1, LANES) if gr.shape[-1] % LANES == 0 else _pad_lanes(gr))
    sizes = [r.shape[0] for r in rows]
    blk = jnp.concatenate(rows, axis=0)
    nrow = blk.shape[0]
    npad = -nrow % SUBLANES
    blk = jnp.pad(blk, ((0, npad), (0, 0)))
    allb = _all_gather8(blk, name="gather_small_grads").reshape(N_DEV, nrow + npad, LANES)
    tot = _sum_slots(allb, tr=nrow + npad, name="sum_small_grads")
    out, r0 = {}, sizes[0]
    loss = tot[0, 0]
    for name, n in zip(_REPLICATED, sizes[1:]):
        gr = grads[name]
        seg = tot[r0:r0 + n]
        out[name] = seg.reshape(gr.shape) if gr.shape[-1] % LANES == 0 else seg[:, :gr.shape[-1]]
        r0 += n
    return loss, out


def kernel(x, p, ln_g, ln_b, ple_w_gate, ple_w_proj, gdn_w_in, gdn_conv_w, gdn_a_log, gdn_dt_bias, gdn_norm_g, gdn_w_out, fox_w_in, fox_b_f, fox_q_norm_g, fox_k_norm_g, fox_w_out, loss_target, m_ln_g, m_ln_b, m_ple_w_gate, m_ple_w_proj, m_gdn_w_in, m_gdn_conv_w, m_gdn_a_log, m_gdn_dt_bias, m_gdn_norm_g, m_gdn_w_out, m_fox_w_in, m_fox_b_f, m_fox_q_norm_g, m_fox_k_norm_g, m_fox_w_out, v_ln_g, v_ln_b, v_ple_w_gate, v_ple_w_proj, v_gdn_w_in, v_gdn_conv_w, v_gdn_a_log, v_gdn_dt_bias, v_gdn_norm_g, v_gdn_w_out, v_fox_w_in, v_fox_b_f, v_fox_q_norm_g, v_fox_k_norm_g, v_fox_w_out):
    local = dict(ln_g=ln_g, ln_b=ln_b, ple_w_gate=ple_w_gate, ple_w_proj=ple_w_proj, gdn_w_in=gdn_w_in,
                 gdn_conv_w=gdn_conv_w, gdn_a_log=gdn_a_log, gdn_dt_bias=gdn_dt_bias, gdn_norm_g=gdn_norm_g,
                 gdn_w_out=gdn_w_out, fox_w_in=fox_w_in, fox_b_f=fox_b_f, fox_q_norm_g=fox_q_norm_g,
                 fox_k_norm_g=fox_k_norm_g, fox_w_out=fox_w_out)
    mom_m = dict(ln_g=m_ln_g, ln_b=m_ln_b, ple_w_gate=m_ple_w_gate, ple_w_proj=m_ple_w_proj, gdn_w_in=m_gdn_w_in,
                 gdn_conv_w=m_gdn_conv_w, gdn_a_log=m_gdn_a_log, gdn_dt_bias=m_gdn_dt_bias, gdn_norm_g=m_gdn_norm_g,
                 gdn_w_out=m_gdn_w_out, fox_w_in=m_fox_w_in, fox_b_f=m_fox_b_f, fox_q_norm_g=m_fox_q_norm_g,
                 fox_k_norm_g=m_fox_k_norm_g, fox_w_out=m_fox_w_out)
    mom_v = dict(ln_g=v_ln_g, ln_b=v_ln_b, ple_w_gate=v_ple_w_gate, ple_w_proj=v_ple_w_proj, gdn_w_in=v_gdn_w_in,
                 gdn_conv_w=v_gdn_conv_w, gdn_a_log=v_gdn_a_log, gdn_dt_bias=v_gdn_dt_bias, gdn_norm_g=v_gdn_norm_g,
                 gdn_w_out=v_gdn_w_out, fox_w_in=v_fox_w_in, fox_b_f=v_fox_b_f, fox_q_norm_g=v_fox_q_norm_g,
                 fox_k_norm_g=v_fox_k_norm_g, fox_w_out=v_fox_w_out)

    items = [(name, idx) for name, _ in _SHARDED for idx in range(local[name].shape[0])]
    w_first = [it for it in items if it in _FIRST_WEIGHTS]
    w_later = [it for it in items if it not in _FIRST_WEIGHTS]
    g_early = [it for it in items if it not in _LAST_GRADS]
    g_last = [it for it in items if it in _LAST_GRADS]

    wts = {name: [None] * local[name].shape[0] for name, _ in _SHARDED}
    for name in _REPLICATED:
        wts[name] = local[name]
    packed, sizes = _pack_weights(local, w_first)
    got = _xy_exchange(packed, gather=True, name="gather_weights_first")
    for (name, idx), arr in _unpack_weights(got, local, w_first, sizes).items():
        wts[name][idx] = arr
    packed_later, sizes_later = _pack_weights(local, w_later)
    early_sizes = []

    def pack_early(g):
        packed_g, sz = _pack_grads(g, g_early)
        early_sizes.extend(sz)
        return packed_g

    loss_row, dx, g, got_early = _local_step(
        x[0], p[:, 0], loss_target[0], wts,
        late_weights=(packed_later, lambda res: _unpack_weights(res, local, w_later, sizes_later)),
        early_grads=pack_early)

    loss, small = _reduce_replicated({name: jnp.stack(g[name]) for name in _REPLICATED}, jnp.sum(loss_row))
    packed_last, last_sizes = _pack_grads(g, g_last)
    got_last = _xy_exchange(packed_last, gather=False, name="exchange_grads_last")
    mine, other = {}, {}
    for tag, res, its, szs in (("early", got_early, g_early, early_sizes), ("last", got_last, g_last, last_sizes)):
        part = _sum_slots(res, tr=4096, name=f"sum_grads_{tag}")
        sib = _c_swap(part, name=f"swap_grads_{tag}")
        mine.update(_unpack_grads(part, local, its, szs))
        other.update(_unpack_grads(sib, local, its, szs))

    outs = {}
    for name in _ORDER:
        if name in _SHARD_AXIS:
            n_layers = local[name].shape[0]
            parts = [jnp.stack([mine[(name, i)] for i in range(n_layers)]),
                     jnp.stack([other[(name, i)] for i in range(n_layers)])]
        else:
            parts = [small[name]]
        outs[name] = _adamw(local[name], parts, mom_m[name], mom_v[name], name=f"adamw_{name}")
    return (loss, dx[None], *[outs[n][0] for n in _ORDER], *[outs[n][1] for n in _ORDER],
            *[outs[n][2] for n in _ORDER], *[outs[n][3] for n in _ORDER])
```

```python
import functools

import jax
import jax.numpy as jnp
from jax import lax
from jax.experimental import pallas as pl
from jax.experimental.pallas import tpu as pltpu

F32 = jnp.float32
BF16 = jnp.bfloat16

LANES = 128
SUBLANES = 8
VMEM_LIMIT_BYTES = 56 * 1024 * 1024

GDN_DK = 128
GDN_CHUNK = 64
GDN_CONV = 4
FOX_DH = 64
LN_EPS = 1e-5
RMS_EPS = 1e-6

ADAM_LR = 0.001
ADAM_B1 = 0.9
ADAM_B2 = 0.999
ADAM_EPS = 1e-08
ADAM_WD = 0.01
ADAM_STEP = 10

_DIMS = {"nn": (((1,), (0,)), ((), ())), "nt": (((1,), (1,)), ((), ())), "tn": (((0,), (0,)), ((), ()))}


def _params(*sem):
    return pltpu.CompilerParams(dimension_semantics=sem, vmem_limit_bytes=VMEM_LIMIT_BYTES)


def _split(a, terms):
    out = []
    rest = a.astype(F32)
    for t in range(terms):
        piece = rest.astype(BF16)
        out.append(piece)
        if t + 1 < terms:
            rest = rest - piece.astype(F32)
    return out


def _raw_mm(a, b, form, mode):
    dot = lambda x, y: lax.dot_general(x, y, _DIMS[form], preferred_element_type=F32)
    if mode == "b":
        return dot(a.astype(BF16), b.astype(BF16))
    if mode == "x3":
        (ah, al), (bh, bl) = _split(a, 2), _split(b, 2)
        return dot(ah, bh) + (dot(ah, bl) + dot(al, bh))
    if mode == "ca":
        ac = a.astype(BF16)
        b1, b2, b3 = _split(b, 3)
        return dot(ac, b1) + (dot(ac, b2) + dot(ac, b3))
    assert mode == "cb", mode
    bc = b.astype(BF16)
    a1, a2, a3 = _split(a, 3)
    return dot(a1, bc) + (dot(a2, bc) + dot(a3, bc))


@functools.partial(jax.custom_vjp, nondiff_argnums=(2, 3))
def _mm(a, b, form, mode):
    return _raw_mm(a, b, form, mode)


def _mm_fwd(a, b, form, mode):
    return _raw_mm(a, b, form, mode), (a, b)


def _mm_bwd(form, mode, res, g):
    a, b = res
    flip = {"b": "b", "x3": "x3", "ca": "cb", "cb": "ca"}[mode]
    if form == "nn":
        da, db = (lambda: _mm(g, b, "nt", mode)), (lambda: _mm(a, g, "tn", mode))
    elif form == "nt":
        da, db = (lambda: _mm(g, b, "nn", mode)), (lambda: _mm(g, a, "tn", flip))
    else:
        da, db = (lambda: _mm(b, g, "nt", flip)), (lambda: _mm(a, g, "nn", mode))
    return (jnp.zeros_like(a) if mode == "ca" else da()), (jnp.zeros_like(b) if mode == "cb" else db())


_mm.defvjp(_mm_fwd, _mm_bwd)


@jax.custom_vjp
def _tri_inv(Ls):
    C = Ls[0].shape[0]
    eye = (_iota((C, C), 0) == _iota((C, C), 1)).astype(F32)
    X = [eye - L for L in Ls]
    P = [_raw_mm(L, L, "nn", "x3") for L in Ls]
    n_sq = max(1, (C - 1).bit_length() - 1)
    for it in range(n_sq):
        XP = [_raw_mm(x, p, "nn", "x3") for x, p in zip(X, P)]
        if it < n_sq - 1:
            P = [_raw_mm(p, p, "nn", "x3") for p in P]
        X = [x + xp for x, xp in zip(X, XP)]
    return X


def _tri_inv_fwd(Ls):
    Ts = _tri_inv(Ls)
    return Ts, Ts


def _tri_inv_bwd(Ts, dTs):
    Ms = [_raw_mm(dT, T, "nt", "x3") for dT, T in zip(dTs, Ts)]
    return ([-_raw_mm(T, M, "tn", "x3") for T, M in zip(Ts, Ms)],)


_tri_inv.defvjp(_tri_inv_fwd, _tri_inv_bwd)


@jax.custom_vjp
def _tri_inv_known(Ls, Ts):
    return Ts


def _tri_inv_known_fwd(Ls, Ts):
    return Ts, Ts


def _tri_inv_known_bwd(Ts, dTs):
    return _tri_inv_bwd(Ts, dTs)[0], [jnp.zeros_like(T) for T in Ts]


_tri_inv_known.defvjp(_tri_inv_known_fwd, _tri_inv_known_bwd)


def _silu(x):
    return x * jax.nn.sigmoid(x)


def _softplus(x):
    return jnp.maximum(x, 0.0) + jnp.log1p(jnp.exp(-jnp.abs(x)))


def _iota(shape, dim):
    return lax.broadcasted_iota(jnp.int32, shape, dim)


def _matmul(a, b, *, ta=False, tb=False, out_dtype=F32, add=None, add_scale=1.0, tm=512, tn=512, tk=512, name):
    if ta:
        K, M = a.shape
    else:
        M, K = a.shape
    if tb:
        N, K2 = b.shape
    else:
        K2, N = b.shape
    assert K == K2, (a.shape, b.shape, ta, tb)
    tm, tn, tk = min(tm, M), min(tn, N), min(tk, K)
    assert M % tm == 0 and N % tn == 0 and K % tk == 0, (M, N, K, tm, tn, tk)
    nk = K // tk
    form = ("t" if ta else "n") + ("t" if tb else "n")
    dims = (((0 if ta else 1,), (1 if tb else 0,)), ((), ()))
    del form
    a_spec = pl.BlockSpec((tk, tm), lambda i, j, k: (k, i)) if ta else pl.BlockSpec((tm, tk), lambda i, j, k: (i, k))
    b_spec = pl.BlockSpec((tn, tk), lambda i, j, k: (j, k)) if tb else pl.BlockSpec((tk, tn), lambda i, j, k: (k, j))
    o_spec = pl.BlockSpec((tm, tn), lambda i, j, k: (i, j))
    has_add = add is not None

    def body(*refs):
        a_ref, b_ref = refs[:2]
        add_ref = refs[2] if has_add else None
        o_ref = refs[3] if has_add else refs[2]
        acc_ref = refs[-1] if nk > 1 else None
        k = pl.program_id(2)
        part = lax.dot_general(a_ref[...].astype(BF16), b_ref[...].astype(BF16), dims, preferred_element_type=F32)

        def finish(total):
            if has_add:
                total = total + add_scale * add_ref[...].astype(F32)
            o_ref[...] = total.astype(o_ref.dtype)

        if nk == 1:
            finish(part)
        else:
            @pl.when(k == 0)
            def _():
                acc_ref[...] = part

            @pl.when(jnp.logical_and(k > 0, k < nk - 1))
            def _():
                acc_ref[...] += part

            @pl.when(k == nk - 1)
            def _():
                finish(acc_ref[...] + part)

    in_specs = [a_spec, b_spec] + ([o_spec] if has_add else [])
    args = (a, b) + ((add,) if has_add else ())
    return pl.pallas_call(
        body, name=name, grid=(M // tm, N // tn, nk),
        in_specs=in_specs, out_specs=o_spec,
        out_shape=jax.ShapeDtypeStruct((M, N), out_dtype),
        scratch_shapes=[pltpu.VMEM((tm, tn), F32)] if nk > 1 else [],
        compiler_params=_params("parallel", "parallel", "arbitrary"),
    )(*args)


def _rowcall(body_fn, rows, consts, out_rows, out_accs, *, tr, name, reverse=False, scratch=()):
    def arr_spec(r):
        return r if isinstance(r, tuple) else (r, None)

    S = arr_spec(rows[0])[0].shape[0]
    tr = min(tr, S)
    assert S % tr == 0
    n = S // tr
    ridx = (lambda i: (n - 1 - i, 0)) if reverse else (lambda i: (i, 0))
    in_specs, args = [], []
    for r in rows:
        arr, spec = arr_spec(r)
        args.append(arr)
        in_specs.append(spec(tr, n) if spec is not None else pl.BlockSpec((tr, arr.shape[1]), ridx))
    for c in consts:
        args.append(c)
        in_specs.append(pl.BlockSpec(c.shape, lambda i: (0, 0)))
    out_specs, out_shape = [], []
    for (ncol, dt) in out_rows:
        out_specs.append(pl.BlockSpec((tr, ncol), ridx))
        out_shape.append(jax.ShapeDtypeStruct((S, ncol), dt))
    for shp in out_accs:
        out_specs.append(pl.BlockSpec(shp, lambda i: (0, 0)))
        out_shape.append(jax.ShapeDtypeStruct(shp, F32))
    nr, nc, no, na = len(rows), len(consts), len(out_rows), len(out_accs)

    def kernel(*refs):
        row_refs = refs[:nr]
        const_refs = refs[nr:nr + nc]
        orow_refs = refs[nr + nc:nr + nc + no]
        oacc_refs = refs[nr + nc + no:nr + nc + no + na]
        scr = refs[nr + nc + no + na:]
        step = pl.program_id(0)
        blk = (n - 1 - step) if reverse else step

        @pl.when(step == 0)
        def _():
            for acc in oacc_refs:
                acc[...] = jnp.zeros(acc.shape, F32)

        body_fn(row_refs, const_refs, orow_refs, oacc_refs, scr, step, blk)

    outs = pl.pallas_call(
        kernel, name=name, grid=(n,), in_specs=in_specs, out_specs=out_specs, out_shape=out_shape,
        scratch_shapes=list(scratch), compiler_params=_params("arbitrary"),
    )(*args)
    return outs


def _row(v):
    return v.astype(F32).reshape(1, -1)


def _pad_lanes(v, width=LANES, offset=0):
    pad = [(0, 0)] * (v.ndim - 1) + [(offset, width - offset - v.shape[-1])]
    return jnp.pad(v, pad)


def _ln_tile(x, y, g, b, alpha):
    u = alpha * x + y
    mu = jnp.mean(u, -1, keepdims=True)
    d = u - mu
    var = jnp.mean(d * d, -1, keepdims=True)
    return d * lax.rsqrt(var + LN_EPS) * g + b


def _ln_fwd(x, y, g, b, alpha, *, tr, name):
    D = x.shape[1]

    def body(rows, consts, orows, oaccs, scr, step, blk):
        orows[0][...] = _ln_tile(rows[0][...], rows[1][...], consts[0][...], consts[1][...], alpha)

    return _rowcall(body, [x, y], [_row(g), _row(b)], [(D, F32)], [], tr=tr, name=name)[0]


def _ln_bwd(x, y, g, b, dxo, t, alpha, *, tr, name):
    D = x.shape[1]

    def body(rows, consts, orows, oaccs, scr, step, blk):
        xv, yv = rows[0][...], rows[1][...]
        ct = rows[2][...] + rows[3][...]
        _, vjp = jax.vjp(lambda yy, gg, bb: _ln_tile(xv, yy, gg, bb, alpha), yv, consts[0][...], consts[1][...])
        du, dg, db = vjp(ct)
        orows[0][...] = du
        oaccs[0][...] += dg
        oaccs[1][...] += db

    du, dg, db = _rowcall(body, [x, y, dxo, t], [_row(g), _row(b)], [(D, F32)], [(1, D), (1, D)], tr=tr, name=name)
    return du, dg[0], db[0]


def _ple_fwd(x_ln, gp, pp, *, tr, name):
    D = x_ln.shape[1]

    def body(rows, consts, orows, oaccs, scr, step, blk):
        orows[0][...] = rows[0][...] + jax.nn.sigmoid(rows[1][...]) * rows[2][...]

    return _rowcall(body, [x_ln, gp, pp], [], [(D, F32)], [], tr=tr, name=name)[0]


def _ple_bwd(dxo, gp, pp, *, tr, name):
    D = dxo.shape[1]

    def body(rows, consts, orows, oaccs, scr, step, blk):
        d = rows[0][...]
        s = jax.nn.sigmoid(rows[1][...])
        orows[0][...] = (d * rows[2][...] * s * (1.0 - s)).astype(BF16)
        orows[1][...] = (d * s).astype(BF16)

    return _rowcall(body, [dxo, gp, pp], [], [(D, BF16), (D, BF16)], [], tr=tr, name=name)


def _loss_fwd_bwd(xf, target, *, tr, name):
    D = xf.shape[1]

    def body(rows, consts, orows, oaccs, scr, step, blk):
        err = rows[0][...] - rows[1][...]
        orows[0][...] = err * (1.0 / D)
        part = jnp.sum(err * err, axis=0, keepdims=True) * (0.5 / D)
        oaccs[0][...] += part

    dx, lrow = _rowcall(body, [xf, target], [], [(D, F32)], [(1, D)], tr=tr, name=name)
    return lrow, dx


def _gdn_qk_tile(c):
    y = _silu(c)
    return y * lax.rsqrt(jnp.sum(y * y, -1, keepdims=True) + RMS_EPS)


def _make_bg_fn(H):
    def fn(hs, alog, dtb):
        lane = _iota((1, LANES), 1)
        beta = jax.nn.sigmoid(hs)
        g = -jnp.exp(alog) * _softplus(hs + dtb)
        return jnp.where(lane < H, beta, jnp.where(lane < 2 * H, g, 0.0))
    return fn


def _halo_spec(ncol):
    def make(tr, n):
        per = tr // SUBLANES
        return pl.BlockSpec((SUBLANES, ncol), lambda i: (jnp.maximum(i * per - 1, 0), 0))
    return make


def _halo_spec_rev(ncol):
    def make(tr, n):
        per = tr // SUBLANES
        return pl.BlockSpec((SUBLANES, ncol), lambda i: (jnp.maximum((n - 1 - i) * per - 1, 0), 0))
    return make


def _gdn_pre_fwd(h_main, h_small, conv_w8, alog_row, dtb_row, H, *, tr, name):
    W = H * GDN_DK
    C3 = 3 * W
    bg_fn = _make_bg_fn(H)

    def body(rows, consts, orows, oaccs, scr, step, blk):
        main_ref, halo_ref, hs_ref = rows
        w_ref, alog_ref, dtb_ref = consts
        q_ref, k_ref, v_ref, bg_ref = orows
        xs = scr[0]
        trr = main_ref.shape[0]
        xs[pl.ds(SUBLANES, trr), :] = main_ref[...]
        xs[pl.ds(0, SUBLANES), :] = jnp.where(blk > 0, halo_ref[...], 0.0)
        for s in range(C3 // LANES):
            ls = slice(s * LANES, (s + 1) * LANES)
            c = jnp.zeros((trr, LANES), F32)
            for j in range(GDN_CONV):
                c = c + w_ref[GDN_CONV - 1 - j:GDN_CONV - j, ls] * xs[pl.ds(SUBLANES - j, trr), ls]
            if s < 2 * H:
                out = _gdn_qk_tile(c)
                (q_ref if s < H else k_ref)[:, (s % H) * LANES:(s % H + 1) * LANES] = out
            else:
                v_ref[:, (s - 2 * H) * LANES:(s - 2 * H + 1) * LANES] = _silu(c)
        bg_ref[...] = bg_fn(hs_ref[...], alog_ref[...], dtb_ref[...])

    main = (h_main, lambda tr_, n: pl.BlockSpec((tr_, C3), lambda i: (i, 0)))
    halo = (h_main, _halo_spec(C3))
    trr = min(tr, h_main.shape[0])
    return _rowcall(body, [main, halo, h_small], [conv_w8, alog_row, dtb_row],
                    [(W, F32), (W, F32), (W, F32), (LANES, F32)], [], tr=tr, name=name,
                    scratch=[pltpu.VMEM((trr + SUBLANES, C3), F32)])


def _gdn_pre_bwd(h_main, h_small, conv_w8, alog_row, dtb_row, dq, dk, dv, dbg, dz, H, *, tr, name):
    W = H * GDN_DK
    C3 = 3 * W
    bg_fn = _make_bg_fn(H)

    def body(rows, consts, orows, oaccs, scr, step, blk):
        main_ref, halo_ref, hs_ref, dq_ref, dk_ref, dv_ref, dbg_ref, dz_ref = rows
        w_ref, alog_ref, dtb_ref = consts
        dmain_ref, dhs_ref = orows
        dw_ref, dalog_ref, ddtb_ref = oaccs
        xs, dcs = scr
        trr = main_ref.shape[0]
        xs[pl.ds(SUBLANES, trr), :] = main_ref[...]
        xs[pl.ds(0, SUBLANES), :] = jnp.where(blk > 0, halo_ref[...], 0.0)

        @pl.when(step == 0)
        def _():
            dcs[pl.ds(trr, SUBLANES), :] = jnp.zeros((SUBLANES, C3), F32)

        for s in range(C3 // LANES):
            ls = slice(s * LANES, (s + 1) * LANES)
            c = jnp.zeros((trr, LANES), F32)
            for j in range(GDN_CONV):
                c = c + w_ref[GDN_CONV - 1 - j:GDN_CONV - j, ls] * xs[pl.ds(SUBLANES - j, trr), ls]
            if s < 2 * H:
                src = dq_ref if s < H else dk_ref
                ct = src[:, (s % H) * LANES:(s % H + 1) * LANES]
                _, vjp = jax.vjp(_gdn_qk_tile, c)
            else:
                ct = dv_ref[:, (s - 2 * H) * LANES:(s - 2 * H + 1) * LANES]
                _, vjp = jax.vjp(_silu, c)
            dcs[pl.ds(0, trr), ls] = vjp(ct)[0]
        for s in range(C3 // LANES):
            ls = slice(s * LANES, (s + 1) * LANES)
            dx = jnp.zeros((trr, LANES), F32)
            dc0 = dcs[pl.ds(0, trr), ls]
            for j in range(GDN_CONV):
                wrow = w_ref[GDN_CONV - 1 - j:GDN_CONV - j, ls]
                dx = dx + wrow * dcs[pl.ds(j, trr), ls]
                dw_ref[GDN_CONV - 1 - j:GDN_CONV - j, ls] += jnp.sum(dc0 * xs[pl.ds(SUBLANES - j, trr), ls], axis=0, keepdims=True)
            dmain_ref[:, ls] = dx.astype(BF16)
        dmain_ref[:, C3:] = dz_ref[...]
        dcs[pl.ds(trr, SUBLANES), :] = dcs[pl.ds(0, SUBLANES), :]
        _, vjp = jax.vjp(bg_fn, hs_ref[...], alog_ref[...], dtb_ref[...])
        dhs, dalog, ddtb = vjp(dbg_ref[...])
        dhs_ref[...] = dhs.astype(BF16)
        dalog_ref[...] += dalog
        ddtb_ref[...] += ddtb

    trr = min(tr, h_main.shape[0])
    main = (h_main, lambda tr_, n: pl.BlockSpec((tr_, C3), lambda i: (n - 1 - i, 0)))
    halo = (h_main, _halo_spec_rev(C3))
    return _rowcall(body, [main, halo, h_small, dq, dk, dv, dbg, dz], [conv_w8, alog_row, dtb_row],
                    [(4 * W, BF16), (LANES, BF16)], [(SUBLANES, C3), (1, LANES), (1, LANES)],
                    tr=tr, name=name, reverse=True,
                    scratch=[pltpu.VMEM((trr + SUBLANES, C3), F32), pltpu.VMEM((trr + SUBLANES, C3), F32)])


def _gdn_chunk(qs, ks, vs, betas, gs, Ss, Ts=None, with_inverse=False):
    C, dk = qs[0].shape
    dv = vs[0].shape[1]
    ri, ci = _iota((C, C), 0), _iota((C, C), 1)
    causal, strict = ri >= ci, ri > ci
    tril = causal.astype(F32)
    lane0 = (_iota((1, LANES), 1) == 0).astype(F32)
    e0 = jnp.ones((C, 1), F32) * lane0
    last = (_iota((C, 1), 0) == C - 1).astype(F32)

    def each(f, *lists):
        return [f(*a) for a in zip(*lists)]

    G = each(lambda g: g * jnp.ones((1, LANES), F32), gs)
    gcB = each(lambda x: _mm(tril, x, "nn", "ca"), G)
    gc = each(lambda x: jnp.sum(x * lane0, -1, keepdims=True), gcB)
    gc_row = each(lambda x: _mm(e0, x, "nt", "ca"), gcB)
    decay = each(lambda a, b: jnp.where(causal, jnp.exp(jnp.where(causal, a - b, 0.0)), 0.0), gc, gc_row)
    kb = each(lambda k, b: k * b, ks, betas)
    kk = each(lambda a, k: _mm(a, k, "nt", "b"), kb, ks)
    L = each(lambda a, d: jnp.where(strict, a * d, 0.0), kk, decay)
    X = _tri_inv(L) if Ts is None else _tri_inv_known(L, Ts)
    egc = each(jnp.exp, gc)
    u = each(lambda x, v, b: _mm(x, v * b, "nn", "x3"), X, vs, betas)
    w = each(lambda x, a, e: _mm(x, a * e, "nn", "x3"), X, kb, egc)
    qsc = each(lambda q: q * (dk ** -0.5), qs)
    qk = each(lambda q, k: _mm(q, k, "nt", "b"), qsc, ks)
    A = each(lambda a, d: jnp.where(causal, a * d, 0.0), qk, decay)
    q_dec = each(lambda q, e: q * e, qsc, egc)
    gl = each(lambda x: jnp.sum(x * last, keepdims=True), gc)
    k_dec = each(lambda k, a, b: k * jnp.exp(a - b), ks, gl, gc)
    wS = each(lambda a, s: _mm(a, s, "nn", "b"), w, Ss)
    qS = each(lambda a, s: _mm(a, s, "nn", "b"), q_dec, Ss)
    v_new = each(lambda a, b: a - b, u, wS)
    Av = each(lambda a, b: _mm(a, b, "nn", "b"), A, v_new)
    kv = each(lambda a, b: _mm(a, b, "tn", "b"), k_dec, v_new)
    o = each(lambda a, b: a + b, qS, Av)
    S_new = each(lambda s, e, x: s * jnp.exp(e) + x, Ss, gl, kv)
    return (o, S_new, X) if with_inverse else (o, S_new)


def _gdn_rule_fwd(q, k, v, bg, H, *, name, carry=None):
    S_len = q.shape[0]
    C = min(GDN_CHUNK, S_len)
    N = S_len // C
    dk = dv = GDN_DK

    def body(*refs):
        if carry is None:
            q_ref, k_ref, v_ref, bg_ref, o_ref, st_ref, ti_ref, s_scr = refs
        else:
            q_ref, k_ref, v_ref, bg_ref, src_ref, o_ref, st_ref, ti_ref, got_ref, s_scr = refs[:10]
            xy = (src_ref, got_ref) + tuple(refs[10:]) + (carry[1],)
        n = pl.program_id(0)

        @pl.when(n == 0)
        def _():
            s_scr[...] = jnp.zeros(s_scr.shape, F32)
            if carry is not None:
                _xy_start(*xy)

        bgt = bg_ref[...]
        sl = [slice(h * dk, (h + 1) * dk) for h in range(H)]
        Ss = [s_scr[h] for h in range(H)]
        for h in range(H):
            st_ref[h] = Ss[h]
        os_, S_new, Ts = _gdn_chunk([q_ref[:, s] for s in sl], [k_ref[:, s] for s in sl], [v_ref[:, s] for s in sl],
                                    [bgt[:, h:h + 1] for h in range(H)], [bgt[:, H + h:H + h + 1] for h in range(H)],
                                    Ss, with_inverse=True)
        for h in range(H):
            o_ref[:, sl[h]] = os_[h]
            s_scr[h] = S_new[h]
            ti_ref[h] = Ts[h]

        if carry is not None:
            @pl.when(n == N - 1)
            def _():
                _xy_wait(*xy)

    rows = pl.BlockSpec((C, H * dk), lambda n: (n, 0))
    extra = carry is not None
    return pl.pallas_call(
        body, name=name, grid=(N,),
        in_specs=[rows, rows, rows, pl.BlockSpec((C, LANES), lambda n: (n, 0))] + ([_ANY] if extra else []),
        out_specs=[rows, pl.BlockSpec((H, dk, dv), lambda n: (n, 0, 0)), pl.BlockSpec((H, C, C), lambda n: (n, 0, 0))]
        + ([_ANY] if extra else []),
        out_shape=[jax.ShapeDtypeStruct((S_len, H * dv), F32), jax.ShapeDtypeStruct((N * H, dk, dv), F32),
                   jax.ShapeDtypeStruct((N * H, C, C), F32)] + ([_xy_out_shape(carry[0])] if extra else []),
        scratch_shapes=[pltpu.VMEM((H, dk, dv), F32)] + (_xy_sems() if extra else []),
        compiler_params=_params("arbitrary"),
    )(q, k, v, bg, *((carry[0],) if extra else ()))


def _gdn_rule_bwd(q, k, v, bg, states, tinv, do, H, *, name, carry=None):
    S_len = q.shape[0]
    C = min(GDN_CHUNK, S_len)
    N = S_len // C
    dk = dv = GDN_DK

    def body(*refs):
        if carry is None:
            q_ref, k_ref, v_ref, bg_ref, st_ref, ti_ref, do_ref, dq_ref, dk_ref, dv_ref, dbg_ref, ds_scr = refs
        else:
            (q_ref, k_ref, v_ref, bg_ref, st_ref, ti_ref, do_ref, src_ref,
             dq_ref, dk_ref, dv_ref, dbg_ref, got_ref, ds_scr) = refs[:14]
            xy = (src_ref, got_ref) + tuple(refs[14:]) + (carry[1],)
        step = pl.program_id(0)

        @pl.when(step == 0)
        def _():
            ds_scr[...] = jnp.zeros(ds_scr.shape, F32)
            if carry is not None:
                _xy_start(*xy)

        bgt = bg_ref[...]
        lane = _iota((1, LANES), 1)
        dbg = jnp.zeros((C, LANES), F32)
        sl = [slice(h * dk, (h + 1) * dk) for h in range(H)]
        Ts = [ti_ref[h] for h in range(H)]
        _, vjp = jax.vjp(lambda *a: _gdn_chunk(*a, Ts=Ts),
                         [q_ref[:, s] for s in sl], [k_ref[:, s] for s in sl], [v_ref[:, s] for s in sl],
                         [bgt[:, h:h + 1] for h in range(H)], [bgt[:, H + h:H + h + 1] for h in range(H)],
                         [st_ref[h] for h in range(H)])
        dq, dkk, dvv, dbeta, dg, dS = vjp(([do_ref[:, s] for s in sl], [ds_scr[h] for h in range(H)]))
        for h in range(H):
            dq_ref[:, sl[h]] = dq[h]
            dk_ref[:, sl[h]] = dkk[h]
            dv_ref[:, sl[h]] = dvv[h]
            dbg = dbg + jnp.where(lane == h, dbeta[h], 0.0) + jnp.where(lane == h + H, dg[h], 0.0)
            ds_scr[h] = dS[h]
        dbg_ref[...] = dbg

        if carry is not None:
            @pl.when(step == N - 1)
            def _():
                _xy_wait(*xy)

    rows = pl.BlockSpec((C, H * dk), lambda s: (N - 1 - s, 0))
    bgs = pl.BlockSpec((C, LANES), lambda s: (N - 1 - s, 0))
    extra = carry is not None
    return pl.pallas_call(
        body, name=name, grid=(N,),
        in_specs=[rows, rows, rows, bgs, pl.BlockSpec((H, dk, dv), lambda s: (N - 1 - s, 0, 0)),
                  pl.BlockSpec((H, C, C), lambda s: (N - 1 - s, 0, 0)), rows] + ([_ANY] if extra else []),
        out_specs=[rows, rows, rows, bgs] + ([_ANY] if extra else []),
        out_shape=[jax.ShapeDtypeStruct((S_len, H * dk), F32)] * 3 + [jax.ShapeDtypeStruct((S_len, LANES), F32)]
        + ([_xy_out_shape(carry[0])] if extra else []),
        scratch_shapes=[pltpu.VMEM((H, dk, dv), F32)] + (_xy_sems() if extra else []),
        compiler_params=_params("arbitrary"),
    )(q, k, v, bg, states, tinv, do, *((carry[0],) if extra else ()))


def _gdn_post_tile(o, z, g):
    return o * lax.rsqrt(jnp.mean(o * o, -1, keepdims=True) + RMS_EPS) * g * _silu(z)


def _gdn_post_fwd(o, h_main, norm_g, H, *, tr, name):
    W = H * GDN_DK

    def body(rows, consts, orows, oaccs, scr, step, blk):
        for h in range(H):
            ls = slice(h * LANES, (h + 1) * LANES)
            orows[0][:, ls] = _gdn_post_tile(rows[0][:, ls], rows[1][:, ls], consts[0][...]).astype(BF16)

    z = (h_main, lambda tr_, n: pl.BlockSpec((tr_, W), lambda i: (i, 3)))
    return _rowcall(body, [o, z], [_row(norm_g)], [(W, BF16)], [], tr=tr, name=name)[0]


def _gdn_post_bwd(o, h_main, norm_g, d_on, H, *, tr, name):
    W = H * GDN_DK

    def body(rows, consts, orows, oaccs, scr, step, blk):
        for h in range(H):
            ls = slice(h * LANES, (h + 1) * LANES)
            _, vjp = jax.vjp(_gdn_post_tile, rows[0][:, ls], rows[1][:, ls], consts[0][...])
            d_o, d_z, d_g = vjp(rows[2][:, ls])
            orows[0][:, ls] = d_o
            orows[1][:, ls] = d_z.astype(BF16)
            oaccs[0][...] += d_g

    z = (h_main, lambda tr_, n: pl.BlockSpec((tr_, W), lambda i: (i, 3)))
    return _rowcall(body, [o, z, d_on], [_row(norm_g)], [(W, F32), (W, BF16)], [(1, LANES)], tr=tr, name=name)


def _seg_ones():
    ri, ci = _iota((LANES, LANES), 0), _iota((LANES, LANES), 1)
    return ((ri < FOX_DH) == (ci < FOX_DH)).astype(F32)


def _fox_qk_tile(x, g2):
    ms = _mm(x * x, _seg_ones(), "nn", "cb") * (1.0 / FOX_DH)
    return x * lax.rsqrt(ms + RMS_EPS) * g2


def _make_lf_fn(Hf):
    def fn(hs, bf):
        lane = _iota((1, LANES), 1)
        return jnp.where(lane < Hf, -_softplus(-(hs + bf)), 0.0)
    return fn


def _fox_pre_fwd(h_main, h_small, gq2, gk2, bf_row, Hf, *, tr, name):
    W = Hf * FOX_DH
    lf_fn = _make_lf_fn(Hf)

    def body(rows, consts, orows, oaccs, scr, step, blk):
        qk_ref, v_ref, hs_ref = rows
        gq_ref, gk_ref, bf_ref = consts
        qn_ref, kn_ref, vb_ref, c_ref, cb_ref = orows
        carry = scr[0]
        trr = qk_ref.shape[0]

        @pl.when(step == 0)
        def _():
            carry[...] = jnp.zeros(carry.shape, F32)

        for s in range(W // LANES):
            ls = slice(s * LANES, (s + 1) * LANES)
            qn_ref[:, ls] = _fox_qk_tile(qk_ref[:, ls], gq_ref[...]).astype(BF16)
            kn_ref[:, ls] = _fox_qk_tile(qk_ref[:, W + s * LANES:W + (s + 1) * LANES], gk_ref[...]).astype(BF16)
        vb_ref[...] = v_ref[...].astype(BF16)
        lf = lf_fn(hs_ref[...], bf_ref[...])
        tril = (_iota((trr, trr), 0) >= _iota((trr, trr), 1)).astype(F32)
        c = _raw_mm(tril, lf, "nn", "ca") + carry[0:1, :]
        c_ref[...] = c
        carry[0:1, :] = c[trr - 1:trr, :]
        col = _iota((LANES, 2 * W), 1)
        parity = (col >= W).astype(jnp.int32)
        slab = jnp.right_shift(col - parity * W, 7)
        expand = (2 * slab + parity == _iota((LANES, 2 * W), 0)).astype(F32)
        cb_ref[...] = _raw_mm(c, expand, "nn", "cb")

    qk = (h_main, lambda tr_, n: pl.BlockSpec((tr_, 2 * W), lambda i: (i, 0)))
    vv = (h_main, lambda tr_, n: pl.BlockSpec((tr_, W), lambda i: (i, 2)))
    return _rowcall(body, [qk, vv, h_small], [gq2, gk2, bf_row],
                    [(W, BF16), (W, BF16), (W, BF16), (LANES, F32), (2 * W, F32)], [], tr=tr, name=name,
                    scratch=[pltpu.VMEM((SUBLANES, LANES), F32)])


def _fox_pre_bwd(h_main, h_small, gq2, gk2, bf_row, dqn, dkn, dvv, dz, dc, Hf, *, tr, name):
    W = Hf * FOX_DH
    lf_fn = _make_lf_fn(Hf)

    def body(rows, consts, orows, oaccs, scr, step, blk):
        qk_ref, hs_ref, dqn_ref, dkn_ref, dvv_ref, dz_ref, dc_ref = rows
        gq_ref, gk_ref, bf_ref = consts
        dmain_ref, dhs_ref = orows
        dgq_ref, dgk_ref, dbf_ref = oaccs
        carry = scr[0]
        trr = qk_ref.shape[0]

        @pl.when(step == 0)
        def _():
            carry[...] = jnp.zeros(carry.shape, F32)

        for s in range(W // LANES):
            ls = slice(s * LANES, (s + 1) * LANES)
            lk = slice(W + s * LANES, W + (s + 1) * LANES)
            _, vjp = jax.vjp(_fox_qk_tile, qk_ref[:, ls], gq_ref[...])
            dx, dg = vjp(dqn_ref[:, ls])
            dmain_ref[:, ls] = dx.astype(BF16)
            dgq_ref[...] += dg
            _, vjp = jax.vjp(_fox_qk_tile, qk_ref[:, lk], gk_ref[...])
            dx, dg = vjp(dkn_ref[:, ls])
            dmain_ref[:, lk] = dx.astype(BF16)
            dgk_ref[...] += dg
        dmain_ref[:, 2 * W:3 * W] = dvv_ref[...].astype(BF16)
        dmain_ref[:, 3 * W:] = dz_ref[...]
        dcv = dc_ref[...]
        triu = (_iota((trr, trr), 0) <= _iota((trr, trr), 1)).astype(F32)
        dlf = _raw_mm(triu, dcv, "nn", "ca") + carry[0:1, :]
        carry[0:1, :] = dlf[0:1, :]
        _, vjp = jax.vjp(lf_fn, hs_ref[...], bf_ref[...])
        dhs, dbf = vjp(dlf)
        dhs_ref[...] = dhs.astype(BF16)
        dbf_ref[...] += dbf

    qk = (h_main, lambda tr_, n: pl.BlockSpec((tr_, 2 * W), lambda i: (n - 1 - i, 0)))
    return _rowcall(body, [qk, h_small, dqn, dkn, dvv, dz, dc], [gq2, gk2, bf_row],
                    [(4 * W, BF16), (LANES, BF16)], [(1, LANES), (1, LANES), (1, LANES)],
                    tr=tr, name=name, reverse=True, scratch=[pltpu.VMEM((SUBLANES, LANES), F32)])


def _fox_attn_fwd(qn, kn, vb, c_b, c_rowp, *, tb, name):
    S_len, W = qn.shape
    HP = W // LANES
    tb = min(tb, S_len)
    nb = S_len // tb
    scale = FOX_DH ** -0.5
    rb, cb = min(ATTN_ROWS, tb), min(ATTN_COLS, tb)
    nblk = 2 * (tb // rb)

    steps = [(i, j) for i in range(nb) for j in range(i + 1)]
    ti = jnp.asarray([s[0] for s in steps], jnp.int32)
    tj = jnp.asarray([s[1] for s in steps], jnp.int32)

    def body(ti_ref, tj_ref, q_ref, k_ref, v_ref, cb0_ref, cb1_ref, cr_ref, o_ref, lse_ref,
             m_scr, l_scr, acc_scr, s_scr, p_scr, a_scr):
        t = pl.program_id(1)
        i, j = ti_ref[t], tj_ref[t]

        @pl.when(j == 0)
        def _():
            m_scr[...] = jnp.full(m_scr.shape, -jnp.inf, F32)
            l_scr[...] = jnp.zeros(l_scr.shape, F32)
            acc_scr[...] = jnp.zeros(acc_scr.shape, F32)

        def compute(diag):
            lo = _iota((1, LANES), 1) < FOX_DH
            v = v_ref[...]
            lane = _iota((1, LANES), 1)
            blocks = [(hh, r) for hh in range(2) for r in range(tb // rb)]
            masks = [lo, jnp.logical_not(lo)]

            def visible(r):
                return ((r + 1) * rb - 1) // LANES + 1 if diag else tb // LANES

            for b, (hh, r) in enumerate(blocks):
                rows = slice(r * rb, (r + 1) * rb)
                qr = q_ref[rows, :]
                qh = jnp.where(masks[hh], qr * scale, jnp.zeros_like(qr))
                ctb = (cb0_ref if hh == 0 else cb1_ref)[rows, :]
                mx = None
                for c in range(tb // cb):
                    if c * cb // LANES >= visible(r):
                        continue
                    s2 = lax.dot_general(qh, k_ref[c * cb:(c + 1) * cb, :], _DIMS["nt"], preferred_element_type=F32)
                    for piece in range(c * cb // LANES, min((c + 1) * cb // LANES, visible(r))):
                        cols = slice(piece * LANES, (piece + 1) * LANES)
                        s = s2[:, piece * LANES - c * cb:(piece + 1) * LANES - c * cb] + ctb - cr_ref[hh:hh + 1, cols]
                        if diag and (piece + 1) * LANES - 1 > r * rb:
                            keep = piece * LANES + _iota((rb, LANES), 1) <= r * rb + _iota((rb, LANES), 0)
                            s = jnp.where(keep, s, -jnp.inf)
                        s_scr[b, :, cols] = s
                        mx = s if mx is None else jnp.maximum(mx, s)
                m_prev = m_scr[hh, rows, :]
                m_new = jnp.maximum(m_prev, jnp.broadcast_to(jnp.max(mx, -1, keepdims=True), (rb, LANES)))
                a_scr[b] = jnp.exp(m_prev - m_new)
                m_scr[hh, rows, :] = m_new
            for b, (hh, r) in enumerate(blocks):
                m_new = m_scr[hh, r * rb:(r + 1) * rb, :]
                for piece in range(visible(r)):
                    cols = slice(piece * LANES, (piece + 1) * LANES)
                    p_scr[b, :, cols] = jnp.exp(s_scr[b, :, cols] - m_new).astype(BF16)
            for b, (hh, r) in enumerate(blocks):
                rows = slice(r * rb, (r + 1) * rb)
                nkv = visible(r) * LANES
                pb = p_scr[b, :, :nkv]
                spare = (1 - hh) * FOX_DH
                vh = jnp.where(masks[hh], v[:nkv], (lane == spare).astype(BF16))
                pv = lax.dot_general(pb, vh, _DIMS["nn"], preferred_element_type=F32)
                psum = jnp.broadcast_to(pv[:, spare:spare + 1], (rb, LANES))
                alpha = a_scr[b]
                l_scr[hh, rows, :] = alpha * l_scr[hh, rows, :] + psum
                acc = acc_scr[rows, :]
                acc_scr[rows, :] = jnp.where(masks[hh], acc * alpha + pv, acc)

        @pl.when(j < i)
        def _():
            compute(False)

        @pl.when(j == i)
        def _():
            compute(True)
            lo = _iota((1, LANES), 1) < FOX_DH
            o_ref[...] = acc_scr[...] / jnp.where(lo, l_scr[0], l_scr[1])
            lse_ref[...] = m_scr[...] + jnp.log(l_scr[...])

    qs = pl.BlockSpec((tb, LANES), lambda h, t, ti_, tj_: (ti_[t], h))
    qs1 = pl.BlockSpec((tb, LANES), lambda h, t, ti_, tj_: (ti_[t], HP + h))
    ks = pl.BlockSpec((tb, LANES), lambda h, t, ti_, tj_: (tj_[t], h))
    crs = pl.BlockSpec((None, SUBLANES, tb), lambda h, t, ti_, tj_: (h, 0, tj_[t]))
    return pl.pallas_call(
        body, name=name,
        grid_spec=pltpu.PrefetchScalarGridSpec(
            num_scalar_prefetch=2, grid=(HP, len(steps)),
            in_specs=[qs, ks, ks, qs, qs1, crs],
            out_specs=[qs, pl.BlockSpec((2, tb, LANES), lambda h, t, ti_, tj_: (0, ti_[t], h))],
            scratch_shapes=[pltpu.VMEM((2, tb, LANES), F32), pltpu.VMEM((2, tb, LANES), F32),
                            pltpu.VMEM((tb, LANES), F32), pltpu.VMEM((nblk, rb, tb), F32),
                            pltpu.VMEM((nblk, rb, tb), BF16), pltpu.VMEM((nblk, rb, LANES), F32)]),
        out_shape=[jax.ShapeDtypeStruct((S_len, W), F32), jax.ShapeDtypeStruct((2, S_len, W), F32)],
        compiler_params=_params("parallel", "arbitrary"),
    )(ti, tj, qn, kn, vb, c_b, c_b, c_rowp)


def _fox_attn_bwd(qn, kn, vb, c_b, c_rowp, lse_b, delta_b, do, *, tb, name):
    S_len, W = qn.shape
    HP = W // LANES
    tb = min(tb, S_len)
    nb = S_len // tb
    scale = FOX_DH ** -0.5
    rb, cb = min(ATTN_ROWS, tb), min(ATTN_COLS, tb)

    steps = [(j, i) for j in range(nb) for i in range(j, nb)]
    tj = jnp.asarray([s[0] for s in steps], jnp.int32)
    ti = jnp.asarray([s[1] for s in steps], jnp.int32)

    def body(tj_ref, ti_ref, q_ref, k_ref, v_ref, cb0_ref, cb1_ref, cr_ref, lse_ref, dl0_ref, dl1_ref, do_ref,
             dq_ref, dk_ref, dv_ref, dcr_ref, dct_ref, dk_scr, dv_scr, dc_scr, p_scr, ds_scr):
        t = pl.program_id(1)
        j, i = tj_ref[t], ti_ref[t]

        @pl.when(t == 0)
        def _():
            dq_ref[...] = jnp.zeros(dq_ref.shape, F32)
            dct_ref[...] = jnp.zeros(dct_ref.shape, F32)

        @pl.when(i == j)
        def _():
            dk_scr[...] = jnp.zeros(dk_scr.shape, F32)
            dv_scr[...] = jnp.zeros(dv_scr.shape, F32)
            dc_scr[...] = jnp.zeros(dc_scr.shape, F32)

        def compute(diag):
            lo = _iota((1, LANES), 1) < FOX_DH
            masks = [lo, jnp.logical_not(lo)]
            row0 = pl.multiple_of(i * tb, tb)
            npiece = tb // LANES
            colsum = [[None] * npiece for _ in range(2)]

            def visible(r):
                return ((r + 1) * rb - 1) // LANES + 1 if diag else npiece

            for hh in range(2):
                for r in range(tb // rb):
                    rows = slice(r * rb, (r + 1) * rb)
                    qr = q_ref[rows, :]
                    qh = jnp.where(masks[hh], qr * scale, jnp.zeros_like(qr))
                    doh = jnp.where(masks[hh], do_ref[rows, :], 0.0).astype(BF16)
                    bq = (cb0_ref if hh == 0 else cb1_ref)[rows, :] - lse_ref[hh, rows, :]
                    dlt = (dl0_ref if hh == 0 else dl1_ref)[rows, :]
                    rsum = None
                    for c in range(tb // cb):
                        first, last = c * cb // LANES, min((c + 1) * cb // LANES, visible(r))
                        for piece in range(max(first, last), (c + 1) * cb // LANES):
                            cols = slice(piece * LANES, (piece + 1) * LANES)
                            p_scr[hh, rows, cols] = jnp.zeros((rb, LANES), BF16)
                            ds_scr[hh, rows, cols] = jnp.zeros((rb, LANES), BF16)
                        if first >= last:
                            continue
                        s2 = lax.dot_general(qh, k_ref[c * cb:(c + 1) * cb, :], _DIMS["nt"], preferred_element_type=F32)
                        dp2 = lax.dot_general(doh, v_ref[c * cb:(c + 1) * cb, :], _DIMS["nt"], preferred_element_type=F32)
                        for piece in range(first, last):
                            cols = slice(piece * LANES, (piece + 1) * LANES)
                            sub = slice(piece * LANES - c * cb, (piece + 1) * LANES - c * cb)
                            s = s2[:, sub] + bq - cr_ref[hh:hh + 1, cols]
                            if diag and (piece + 1) * LANES - 1 > r * rb:
                                keep = piece * LANES + _iota((rb, LANES), 1) <= r * rb + _iota((rb, LANES), 0)
                                s = jnp.where(keep, s, -jnp.inf)
                            p = jnp.exp(s)
                            ds = p * (dp2[:, sub] - dlt)
                            p_scr[hh, rows, cols] = p.astype(BF16)
                            ds_scr[hh, rows, cols] = ds.astype(BF16)
                            rsum = ds if rsum is None else rsum + ds
                            csum = jnp.sum(ds, axis=0, keepdims=True)
                            colsum[hh][piece] = csum if colsum[hh][piece] is None else colsum[hh][piece] + csum
                    grow = pl.ds(row0 + r * rb, rb)
                    dct_ref[grow, :] += jnp.where(_iota((1, SUBLANES), 1) == hh, jnp.sum(rsum, -1, keepdims=True), 0.0)
            k = k_ref[...]
            qf = q_ref[...]
            dof = do_ref[...]
            dq_part = jnp.zeros((tb, LANES), F32)
            for hh in range(2):
                kh = jnp.where(masks[hh], k, jnp.zeros_like(k))
                qhf = jnp.where(masks[hh], qf * scale, jnp.zeros_like(qf))
                dohf = jnp.where(masks[hh], dof, 0.0).astype(BF16)
                dv_scr[...] += lax.dot_general(p_scr[hh], dohf, _DIMS["tn"], preferred_element_type=F32)
                dk_scr[...] += lax.dot_general(ds_scr[hh], qhf, _DIMS["tn"], preferred_element_type=F32)
                dq_part = dq_part + lax.dot_general(ds_scr[hh], kh, _DIMS["nn"], preferred_element_type=F32)
                for piece in range(npiece):
                    if colsum[hh][piece] is not None:
                        dc_scr[hh:hh + 1, piece * LANES:(piece + 1) * LANES] -= colsum[hh][piece]
            dq_ref[pl.ds(row0, tb), :] += dq_part * scale

        @pl.when(i > j)
        def _():
            compute(False)

        @pl.when(i == j)
        def _():
            compute(True)

        @pl.when(i == nb - 1)
        def _():
            dk_ref[...] = dk_scr[...]
            dv_ref[...] = dv_scr[...]
            dcr_ref[...] = dc_scr[...]

    qs = pl.BlockSpec((tb, LANES), lambda h, t, tj_, ti_: (ti_[t], h))
    qs1 = pl.BlockSpec((tb, LANES), lambda h, t, tj_, ti_: (ti_[t], HP + h))
    ks = pl.BlockSpec((tb, LANES), lambda h, t, tj_, ti_: (tj_[t], h))
    crs = pl.BlockSpec((None, SUBLANES, tb), lambda h, t, tj_, ti_: (h, 0, tj_[t]))
    whole = pl.BlockSpec((S_len, LANES), lambda h, t, tj_, ti_: (0, h))
    return pl.pallas_call(
        body, name=name,
        grid_spec=pltpu.PrefetchScalarGridSpec(
            num_scalar_prefetch=2, grid=(HP, len(steps)),
            in_specs=[qs, ks, ks, qs, qs1, crs, pl.BlockSpec((2, tb, LANES), lambda h, t, tj_, ti_: (0, ti_[t], h)),
                      qs, qs1, qs],
            out_specs=[whole, ks, ks, crs, pl.BlockSpec((None, S_len, SUBLANES), lambda h, t, tj_, ti_: (h, 0, 0))],
            scratch_shapes=[pltpu.VMEM((tb, LANES), F32), pltpu.VMEM((tb, LANES), F32),
                            pltpu.VMEM((SUBLANES, tb), F32), pltpu.VMEM((2, tb, tb), BF16),
                            pltpu.VMEM((2, tb, tb), BF16)]),
        out_shape=[jax.ShapeDtypeStruct((S_len, W), F32)] * 3 + [jax.ShapeDtypeStruct((HP, SUBLANES, S_len), F32),
                                                                 jax.ShapeDtypeStruct((HP, S_len, SUBLANES), F32)],
        compiler_params=_params("parallel", "arbitrary"),
    )(tj, ti, qn, kn, vb, c_b, c_b, c_rowp, lse_b, delta_b, delta_b, do)


def _fox_post_tile(o, z):
    return o * _silu(z)


def _fox_post_fwd(o, h_main, *, tr, name):
    W = o.shape[1]

    def body(rows, consts, orows, oaccs, scr, step, blk):
        orows[0][...] = _fox_post_tile(rows[0][...], rows[1][...]).astype(BF16)

    z = (h_main, lambda tr_, n: pl.BlockSpec((tr_, W), lambda i: (i, 3)))
    return _rowcall(body, [o, z], [], [(W, BF16)], [], tr=tr, name=name)[0]


def _fox_post_bwd(o, h_main, d_og, *, tr, name):
    W = o.shape[1]

    def body(rows, consts, orows, oaccs, scr, step, blk):
        _, vjp = jax.vjp(_fox_post_tile, rows[0][...], rows[1][...])
        d_o, d_z = vjp(rows[2][...])
        orows[0][...] = d_o
        orows[1][...] = d_z.astype(BF16)
        lo_rows = (_iota((LANES, LANES), 0) < FOX_DH)
        for s in range(W // LANES):
            ls = slice(s * LANES, (s + 1) * LANES)
            prod = d_o[:, ls] * rows[0][:, ls]
            orows[2][:, ls] = _raw_mm(prod, lo_rows.astype(F32), "nn", "cb")
            orows[2][:, W + s * LANES:W + (s + 1) * LANES] = _raw_mm(prod, jnp.logical_not(lo_rows).astype(F32), "nn", "cb")

    z = (h_main, lambda tr_, n: pl.BlockSpec((tr_, W), lambda i: (i, 3)))
    return _rowcall(body, [o, z, d_og], [], [(W, F32), (W, BF16), (2 * W, F32)], [], tr=tr, name=name)


MESH_IDS = pl.DeviceIdType.MESH
N_CHIPS = 4
N_DEV = 8
_ANY = pl.BlockSpec(memory_space=pl.ANY)


def _xy_exchange(src, *, gather, name):
    def body(src_ref, out_ref, send_sems, recv_sems, local_sem):
        _xy_start(src_ref, out_ref, send_sems, recv_sems, local_sem, gather)
        _xy_wait(src_ref, out_ref, send_sems, recv_sems, local_sem, gather)

    return pl.pallas_call(
        body, name=name, in_specs=[_ANY], out_specs=_ANY,
        out_shape=_xy_out_shape(src), scratch_shapes=_xy_sems(),
    )(src)


def _xy_out_shape(src):
    return jax.ShapeDtypeStruct((N_CHIPS,) + tuple(src.shape[-2:]), src.dtype)


def _xy_sems():
    return [pltpu.SemaphoreType.DMA((N_CHIPS - 1,)), pltpu.SemaphoreType.DMA((N_CHIPS - 1,)), pltpu.SemaphoreType.DMA]


def _xy_copies(src_ref, out_ref, send_sems, recv_sems, local_sem, gather, with_arrivals=True):
    x, y, c = lax.axis_index("x"), lax.axis_index("y"), lax.axis_index("c")
    me = 2 * x + y
    peers = [(1 - x, y), (x, 1 - y), (1 - x, 1 - y)]

    def outgoing(px, py):
        return src_ref if gather else src_ref.at[2 * px + py]

    def copy(j, px, py, slot):
        return pltpu.make_async_remote_copy(
            src_ref=outgoing(px, py), dst_ref=out_ref.at[slot], send_sem=send_sems.at[j], recv_sem=recv_sems.at[j],
            device_id=(px, py, c), device_id_type=MESH_IDS)

    mine = pltpu.make_async_copy(outgoing(x, y), out_ref.at[me], local_sem)
    sends = [copy(j, px, py, me) for j, (px, py) in enumerate(peers)]
    arrivals = [copy(j, px, py, 2 * px + py) for j, (px, py) in enumerate(peers)] if with_arrivals else []
    return mine, sends, arrivals


def _xy_start(src_ref, out_ref, send_sems, recv_sems, local_sem, gather):
    mine, sends, _ = _xy_copies(src_ref, out_ref, send_sems, recv_sems, local_sem, gather, with_arrivals=False)
    mine.start()
    for cp in sends:
        cp.start()


def _xy_wait(src_ref, out_ref, send_sems, recv_sems, local_sem, gather):
    mine, sends, arrivals = _xy_copies(src_ref, out_ref, send_sems, recv_sems, local_sem, gather)
    for cp in arrivals:
        cp.wait_recv()
    for cp in sends:
        cp.wait_send()
    mine.wait()


def _c_swap(src, *, name):
    def body(src_ref, out_ref, send_sem, recv_sem):
        x, y, c = lax.axis_index("x"), lax.axis_index("y"), lax.axis_index("c")
        cp = pltpu.make_async_remote_copy(
            src_ref=src_ref, dst_ref=out_ref, send_sem=send_sem, recv_sem=recv_sem,
            device_id=(x, y, 1 - c), device_id_type=MESH_IDS)
        cp.start()
        cp.wait()

    return pl.pallas_call(
        body, name=name, in_specs=[_ANY], out_specs=_ANY,
        out_shape=jax.ShapeDtypeStruct(src.shape, src.dtype),
        scratch_shapes=[pltpu.SemaphoreType.DMA, pltpu.SemaphoreType.DMA],
    )(src)


def _all_gather8(blk, *, name):
    m_per, n = blk.shape

    def body(x_ref, out_ref, send_sems, recv_sems, local_sem):
        x, y, c = lax.axis_index("x"), lax.axis_index("y"), lax.axis_index("c")
        me, sibling = (x, y, c), (x, y, 1 - c)
        chips = [(1 - x, y), (x, 1 - y), (1 - x, 1 - y)]

        def rows(px, py, pc):
            return out_ref.at[pl.ds((4 * px + 2 * py + pc) * m_per, m_per), :]

        def copy(k, block, to, src=None):
            return pltpu.make_async_remote_copy(
                src_ref=rows(*block) if src is None else src, dst_ref=rows(*block),
                send_sem=send_sems.at[k], recv_sem=recv_sems.at[k], device_id=to, device_id_type=MESH_IDS)

        mine = pltpu.make_async_copy(x_ref, rows(*me), local_sem)
        mine.start()
        first = [copy(0, me, sibling, src=x_ref)]
        first += [copy(1 + j, me, (*chip, c), src=x_ref) for j, chip in enumerate(chips)]
        for cp in first:
            cp.start()
        passed = [copy(4 + j, (*chip, c), sibling) for j, chip in enumerate(chips)]
        for j, chip in enumerate(chips):
            copy(1 + j, (*chip, c), me).wait_recv()
            passed[j].start()
        copy(0, sibling, me).wait_recv()
        for j, chip in enumerate(chips):
            copy(4 + j, (*chip, 1 - c), me).wait_recv()
        for cp in first + passed:
            cp.wait_send()
        mine.wait()

    return pl.pallas_call(
        body, name=name,
        out_shape=jax.ShapeDtypeStruct((N_DEV * m_per, n), blk.dtype),
        in_specs=[pl.BlockSpec(memory_space=pltpu.VMEM)], out_specs=pl.BlockSpec(memory_space=pltpu.VMEM),
        scratch_shapes=[pltpu.SemaphoreType.DMA((7,)), pltpu.SemaphoreType.DMA((7,)), pltpu.SemaphoreType.DMA],
    )(blk)


def _sum_slots(parts, *, tr, name):
    n, R, _ = parts.shape
    pack = 2 * SUBLANES
    tr = max(t for t in range(pack, min(tr, R) + 1, pack) if R % t == 0) if R % pack == 0 else R

    def body(p_ref, o_ref):
        tot = p_ref[0].astype(F32)
        for s in range(1, n):
            tot = tot + p_ref[s].astype(F32)
        o_ref[...] = tot

    return pl.pallas_call(
        body, name=name, grid=(R // tr,),
        in_specs=[pl.BlockSpec((n, tr, LANES), lambda i: (0, i, 0))], out_specs=pl.BlockSpec((tr, LANES), lambda i: (i, 0)),
        out_shape=jax.ShapeDtypeStruct((R, LANES), F32), compiler_params=_params("parallel"),
    )(parts)


def _adamw(w, g_parts, m, v, *, name):
    shape = w.shape
    as2d = lambda a: a.reshape(-1, shape[-1])
    w2, m2, v2 = as2d(w), as2d(m), as2d(v)
    gs = [as2d(g) for g in g_parts]
    R, C = w2.shape
    tr = R
    while tr * C * 4 > (1 << 20) and tr % 2 == 0 and (tr // 2) % SUBLANES == 0:
        tr //= 2
    ng = len(gs)

    def body(*refs):
        w_ref, m_ref, v_ref = refs[:3]
        g_refs = refs[3:3 + ng]
        go_ref, d_ref, mo_ref, vo_ref = refs[3 + ng:]
        g = g_refs[0][...]
        for r in g_refs[1:]:
            g = g + r[...]
        mn = ADAM_B1 * m_ref[...] + (1.0 - ADAM_B1) * g
        vn = ADAM_B2 * v_ref[...] + (1.0 - ADAM_B2) * jnp.square(g)
        m_hat = mn / (1.0 - ADAM_B1 ** ADAM_STEP)
        v_hat = vn / (1.0 - ADAM_B2 ** ADAM_STEP)
        go_ref[...] = g
        d_ref[...] = -ADAM_LR * (m_hat / (jnp.sqrt(v_hat) + ADAM_EPS) + ADAM_WD * w_ref[...])
        mo_ref[...] = mn
        vo_ref[...] = vn

    spec = pl.BlockSpec((tr, C), lambda i: (i, 0))
    outs = pl.pallas_call(
        body, name=name, grid=(R // tr,), in_specs=[spec] * (3 + ng), out_specs=[spec] * 4,
        out_shape=[jax.ShapeDtypeStruct((R, C), F32)] * 4, compiler_params=_params("parallel"),
    )(w2, m2, v2, *gs)
    return tuple(o.reshape(shape) for o in outs)


TR = 256
ATTN_TILE = 1024
ATTN_ROWS = 128
ATTN_COLS = 256


def _mm_nn(a, b, name, **kw):
    return _matmul(a, b, tm=2048, tn=1024, tk=1024, name=name, **kw)


def _mm_nt(a, b, name, **kw):
    return _matmul(a, b, tb=True, tm=1024, tn=1024, tk=1024, name=name, **kw)


def _mm_tn(a, b, name, **kw):
    return _matmul(a, b, ta=True, tm=1024, tn=2048, tk=512, name=name, **kw)


def _c_rows(c, Hf):
    S_len = c.shape[0]
    ct = c[:, :Hf].T.reshape(Hf // 2, 2, S_len)
    return jnp.pad(ct, ((0, 0), (0, SUBLANES - 2), (0, 0)))


def _local_step(x, p, target, wts, late_weights=None, early_grads=None):
    L = len(wts["ln_g"])
    alpha = (2 * L) ** 0.25
    Hg = wts["gdn_a_log"][0].shape[-1]
    Hf = wts["fox_b_f"][0].shape[-1]
    Wg_ = Hg * GDN_DK
    Wf_ = Hf * FOX_DH
    saved = []
    for i in range(L):
        j = i // 2
        sv = {"x": x}
        if i % 2 == 0:
            w_in = wts["gdn_w_in"][j]
            wm, ws = w_in[:, :4 * Wg_], _pad_lanes(w_in[:, 4 * Wg_:])
            cw8 = jnp.pad(wts["gdn_conv_w"][j], ((0, SUBLANES - GDN_CONV), (0, 0)))
            alog = _pad_lanes(_row(wts["gdn_a_log"][j]), offset=Hg)
            dtb = _pad_lanes(_row(wts["gdn_dt_bias"][j]), offset=Hg)
            hm = _mm_nn(x, wm, f"gdn{j}_in_main")
            hs = _mm_nn(x, ws, f"gdn{j}_in_small")
            q, k, v, bg = _gdn_pre_fwd(hm, hs, cw8, alog, dtb, Hg, tr=TR, name=f"gdn{j}_pre")
            if i == 0 and late_weights is not None:
                o, states, tinv, got = _gdn_rule_fwd(q, k, v, bg, Hg, name=f"gdn{j}_rule", carry=(late_weights[0], True))
                for (wname_, idx), arr in late_weights[1](got).items():
                    wts[wname_][idx] = arr
            else:
                o, states, tinv = _gdn_rule_fwd(q, k, v, bg, Hg, name=f"gdn{j}_rule")
            on = _gdn_post_fwd(o, hm, wts["gdn_norm_g"][j], Hg, tr=TR, name=f"gdn{j}_post")
            y = _mm_nn(on, wts["gdn_w_out"][j], f"gdn{j}_out")
            sv.update(wm=wm, ws=ws, cw8=cw8, alog=alog, dtb=dtb, hm=hm, hs=hs, q=q, k=k, v=v, bg=bg, o=o,
                      states=states, tinv=tinv, on=on)
        else:
            w_in = wts["fox_w_in"][j]
            wm, ws = w_in[:, :4 * Wf_], _pad_lanes(w_in[:, 4 * Wf_:])
            gq2 = _row(jnp.tile(wts["fox_q_norm_g"][j], 2))
            gk2 = _row(jnp.tile(wts["fox_k_norm_g"][j], 2))
            bf = _pad_lanes(_row(wts["fox_b_f"][j]))
            hm = _mm_nn(x, wm, f"fox{j}_in_main")
            hs = _mm_nn(x, ws, f"fox{j}_in_small")
            qn, kn, vb, c, c_b = _fox_pre_fwd(hm, hs, gq2, gk2, bf, Hf, tr=TR, name=f"fox{j}_pre")
            c_rowp = _c_rows(c, Hf)
            o, lse_b = _fox_attn_fwd(qn, kn, vb, c_b, c_rowp, tb=ATTN_TILE, name=f"fox{j}_attn")
            on = _fox_post_fwd(o, hm, tr=TR, name=f"fox{j}_post")
            y = _mm_nn(on, wts["fox_w_out"][j], f"fox{j}_out")
            sv.update(wm=wm, ws=ws, gq2=gq2, gk2=gk2, bf=bf, hm=hm, hs=hs, qn=qn, kn=kn, vb=vb, c_b=c_b,
                      c_rowp=c_rowp, o=o, lse_b=lse_b, on=on)
        x_ln = _ln_fwd(x, y, wts["ln_g"][i], wts["ln_b"][i], alpha, tr=TR, name=f"ln{i}")
        gp = _mm_nn(x_ln, wts["ple_w_gate"][i], f"ple{i}_gate")
        pp = _mm_nn(p[i], wts["ple_w_proj"][i], f"ple{i}_proj")
        x_out = _ple_fwd(x_ln, gp, pp, tr=TR, name=f"ple{i}_mix")
        sv.update(y=y, x_ln=x_ln, gp=gp, pp=pp)
        saved.append(sv)
        x = x_out

    loss_row, dx = _loss_fwd_bwd(x, target, tr=TR, name="loss")

    g = {n: [None] * len(wts[n]) for n in wts}
    got_early = None
    for i in reversed(range(L)):
        j = i // 2
        sv = saved[i]
        d_pre, d_pp = _ple_bwd(dx, sv["gp"], sv["pp"], tr=TR, name=f"ple{i}_mix_bwd")
        g["ple_w_gate"][i] = _mm_tn(sv["x_ln"], d_pre, f"ple{i}_gate_dw")
        g["ple_w_proj"][i] = _mm_tn(p[i], d_pp, f"ple{i}_proj_dw")
        t = _mm_nt(d_pre, wts["ple_w_gate"][i], f"ple{i}_gate_dx")
        du, g["ln_g"][i], g["ln_b"][i] = _ln_bwd(sv["x"], sv["y"], wts["ln_g"][i], wts["ln_b"][i], dx, t, alpha,
                                                 tr=TR, name=f"ln{i}_bwd")
        if i % 2 == 0:
            g["gdn_w_out"][j] = _mm_tn(sv["on"], du, f"gdn{j}_out_dw")
            d_on = _mm_nt(du, wts["gdn_w_out"][j], f"gdn{j}_out_dx")
            d_o, d_z, d_ng = _gdn_post_bwd(sv["o"], sv["hm"], wts["gdn_norm_g"][j], d_on, Hg, tr=TR, name=f"gdn{j}_post_bwd")
            rule_args = (sv["q"], sv["k"], sv["v"], sv["bg"], sv["states"], sv["tinv"], d_o, Hg)
            if i == 0 and early_grads is not None:
                dq, dk, dv, dbg, got_early = _gdn_rule_bwd(*rule_args, name=f"gdn{j}_rule_bwd",
                                                           carry=(early_grads(g), False))
            else:
                dq, dk, dv, dbg = _gdn_rule_bwd(*rule_args, name=f"gdn{j}_rule_bwd")
            d_hm, d_hs, d_cw, d_al, d_dtb = _gdn_pre_bwd(sv["hm"], sv["hs"], sv["cw8"], sv["alog"], sv["dtb"],
                                                         dq, dk, dv, dbg, d_z, Hg, tr=TR, name=f"gdn{j}_pre_bwd")
            g["gdn_norm_g"][j] = d_ng[0]
            g["gdn_conv_w"][j] = d_cw[:GDN_CONV]
            g["gdn_a_log"][j] = d_al[0, Hg:2 * Hg]
            g["gdn_dt_bias"][j] = d_dtb[0, Hg:2 * Hg]
            wname, nsmall = "gdn_w_in", 2 * Hg
        else:
            g["fox_w_out"][j] = _mm_tn(sv["on"], du, f"fox{j}_out_dw")
            d_og = _mm_nt(du, wts["fox_w_out"][j], f"fox{j}_out_dx")
            d_o, d_z, delta_b = _fox_post_bwd(sv["o"], sv["hm"], d_og, tr=TR, name=f"fox{j}_post_bwd")
            dqn, dkn, dvv, dcr, dct = _fox_attn_bwd(sv["qn"], sv["kn"], sv["vb"], sv["c_b"], sv["c_rowp"], sv["lse_b"],
                                                    delta_b, d_o, tb=ATTN_TILE, name=f"fox{j}_attn_bwd")
            dc = _pad_lanes(dcr[:, :2, :].reshape(Hf, -1).T + dct[:, :, :2].transpose(1, 0, 2).reshape(-1, Hf))
            d_hm, d_hs, d_gq, d_gk, d_bf = _fox_pre_bwd(sv["hm"], sv["hs"], sv["gq2"], sv["gk2"], sv["bf"],
                                                        dqn, dkn, dvv, d_z, dc, Hf, tr=TR, name=f"fox{j}_pre_bwd")
            g["fox_q_norm_g"][j] = d_gq[0, :FOX_DH] + d_gq[0, FOX_DH:]
            g["fox_k_norm_g"][j] = d_gk[0, :FOX_DH] + d_gk[0, FOX_DH:]
            g["fox_b_f"][j] = d_bf[0, :Hf]
            wname, nsmall = "fox_w_in", Hf
        dwm = _mm_tn(sv["x"], d_hm, f"{wname}{j}_main_dw")
        dws = _mm_tn(sv["x"], d_hs, f"{wname}{j}_small_dw")
        g[wname][j] = jnp.concatenate([dwm, dws[:, :nsmall]], axis=1)
        t1 = _mm_nt(d_hs, sv["ws"], f"{wname}{j}_small_dx", add=du, add_scale=alpha)
        dx = _mm_nt(d_hm, sv["wm"], f"{wname}{j}_main_dx", add=t1)
    return loss_row, dx, g, got_early


_SHARDED = (("ple_w_gate", 1), ("ple_w_proj", 2), ("gdn_w_in", 2), ("gdn_conv_w", 2), ("gdn_w_out", 1),
            ("fox_w_in", 2), ("fox_w_out", 1))
_SHARD_AXIS = dict(_SHARDED)
_REPLICATED = ("ln_g", "ln_b", "gdn_a_log", "gdn_dt_bias", "gdn_norm_g", "fox_b_f", "fox_q_norm_g", "fox_k_norm_g")
_EXACT = ("gdn_conv_w",)
_ORDER = ("ln_g", "ln_b", "ple_w_gate", "ple_w_proj", "gdn_w_in", "gdn_conv_w", "gdn_a_log", "gdn_dt_bias",
          "gdn_norm_g", "gdn_w_out", "fox_w_in", "fox_b_f", "fox_q_norm_g", "fox_k_norm_g", "fox_w_out")
_FIRST_WEIGHTS = (("gdn_w_in", 0), ("gdn_conv_w", 0))
_LAST_GRADS = (("ple_w_gate", 0), ("ple_w_proj", 0), ("gdn_w_in", 0), ("gdn_conv_w", 0), ("gdn_w_out", 0))
PACK_ROWS = 2 * SUBLANES


def _as_rows(a):
    rows = a.reshape(-1, LANES)
    return jnp.pad(rows, ((0, -rows.shape[0] % PACK_ROWS), (0, 0)))


def _n_elements(shape):
    n = 1
    for d in shape:
        n *= d
    return n


def _pack_weights(local, items):
    parts = []
    for name, idx in items:
        w = local[name][idx]
        parts.append(_as_rows(lax.bitcast_convert_type(w, BF16) if name in _EXACT else w.astype(BF16)))
    return jnp.concatenate(parts, axis=0), [q.shape[0] for q in parts]


def _unpack_weights(got, local, items, sizes):
    out, r0 = {}, 0
    for (name, idx), nrow in zip(items, sizes):
        shp = tuple(local[name].shape[1:])
        n_el = _n_elements(shp) * (2 if name in _EXACT else 1)
        seg = got[:, r0:r0 + nrow].reshape(N_CHIPS, -1)[:, :n_el]
        r0 += nrow
        if name in _EXACT:
            blocks = lax.bitcast_convert_type(seg.reshape((N_CHIPS,) + shp + (2,)), F32)
        else:
            blocks = seg.reshape((N_CHIPS,) + shp)
        axis = _SHARD_AXIS[name] - 1
        joined = shp[:axis] + (N_CHIPS * shp[axis],) + shp[axis + 1:]
        out[(name, idx)] = jnp.moveaxis(blocks, 0, axis).reshape(joined)
    return out


def _pack_grads(g, items):
    parts = []
    for name, idx in items:
        gfull = g[name][idx]
        axis = _SHARD_AXIS[name] - 1
        shp = gfull.shape
        split = shp[:axis] + (N_CHIPS, shp[axis] // N_CHIPS) + shp[axis + 1:]
        rows = jnp.moveaxis(gfull.reshape(split), axis, 0).astype(BF16).reshape(N_CHIPS, -1, LANES)
        parts.append(jnp.pad(rows, ((0, 0), (0, -rows.shape[1] % PACK_ROWS), (0, 0))))
    return jnp.concatenate(parts, axis=1), [q.shape[1] for q in parts]


def _unpack_grads(flat, local, items, sizes):
    out, r0 = {}, 0
    for (name, idx), nrow in zip(items, sizes):
        shp = tuple(local[name].shape[1:])
        out[(name, idx)] = flat[r0:r0 + nrow].reshape(-1)[:_n_elements(shp)].reshape(shp)
        r0 += nrow
    return out


def _reduce_replicated(grads, loss_part):
    rows = [_pad_lanes(jnp.reshape(loss_part, (1, 1)))]
    for name in _REPLICATED:
        gr = grads[name]
        rows.append(gr.reshape(-1, LANES) if gr.shape[-1] % LANES == 0 else _pad_lanes(gr))
    sizes = [r.shape[0] for r in rows]
    blk = jnp.concatenate(rows, axis=0)
    nrow = blk.shape[0]
    npad = -nrow % SUBLANES
    blk = jnp.pad(blk, ((0, npad), (0, 0)))
    allb = _all_gather8(blk, name="gather_small_grads").reshape(N_DEV, nrow + npad, LANES)
    tot = _sum_slots(allb, tr=nrow + npad, name="sum_small_grads")
    out, r0 = {}, sizes[0]
    loss = tot[0, 0]
    for name, n in zip(_REPLICATED, sizes[1:]):
        gr = grads[name]
        seg = tot[r0:r0 + n]
        out[name] = seg.reshape(gr.shape) if gr.shape[-1] % LANES == 0 else seg[:, :gr.shape[-1]]
        r0 += n
    return loss, out


def kernel(x, p, ln_g, ln_b, ple_w_gate, ple_w_proj, gdn_w_in, gdn_conv_w, gdn_a_log, gdn_dt_bias, gdn_norm_g, gdn_w_out, fox_w_in, fox_b_f, fox_q_norm_g, fox_k_norm_g, fox_w_out, loss_target, m_ln_g, m_ln_b, m_ple_w_gate, m_ple_w_proj, m_gdn_w_in, m_gdn_conv_w, m_gdn_a_log, m_gdn_dt_bias, m_gdn_norm_g, m_gdn_w_out, m_fox_w_in, m_fox_b_f, m_fox_q_norm_g, m_fox_k_norm_g, m_fox_w_out, v_ln_g, v_ln_b, v_ple_w_gate, v_ple_w_proj, v_gdn_w_in, v_gdn_conv_w, v_gdn_a_log, v_gdn_dt_bias, v_gdn_norm_g, v_gdn_w_out, v_fox_w_in, v_fox_b_f, v_fox_q_norm_g, v_fox_k_norm_g, v_fox_w_out):
    local = dict(ln_g=ln_g, ln_b=ln_b, ple_w_gate=ple_w_gate, ple_w_proj=ple_w_proj, gdn_w_in=gdn_w_in,
                 gdn_conv_w=gdn_conv_w, gdn_a_log=gdn_a_log, gdn_dt_bias=gdn_dt_bias, gdn_norm_g=gdn_norm_g,
                 gdn_w_out=gdn_w_out, fox_w_in=fox_w_in, fox_b_f=fox_b_f, fox_q_norm_g=fox_q_norm_g,
                 fox_k_norm_g=fox_k_norm_g, fox_w_out=fox_w_out)
    mom_m = dict(ln_g=m_ln_g, ln_b=m_ln_b, ple_w_gate=m_ple_w_gate, ple_w_proj=m_ple_w_proj, gdn_w_in=m_gdn_w_in,
                 gdn_conv_w=m_gdn_conv_w, gdn_a_log=m_gdn_a_log, gdn_dt_bias=m_gdn_dt_bias, gdn_norm_g=m_gdn_norm_g,
                 gdn_w_out=m_gdn_w_out, fox_w_in=m_fox_w_in, fox_b_f=m_fox_b_f, fox_q_norm_g=m_fox_q_norm_g,
                 fox_k_norm_g=m_fox_k_norm_g, fox_w_out=m_fox_w_out)
    mom_v = dict(ln_g=v_ln_g, ln_b=v_ln_b, ple_w_gate=v_ple_w_gate, ple_w_proj=v_ple_w_proj, gdn_w_in=v_gdn_w_in,
                 gdn_conv_w=v_gdn_conv_w, gdn_a_log=v_gdn_a_log, gdn_dt_bias=v_gdn_dt_bias, gdn_norm_g=v_gdn_norm_g,
                 gdn_w_out=v_gdn_w_out, fox_w_in=v_fox_w_in, fox_b_f=v_fox_b_f, fox_q_norm_g=v_fox_q_norm_g,
                 fox_k_norm_g=v_fox_k_norm_g, fox_w_out=v_fox_w_out)

    items = [(name, idx) for name, _ in _SHARDED for idx in range(local[name].shape[0])]
    w_first = [it for it in items if it in _FIRST_WEIGHTS]
    w_later = [it for it in items if it not in _FIRST_WEIGHTS]
    g_early = [it for it in items if it not in _LAST_GRADS]
    g_last = [it for it in items if it in _LAST_GRADS]

    wts = {name: [None] * local[name].shape[0] for name, _ in _SHARDED}
    for name in _REPLICATED:
        wts[name] = local[name]
    packed, sizes = _pack_weights(local, w_first)
    got = _xy_exchange(packed, gather=True, name="gather_weights_first")
    for (name, idx), arr in _unpack_weights(got, local, w_first, sizes).items():
        wts[name][idx] = arr
    packed_later, sizes_later = _pack_weights(local, w_later)
    early_sizes = []

    def pack_early(g):
        packed_g, sz = _pack_grads(g, g_early)
        early_sizes.extend(sz)
        return packed_g

    loss_row, dx, g, got_early = _local_step(
        x[0], p[:, 0], loss_target[0], wts,
        late_weights=(packed_later, lambda res: _unpack_weights(res, local, w_later, sizes_later)),
        early_grads=pack_early)

    loss, small = _reduce_replicated({name: jnp.stack(g[name]) for name in _REPLICATED}, jnp.sum(loss_row))
    packed_last, last_sizes = _pack_grads(g, g_last)
    got_last = _xy_exchange(packed_last, gather=False, name="exchange_grads_last")
    mine, other = {}, {}
    for tag, res, its, szs in (("early", got_early, g_early, early_sizes), ("last", got_last, g_last, last_sizes)):
        part = _sum_slots(res, tr=4096, name=f"sum_grads_{tag}")
        sib = _c_swap(part, name=f"swap_grads_{tag}")
        mine.update(_unpack_grads(part, local, its, szs))
        other.update(_unpack_grads(sib, local, its, szs))

    outs = {}
    for name in _ORDER:
        if name in _SHARD_AXIS:
            n_layers = local[name].shape[0]
            parts = [jnp.stack([mine[(name, i)] for i in range(n_layers)]),
                     jnp.stack([other[(name, i)] for i in range(n_layers)])]
        else:
            parts = [small[name]]
        outs[name] = _adamw(local[name], parts, mom_m[name], mom_v[name], name=f"adamw_{name}")
    return (loss, dx[None], *[outs[n][0] for n in _ORDER], *[outs[n][1] for n in _ORDER],
            *[outs[n][2] for n in _ORDER], *[outs[n][3] for n in _ORDER])
```

```python
import functools

import jax
import jax.numpy as jnp
from jax import lax
from jax.experimental import pallas as pl
from jax.experimental.pallas import tpu as pltpu

F32 = jnp.float32
BF16 = jnp.bfloat16

LANES = 128
SUBLANES = 8
VMEM_LIMIT_BYTES = 56 * 1024 * 1024

GDN_DK = 128
GDN_CHUNK = 64
GDN_CONV = 4
FOX_DH = 64
LN_EPS = 1e-5
RMS_EPS = 1e-6

ADAM_LR = 0.001
ADAM_B1 = 0.9
ADAM_B2 = 0.999
ADAM_EPS = 1e-08
ADAM_WD = 0.01
ADAM_STEP = 10

_DIMS = {"nn": (((1,), (0,)), ((), ())), "nt": (((1,), (1,)), ((), ())), "tn": (((0,), (0,)), ((), ()))}


def _params(*sem):
    return pltpu.CompilerParams(dimension_semantics=sem, vmem_limit_bytes=VMEM_LIMIT_BYTES)


def _split(a, terms):
    out = []
    rest = a.astype(F32)
    for t in range(terms):
        piece = rest.astype(BF16)
        out.append(piece)
        if t + 1 < terms:
            rest = rest - piece.astype(F32)
    return out


def _raw_mm(a, b, form, mode):
    dot = lambda x, y: lax.dot_general(x, y, _DIMS[form], preferred_element_type=F32)
    if mode == "b":
        return dot(a.astype(BF16), b.astype(BF16))
    if mode == "x3":
        (ah, al), (bh, bl) = _split(a, 2), _split(b, 2)
        return dot(ah, bh) + (dot(ah, bl) + dot(al, bh))
    if mode == "ca":
        ac = a.astype(BF16)
        b1, b2, b3 = _split(b, 3)
        return dot(ac, b1) + (dot(ac, b2) + dot(ac, b3))
    assert mode == "cb", mode
    bc = b.astype(BF16)
    a1, a2, a3 = _split(a, 3)
    return dot(a1, bc) + (dot(a2, bc) + dot(a3, bc))


@functools.partial(jax.custom_vjp, nondiff_argnums=(2, 3))
def _mm(a, b, form, mode):
    return _raw_mm(a, b, form, mode)


def _mm_fwd(a, b, form, mode):
    return _raw_mm(a, b, form, mode), (a, b)


def _mm_bwd(form, mode, res, g):
    a, b = res
    flip = {"b": "b", "x3": "x3", "ca": "cb", "cb": "ca"}[mode]
    if form == "nn":
        da, db = (lambda: _mm(g, b, "nt", mode)), (lambda: _mm(a, g, "tn", mode))
    elif form == "nt":
        da, db = (lambda: _mm(g, b, "nn", mode)), (lambda: _mm(g, a, "tn", flip))
    else:
        da, db = (lambda: _mm(b, g, "nt", flip)), (lambda: _mm(a, g, "nn", mode))
    return (jnp.zeros_like(a) if mode == "ca" else da()), (jnp.zeros_like(b) if mode == "cb" else db())


_mm.defvjp(_mm_fwd, _mm_bwd)


@jax.custom_vjp
def _tri_inv(Ls):
    C = Ls[0].shape[0]
    eye = (_iota((C, C), 0) == _iota((C, C), 1)).astype(F32)
    X = [eye - L for L in Ls]
    P = [_raw_mm(L, L, "nn", "x3") for L in Ls]
    n_sq = max(1, (C - 1).bit_length() - 1)
    for it in range(n_sq):
        XP = [_raw_mm(x, p, "nn", "x3") for x, p in zip(X, P)]
        if it < n_sq - 1:
            P = [_raw_mm(p, p, "nn", "x3") for p in P]
        X = [x + xp for x, xp in zip(X, XP)]
    return X


def _tri_inv_fwd(Ls):
    Ts = _tri_inv(Ls)
    return Ts, Ts


def _tri_inv_bwd(Ts, dTs):
    Ms = [_raw_mm(dT, T, "nt", "x3") for dT, T in zip(dTs, Ts)]
    return ([-_raw_mm(T, M, "tn", "x3") for T, M in zip(Ts, Ms)],)


_tri_inv.defvjp(_tri_inv_fwd, _tri_inv_bwd)


@jax.custom_vjp
def _tri_inv_known(Ls, Ts):
    return Ts


def _tri_inv_known_fwd(Ls, Ts):
    return Ts, Ts


def _tri_inv_known_bwd(Ts, dTs):
    return _tri_inv_bwd(Ts, dTs)[0], [jnp.zeros_like(T) for T in Ts]


_tri_inv_known.defvjp(_tri_inv_known_fwd, _tri_inv_known_bwd)


def _silu(x):
    return x * jax.nn.sigmoid(x)


def _softplus(x):
    return jnp.maximum(x, 0.0) + jnp.log1p(jnp.exp(-jnp.abs(x)))


def _iota(shape, dim):
    return lax.broadcasted_iota(jnp.int32, shape, dim)


def _matmul(a, b, *, ta=False, tb=False, out_dtype=F32, add=None, add_scale=1.0, tm=512, tn=512, tk=512, name):
    if ta:
        K, M = a.shape
    else:
        M, K = a.shape
    if tb:
        N, K2 = b.shape
    else:
        K2, N = b.shape
    assert K == K2, (a.shape, b.shape, ta, tb)
    tm, tn, tk = min(tm, M), min(tn, N), min(tk, K)
    assert M % tm == 0 and N % tn == 0 and K % tk == 0, (M, N, K, tm, tn, tk)
    nk = K // tk
    form = ("t" if ta else "n") + ("t" if tb else "n")
    dims = (((0 if ta else 1,), (1 if tb else 0,)), ((), ()))
    del form
    a_spec = pl.BlockSpec((tk, tm), lambda i, j, k: (k, i)) if ta else pl.BlockSpec((tm, tk), lambda i, j, k: (i, k))
    b_spec = pl.BlockSpec((tn, tk), lambda i, j, k: (j, k)) if tb else pl.BlockSpec((tk, tn), lambda i, j, k: (k, j))
    o_spec = pl.BlockSpec((tm, tn), lambda i, j, k: (i, j))
    has_add = add is not None

    def body(*refs):
        a_ref, b_ref = refs[:2]
        add_ref = refs[2] if has_add else None
        o_ref = refs[3] if has_add else refs[2]
        acc_ref = refs[-1] if nk > 1 else None
        k = pl.program_id(2)
        part = lax.dot_general(a_ref[...].astype(BF16), b_ref[...].astype(BF16), dims, preferred_element_type=F32)

        def finish(total):
            if has_add:
                total = total + add_scale * add_ref[...].astype(F32)
            o_ref[...] = total.astype(o_ref.dtype)

        if nk == 1:
            finish(part)
        else:
            @pl.when(k == 0)
            def _():
                acc_ref[...] = part

            @pl.when(jnp.logical_and(k > 0, k < nk - 1))
            def _():
                acc_ref[...] += part

            @pl.when(k == nk - 1)
            def _():
                finish(acc_ref[...] + part)

    in_specs = [a_spec, b_spec] + ([o_spec] if has_add else [])
    args = (a, b) + ((add,) if has_add else ())
    return pl.pallas_call(
        body, name=name, grid=(M // tm, N // tn, nk),
        in_specs=in_specs, out_specs=o_spec,
        out_shape=jax.ShapeDtypeStruct((M, N), out_dtype),
        scratch_shapes=[pltpu.VMEM((tm, tn), F32)] if nk > 1 else [],
        compiler_params=_params("parallel", "parallel", "arbitrary"),
    )(*args)


def _rowcall(body_fn, rows, consts, out_rows, out_accs, *, tr, name, reverse=False, scratch=()):
    def arr_spec(r):
        return r if isinstance(r, tuple) else (r, None)

    S = arr_spec(rows[0])[0].shape[0]
    tr = min(tr, S)
    assert S % tr == 0
    n = S // tr
    ridx = (lambda i: (n - 1 - i, 0)) if reverse else (lambda i: (i, 0))
    in_specs, args = [], []
    for r in rows:
        arr, spec = arr_spec(r)
        args.append(arr)
        in_specs.append(spec(tr, n) if spec is not None else pl.BlockSpec((tr, arr.shape[1]), ridx))
    for c in consts:
        args.append(c)
        in_specs.append(pl.BlockSpec(c.shape, lambda i: (0, 0)))
    out_specs, out_shape = [], []
    for (ncol, dt) in out_rows:
        out_specs.append(pl.BlockSpec((tr, ncol), ridx))
        out_shape.append(jax.ShapeDtypeStruct((S, ncol), dt))
    for shp in out_accs:
        out_specs.append(pl.BlockSpec(shp, lambda i: (0, 0)))
        out_shape.append(jax.ShapeDtypeStruct(shp, F32))
    nr, nc, no, na = len(rows), len(consts), len(out_rows), len(out_accs)

    def kernel(*refs):
        row_refs = refs[:nr]
        const_refs = refs[nr:nr + nc]
        orow_refs = refs[nr + nc:nr + nc + no]
        oacc_refs = refs[nr + nc + no:nr + nc + no + na]
        scr = refs[nr + nc + no + na:]
        step = pl.program_id(0)
        blk = (n - 1 - step) if reverse else step

        @pl.when(step == 0)
        def _():
            for acc in oacc_refs:
                acc[...] = jnp.zeros(acc.shape, F32)

        body_fn(row_refs, const_refs, orow_refs, oacc_refs, scr, step, blk)

    outs = pl.pallas_call(
        kernel, name=name, grid=(n,), in_specs=in_specs, out_specs=out_specs, out_shape=out_shape,
        scratch_shapes=list(scratch), compiler_params=_params("arbitrary"),
    )(*args)
    return outs


def _row(v):
    return v.astype(F32).reshape(1, -1)


def _pad_lanes(v, width=LANES, offset=0):
    pad = [(0, 0)] * (v.ndim - 1) + [(offset, width - offset - v.shape[-1])]
    return jnp.pad(v, pad)


def _ln_tile(x, y, g, b, alpha):
    u = alpha * x + y
    mu = jnp.mean(u, -1, keepdims=True)
    d = u - mu
    var = jnp.mean(d * d, -1, keepdims=True)
    return d * lax.rsqrt(var + LN_EPS) * g + b


def _out_ln_fwd(on, w_out, x, g, b, alpha, *, tr, name):
    D = x.shape[1]

    def body(rows, consts, orows, oaccs, scr, step, blk):
        y = lax.dot_general(rows[0][...].astype(BF16), consts[0][...], _DIMS["nn"], preferred_element_type=F32)
        orows[0][...] = y
        orows[1][...] = _ln_tile(rows[1][...], y, consts[1][...], consts[2][...], alpha)

    return _rowcall(body, [on, x], [w_out, _row(g), _row(b)], [(D, F32), (D, F32)], [], tr=tr, name=name)


def _gate_mix_fwd(x_ln, w_gate, pp, *, tr, name):
    D = x_ln.shape[1]

    def body(rows, consts, orows, oaccs, scr, step, blk):
        a = rows[0][...]
        gp = lax.dot_general(a.astype(BF16), consts[0][...], _DIMS["nn"], preferred_element_type=F32)
        orows[0][...] = gp
        orows[1][...] = a + jax.nn.sigmoid(gp) * rows[1][...]

    return _rowcall(body, [x_ln, pp], [w_gate], [(D, F32), (D, F32)], [], tr=tr, name=name)


def _ple_ln_bwd(dxo, gp, pp, w_gate, x, y, g, b, alpha, *, tr, name):
    D = x.shape[1]

    def body(rows, consts, orows, oaccs, scr, step, blk):
        d = rows[0][...]
        s = jax.nn.sigmoid(rows[1][...])
        d_pre = (d * rows[2][...] * s * (1.0 - s)).astype(BF16)
        orows[0][...] = d_pre
        orows[1][...] = (d * s).astype(BF16)
        ct = d + lax.dot_general(d_pre, consts[0][...], _DIMS["nt"], preferred_element_type=F32)
        xv, yv = rows[3][...], rows[4][...]
        _, vjp = jax.vjp(lambda yy, gg, bb: _ln_tile(xv, yy, gg, bb, alpha), yv, consts[1][...], consts[2][...])
        du, dg, db = vjp(ct)
        orows[2][...] = du
        oaccs[0][...] += dg
        oaccs[1][...] += db

    d_pre, d_pp, du, dg, db = _rowcall(body, [dxo, gp, pp, x, y], [w_gate, _row(g), _row(b)],
                                       [(D, BF16), (D, BF16), (D, F32)], [(1, D), (1, D)], tr=tr, name=name)
    return d_pre, d_pp, du, dg[0], db[0]


def _loss_fwd_bwd(xf, target, *, tr, name):
    D = xf.shape[1]

    def body(rows, consts, orows, oaccs, scr, step, blk):
        err = rows[0][...] - rows[1][...]
        orows[0][...] = err * (1.0 / D)
        part = jnp.sum(err * err, axis=0, keepdims=True) * (0.5 / D)
        oaccs[0][...] += part

    dx, lrow = _rowcall(body, [xf, target], [], [(D, F32)], [(1, D)], tr=tr, name=name)
    return lrow, dx


def _gdn_qk_tile(c):
    y = _silu(c)
    return y * lax.rsqrt(jnp.sum(y * y, -1, keepdims=True) + RMS_EPS)


def _make_bg_fn(H):
    def fn(hs, alog, dtb):
        lane = _iota((1, LANES), 1)
        beta = jax.nn.sigmoid(hs)
        g = -jnp.exp(alog) * _softplus(hs + dtb)
        return jnp.where(lane < H, beta, jnp.where(lane < 2 * H, g, 0.0))
    return fn


def _halo_spec(ncol):
    def make(tr, n):
        per = tr // SUBLANES
        return pl.BlockSpec((SUBLANES, ncol), lambda i: (jnp.maximum(i * per - 1, 0), 0))
    return make


def _halo_spec_rev(ncol):
    def make(tr, n):
        per = tr // SUBLANES
        return pl.BlockSpec((SUBLANES, ncol), lambda i: (jnp.maximum((n - 1 - i) * per - 1, 0), 0))
    return make


def _gdn_pre_fwd(h_main, h_small, conv_w8, alog_row, dtb_row, H, *, tr, name):
    W = H * GDN_DK
    C3 = 3 * W
    bg_fn = _make_bg_fn(H)

    def body(rows, consts, orows, oaccs, scr, step, blk):
        main_ref, halo_ref, hs_ref = rows
        w_ref, alog_ref, dtb_ref = consts
        q_ref, k_ref, v_ref, bg_ref = orows
        xs = scr[0]
        trr = main_ref.shape[0]
        xs[pl.ds(SUBLANES, trr), :] = main_ref[...]
        xs[pl.ds(0, SUBLANES), :] = jnp.where(blk > 0, halo_ref[...], 0.0)
        for s in range(C3 // LANES):
            ls = slice(s * LANES, (s + 1) * LANES)
            c = jnp.zeros((trr, LANES), F32)
            for j in range(GDN_CONV):
                c = c + w_ref[GDN_CONV - 1 - j:GDN_CONV - j, ls] * xs[pl.ds(SUBLANES - j, trr), ls]
            if s < 2 * H:
                out = _gdn_qk_tile(c)
                (q_ref if s < H else k_ref)[:, (s % H) * LANES:(s % H + 1) * LANES] = out
            else:
                v_ref[:, (s - 2 * H) * LANES:(s - 2 * H + 1) * LANES] = _silu(c)
        bg_ref[...] = bg_fn(hs_ref[...], alog_ref[...], dtb_ref[...])

    main = (h_main, lambda tr_, n: pl.BlockSpec((tr_, C3), lambda i: (i, 0)))
    halo = (h_main, _halo_spec(C3))
    trr = min(tr, h_main.shape[0])
    return _rowcall(body, [main, halo, h_small], [conv_w8, alog_row, dtb_row],
                    [(W, F32), (W, F32), (W, F32), (LANES, F32)], [], tr=tr, name=name,
                    scratch=[pltpu.VMEM((trr + SUBLANES, C3), F32)])


def _gdn_pre_bwd(h_main, h_small, conv_w8, alog_row, dtb_row, dq, dk, dv, dbg, dz, H, *, tr, name):
    W = H * GDN_DK
    C3 = 3 * W
    bg_fn = _make_bg_fn(H)

    def body(rows, consts, orows, oaccs, scr, step, blk):
        main_ref, halo_ref, hs_ref, dq_ref, dk_ref, dv_ref, dbg_ref, dz_ref = rows
        w_ref, alog_ref, dtb_ref = consts
        dmain_ref, dhs_ref = orows
        dw_ref, dalog_ref, ddtb_ref = oaccs
        xs, dcs = scr
        trr = main_ref.shape[0]
        xs[pl.ds(SUBLANES, trr), :] = main_ref[...]
        xs[pl.ds(0, SUBLANES), :] = jnp.where(blk > 0, halo_ref[...], 0.0)

        @pl.when(step == 0)
        def _():
            dcs[pl.ds(trr, SUBLANES), :] = jnp.zeros((SUBLANES, C3), F32)

        for s in range(C3 // LANES):
            ls = slice(s * LANES, (s + 1) * LANES)
            c = jnp.zeros((trr, LANES), F32)
            for j in range(GDN_CONV):
                c = c + w_ref[GDN_CONV - 1 - j:GDN_CONV - j, ls] * xs[pl.ds(SUBLANES - j, trr), ls]
            if s < 2 * H:
                src = dq_ref if s < H else dk_ref
                ct = src[:, (s % H) * LANES:(s % H + 1) * LANES]
                _, vjp = jax.vjp(_gdn_qk_tile, c)
            else:
                ct = dv_ref[:, (s - 2 * H) * LANES:(s - 2 * H + 1) * LANES]
                _, vjp = jax.vjp(_silu, c)
            dcs[pl.ds(0, trr), ls] = vjp(ct)[0]
        for s in range(C3 // LANES):
            ls = slice(s * LANES, (s + 1) * LANES)
            dx = jnp.zeros((trr, LANES), F32)
            dc0 = dcs[pl.ds(0, trr), ls]
            for j in range(GDN_CONV):
                wrow = w_ref[GDN_CONV - 1 - j:GDN_CONV - j, ls]
                dx = dx + wrow * dcs[pl.ds(j, trr), ls]
                dw_ref[GDN_CONV - 1 - j:GDN_CONV - j, ls] += jnp.sum(dc0 * xs[pl.ds(SUBLANES - j, trr), ls], axis=0, keepdims=True)
            dmain_ref[:, ls] = dx.astype(BF16)
        dmain_ref[:, C3:] = dz_ref[...]
        dcs[pl.ds(trr, SUBLANES), :] = dcs[pl.ds(0, SUBLANES), :]
        _, vjp = jax.vjp(bg_fn, hs_ref[...], alog_ref[...], dtb_ref[...])
        dhs, dalog, ddtb = vjp(dbg_ref[...])
        dhs_ref[...] = dhs.astype(BF16)
        dalog_ref[...] += dalog
        ddtb_ref[...] += ddtb

    trr = min(tr, h_main.shape[0])
    main = (h_main, lambda tr_, n: pl.BlockSpec((tr_, C3), lambda i: (n - 1 - i, 0)))
    halo = (h_main, _halo_spec_rev(C3))
    return _rowcall(body, [main, halo, h_small, dq, dk, dv, dbg, dz], [conv_w8, alog_row, dtb_row],
                    [(4 * W, BF16), (LANES, BF16)], [(SUBLANES, C3), (1, LANES), (1, LANES)],
                    tr=tr, name=name, reverse=True,
                    scratch=[pltpu.VMEM((trr + SUBLANES, C3), F32), pltpu.VMEM((trr + SUBLANES, C3), F32)])


def _gdn_chunk(qs, ks, vs, betas, gs, Ss, Ts=None, with_inverse=False):
    C, dk = qs[0].shape
    dv = vs[0].shape[1]
    ri, ci = _iota((C, C), 0), _iota((C, C), 1)
    causal, strict = ri >= ci, ri > ci
    tril = causal.astype(F32)
    lane0 = (_iota((1, LANES), 1) == 0).astype(F32)
    e0 = jnp.ones((C, 1), F32) * lane0
    last = (_iota((C, 1), 0) == C - 1).astype(F32)

    def each(f, *lists):
        return [f(*a) for a in zip(*lists)]

    G = each(lambda g: g * jnp.ones((1, LANES), F32), gs)
    gcB = each(lambda x: _mm(tril, x, "nn", "ca"), G)
    gc = each(lambda x: jnp.sum(x * lane0, -1, keepdims=True), gcB)
    gc_row = each(lambda x: _mm(e0, x, "nt", "ca"), gcB)
    decay = each(lambda a, b: jnp.where(causal, jnp.exp(jnp.where(causal, a - b, 0.0)), 0.0), gc, gc_row)
    kb = each(lambda k, b: k * b, ks, betas)
    kk = each(lambda a, k: _mm(a, k, "nt", "b"), kb, ks)
    L = each(lambda a, d: jnp.where(strict, a * d, 0.0), kk, decay)
    X = _tri_inv(L) if Ts is None else _tri_inv_known(L, Ts)
    egc = each(jnp.exp, gc)
    u = each(lambda x, v, b: _mm(x, v * b, "nn", "x3"), X, vs, betas)
    w = each(lambda x, a, e: _mm(x, a * e, "nn", "x3"), X, kb, egc)
    qsc = each(lambda q: q * (dk ** -0.5), qs)
    qk = each(lambda q, k: _mm(q, k, "nt", "b"), qsc, ks)
    A = each(lambda a, d: jnp.where(causal, a * d, 0.0), qk, decay)
    q_dec = each(lambda q, e: q * e, qsc, egc)
    gl = each(lambda x: jnp.sum(x * last, keepdims=True), gc)
    k_dec = each(lambda k, a, b: k * jnp.exp(a - b), ks, gl, gc)
    wS = each(lambda a, s: _mm(a, s, "nn", "b"), w, Ss)
    qS = each(lambda a, s: _mm(a, s, "nn", "b"), q_dec, Ss)
    v_new = each(lambda a, b: a - b, u, wS)
    Av = each(lambda a, b: _mm(a, b, "nn", "b"), A, v_new)
    kv = each(lambda a, b: _mm(a, b, "tn", "b"), k_dec, v_new)
    o = each(lambda a, b: a + b, qS, Av)
    S_new = each(lambda s, e, x: s * jnp.exp(e) + x, Ss, gl, kv)
    return (o, S_new, X) if with_inverse else (o, S_new)


def _gdn_rule_fwd(q, k, v, bg, H, *, name, carry=None):
    S_len = q.shape[0]
    C = min(GDN_CHUNK, S_len)
    N = S_len // C
    dk = dv = GDN_DK

    def body(*refs):
        if carry is None:
            q_ref, k_ref, v_ref, bg_ref, o_ref, st_ref, ti_ref, s_scr = refs
        else:
            q_ref, k_ref, v_ref, bg_ref, src_ref, o_ref, st_ref, ti_ref, got_ref, s_scr = refs[:10]
            xy = (src_ref, got_ref) + tuple(refs[10:]) + (carry[1],)
        n = pl.program_id(0)

        @pl.when(n == 0)
        def _():
            s_scr[...] = jnp.zeros(s_scr.shape, F32)
            if carry is not None:
                _xy_start(*xy)

        bgt = bg_ref[...]
        sl = [slice(h * dk, (h + 1) * dk) for h in range(H)]
        Ss = [s_scr[h] for h in range(H)]
        for h in range(H):
            st_ref[h] = Ss[h]
        os_, S_new, Ts = _gdn_chunk([q_ref[:, s] for s in sl], [k_ref[:, s] for s in sl], [v_ref[:, s] for s in sl],
                                    [bgt[:, h:h + 1] for h in range(H)], [bgt[:, H + h:H + h + 1] for h in range(H)],
                                    Ss, with_inverse=True)
        for h in range(H):
            o_ref[:, sl[h]] = os_[h]
            s_scr[h] = S_new[h]
            ti_ref[h] = Ts[h]

        if carry is not None:
            @pl.when(n == N - 1)
            def _():
                _xy_wait(*xy)

    rows = pl.BlockSpec((C, H * dk), lambda n: (n, 0))
    extra = carry is not None
    return pl.pallas_call(
        body, name=name, grid=(N,),
        in_specs=[rows, rows, rows, pl.BlockSpec((C, LANES), lambda n: (n, 0))] + ([_ANY] if extra else []),
        out_specs=[rows, pl.BlockSpec((H, dk, dv), lambda n: (n, 0, 0)), pl.BlockSpec((H, C, C), lambda n: (n, 0, 0))]
        + ([_ANY] if extra else []),
        out_shape=[jax.ShapeDtypeStruct((S_len, H * dv), F32), jax.ShapeDtypeStruct((N * H, dk, dv), F32),
                   jax.ShapeDtypeStruct((N * H, C, C), F32)] + ([_xy_out_shape(carry[0])] if extra else []),
        scratch_shapes=[pltpu.VMEM((H, dk, dv), F32)] + (_xy_sems() if extra else []),
        compiler_params=_params("arbitrary"),
    )(q, k, v, bg, *((carry[0],) if extra else ()))


def _gdn_rule_bwd(q, k, v, bg, states, tinv, do, H, *, name, carry=None):
    S_len = q.shape[0]
    C = min(GDN_CHUNK, S_len)
    N = S_len // C
    dk = dv = GDN_DK

    def body(*refs):
        if carry is None:
            q_ref, k_ref, v_ref, bg_ref, st_ref, ti_ref, do_ref, dq_ref, dk_ref, dv_ref, dbg_ref, ds_scr = refs
        else:
            (q_ref, k_ref, v_ref, bg_ref, st_ref, ti_ref, do_ref, src_ref,
             dq_ref, dk_ref, dv_ref, dbg_ref, got_ref, ds_scr) = refs[:14]
            xy = (src_ref, got_ref) + tuple(refs[14:]) + (carry[1],)
        step = pl.program_id(0)

        @pl.when(step == 0)
        def _():
            ds_scr[...] = jnp.zeros(ds_scr.shape, F32)
            if carry is not None:
                _xy_start(*xy)

        bgt = bg_ref[...]
        lane = _iota((1, LANES), 1)
        dbg = jnp.zeros((C, LANES), F32)
        sl = [slice(h * dk, (h + 1) * dk) for h in range(H)]
        Ts = [ti_ref[h] for h in range(H)]
        _, vjp = jax.vjp(lambda *a: _gdn_chunk(*a, Ts=Ts),
                         [q_ref[:, s] for s in sl], [k_ref[:, s] for s in sl], [v_ref[:, s] for s in sl],
                         [bgt[:, h:h + 1] for h in range(H)], [bgt[:, H + h:H + h + 1] for h in range(H)],
                         [st_ref[h] for h in range(H)])
        dq, dkk, dvv, dbeta, dg, dS = vjp(([do_ref[:, s] for s in sl], [ds_scr[h] for h in range(H)]))
        for h in range(H):
            dq_ref[:, sl[h]] = dq[h]
            dk_ref[:, sl[h]] = dkk[h]
            dv_ref[:, sl[h]] = dvv[h]
            dbg = dbg + jnp.where(lane == h, dbeta[h], 0.0) + jnp.where(lane == h + H, dg[h], 0.0)
            ds_scr[h] = dS[h]
        dbg_ref[...] = dbg

        if carry is not None:
            @pl.when(step == N - 1)
            def _():
                _xy_wait(*xy)

    rows = pl.BlockSpec((C, H * dk), lambda s: (N - 1 - s, 0))
    bgs = pl.BlockSpec((C, LANES), lambda s: (N - 1 - s, 0))
    extra = carry is not None
    return pl.pallas_call(
        body, name=name, grid=(N,),
        in_specs=[rows, rows, rows, bgs, pl.BlockSpec((H, dk, dv), lambda s: (N - 1 - s, 0, 0)),
                  pl.BlockSpec((H, C, C), lambda s: (N - 1 - s, 0, 0)), rows] + ([_ANY] if extra else []),
        out_specs=[rows, rows, rows, bgs] + ([_ANY] if extra else []),
        out_shape=[jax.ShapeDtypeStruct((S_len, H * dk), F32)] * 3 + [jax.ShapeDtypeStruct((S_len, LANES), F32)]
        + ([_xy_out_shape(carry[0])] if extra else []),
        scratch_shapes=[pltpu.VMEM((H, dk, dv), F32)] + (_xy_sems() if extra else []),
        compiler_params=_params("arbitrary"),
    )(q, k, v, bg, states, tinv, do, *((carry[0],) if extra else ()))


def _gdn_post_tile(o, z, g):
    return o * lax.rsqrt(jnp.mean(o * o, -1, keepdims=True) + RMS_EPS) * g * _silu(z)


def _gdn_post_fwd(o, h_main, norm_g, H, *, tr, name):
    W = H * GDN_DK

    def body(rows, consts, orows, oaccs, scr, step, blk):
        for h in range(H):
            ls = slice(h * LANES, (h + 1) * LANES)
            orows[0][:, ls] = _gdn_post_tile(rows[0][:, ls], rows[1][:, ls], consts[0][...]).astype(BF16)

    z = (h_main, lambda tr_, n: pl.BlockSpec((tr_, W), lambda i: (i, 3)))
    return _rowcall(body, [o, z], [_row(norm_g)], [(W, BF16)], [], tr=tr, name=name)[0]


def _gdn_post_bwd(o, h_main, norm_g, d_on, H, *, tr, name):
    W = H * GDN_DK

    def body(rows, consts, orows, oaccs, scr, step, blk):
        for h in range(H):
            ls = slice(h * LANES, (h + 1) * LANES)
            _, vjp = jax.vjp(_gdn_post_tile, rows[0][:, ls], rows[1][:, ls], consts[0][...])
            d_o, d_z, d_g = vjp(rows[2][:, ls])
            orows[0][:, ls] = d_o
            orows[1][:, ls] = d_z.astype(BF16)
            oaccs[0][...] += d_g

    z = (h_main, lambda tr_, n: pl.BlockSpec((tr_, W), lambda i: (i, 3)))
    return _rowcall(body, [o, z, d_on], [_row(norm_g)], [(W, F32), (W, BF16)], [(1, LANES)], tr=tr, name=name)


def _seg_ones():
    ri, ci = _iota((LANES, LANES), 0), _iota((LANES, LANES), 1)
    return ((ri < FOX_DH) == (ci < FOX_DH)).astype(F32)


def _fox_qk_tile(x, g2):
    ms = _mm(x * x, _seg_ones(), "nn", "cb") * (1.0 / FOX_DH)
    return x * lax.rsqrt(ms + RMS_EPS) * g2


def _make_lf_fn(Hf):
    def fn(hs, bf):
        lane = _iota((1, LANES), 1)
        return jnp.where(lane < Hf, -_softplus(-(hs + bf)), 0.0)
    return fn


def _fox_pre_fwd(h_main, h_small, gq2, gk2, bf_row, Hf, *, tr, name):
    W = Hf * FOX_DH
    lf_fn = _make_lf_fn(Hf)

    def body(rows, consts, orows, oaccs, scr, step, blk):
        qk_ref, v_ref, hs_ref = rows
        gq_ref, gk_ref, bf_ref = consts
        qn_ref, kn_ref, vb_ref, c_ref, cb_ref = orows
        carry = scr[0]
        trr = qk_ref.shape[0]

        @pl.when(step == 0)
        def _():
            carry[...] = jnp.zeros(carry.shape, F32)

        for s in range(W // LANES):
            ls = slice(s * LANES, (s + 1) * LANES)
            qn_ref[:, ls] = _fox_qk_tile(qk_ref[:, ls], gq_ref[...]).astype(BF16)
            kn_ref[:, ls] = _fox_qk_tile(qk_ref[:, W + s * LANES:W + (s + 1) * LANES], gk_ref[...]).astype(BF16)
        vb_ref[...] = v_ref[...].astype(BF16)
        lf = lf_fn(hs_ref[...], bf_ref[...])
        tril = (_iota((trr, trr), 0) >= _iota((trr, trr), 1)).astype(F32)
        c = _raw_mm(tril, lf, "nn", "ca") + carry[0:1, :]
        c_ref[...] = c
        carry[0:1, :] = c[trr - 1:trr, :]
        col = _iota((LANES, 2 * W), 1)
        parity = (col >= W).astype(jnp.int32)
        slab = jnp.right_shift(col - parity * W, 7)
        expand = (2 * slab + parity == _iota((LANES, 2 * W), 0)).astype(F32)
        cb_ref[...] = _raw_mm(c, expand, "nn", "cb")

    qk = (h_main, lambda tr_, n: pl.BlockSpec((tr_, 2 * W), lambda i: (i, 0)))
    vv = (h_main, lambda tr_, n: pl.BlockSpec((tr_, W), lambda i: (i, 2)))
    return _rowcall(body, [qk, vv, h_small], [gq2, gk2, bf_row],
                    [(W, BF16), (W, BF16), (W, BF16), (LANES, F32), (2 * W, F32)], [], tr=tr, name=name,
                    scratch=[pltpu.VMEM((SUBLANES, LANES), F32)])


def _fox_pre_bwd(h_main, h_small, gq2, gk2, bf_row, dqn, dkn, dvv, dz, dc, Hf, *, tr, name):
    W = Hf * FOX_DH
    lf_fn = _make_lf_fn(Hf)

    def body(rows, consts, orows, oaccs, scr, step, blk):
        qk_ref, hs_ref, dqn_ref, dkn_ref, dvv_ref, dz_ref, dc_ref = rows
        gq_ref, gk_ref, bf_ref = consts
        dmain_ref, dhs_ref = orows
        dgq_ref, dgk_ref, dbf_ref = oaccs
        carry = scr[0]
        trr = qk_ref.shape[0]

        @pl.when(step == 0)
        def _():
            carry[...] = jnp.zeros(carry.shape, F32)

        for s in range(W // LANES):
            ls = slice(s * LANES, (s + 1) * LANES)
            lk = slice(W + s * LANES, W + (s + 1) * LANES)
            _, vjp = jax.vjp(_fox_qk_tile, qk_ref[:, ls], gq_ref[...])
            dx, dg = vjp(dqn_ref[:, ls])
            dmain_ref[:, ls] = dx.astype(BF16)
            dgq_ref[...] += dg
            _, vjp = jax.vjp(_fox_qk_tile, qk_ref[:, lk], gk_ref[...])
            dx, dg = vjp(dkn_ref[:, ls])
            dmain_ref[:, lk] = dx.astype(BF16)
            dgk_ref[...] += dg
        dmain_ref[:, 2 * W:3 * W] = dvv_ref[...].astype(BF16)
        dmain_ref[:, 3 * W:] = dz_ref[...]
        dcv = dc_ref[...]
        triu = (_iota((trr, trr), 0) <= _iota((trr, trr), 1)).astype(F32)
        dlf = _raw_mm(triu, dcv, "nn", "ca") + carry[0:1, :]
        carry[0:1, :] = dlf[0:1, :]
        _, vjp = jax.vjp(lf_fn, hs_ref[...], bf_ref[...])
        dhs, dbf = vjp(dlf)
        dhs_ref[...] = dhs.astype(BF16)
        dbf_ref[...] += dbf

    qk = (h_main, lambda tr_, n: pl.BlockSpec((tr_, 2 * W), lambda i: (n - 1 - i, 0)))
    return _rowcall(body, [qk, h_small, dqn, dkn, dvv, dz, dc], [gq2, gk2, bf_row],
                    [(4 * W, BF16), (LANES, BF16)], [(1, LANES), (1, LANES), (1, LANES)],
                    tr=tr, name=name, reverse=True, scratch=[pltpu.VMEM((SUBLANES, LANES), F32)])


def _fox_attn_fwd(qn, kn, vb, c_b, c_rowp, *, tb, name):
    S_len, W = qn.shape
    HP = W // LANES
    tb = min(tb, S_len)
    nb = S_len // tb
    scale = FOX_DH ** -0.5
    rb, cb = min(ATTN_ROWS, tb), min(ATTN_COLS, tb)
    nblk = 2 * (tb // rb)

    steps = [(i, j) for i in range(nb) for j in range(i + 1)]
    ti = jnp.asarray([s[0] for s in steps], jnp.int32)
    tj = jnp.asarray([s[1] for s in steps], jnp.int32)

    def body(ti_ref, tj_ref, q_ref, k_ref, v_ref, cb0_ref, cb1_ref, cr_ref, o_ref, lse_ref,
             m_scr, l_scr, acc_scr, s_scr, p_scr, a_scr):
        t = pl.program_id(1)
        i, j = ti_ref[t], tj_ref[t]

        @pl.when(j == 0)
        def _():
            m_scr[...] = jnp.full(m_scr.shape, -jnp.inf, F32)
            l_scr[...] = jnp.zeros(l_scr.shape, F32)
            acc_scr[...] = jnp.zeros(acc_scr.shape, F32)

        def compute(diag):
            lo = _iota((1, LANES), 1) < FOX_DH
            v = v_ref[...]
            lane = _iota((1, LANES), 1)
            blocks = [(hh, r) for hh in range(2) for r in range(tb // rb)]
            masks = [lo, jnp.logical_not(lo)]

            def visible(r):
                return ((r + 1) * rb - 1) // LANES + 1 if diag else tb // LANES

            for b, (hh, r) in enumerate(blocks):
                rows = slice(r * rb, (r + 1) * rb)
                qr = q_ref[rows, :]
                qh = jnp.where(masks[hh], qr * scale, jnp.zeros_like(qr))
                ctb = (cb0_ref if hh == 0 else cb1_ref)[rows, :]
                mx = None
                for c in range(tb // cb):
                    if c * cb // LANES >= visible(r):
                        continue
                    s2 = lax.dot_general(qh, k_ref[c * cb:(c + 1) * cb, :], _DIMS["nt"], preferred_element_type=F32)
                    for piece in range(c * cb // LANES, min((c + 1) * cb // LANES, visible(r))):
                        cols = slice(piece * LANES, (piece + 1) * LANES)
                        s = s2[:, piece * LANES - c * cb:(piece + 1) * LANES - c * cb] + ctb - cr_ref[hh:hh + 1, cols]
                        if diag and (piece + 1) * LANES - 1 > r * rb:
                            keep = piece * LANES + _iota((rb, LANES), 1) <= r * rb + _iota((rb, LANES), 0)
                            s = jnp.where(keep, s, -jnp.inf)
                        s_scr[b, :, cols] = s
                        mx = s if mx is None else jnp.maximum(mx, s)
                m_prev = m_scr[hh, rows, :]
                m_new = jnp.maximum(m_prev, jnp.broadcast_to(jnp.max(mx, -1, keepdims=True), (rb, LANES)))
                a_scr[b] = jnp.exp(m_prev - m_new)
                m_scr[hh, rows, :] = m_new
            for b, (hh, r) in enumerate(blocks):
                m_new = m_scr[hh, r * rb:(r + 1) * rb, :]
                for piece in range(visible(r)):
                    cols = slice(piece * LANES, (piece + 1) * LANES)
                    p_scr[b, :, cols] = jnp.exp(s_scr[b, :, cols] - m_new).astype(BF16)
            for b, (hh, r) in enumerate(blocks):
                rows = slice(r * rb, (r + 1) * rb)
                nkv = visible(r) * LANES
                pb = p_scr[b, :, :nkv]
                spare = (1 - hh) * FOX_DH
                vh = jnp.where(masks[hh], v[:nkv], (lane == spare).astype(BF16))
                pv = lax.dot_general(pb, vh, _DIMS["nn"], preferred_element_type=F32)
                psum = jnp.broadcast_to(pv[:, spare:spare + 1], (rb, LANES))
                alpha = a_scr[b]
                l_scr[hh, rows, :] = alpha * l_scr[hh, rows, :] + psum
                acc = acc_scr[rows, :]
                acc_scr[rows, :] = jnp.where(masks[hh], acc * alpha + pv, acc)

        @pl.when(j < i)
        def _():
            compute(False)

        @pl.when(j == i)
        def _():
            compute(True)
            lo = _iota((1, LANES), 1) < FOX_DH
            o_ref[...] = acc_scr[...] / jnp.where(lo, l_scr[0], l_scr[1])
            lse_ref[...] = m_scr[...] + jnp.log(l_scr[...])

    qs = pl.BlockSpec((tb, LANES), lambda h, t, ti_, tj_: (ti_[t], h))
    qs1 = pl.BlockSpec((tb, LANES), lambda h, t, ti_, tj_: (ti_[t], HP + h))
    ks = pl.BlockSpec((tb, LANES), lambda h, t, ti_, tj_: (tj_[t], h))
    crs = pl.BlockSpec((None, SUBLANES, tb), lambda h, t, ti_, tj_: (h, 0, tj_[t]))
    return pl.pallas_call(
        body, name=name,
        grid_spec=pltpu.PrefetchScalarGridSpec(
            num_scalar_prefetch=2, grid=(HP, len(steps)),
            in_specs=[qs, ks, ks, qs, qs1, crs],
            out_specs=[qs, pl.BlockSpec((2, tb, LANES), lambda h, t, ti_, tj_: (0, ti_[t], h))],
            scratch_shapes=[pltpu.VMEM((2, tb, LANES), F32), pltpu.VMEM((2, tb, LANES), F32),
                            pltpu.VMEM((tb, LANES), F32), pltpu.VMEM((nblk, rb, tb), F32),
                            pltpu.VMEM((nblk, rb, tb), BF16), pltpu.VMEM((nblk, rb, LANES), F32)]),
        out_shape=[jax.ShapeDtypeStruct((S_len, W), F32), jax.ShapeDtypeStruct((2, S_len, W), F32)],
        compiler_params=_params("parallel", "arbitrary"),
    )(ti, tj, qn, kn, vb, c_b, c_b, c_rowp)


def _fox_attn_bwd(qn, kn, vb, c_b, c_rowp, lse_b, delta_b, do, *, tb, name):
    S_len, W = qn.shape
    HP = W // LANES
    tb = min(tb, S_len)
    nb = S_len // tb
    scale = FOX_DH ** -0.5
    rb, cb = min(ATTN_ROWS, tb), min(ATTN_COLS, tb)

    steps = [(j, i) for j in range(nb) for i in range(j, nb)]
    tj = jnp.asarray([s[0] for s in steps], jnp.int32)
    ti = jnp.asarray([s[1] for s in steps], jnp.int32)

    def body(tj_ref, ti_ref, q_ref, k_ref, v_ref, cb0_ref, cb1_ref, cr_ref, lse_ref, dl0_ref, dl1_ref, do_ref,
             dq_ref, dk_ref, dv_ref, dcr_ref, dct_ref, dk_scr, dv_scr, dc_scr, p_scr, ds_scr):
        t = pl.program_id(1)
        j, i = tj_ref[t], ti_ref[t]

        @pl.when(t == 0)
        def _():
            dq_ref[...] = jnp.zeros(dq_ref.shape, F32)
            dct_ref[...] = jnp.zeros(dct_ref.shape, F32)

        @pl.when(i == j)
        def _():
            dk_scr[...] = jnp.zeros(dk_scr.shape, F32)
            dv_scr[...] = jnp.zeros(dv_scr.shape, F32)
            dc_scr[...] = jnp.zeros(dc_scr.shape, F32)

        def compute(diag):
            lo = _iota((1, LANES), 1) < FOX_DH
            masks = [lo, jnp.logical_not(lo)]
            row0 = pl.multiple_of(i * tb, tb)
            npiece = tb // LANES
            colsum = [[None] * npiece for _ in range(2)]

            def visible(r):
                return ((r + 1) * rb - 1) // LANES + 1 if diag else npiece

            for hh in range(2):
                for r in range(tb // rb):
                    rows = slice(r * rb, (r + 1) * rb)
                    qr = q_ref[rows, :]
                    qh = jnp.where(masks[hh], qr * scale, jnp.zeros_like(qr))
                    doh = jnp.where(masks[hh], do_ref[rows, :], 0.0).astype(BF16)
                    bq = (cb0_ref if hh == 0 else cb1_ref)[rows, :] - lse_ref[hh, rows, :]
                    dlt = (dl0_ref if hh == 0 else dl1_ref)[rows, :]
                    rsum = None
                    for c in range(tb // cb):
                        first, last = c * cb // LANES, min((c + 1) * cb // LANES, visible(r))
                        for piece in range(max(first, last), (c + 1) * cb // LANES):
                            cols = slice(piece * LANES, (piece + 1) * LANES)
                            p_scr[hh, rows, cols] = jnp.zeros((rb, LANES), BF16)
                            ds_scr[hh, rows, cols] = jnp.zeros((rb, LANES), BF16)
                        if first >= last:
                            continue
                        s2 = lax.dot_general(qh, k_ref[c * cb:(c + 1) * cb, :], _DIMS["nt"], preferred_element_type=F32)
                        dp2 = lax.dot_general(doh, v_ref[c * cb:(c + 1) * cb, :], _DIMS["nt"], preferred_element_type=F32)
                        for piece in range(first, last):
                            cols = slice(piece * LANES, (piece + 1) * LANES)
                            sub = slice(piece * LANES - c * cb, (piece + 1) * LANES - c * cb)
                            s = s2[:, sub] + bq - cr_ref[hh:hh + 1, cols]
                            if diag and (piece + 1) * LANES - 1 > r * rb:
                                keep = piece * LANES + _iota((rb, LANES), 1) <= r * rb + _iota((rb, LANES), 0)
                                s = jnp.where(keep, s, -jnp.inf)
                            p = jnp.exp(s)
                            ds = p * (dp2[:, sub] - dlt)
                            p_scr[hh, rows, cols] = p.astype(BF16)
                            ds_scr[hh, rows, cols] = ds.astype(BF16)
                            rsum = ds if rsum is None else rsum + ds
                            csum = jnp.sum(ds, axis=0, keepdims=True)
                            colsum[hh][piece] = csum if colsum[hh][piece] is None else colsum[hh][piece] + csum
                    grow = pl.ds(row0 + r * rb, rb)
                    dct_ref[grow, :] += jnp.where(_iota((1, SUBLANES), 1) == hh, jnp.sum(rsum, -1, keepdims=True), 0.0)
            k = k_ref[...]
            qf = q_ref[...]
            dof = do_ref[...]
            dq_part = jnp.zeros((tb, LANES), F32)
            for hh in range(2):
                kh = jnp.where(masks[hh], k, jnp.zeros_like(k))
                qhf = jnp.where(masks[hh], qf * scale, jnp.zeros_like(qf))
                dohf = jnp.where(masks[hh], dof, 0.0).astype(BF16)
                dv_scr[...] += lax.dot_general(p_scr[hh], dohf, _DIMS["tn"], preferred_element_type=F32)
                dk_scr[...] += lax.dot_general(ds_scr[hh], qhf, _DIMS["tn"], preferred_element_type=F32)
                dq_part = dq_part + lax.dot_general(ds_scr[hh], kh, _DIMS["nn"], preferred_element_type=F32)
                for piece in range(npiece):
                    if colsum[hh][piece] is not None:
                        dc_scr[hh:hh + 1, piece * LANES:(piece + 1) * LANES] -= colsum[hh][piece]
            dq_ref[pl.ds(row0, tb), :] += dq_part * scale

        @pl.when(i > j)
        def _():
            compute(False)

        @pl.when(i == j)
        def _():
            compute(True)

        @pl.when(i == nb - 1)
        def _():
            dk_ref[...] = dk_scr[...]
            dv_ref[...] = dv_scr[...]
            dcr_ref[...] = dc_scr[...]

    qs = pl.BlockSpec((tb, LANES), lambda h, t, tj_, ti_: (ti_[t], h))
    qs1 = pl.BlockSpec((tb, LANES), lambda h, t, tj_, ti_: (ti_[t], HP + h))
    ks = pl.BlockSpec((tb, LANES), lambda h, t, tj_, ti_: (tj_[t], h))
    crs = pl.BlockSpec((None, SUBLANES, tb), lambda h, t, tj_, ti_: (h, 0, tj_[t]))
    whole = pl.BlockSpec((S_len, LANES), lambda h, t, tj_, ti_: (0, h))
    return pl.pallas_call(
        body, name=name,
        grid_spec=pltpu.PrefetchScalarGridSpec(
            num_scalar_prefetch=2, grid=(HP, len(steps)),
            in_specs=[qs, ks, ks, qs, qs1, crs, pl.BlockSpec((2, tb, LANES), lambda h, t, tj_, ti_: (0, ti_[t], h)),
                      qs, qs1, qs],
            out_specs=[whole, ks, ks, crs, pl.BlockSpec((None, S_len, SUBLANES), lambda h, t, tj_, ti_: (h, 0, 0))],
            scratch_shapes=[pltpu.VMEM((tb, LANES), F32), pltpu.VMEM((tb, LANES), F32),
                            pltpu.VMEM((SUBLANES, tb), F32), pltpu.VMEM((2, tb, tb), BF16),
                            pltpu.VMEM((2, tb, tb), BF16)]),
        out_shape=[jax.ShapeDtypeStruct((S_len, W), F32)] * 3 + [jax.ShapeDtypeStruct((HP, SUBLANES, S_len), F32),
                                                                 jax.ShapeDtypeStruct((HP, S_len, SUBLANES), F32)],
        compiler_params=_params("parallel", "arbitrary"),
    )(tj, ti, qn, kn, vb, c_b, c_b, c_rowp, lse_b, delta_b, delta_b, do)


def _fox_post_tile(o, z):
    return o * _silu(z)


def _fox_post_fwd(o, h_main, *, tr, name):
    W = o.shape[1]

    def body(rows, consts, orows, oaccs, scr, step, blk):
        orows[0][...] = _fox_post_tile(rows[0][...], rows[1][...]).astype(BF16)

    z = (h_main, lambda tr_, n: pl.BlockSpec((tr_, W), lambda i: (i, 3)))
    return _rowcall(body, [o, z], [], [(W, BF16)], [], tr=tr, name=name)[0]


def _fox_post_bwd(o, h_main, d_og, *, tr, name):
    W = o.shape[1]

    def body(rows, consts, orows, oaccs, scr, step, blk):
        _, vjp = jax.vjp(_fox_post_tile, rows[0][...], rows[1][...])
        d_o, d_z = vjp(rows[2][...])
        orows[0][...] = d_o
        orows[1][...] = d_z.astype(BF16)
        lo_rows = (_iota((LANES, LANES), 0) < FOX_DH)
        for s in range(W // LANES):
            ls = slice(s * LANES, (s + 1) * LANES)
            prod = d_o[:, ls] * rows[0][:, ls]
            orows[2][:, ls] = _raw_mm(prod, lo_rows.astype(F32), "nn", "cb")
            orows[2][:, W + s * LANES:W + (s + 1) * LANES] = _raw_mm(prod, jnp.logical_not(lo_rows).astype(F32), "nn", "cb")

    z = (h_main, lambda tr_, n: pl.BlockSpec((tr_, W), lambda i: (i, 3)))
    return _rowcall(body, [o, z, d_og], [], [(W, F32), (W, BF16), (2 * W, F32)], [], tr=tr, name=name)


MESH_IDS = pl.DeviceIdType.MESH
N_CHIPS = 4
N_DEV = 8
_ANY = pl.BlockSpec(memory_space=pl.ANY)


def _xy_exchange(src, *, gather, name):
    def body(src_ref, out_ref, send_sems, recv_sems, local_sem):
        _xy_start(src_ref, out_ref, send_sems, recv_sems, local_sem, gather)
        _xy_wait(src_ref, out_ref, send_sems, recv_sems, local_sem, gather)

    return pl.pallas_call(
        body, name=name, in_specs=[_ANY], out_specs=_ANY,
        out_shape=_xy_out_shape(src), scratch_shapes=_xy_sems(),
    )(src)


def _xy_out_shape(src):
    return jax.ShapeDtypeStruct((N_CHIPS,) + tuple(src.shape[-2:]), src.dtype)


def _xy_sems():
    return [pltpu.SemaphoreType.DMA((N_CHIPS - 1,)), pltpu.SemaphoreType.DMA((N_CHIPS - 1,)), pltpu.SemaphoreType.DMA]


def _xy_copies(src_ref, out_ref, send_sems, recv_sems, local_sem, gather, with_arrivals=True):
    x, y, c = lax.axis_index("x"), lax.axis_index("y"), lax.axis_index("c")
    me = 2 * x + y
    peers = [(1 - x, y), (x, 1 - y), (1 - x, 1 - y)]

    def outgoing(px, py):
        return src_ref if gather else src_ref.at[2 * px + py]

    def copy(j, px, py, slot):
        return pltpu.make_async_remote_copy(
            src_ref=outgoing(px, py), dst_ref=out_ref.at[slot], send_sem=send_sems.at[j], recv_sem=recv_sems.at[j],
            device_id=(px, py, c), device_id_type=MESH_IDS)

    mine = pltpu.make_async_copy(outgoing(x, y), out_ref.at[me], local_sem)
    sends = [copy(j, px, py, me) for j, (px, py) in enumerate(peers)]
    arrivals = [copy(j, px, py, 2 * px + py) for j, (px, py) in enumerate(peers)] if with_arrivals else []
    return mine, sends, arrivals


def _xy_start(src_ref, out_ref, send_sems, recv_sems, local_sem, gather):
    mine, sends, _ = _xy_copies(src_ref, out_ref, send_sems, recv_sems, local_sem, gather, with_arrivals=False)
    mine.start()
    for cp in sends:
        cp.start()


def _xy_wait(src_ref, out_ref, send_sems, recv_sems, local_sem, gather):
    mine, sends, arrivals = _xy_copies(src_ref, out_ref, send_sems, recv_sems, local_sem, gather)
    for cp in arrivals:
        cp.wait_recv()
    for cp in sends:
        cp.wait_send()
    mine.wait()


def _c_swap(src, *, name):
    def body(src_ref, out_ref, send_sem, recv_sem):
        x, y, c = lax.axis_index("x"), lax.axis_index("y"), lax.axis_index("c")
        cp = pltpu.make_async_remote_copy(
            src_ref=src_ref, dst_ref=out_ref, send_sem=send_sem, recv_sem=recv_sem,
            device_id=(x, y, 1 - c), device_id_type=MESH_IDS)
        cp.start()
        cp.wait()

    return pl.pallas_call(
        body, name=name, in_specs=[_ANY], out_specs=_ANY,
        out_shape=jax.ShapeDtypeStruct(src.shape, src.dtype),
        scratch_shapes=[pltpu.SemaphoreType.DMA, pltpu.SemaphoreType.DMA],
    )(src)


def _all_gather8(blk, *, name):
    m_per, n = blk.shape

    def body(x_ref, out_ref, send_sems, recv_sems, local_sem):
        x, y, c = lax.axis_index("x"), lax.axis_index("y"), lax.axis_index("c")
        me, sibling = (x, y, c), (x, y, 1 - c)
        chips = [(1 - x, y), (x, 1 - y), (1 - x, 1 - y)]

        def rows(px, py, pc):
            return out_ref.at[pl.ds((4 * px + 2 * py + pc) * m_per, m_per), :]

        def copy(k, block, to, src=None):
            return pltpu.make_async_remote_copy(
                src_ref=rows(*block) if src is None else src, dst_ref=rows(*block),
                send_sem=send_sems.at[k], recv_sem=recv_sems.at[k], device_id=to, device_id_type=MESH_IDS)

        mine = pltpu.make_async_copy(x_ref, rows(*me), local_sem)
        mine.start()
        first = [copy(0, me, sibling, src=x_ref)]
        first += [copy(1 + j, me, (*chip, c), src=x_ref) for j, chip in enumerate(chips)]
        for cp in first:
            cp.start()
        passed = [copy(4 + j, (*chip, c), sibling) for j, chip in enumerate(chips)]
        for j, chip in enumerate(chips):
            copy(1 + j, (*chip, c), me).wait_recv()
            passed[j].start()
        copy(0, sibling, me).wait_recv()
        for j, chip in enumerate(chips):
            copy(4 + j, (*chip, 1 - c), me).wait_recv()
        for cp in first + passed:
            cp.wait_send()
        mine.wait()

    return pl.pallas_call(
        body, name=name,
        out_shape=jax.ShapeDtypeStruct((N_DEV * m_per, n), blk.dtype),
        in_specs=[pl.BlockSpec(memory_space=pltpu.VMEM)], out_specs=pl.BlockSpec(memory_space=pltpu.VMEM),
        scratch_shapes=[pltpu.SemaphoreType.DMA((7,)), pltpu.SemaphoreType.DMA((7,)), pltpu.SemaphoreType.DMA],
    )(blk)


def _sum_slots(parts, *, tr, name):
    n, R, _ = parts.shape
    pack = 2 * SUBLANES
    tr = max(t for t in range(pack, min(tr, R) + 1, pack) if R % t == 0) if R % pack == 0 else R

    def body(p_ref, o_ref):
        tot = p_ref[0].astype(F32)
        for s in range(1, n):
            tot = tot + p_ref[s].astype(F32)
        o_ref[...] = tot

    return pl.pallas_call(
        body, name=name, grid=(R // tr,),
        in_specs=[pl.BlockSpec((n, tr, LANES), lambda i: (0, i, 0))], out_specs=pl.BlockSpec((tr, LANES), lambda i: (i, 0)),
        out_shape=jax.ShapeDtypeStruct((R, LANES), F32), compiler_params=_params("parallel"),
    )(parts)


def _adamw(w, g_parts, m, v, *, name):
    shape = w.shape
    as2d = lambda a: a.reshape(-1, shape[-1])
    w2, m2, v2 = as2d(w), as2d(m), as2d(v)
    gs = [as2d(g) for g in g_parts]
    R, C = w2.shape
    tr = R
    while tr * C * 4 > (1 << 20) and tr % 2 == 0 and (tr // 2) % SUBLANES == 0:
        tr //= 2
    ng = len(gs)

    def body(*refs):
        w_ref, m_ref, v_ref = refs[:3]
        g_refs = refs[3:3 + ng]
        go_ref, d_ref, mo_ref, vo_ref = refs[3 + ng:]
        g = g_refs[0][...]
        for r in g_refs[1:]:
            g = g + r[...]
        mn = ADAM_B1 * m_ref[...] + (1.0 - ADAM_B1) * g
        vn = ADAM_B2 * v_ref[...] + (1.0 - ADAM_B2) * jnp.square(g)
        m_hat = mn / (1.0 - ADAM_B1 ** ADAM_STEP)
        v_hat = vn / (1.0 - ADAM_B2 ** ADAM_STEP)
        go_ref[...] = g
        d_ref[...] = -ADAM_LR * (m_hat / (jnp.sqrt(v_hat) + ADAM_EPS) + ADAM_WD * w_ref[...])
        mo_ref[...] = mn
        vo_ref[...] = vn

    spec = pl.BlockSpec((tr, C), lambda i: (i, 0))
    outs = pl.pallas_call(
        body, name=name, grid=(R // tr,), in_specs=[spec] * (3 + ng), out_specs=[spec] * 4,
        out_shape=[jax.ShapeDtypeStruct((R, C), F32)] * 4, compiler_params=_params("parallel"),
    )(w2, m2, v2, *gs)
    return tuple(o.reshape(shape) for o in outs)


TR = 256
ATTN_TILE = 1024
ATTN_ROWS = 256
ATTN_COLS = 256


def _mm_nn(a, b, name, **kw):
    return _matmul(a, b, tm=2048, tn=1024, tk=1024, name=name, **kw)


def _mm_nt(a, b, name, **kw):
    return _matmul(a, b, tb=True, tm=1024, tn=1024, tk=1024, name=name, **kw)


def _mm_tn(a, b, name, **kw):
    return _matmul(a, b, ta=True, tm=1024, tn=2048, tk=512, name=name, **kw)


def _c_rows(c, Hf):
    S_len = c.shape[0]
    ct = c[:, :Hf].T.reshape(Hf // 2, 2, S_len)
    return jnp.pad(ct, ((0, 0), (0, SUBLANES - 2), (0, 0)))


def _local_step(x, p, target, wts, late_weights=None, early_grads=None):
    L = len(wts["ln_g"])
    alpha = (2 * L) ** 0.25
    Hg = wts["gdn_a_log"][0].shape[-1]
    Hf = wts["fox_b_f"][0].shape[-1]
    Wg_ = Hg * GDN_DK
    Wf_ = Hf * FOX_DH
    saved = []
    for i in range(L):
        j = i // 2
        sv = {"x": x}
        if i % 2 == 0:
            w_in = wts["gdn_w_in"][j]
            wm, ws = w_in[:, :4 * Wg_], _pad_lanes(w_in[:, 4 * Wg_:])
            cw8 = jnp.pad(wts["gdn_conv_w"][j], ((0, SUBLANES - GDN_CONV), (0, 0)))
            alog = _pad_lanes(_row(wts["gdn_a_log"][j]), offset=Hg)
            dtb = _pad_lanes(_row(wts["gdn_dt_bias"][j]), offset=Hg)
            hm = _mm_nn(x, wm, f"gdn{j}_in_main")
            hs = _mm_nn(x, ws, f"gdn{j}_in_small")
            q, k, v, bg = _gdn_pre_fwd(hm, hs, cw8, alog, dtb, Hg, tr=TR, name=f"gdn{j}_pre")
            if i == 0 and late_weights is not None:
                o, states, tinv, got = _gdn_rule_fwd(q, k, v, bg, Hg, name=f"gdn{j}_rule", carry=(late_weights[0], True))
                for (wname_, idx), arr in late_weights[1](got).items():
                    wts[wname_][idx] = arr
            else:
                o, states, tinv = _gdn_rule_fwd(q, k, v, bg, Hg, name=f"gdn{j}_rule")
            on = _gdn_post_fwd(o, hm, wts["gdn_norm_g"][j], Hg, tr=TR, name=f"gdn{j}_post")
            w_out = wts["gdn_w_out"][j]
            sv.update(wm=wm, ws=ws, cw8=cw8, alog=alog, dtb=dtb, hm=hm, hs=hs, q=q, k=k, v=v, bg=bg, o=o,
                      states=states, tinv=tinv, on=on)
        else:
            w_in = wts["fox_w_in"][j]
            wm, ws = w_in[:, :4 * Wf_], _pad_lanes(w_in[:, 4 * Wf_:])
            gq2 = _row(jnp.tile(wts["fox_q_norm_g"][j], 2))
            gk2 = _row(jnp.tile(wts["fox_k_norm_g"][j], 2))
            bf = _pad_lanes(_row(wts["fox_b_f"][j]))
            hm = _mm_nn(x, wm, f"fox{j}_in_main")
            hs = _mm_nn(x, ws, f"fox{j}_in_small")
            qn, kn, vb, c, c_b = _fox_pre_fwd(hm, hs, gq2, gk2, bf, Hf, tr=TR, name=f"fox{j}_pre")
            c_rowp = _c_rows(c, Hf)
            o, lse_b = _fox_attn_fwd(qn, kn, vb, c_b, c_rowp, tb=ATTN_TILE, name=f"fox{j}_attn")
            on = _fox_post_fwd(o, hm, tr=TR, name=f"fox{j}_post")
            w_out = wts["fox_w_out"][j]
            sv.update(wm=wm, ws=ws, gq2=gq2, gk2=gk2, bf=bf, hm=hm, hs=hs, qn=qn, kn=kn, vb=vb, c_b=c_b,
                      c_rowp=c_rowp, o=o, lse_b=lse_b, on=on)
        y, x_ln = _out_ln_fwd(on, w_out, x, wts["ln_g"][i], wts["ln_b"][i], alpha, tr=2 * TR, name=f"out_ln{i}")
        pp = _mm_nn(p[i], wts["ple_w_proj"][i], f"ple{i}_proj")
        gp, x_out = _gate_mix_fwd(x_ln, wts["ple_w_gate"][i], pp, tr=2 * TR, name=f"ple{i}_gate_mix")
        sv.update(y=y, x_ln=x_ln, gp=gp, pp=pp)
        saved.append(sv)
        x = x_out

    loss_row, dx = _loss_fwd_bwd(x, target, tr=TR, name="loss")

    g = {n: [None] * len(wts[n]) for n in wts}
    got_early = None
    for i in reversed(range(L)):
        j = i // 2
        sv = saved[i]
        d_pre, d_pp, du, g["ln_g"][i], g["ln_b"][i] = _ple_ln_bwd(
            dx, sv["gp"], sv["pp"], wts["ple_w_gate"][i], sv["x"], sv["y"], wts["ln_g"][i], wts["ln_b"][i], alpha,
            tr=2 * TR, name=f"ple_ln{i}_bwd")
        g["ple_w_gate"][i] = _mm_tn(sv["x_ln"], d_pre, f"ple{i}_gate_dw")
        g["ple_w_proj"][i] = _mm_tn(p[i], d_pp, f"ple{i}_proj_dw")
        if i % 2 == 0:
            g["gdn_w_out"][j] = _mm_tn(sv["on"], du, f"gdn{j}_out_dw")
            d_on = _mm_nt(du, wts["gdn_w_out"][j], f"gdn{j}_out_dx")
            d_o, d_z, d_ng = _gdn_post_bwd(sv["o"], sv["hm"], wts["gdn_norm_g"][j], d_on, Hg, tr=TR, name=f"gdn{j}_post_bwd")
            rule_args = (sv["q"], sv["k"], sv["v"], sv["bg"], sv["states"], sv["tinv"], d_o, Hg)
            if i == 0 and early_grads is not None:
                dq, dk, dv, dbg, got_early = _gdn_rule_bwd(*rule_args, name=f"gdn{j}_rule_bwd",
                                                           carry=(early_grads(g), False))
            else:
                dq, dk, dv, dbg = _gdn_rule_bwd(*rule_args, name=f"gdn{j}_rule_bwd")
            d_hm, d_hs, d_cw, d_al, d_dtb = _gdn_pre_bwd(sv["hm"], sv["hs"], sv["cw8"], sv["alog"], sv["dtb"],
                                                         dq, dk, dv, dbg, d_z, Hg, tr=TR, name=f"gdn{j}_pre_bwd")
            g["gdn_norm_g"][j] = d_ng[0]
            g["gdn_conv_w"][j] = d_cw[:GDN_CONV]
            g["gdn_a_log"][j] = d_al[0, Hg:2 * Hg]
            g["gdn_dt_bias"][j] = d_dtb[0, Hg:2 * Hg]
            wname, nsmall = "gdn_w_in", 2 * Hg
        else:
            g["fox_w_out"][j] = _mm_tn(sv["on"], du, f"fox{j}_out_dw")
            d_og = _mm_nt(du, wts["fox_w_out"][j], f"fox{j}_out_dx")
            d_o, d_z, delta_b = _fox_post_bwd(sv["o"], sv["hm"], d_og, tr=TR, name=f"fox{j}_post_bwd")
            dqn, dkn, dvv, dcr, dct = _fox_attn_bwd(sv["qn"], sv["kn"], sv["vb"], sv["c_b"], sv["c_rowp"], sv["lse_b"],
                                                    delta_b, d_o, tb=ATTN_TILE, name=f"fox{j}_attn_bwd")
            dc = _pad_lanes(dcr[:, :2, :].reshape(Hf, -1).T + dct[:, :, :2].transpose(1, 0, 2).reshape(-1, Hf))
            d_hm, d_hs, d_gq, d_gk, d_bf = _fox_pre_bwd(sv["hm"], sv["hs"], sv["gq2"], sv["gk2"], sv["bf"],
                                                        dqn, dkn, dvv, d_z, dc, Hf, tr=TR, name=f"fox{j}_pre_bwd")
            g["fox_q_norm_g"][j] = d_gq[0, :FOX_DH] + d_gq[0, FOX_DH:]
            g["fox_k_norm_g"][j] = d_gk[0, :FOX_DH] + d_gk[0, FOX_DH:]
            g["fox_b_f"][j] = d_bf[0, :Hf]
            wname, nsmall = "fox_w_in", Hf
        dwm = _mm_tn(sv["x"], d_hm, f"{wname}{j}_main_dw")
        dws = _mm_tn(sv["x"], d_hs, f"{wname}{j}_small_dw")
        g[wname][j] = jnp.concatenate([dwm, dws[:, :nsmall]], axis=1)
        t1 = _mm_nt(d_hs, sv["ws"], f"{wname}{j}_small_dx", add=du, add_scale=alpha)
        dx = _mm_nt(d_hm, sv["wm"], f"{wname}{j}_main_dx", add=t1)
    return loss_row, dx, g, got_early


_SHARDED = (("ple_w_gate", 1), ("ple_w_proj", 2), ("gdn_w_in", 2), ("gdn_conv_w", 2), ("gdn_w_out", 1),
            ("fox_w_in", 2), ("fox_w_out", 1))
_SHARD_AXIS = dict(_SHARDED)
_REPLICATED = ("ln_g", "ln_b", "gdn_a_log", "gdn_dt_bias", "gdn_norm_g", "fox_b_f", "fox_q_norm_g", "fox_k_norm_g")
_EXACT = ("gdn_conv_w",)
_ORDER = ("ln_g", "ln_b", "ple_w_gate", "ple_w_proj", "gdn_w_in", "gdn_conv_w", "gdn_a_log", "gdn_dt_bias",
          "gdn_norm_g", "gdn_w_out", "fox_w_in", "fox_b_f", "fox_q_norm_g", "fox_k_norm_g", "fox_w_out")
_FIRST_WEIGHTS = (("gdn_w_in", 0), ("gdn_conv_w", 0))
_LAST_GRADS = (("ple_w_gate", 0), ("ple_w_proj", 0), ("gdn_w_in", 0), ("gdn_conv_w", 0), ("gdn_w_out", 0))
PACK_ROWS = 2 * SUBLANES


def _as_rows(a):
    rows = a.reshape(-1, LANES)
    return jnp.pad(rows, ((0, -rows.shape[0] % PACK_ROWS), (0, 0)))


def _n_elements(shape):
    n = 1
    for d in shape:
        n *= d
    return n


def _pack_weights(local, items):
    parts = []
    for name, idx in items:
        w = local[name][idx]
        parts.append(_as_rows(lax.bitcast_convert_type(w, BF16) if name in _EXACT else w.astype(BF16)))
    return jnp.concatenate(parts, axis=0), [q.shape[0] for q in parts]


def _unpack_weights(got, local, items, sizes):
    out, r0 = {}, 0
    for (name, idx), nrow in zip(items, sizes):
        shp = tuple(local[name].shape[1:])
        n_el = _n_elements(shp) * (2 if name in _EXACT else 1)
        seg = got[:, r0:r0 + nrow].reshape(N_CHIPS, -1)[:, :n_el]
        r0 += nrow
        if name in _EXACT:
            blocks = lax.bitcast_convert_type(seg.reshape((N_CHIPS,) + shp + (2,)), F32)
        else:
            blocks = seg.reshape((N_CHIPS,) + shp)
        axis = _SHARD_AXIS[name] - 1
        joined = shp[:axis] + (N_CHIPS * shp[axis],) + shp[axis + 1:]
        out[(name, idx)] = jnp.moveaxis(blocks, 0, axis).reshape(joined)
    return out


def _pack_grads(g, items):
    parts = []
    for name, idx in items:
        gfull = g[name][idx]
        axis = _SHARD_AXIS[name] - 1
        shp = gfull.shape
        split = shp[:axis] + (N_CHIPS, shp[axis] // N_CHIPS) + shp[axis + 1:]
        rows = jnp.moveaxis(gfull.reshape(split), axis, 0).astype(BF16).reshape(N_CHIPS, -1, LANES)
        parts.append(jnp.pad(rows, ((0, 0), (0, -rows.shape[1] % PACK_ROWS), (0, 0))))
    return jnp.concatenate(parts, axis=1), [q.shape[1] for q in parts]


def _unpack_grads(flat, local, items, sizes):
    out, r0 = {}, 0
    for (name, idx), nrow in zip(items, sizes):
        shp = tuple(local[name].shape[1:])
        out[(name, idx)] = flat[r0:r0 + nrow].reshape(-1)[:_n_elements(shp)].reshape(shp)
        r0 += nrow
    return out


def _reduce_replicated(grads, loss_part):
    rows = [_pad_lanes(jnp.reshape(loss_part, (1, 1)))]
    for name in _REPLICATED:
        gr = grads[name]
        rows.append(gr.reshape(-1, LANES) if gr.shape[-1] % LANES == 0 else _pad_lanes(gr))
    sizes = [r.shape[0] for r in rows]
    blk = jnp.concatenate(rows, axis=0)
    nrow = blk.shape[0]
    npad = -nrow % SUBLANES
    blk = jnp.pad(blk, ((0, npad), (0, 0)))
    allb = _all_gather8(blk, name="gather_small_grads").reshape(N_DEV, nrow + npad, LANES)
    tot = _sum_slots(allb, tr=nrow + npad, name="sum_small_grads")
    out, r0 = {}, sizes[0]
    loss = tot[0, 0]
    for name, n in zip(_REPLICATED, sizes[1:]):
        gr = grads[name]
        seg = tot[r0:r0 + n]
        out[name] = seg.reshape(gr.shape) if gr.shape[-1] % LANES == 0 else seg[:, :gr.shape[-1]]
        r0 += n
    return loss, out


def kernel(x, p, ln_g, ln_b, ple_w_gate, ple_w_proj, gdn_w_in, gdn_conv_w, gdn_a_log, gdn_dt_bias, gdn_norm_g, gdn_w_out, fox_w_in, fox_b_f, fox_q_norm_g, fox_k_norm_g, fox_w_out, loss_target, m_ln_g, m_ln_b, m_ple_w_gate, m_ple_w_proj, m_gdn_w_in, m_gdn_conv_w, m_gdn_a_log, m_gdn_dt_bias, m_gdn_norm_g, m_gdn_w_out, m_fox_w_in, m_fox_b_f, m_fox_q_norm_g, m_fox_k_norm_g, m_fox_w_out, v_ln_g, v_ln_b, v_ple_w_gate, v_ple_w_proj, v_gdn_w_in, v_gdn_conv_w, v_gdn_a_log, v_gdn_dt_bias, v_gdn_norm_g, v_gdn_w_out, v_fox_w_in, v_fox_b_f, v_fox_q_norm_g, v_fox_k_norm_g, v_fox_w_out):
    local = dict(ln_g=ln_g, ln_b=ln_b, ple_w_gate=ple_w_gate, ple_w_proj=ple_w_proj, gdn_w_in=gdn_w_in,
                 gdn_conv_w=gdn_conv_w, gdn_a_log=gdn_a_log, gdn_dt_bias=gdn_dt_bias, gdn_norm_g=gdn_norm_g,
                 gdn_w_out=gdn_w_out, fox_w_in=fox_w_in, fox_b_f=fox_b_f, fox_q_norm_g=fox_q_norm_g,
                 fox_k_norm_g=fox_k_norm_g, fox_w_out=fox_w_out)
    mom_m = dict(ln_g=m_ln_g, ln_b=m_ln_b, ple_w_gate=m_ple_w_gate, ple_w_proj=m_ple_w_proj, gdn_w_in=m_gdn_w_in,
                 gdn_conv_w=m_gdn_conv_w, gdn_a_log=m_gdn_a_log, gdn_dt_bias=m_gdn_dt_bias, gdn_norm_g=m_gdn_norm_g,
                 gdn_w_out=m_gdn_w_out, fox_w_in=m_fox_w_in, fox_b_f=m_fox_b_f, fox_q_norm_g=m_fox_q_norm_g,
                 fox_k_norm_g=m_fox_k_norm_g, fox_w_out=m_fox_w_out)
    mom_v = dict(ln_g=v_ln_g, ln_b=v_ln_b, ple_w_gate=v_ple_w_gate, ple_w_proj=v_ple_w_proj, gdn_w_in=v_gdn_w_in,
                 gdn_conv_w=v_gdn_conv_w, gdn_a_log=v_gdn_a_log, gdn_dt_bias=v_gdn_dt_bias, gdn_norm_g=v_gdn_norm_g,
                 gdn_w_out=v_gdn_w_out, fox_w_in=v_fox_w_in, fox_b_f=v_fox_b_f, fox_q_norm_g=v_fox_q_norm_g,
                 fox_k_norm_g=v_fox_k_norm_g, fox_w_out=v_fox_w_out)

    items = [(name, idx) for name, _ in _SHARDED for idx in range(local[name].shape[0])]
    w_first = [it for it in items if it in _FIRST_WEIGHTS]
    w_later = [it for it in items if it not in _FIRST_WEIGHTS]
    g_early = [it for it in items if it not in _LAST_GRADS]
    g_last = [it for it in items if it in _LAST_GRADS]

    wts = {name: [None] * local[name].shape[0] for name, _ in _SHARDED}
    for name in _REPLICATED:
        wts[name] = local[name]
    packed, sizes = _pack_weights(local, w_first)
    got = _xy_exchange(packed, gather=True, name="gather_weights_first")
    for (name, idx), arr in _unpack_weights(got, local, w_first, sizes).items():
        wts[name][idx] = arr
    packed_later, sizes_later = _pack_weights(local, w_later)
    early_sizes = []

    def pack_early(g):
        packed_g, sz = _pack_grads(g, g_early)
        early_sizes.extend(sz)
        return packed_g

    loss_row, dx, g, got_early = _local_step(
        x[0], p[:, 0], loss_target[0], wts,
        late_weights=(packed_later, lambda res: _unpack_weights(res, local, w_later, sizes_later)),
        early_grads=pack_early)

    loss, small = _reduce_replicated({name: jnp.stack(g[name]) for name in _REPLICATED}, jnp.sum(loss_row))
    packed_last, last_sizes = _pack_grads(g, g_last)
    got_last = _xy_exchange(packed_last, gather=False, name="exchange_grads_last")
    mine, other = {}, {}
    for tag, res, its, szs in (("early", got_early, g_early, early_sizes), ("last", got_last, g_last, last_sizes)):
        part = _sum_slots(res, tr=4096, name=f"sum_grads_{tag}")
        sib = _c_swap(part, name=f"swap_grads_{tag}")
        mine.update(_unpack_grads(part, local, its, szs))
        other.update(_unpack_grads(sib, local, its, szs))

    outs = {}
    for name in _ORDER:
        if name in _SHARD_AXIS:
            n_layers = local[name].shape[0]
            parts = [jnp.stack([mine[(name, i)] for i in range(n_layers)]),
                     jnp.stack([other[(name, i)] for i in range(n_layers)])]
        else:
            parts = [small[name]]
        outs[name] = _adamw(local[name], parts, mom_m[name], mom_v[name], name=f"adamw_{name}")
    return (loss, dx[None], *[outs[n][0] for n in _ORDER], *[outs[n][1] for n in _ORDER],
            *[outs[n][2] for n in _ORDER], *[outs[n][3] for n in _ORDER])
```

```python
import functools

import jax
import jax.numpy as jnp
from jax import lax
from jax.experimental import pallas as pl
from jax.experimental.pallas import tpu as pltpu

F32 = jnp.float32
BF16 = jnp.bfloat16

LANES = 128
SUBLANES = 8
VMEM_LIMIT_BYTES = 56 * 1024 * 1024

GDN_DK = 128
GDN_CHUNK = 64
GDN_CONV = 4
FOX_DH = 64
LN_EPS = 1e-5
RMS_EPS = 1e-6

ADAM_LR = 0.001
ADAM_B1 = 0.9
ADAM_B2 = 0.999
ADAM_EPS = 1e-08
ADAM_WD = 0.01
ADAM_STEP = 10

_DIMS = {"nn": (((1,), (0,)), ((), ())), "nt": (((1,), (1,)), ((), ())), "tn": (((0,), (0,)), ((), ()))}


def _params(*sem):
    return pltpu.CompilerParams(dimension_semantics=sem, vmem_limit_bytes=VMEM_LIMIT_BYTES)


def _split(a, terms):
    out = []
    rest = a.astype(F32)
    for t in range(terms):
        piece = rest.astype(BF16)
        out.append(piece)
        if t + 1 < terms:
            rest = rest - piece.astype(F32)
    return out


def _raw_mm(a, b, form, mode):
    dot = lambda x, y: lax.dot_general(x, y, _DIMS[form], preferred_element_type=F32)
    if mode == "b":
        return dot(a.astype(BF16), b.astype(BF16))
    if mode == "x3":
        (ah, al), (bh, bl) = _split(a, 2), _split(b, 2)
        return dot(ah, bh) + (dot(ah, bl) + dot(al, bh))
    if mode == "ca":
        ac = a.astype(BF16)
        b1, b2, b3 = _split(b, 3)
        return dot(ac, b1) + (dot(ac, b2) + dot(ac, b3))
    assert mode == "cb", mode
    bc = b.astype(BF16)
    a1, a2, a3 = _split(a, 3)
    return dot(a1, bc) + (dot(a2, bc) + dot(a3, bc))


@functools.partial(jax.custom_vjp, nondiff_argnums=(2, 3))
def _mm(a, b, form, mode):
    return _raw_mm(a, b, form, mode)


def _mm_fwd(a, b, form, mode):
    return _raw_mm(a, b, form, mode), (a, b)


def _mm_bwd(form, mode, res, g):
    a, b = res
    flip = {"b": "b", "x3": "x3", "ca": "cb", "cb": "ca"}[mode]
    if form == "nn":
        da, db = (lambda: _mm(g, b, "nt", mode)), (lambda: _mm(a, g, "tn", mode))
    elif form == "nt":
        da, db = (lambda: _mm(g, b, "nn", mode)), (lambda: _mm(g, a, "tn", flip))
    else:
        da, db = (lambda: _mm(b, g, "nt", flip)), (lambda: _mm(a, g, "nn", mode))
    return (jnp.zeros_like(a) if mode == "ca" else da()), (jnp.zeros_like(b) if mode == "cb" else db())


_mm.defvjp(_mm_fwd, _mm_bwd)


@jax.custom_vjp
def _tri_inv(Ls):
    C = Ls[0].shape[0]
    eye = (_iota((C, C), 0) == _iota((C, C), 1)).astype(F32)
    X = [eye - L for L in Ls]
    P = [_raw_mm(L, L, "nn", "x3") for L in Ls]
    n_sq = max(1, (C - 1).bit_length() - 1)
    for it in range(n_sq):
        XP = [_raw_mm(x, p, "nn", "x3") for x, p in zip(X, P)]
        if it < n_sq - 1:
            P = [_raw_mm(p, p, "nn", "x3") for p in P]
        X = [x + xp for x, xp in zip(X, XP)]
    return X


def _tri_inv_fwd(Ls):
    Ts = _tri_inv(Ls)
    return Ts, Ts


def _tri_inv_bwd(Ts, dTs):
    Ms = [_raw_mm(dT, T, "nt", "x3") for dT, T in zip(dTs, Ts)]
    return ([-_raw_mm(T, M, "tn", "x3") for T, M in zip(Ts, Ms)],)


_tri_inv.defvjp(_tri_inv_fwd, _tri_inv_bwd)


@jax.custom_vjp
def _tri_inv_known(Ls, Ts):
    return Ts


def _tri_inv_known_fwd(Ls, Ts):
    return Ts, Ts


def _tri_inv_known_bwd(Ts, dTs):
    return _tri_inv_bwd(Ts, dTs)[0], [jnp.zeros_like(T) for T in Ts]


_tri_inv_known.defvjp(_tri_inv_known_fwd, _tri_inv_known_bwd)


def _silu(x):
    return x * jax.nn.sigmoid(x)


def _softplus(x):
    return jnp.maximum(x, 0.0) + jnp.log1p(jnp.exp(-jnp.abs(x)))


def _iota(shape, dim):
    return lax.broadcasted_iota(jnp.int32, shape, dim)


def _matmul(a, b, *, ta=False, tb=False, out_dtype=F32, add=None, add_scale=1.0, tm=512, tn=512, tk=512, name):
    if ta:
        K, M = a.shape
    else:
        M, K = a.shape
    if tb:
        N, K2 = b.shape
    else:
        K2, N = b.shape
    assert K == K2, (a.shape, b.shape, ta, tb)
    tm, tn, tk = min(tm, M), min(tn, N), min(tk, K)
    assert M % tm == 0 and N % tn == 0 and K % tk == 0, (M, N, K, tm, tn, tk)
    nk = K // tk
    form = ("t" if ta else "n") + ("t" if tb else "n")
    dims = (((0 if ta else 1,), (1 if tb else 0,)), ((), ()))
    del form
    a_spec = pl.BlockSpec((tk, tm), lambda i, j, k: (k, i)) if ta else pl.BlockSpec((tm, tk), lambda i, j, k: (i, k))
    b_spec = pl.BlockSpec((tn, tk), lambda i, j, k: (j, k)) if tb else pl.BlockSpec((tk, tn), lambda i, j, k: (k, j))
    o_spec = pl.BlockSpec((tm, tn), lambda i, j, k: (i, j))
    has_add = add is not None

    def body(*refs):
        a_ref, b_ref = refs[:2]
        add_ref = refs[2] if has_add else None
        o_ref = refs[3] if has_add else refs[2]
        acc_ref = refs[-1] if nk > 1 else None
        k = pl.program_id(2)
        part = lax.dot_general(a_ref[...].astype(BF16), b_ref[...].astype(BF16), dims, preferred_element_type=F32)

        def finish(total):
            if has_add:
                total = total + add_scale * add_ref[...].astype(F32)
            o_ref[...] = total.astype(o_ref.dtype)

        if nk == 1:
            finish(part)
        else:
            @pl.when(k == 0)
            def _():
                acc_ref[...] = part

            @pl.when(jnp.logical_and(k > 0, k < nk - 1))
            def _():
                acc_ref[...] += part

            @pl.when(k == nk - 1)
            def _():
                finish(acc_ref[...] + part)

    in_specs = [a_spec, b_spec] + ([o_spec] if has_add else [])
    args = (a, b) + ((add,) if has_add else ())
    return pl.pallas_call(
        body, name=name, grid=(M // tm, N // tn, nk),
        in_specs=in_specs, out_specs=o_spec,
        out_shape=jax.ShapeDtypeStruct((M, N), out_dtype),
        scratch_shapes=[pltpu.VMEM((tm, tn), F32)] if nk > 1 else [],
        compiler_params=_params("parallel", "parallel", "arbitrary"),
    )(*args)


def _rowcall(body_fn, rows, consts, out_rows, out_accs, *, tr, name, reverse=False, scratch=()):
    def arr_spec(r):
        return r if isinstance(r, tuple) else (r, None)

    S = arr_spec(rows[0])[0].shape[0]
    tr = min(tr, S)
    assert S % tr == 0
    n = S // tr
    ridx = (lambda i: (n - 1 - i, 0)) if reverse else (lambda i: (i, 0))
    in_specs, args = [], []
    for r in rows:
        arr, spec = arr_spec(r)
        args.append(arr)
        in_specs.append(spec(tr, n) if spec is not None else pl.BlockSpec((tr, arr.shape[1]), ridx))
    for c in consts:
        args.append(c)
        in_specs.append(pl.BlockSpec(c.shape, lambda i: (0, 0)))
    out_specs, out_shape = [], []
    for (ncol, dt) in out_rows:
        out_specs.append(pl.BlockSpec((tr, ncol), ridx))
        out_shape.append(jax.ShapeDtypeStruct((S, ncol), dt))
    for shp in out_accs:
        out_specs.append(pl.BlockSpec(shp, lambda i: (0, 0)))
        out_shape.append(jax.ShapeDtypeStruct(shp, F32))
    nr, nc, no, na = len(rows), len(consts), len(out_rows), len(out_accs)

    def kernel(*refs):
        row_refs = refs[:nr]
        const_refs = refs[nr:nr + nc]
        orow_refs = refs[nr + nc:nr + nc + no]
        oacc_refs = refs[nr + nc + no:nr + nc + no + na]
        scr = refs[nr + nc + no + na:]
        step = pl.program_id(0)
        blk = (n - 1 - step) if reverse else step

        @pl.when(step == 0)
        def _():
            for acc in oacc_refs:
                acc[...] = jnp.zeros(acc.shape, F32)

        body_fn(row_refs, const_refs, orow_refs, oacc_refs, scr, step, blk)

    outs = pl.pallas_call(
        kernel, name=name, grid=(n,), in_specs=in_specs, out_specs=out_specs, out_shape=out_shape,
        scratch_shapes=list(scratch), compiler_params=_params("arbitrary"),
    )(*args)
    return outs


def _row(v):
    return v.astype(F32).reshape(1, -1)


def _pad_lanes(v, width=LANES, offset=0):
    pad = [(0, 0)] * (v.ndim - 1) + [(offset, width - offset - v.shape[-1])]
    return jnp.pad(v, pad)


def _ln_tile(x, y, g, b, alpha):
    u = alpha * x + y
    mu = jnp.mean(u, -1, keepdims=True)
    d = u - mu
    var = jnp.mean(d * d, -1, keepdims=True)
    return d * lax.rsqrt(var + LN_EPS) * g + b


def _in_proj_fwd(x, wm, ws, *, tr, name):
    def body(rows, consts, orows, oaccs, scr, step, blk):
        xb = rows[0][...].astype(BF16)
        orows[0][...] = lax.dot_general(xb, consts[0][...], _DIMS["nn"], preferred_element_type=F32)
        orows[1][...] = lax.dot_general(xb, consts[1][...], _DIMS["nn"], preferred_element_type=F32)

    return _rowcall(body, [x], [wm, ws], [(wm.shape[1], F32), (ws.shape[1], F32)], [], tr=tr, name=name)


def _in_proj_bwd_x(d_hm, d_hs, du, wm, ws, alpha, *, tr, name):
    D = du.shape[1]

    def body(rows, consts, orows, oaccs, scr, step, blk):
        acc = lax.dot_general(rows[0][...], consts[0][...], _DIMS["nt"], preferred_element_type=F32)
        acc = acc + lax.dot_general(rows[1][...], consts[1][...], _DIMS["nt"], preferred_element_type=F32)
        orows[0][...] = acc + alpha * rows[2][...]

    return _rowcall(body, [d_hm, d_hs, du], [wm, ws], [(D, F32)], [], tr=tr, name=name)[0]


def _out_ln_fwd(on, w_out, x, g, b, alpha, *, tr, name):
    D = x.shape[1]

    def body(rows, consts, orows, oaccs, scr, step, blk):
        y = lax.dot_general(rows[0][...].astype(BF16), consts[0][...], _DIMS["nn"], preferred_element_type=F32)
        orows[0][...] = y
        orows[1][...] = _ln_tile(rows[1][...], y, consts[1][...], consts[2][...], alpha)

    return _rowcall(body, [on, x], [w_out, _row(g), _row(b)], [(D, F32), (D, F32)], [], tr=tr, name=name)


def _gate_mix_fwd(x_ln, w_gate, pp, *, tr, name):
    D = x_ln.shape[1]

    def body(rows, consts, orows, oaccs, scr, step, blk):
        a = rows[0][...]
        gp = lax.dot_general(a.astype(BF16), consts[0][...], _DIMS["nn"], preferred_element_type=F32)
        orows[0][...] = gp
        orows[1][...] = a + jax.nn.sigmoid(gp) * rows[1][...]

    return _rowcall(body, [x_ln, pp], [w_gate], [(D, F32), (D, F32)], [], tr=tr, name=name)


def _ple_ln_bwd(dxo, gp, pp, w_gate, x, y, g, b, alpha, *, tr, name):
    D = x.shape[1]

    def body(rows, consts, orows, oaccs, scr, step, blk):
        d = rows[0][...]
        s = jax.nn.sigmoid(rows[1][...])
        d_pre = (d * rows[2][...] * s * (1.0 - s)).astype(BF16)
        orows[0][...] = d_pre
        orows[1][...] = (d * s).astype(BF16)
        ct = d + lax.dot_general(d_pre, consts[0][...], _DIMS["nt"], preferred_element_type=F32)
        xv, yv = rows[3][...], rows[4][...]
        _, vjp = jax.vjp(lambda yy, gg, bb: _ln_tile(xv, yy, gg, bb, alpha), yv, consts[1][...], consts[2][...])
        du, dg, db = vjp(ct)
        orows[2][...] = du
        oaccs[0][...] += dg
        oaccs[1][...] += db

    d_pre, d_pp, du, dg, db = _rowcall(body, [dxo, gp, pp, x, y], [w_gate, _row(g), _row(b)],
                                       [(D, BF16), (D, BF16), (D, F32)], [(1, D), (1, D)], tr=tr, name=name)
    return d_pre, d_pp, du, dg[0], db[0]


def _loss_fwd_bwd(xf, target, *, tr, name):
    D = xf.shape[1]

    def body(rows, consts, orows, oaccs, scr, step, blk):
        err = rows[0][...] - rows[1][...]
        orows[0][...] = err * (1.0 / D)
        part = jnp.sum(err * err, axis=0, keepdims=True) * (0.5 / D)
        oaccs[0][...] += part

    dx, lrow = _rowcall(body, [xf, target], [], [(D, F32)], [(1, D)], tr=tr, name=name)
    return lrow, dx


def _gdn_qk_tile(c):
    y = _silu(c)
    return y * lax.rsqrt(jnp.sum(y * y, -1, keepdims=True) + RMS_EPS)


def _make_bg_fn(H):
    def fn(hs, alog, dtb):
        lane = _iota((1, LANES), 1)
        beta = jax.nn.sigmoid(hs)
        g = -jnp.exp(alog) * _softplus(hs + dtb)
        return jnp.where(lane < H, beta, jnp.where(lane < 2 * H, g, 0.0))
    return fn


def _halo_spec(ncol):
    def make(tr, n):
        per = tr // SUBLANES
        return pl.BlockSpec((SUBLANES, ncol), lambda i: (jnp.maximum(i * per - 1, 0), 0))
    return make


def _halo_spec_rev(ncol):
    def make(tr, n):
        per = tr // SUBLANES
        return pl.BlockSpec((SUBLANES, ncol), lambda i: (jnp.maximum((n - 1 - i) * per - 1, 0), 0))
    return make


def _gdn_pre_fwd(h_main, h_small, conv_w8, alog_row, dtb_row, H, *, tr, name):
    W = H * GDN_DK
    C3 = 3 * W
    bg_fn = _make_bg_fn(H)

    def body(rows, consts, orows, oaccs, scr, step, blk):
        main_ref, halo_ref, hs_ref = rows
        w_ref, alog_ref, dtb_ref = consts
        q_ref, k_ref, v_ref, bg_ref = orows
        xs = scr[0]
        trr = main_ref.shape[0]
        xs[pl.ds(SUBLANES, trr), :] = main_ref[...]
        xs[pl.ds(0, SUBLANES), :] = jnp.where(blk > 0, halo_ref[...], 0.0)
        for s in range(C3 // LANES):
            ls = slice(s * LANES, (s + 1) * LANES)
            c = jnp.zeros((trr, LANES), F32)
            for j in range(GDN_CONV):
                c = c + w_ref[GDN_CONV - 1 - j:GDN_CONV - j, ls] * xs[pl.ds(SUBLANES - j, trr), ls]
            if s < 2 * H:
                out = _gdn_qk_tile(c)
                (q_ref if s < H else k_ref)[:, (s % H) * LANES:(s % H + 1) * LANES] = out
            else:
                v_ref[:, (s - 2 * H) * LANES:(s - 2 * H + 1) * LANES] = _silu(c)
        bg_ref[...] = bg_fn(hs_ref[...], alog_ref[...], dtb_ref[...])

    main = (h_main, lambda tr_, n: pl.BlockSpec((tr_, C3), lambda i: (i, 0)))
    halo = (h_main, _halo_spec(C3))
    trr = min(tr, h_main.shape[0])
    return _rowcall(body, [main, halo, h_small], [conv_w8, alog_row, dtb_row],
                    [(W, F32), (W, F32), (W, F32), (LANES, F32)], [], tr=tr, name=name,
                    scratch=[pltpu.VMEM((trr + SUBLANES, C3), F32)])


def _gdn_pre_bwd(h_main, h_small, conv_w8, alog_row, dtb_row, dq, dk, dv, dbg, dz, H, *, tr, name):
    W = H * GDN_DK
    C3 = 3 * W
    bg_fn = _make_bg_fn(H)

    def body(rows, consts, orows, oaccs, scr, step, blk):
        main_ref, halo_ref, hs_ref, dq_ref, dk_ref, dv_ref, dbg_ref, dz_ref = rows
        w_ref, alog_ref, dtb_ref = consts
        dmain_ref, dhs_ref = orows
        dw_ref, dalog_ref, ddtb_ref = oaccs
        xs, dcs = scr
        trr = main_ref.shape[0]
        xs[pl.ds(SUBLANES, trr), :] = main_ref[...]
        xs[pl.ds(0, SUBLANES), :] = jnp.where(blk > 0, halo_ref[...], 0.0)

        @pl.when(step == 0)
        def _():
            dcs[pl.ds(trr, SUBLANES), :] = jnp.zeros((SUBLANES, C3), F32)

        for s in range(C3 // LANES):
            ls = slice(s * LANES, (s + 1) * LANES)
            c = jnp.zeros((trr, LANES), F32)
            for j in range(GDN_CONV):
                c = c + w_ref[GDN_CONV - 1 - j:GDN_CONV - j, ls] * xs[pl.ds(SUBLANES - j, trr), ls]
            if s < 2 * H:
                src = dq_ref if s < H else dk_ref
                ct = src[:, (s % H) * LANES:(s % H + 1) * LANES]
                _, vjp = jax.vjp(_gdn_qk_tile, c)
            else:
                ct = dv_ref[:, (s - 2 * H) * LANES:(s - 2 * H + 1) * LANES]
                _, vjp = jax.vjp(_silu, c)
            dcs[pl.ds(0, trr), ls] = vjp(ct)[0]
        for s in range(C3 // LANES):
            ls = slice(s * LANES, (s + 1) * LANES)
            dx = jnp.zeros((trr, LANES), F32)
            dc0 = dcs[pl.ds(0, trr), ls]
            for j in range(GDN_CONV):
                wrow = w_ref[GDN_CONV - 1 - j:GDN_CONV - j, ls]
                dx = dx + wrow * dcs[pl.ds(j, trr), ls]
                dw_ref[GDN_CONV - 1 - j:GDN_CONV - j, ls] += jnp.sum(dc0 * xs[pl.ds(SUBLANES - j, trr), ls], axis=0, keepdims=True)
            dmain_ref[:, ls] = dx.astype(BF16)
        dmain_ref[:, C3:] = dz_ref[...]
        dcs[pl.ds(trr, SUBLANES), :] = dcs[pl.ds(0, SUBLANES), :]
        _, vjp = jax.vjp(bg_fn, hs_ref[...], alog_ref[...], dtb_ref[...])
        dhs, dalog, ddtb = vjp(dbg_ref[...])
        dhs_ref[...] = dhs.astype(BF16)
        dalog_ref[...] += dalog
        ddtb_ref[...] += ddtb

    trr = min(tr, h_main.shape[0])
    main = (h_main, lambda tr_, n: pl.BlockSpec((tr_, C3), lambda i: (n - 1 - i, 0)))
    halo = (h_main, _halo_spec_rev(C3))
    return _rowcall(body, [main, halo, h_small, dq, dk, dv, dbg, dz], [conv_w8, alog_row, dtb_row],
                    [(4 * W, BF16), (LANES, BF16)], [(SUBLANES, C3), (1, LANES), (1, LANES)],
                    tr=tr, name=name, reverse=True,
                    scratch=[pltpu.VMEM((trr + SUBLANES, C3), F32), pltpu.VMEM((trr + SUBLANES, C3), F32)])


def _gdn_chunk(qs, ks, vs, betas, gs, Ss, Ts=None, with_inverse=False):
    C, dk = qs[0].shape
    dv = vs[0].shape[1]
    ri, ci = _iota((C, C), 0), _iota((C, C), 1)
    causal, strict = ri >= ci, ri > ci
    tril = causal.astype(F32)
    lane0 = (_iota((1, LANES), 1) == 0).astype(F32)
    e0 = jnp.ones((C, 1), F32) * lane0
    last = (_iota((C, 1), 0) == C - 1).astype(F32)

    def each(f, *lists):
        return [f(*a) for a in zip(*lists)]

    G = each(lambda g: g * jnp.ones((1, LANES), F32), gs)
    gcB = each(lambda x: _mm(tril, x, "nn", "ca"), G)
    gc = each(lambda x: jnp.sum(x * lane0, -1, keepdims=True), gcB)
    gc_row = each(lambda x: _mm(e0, x, "nt", "ca"), gcB)
    decay = each(lambda a, b: jnp.where(causal, jnp.exp(jnp.where(causal, a - b, 0.0)), 0.0), gc, gc_row)
    kb = each(lambda k, b: k * b, ks, betas)
    kk = each(lambda a, k: _mm(a, k, "nt", "b"), kb, ks)
    L = each(lambda a, d: jnp.where(strict, a * d, 0.0), kk, decay)
    X = _tri_inv(L) if Ts is None else _tri_inv_known(L, Ts)
    egc = each(jnp.exp, gc)
    u = each(lambda x, v, b: _mm(x, v * b, "nn", "x3"), X, vs, betas)
    w = each(lambda x, a, e: _mm(x, a * e, "nn", "x3"), X, kb, egc)
    qsc = each(lambda q: q * (dk ** -0.5), qs)
    qk = each(lambda q, k: _mm(q, k, "nt", "b"), qsc, ks)
    A = each(lambda a, d: jnp.where(causal, a * d, 0.0), qk, decay)
    q_dec = each(lambda q, e: q * e, qsc, egc)
    gl = each(lambda x: jnp.sum(x * last, keepdims=True), gc)
    k_dec = each(lambda k, a, b: k * jnp.exp(a - b), ks, gl, gc)
    wS = each(lambda a, s: _mm(a, s, "nn", "b"), w, Ss)
    qS = each(lambda a, s: _mm(a, s, "nn", "b"), q_dec, Ss)
    v_new = each(lambda a, b: a - b, u, wS)
    Av = each(lambda a, b: _mm(a, b, "nn", "b"), A, v_new)
    kv = each(lambda a, b: _mm(a, b, "tn", "b"), k_dec, v_new)
    o = each(lambda a, b: a + b, qS, Av)
    S_new = each(lambda s, e, x: s * jnp.exp(e) + x, Ss, gl, kv)
    return (o, S_new, X) if with_inverse else (o, S_new)


def _gdn_rule_fwd(q, k, v, bg, H, *, name, carry=None):
    S_len = q.shape[0]
    C = min(GDN_CHUNK, S_len)
    N = S_len // C
    dk = dv = GDN_DK

    def body(*refs):
        if carry is None:
            q_ref, k_ref, v_ref, bg_ref, o_ref, st_ref, ti_ref, s_scr = refs
        else:
            q_ref, k_ref, v_ref, bg_ref, src_ref, o_ref, st_ref, ti_ref, got_ref, s_scr = refs[:10]
            xy = (src_ref, got_ref) + tuple(refs[10:]) + (carry[1],)
        n = pl.program_id(0)

        @pl.when(n == 0)
        def _():
            s_scr[...] = jnp.zeros(s_scr.shape, F32)
            if carry is not None:
                _xy_start(*xy)

        bgt = bg_ref[...]
        sl = [slice(h * dk, (h + 1) * dk) for h in range(H)]
        Ss = [s_scr[h] for h in range(H)]
        for h in range(H):
            st_ref[h] = Ss[h]
        os_, S_new, Ts = _gdn_chunk([q_ref[:, s] for s in sl], [k_ref[:, s] for s in sl], [v_ref[:, s] for s in sl],
                                    [bgt[:, h:h + 1] for h in range(H)], [bgt[:, H + h:H + h + 1] for h in range(H)],
                                    Ss, with_inverse=True)
        for h in range(H):
            o_ref[:, sl[h]] = os_[h]
            s_scr[h] = S_new[h]
            ti_ref[h] = Ts[h]

        if carry is not None:
            @pl.when(n == N - 1)
            def _():
                _xy_wait(*xy)

    rows = pl.BlockSpec((C, H * dk), lambda n: (n, 0))
    extra = carry is not None
    return pl.pallas_call(
        body, name=name, grid=(N,),
        in_specs=[rows, rows, rows, pl.BlockSpec((C, LANES), lambda n: (n, 0))] + ([_ANY] if extra else []),
        out_specs=[rows, pl.BlockSpec((H, dk, dv), lambda n: (n, 0, 0)), pl.BlockSpec((H, C, C), lambda n: (n, 0, 0))]
        + ([_ANY] if extra else []),
        out_shape=[jax.ShapeDtypeStruct((S_len, H * dv), F32), jax.ShapeDtypeStruct((N * H, dk, dv), F32),
                   jax.ShapeDtypeStruct((N * H, C, C), F32)] + ([_xy_out_shape(carry[0])] if extra else []),
        scratch_shapes=[pltpu.VMEM((H, dk, dv), F32)] + (_xy_sems() if extra else []),
        compiler_params=_params("arbitrary"),
    )(q, k, v, bg, *((carry[0],) if extra else ()))


def _gdn_rule_bwd(q, k, v, bg, states, tinv, do, H, *, name, carry=None):
    S_len = q.shape[0]
    C = min(GDN_CHUNK, S_len)
    N = S_len // C
    dk = dv = GDN_DK

    def body(*refs):
        if carry is None:
            q_ref, k_ref, v_ref, bg_ref, st_ref, ti_ref, do_ref, dq_ref, dk_ref, dv_ref, dbg_ref, ds_scr = refs
        else:
            (q_ref, k_ref, v_ref, bg_ref, st_ref, ti_ref, do_ref, src_ref,
             dq_ref, dk_ref, dv_ref, dbg_ref, got_ref, ds_scr) = refs[:14]
            xy = (src_ref, got_ref) + tuple(refs[14:]) + (carry[1],)
        step = pl.program_id(0)

        @pl.when(step == 0)
        def _():
            ds_scr[...] = jnp.zeros(ds_scr.shape, F32)
            if carry is not None:
                _xy_start(*xy)

        bgt = bg_ref[...]
        lane = _iota((1, LANES), 1)
        dbg = jnp.zeros((C, LANES), F32)
        sl = [slice(h * dk, (h + 1) * dk) for h in range(H)]
        Ts = [ti_ref[h] for h in range(H)]
        _, vjp = jax.vjp(lambda *a: _gdn_chunk(*a, Ts=Ts),
                         [q_ref[:, s] for s in sl], [k_ref[:, s] for s in sl], [v_ref[:, s] for s in sl],
                         [bgt[:, h:h + 1] for h in range(H)], [bgt[:, H + h:H + h + 1] for h in range(H)],
                         [st_ref[h] for h in range(H)])
        dq, dkk, dvv, dbeta, dg, dS = vjp(([do_ref[:, s] for s in sl], [ds_scr[h] for h in range(H)]))
        for h in range(H):
            dq_ref[:, sl[h]] = dq[h]
            dk_ref[:, sl[h]] = dkk[h]
            dv_ref[:, sl[h]] = dvv[h]
            dbg = dbg + jnp.where(lane == h, dbeta[h], 0.0) + jnp.where(lane == h + H, dg[h], 0.0)
            ds_scr[h] = dS[h]
        dbg_ref[...] = dbg

        if carry is not None:
            @pl.when(step == N - 1)
            def _():
                _xy_wait(*xy)

    rows = pl.BlockSpec((C, H * dk), lambda s: (N - 1 - s, 0))
    bgs = pl.BlockSpec((C, LANES), lambda s: (N - 1 - s, 0))
    extra = carry is not None
    return pl.pallas_call(
        body, name=name, grid=(N,),
        in_specs=[rows, rows, rows, bgs, pl.BlockSpec((H, dk, dv), lambda s: (N - 1 - s, 0, 0)),
                  pl.BlockSpec((H, C, C), lambda s: (N - 1 - s, 0, 0)), rows] + ([_ANY] if extra else []),
        out_specs=[rows, rows, rows, bgs] + ([_ANY] if extra else []),
        out_shape=[jax.ShapeDtypeStruct((S_len, H * dk), F32)] * 3 + [jax.ShapeDtypeStruct((S_len, LANES), F32)]
        + ([_xy_out_shape(carry[0])] if extra else []),
        scratch_shapes=[pltpu.VMEM((H, dk, dv), F32)] + (_xy_sems() if extra else []),
        compiler_params=_params("arbitrary"),
    )(q, k, v, bg, states, tinv, do, *((carry[0],) if extra else ()))


def _gdn_post_tile(o, z, g):
    return o * lax.rsqrt(jnp.mean(o * o, -1, keepdims=True) + RMS_EPS) * g * _silu(z)


def _gdn_post_fwd(o, h_main, norm_g, H, *, tr, name):
    W = H * GDN_DK

    def body(rows, consts, orows, oaccs, scr, step, blk):
        for h in range(H):
            ls = slice(h * LANES, (h + 1) * LANES)
            orows[0][:, ls] = _gdn_post_tile(rows[0][:, ls], rows[1][:, ls], consts[0][...]).astype(BF16)

    z = (h_main, lambda tr_, n: pl.BlockSpec((tr_, W), lambda i: (i, 3)))
    return _rowcall(body, [o, z], [_row(norm_g)], [(W, BF16)], [], tr=tr, name=name)[0]


def _gdn_post_bwd(o, h_main, norm_g, du, w_out, H, *, tr, name):
    W = H * GDN_DK

    def body(rows, consts, orows, oaccs, scr, step, blk):
        d_on = lax.dot_general(rows[2][...].astype(BF16), consts[1][...], _DIMS["nt"], preferred_element_type=F32)
        for h in range(H):
            ls = slice(h * LANES, (h + 1) * LANES)
            _, vjp = jax.vjp(_gdn_post_tile, rows[0][:, ls], rows[1][:, ls], consts[0][...])
            d_o, d_z, d_g = vjp(d_on[:, ls])
            orows[0][:, ls] = d_o
            orows[1][:, ls] = d_z.astype(BF16)
            oaccs[0][...] += d_g

    z = (h_main, lambda tr_, n: pl.BlockSpec((tr_, W), lambda i: (i, 3)))
    return _rowcall(body, [o, z, du], [_row(norm_g), w_out], [(W, F32), (W, BF16)], [(1, LANES)], tr=tr, name=name)


def _seg_ones():
    ri, ci = _iota((LANES, LANES), 0), _iota((LANES, LANES), 1)
    return ((ri < FOX_DH) == (ci < FOX_DH)).astype(F32)


def _fox_qk_tile(x, g2):
    ms = _mm(x * x, _seg_ones(), "nn", "cb") * (1.0 / FOX_DH)
    return x * lax.rsqrt(ms + RMS_EPS) * g2


def _make_lf_fn(Hf):
    def fn(hs, bf):
        lane = _iota((1, LANES), 1)
        return jnp.where(lane < Hf, -_softplus(-(hs + bf)), 0.0)
    return fn


def _fox_pre_fwd(h_main, h_small, gq2, gk2, bf_row, Hf, *, tr, name):
    W = Hf * FOX_DH
    lf_fn = _make_lf_fn(Hf)

    def body(rows, consts, orows, oaccs, scr, step, blk):
        qk_ref, v_ref, hs_ref = rows
        gq_ref, gk_ref, bf_ref = consts
        qn_ref, kn_ref, vb_ref, c_ref, cb_ref = orows
        carry = scr[0]
        trr = qk_ref.shape[0]

        @pl.when(step == 0)
        def _():
            carry[...] = jnp.zeros(carry.shape, F32)

        for s in range(W // LANES):
            ls = slice(s * LANES, (s + 1) * LANES)
            qn_ref[:, ls] = _fox_qk_tile(qk_ref[:, ls], gq_ref[...]).astype(BF16)
            kn_ref[:, ls] = _fox_qk_tile(qk_ref[:, W + s * LANES:W + (s + 1) * LANES], gk_ref[...]).astype(BF16)
        vb_ref[...] = v_ref[...].astype(BF16)
        lf = lf_fn(hs_ref[...], bf_ref[...])
        tril = (_iota((trr, trr), 0) >= _iota((trr, trr), 1)).astype(F32)
        c = _raw_mm(tril, lf, "nn", "ca") + carry[0:1, :]
        c_ref[...] = c
        carry[0:1, :] = c[trr - 1:trr, :]
        col = _iota((LANES, 2 * W), 1)
        parity = (col >= W).astype(jnp.int32)
        slab = jnp.right_shift(col - parity * W, 7)
        expand = (2 * slab + parity == _iota((LANES, 2 * W), 0)).astype(F32)
        cb_ref[...] = _raw_mm(c, expand, "nn", "cb")

    qk = (h_main, lambda tr_, n: pl.BlockSpec((tr_, 2 * W), lambda i: (i, 0)))
    vv = (h_main, lambda tr_, n: pl.BlockSpec((tr_, W), lambda i: (i, 2)))
    return _rowcall(body, [qk, vv, h_small], [gq2, gk2, bf_row],
                    [(W, BF16), (W, BF16), (W, BF16), (LANES, F32), (2 * W, F32)], [], tr=tr, name=name,
                    scratch=[pltpu.VMEM((SUBLANES, LANES), F32)])


def _fox_pre_bwd(h_main, h_small, gq2, gk2, bf_row, dqn, dkn, dvv, dz, dc, Hf, *, tr, name):
    W = Hf * FOX_DH
    lf_fn = _make_lf_fn(Hf)

    def body(rows, consts, orows, oaccs, scr, step, blk):
        qk_ref, hs_ref, dqn_ref, dkn_ref, dvv_ref, dz_ref, dc_ref = rows
        gq_ref, gk_ref, bf_ref = consts
        dmain_ref, dhs_ref = orows
        dgq_ref, dgk_ref, dbf_ref = oaccs
        carry = scr[0]
        trr = qk_ref.shape[0]

        @pl.when(step == 0)
        def _():
            carry[...] = jnp.zeros(carry.shape, F32)

        for s in range(W // LANES):
            ls = slice(s * LANES, (s + 1) * LANES)
            lk = slice(W + s * LANES, W + (s + 1) * LANES)
            _, vjp = jax.vjp(_fox_qk_tile, qk_ref[:, ls], gq_ref[...])
            dx, dg = vjp(dqn_ref[:, ls])
            dmain_ref[:, ls] = dx.astype(BF16)
            dgq_ref[...] += dg
            _, vjp = jax.vjp(_fox_qk_tile, qk_ref[:, lk], gk_ref[...])
            dx, dg = vjp(dkn_ref[:, ls])
            dmain_ref[:, lk] = dx.astype(BF16)
            dgk_ref[...] += dg
        dmain_ref[:, 2 * W:3 * W] = dvv_ref[...].astype(BF16)
        dmain_ref[:, 3 * W:] = dz_ref[...]
        dcv = dc_ref[...]
        triu = (_iota((trr, trr), 0) <= _iota((trr, trr), 1)).astype(F32)
        dlf = _raw_mm(triu, dcv, "nn", "ca") + carry[0:1, :]
        carry[0:1, :] = dlf[0:1, :]
        _, vjp = jax.vjp(lf_fn, hs_ref[...], bf_ref[...])
        dhs, dbf = vjp(dlf)
        dhs_ref[...] = dhs.astype(BF16)
        dbf_ref[...] += dbf

    qk = (h_main, lambda tr_, n: pl.BlockSpec((tr_, 2 * W), lambda i: (n - 1 - i, 0)))
    return _rowcall(body, [qk, h_small, dqn, dkn, dvv, dz, dc], [gq2, gk2, bf_row],
                    [(4 * W, BF16), (LANES, BF16)], [(1, LANES), (1, LANES), (1, LANES)],
                    tr=tr, name=name, reverse=True, scratch=[pltpu.VMEM((SUBLANES, LANES), F32)])


def _fox_attn_fwd(qn, kn, vb, c_b, c_rowp, *, tb, name):
    S_len, W = qn.shape
    HP = W // LANES
    tb = min(tb, S_len)
    nb = S_len // tb
    scale = FOX_DH ** -0.5
    rb, cb = min(ATTN_ROWS, tb), min(ATTN_COLS, tb)
    nblk = 2 * (tb // rb)

    steps = [(i, j) for i in range(nb) for j in range(i + 1)]
    ti = jnp.asarray([s[0] for s in steps], jnp.int32)
    tj = jnp.asarray([s[1] for s in steps], jnp.int32)

    def body(ti_ref, tj_ref, q_ref, k_ref, v_ref, cb0_ref, cb1_ref, cr_ref, o_ref, lse_ref,
             m_scr, l_scr, acc_scr, s_scr, p_scr, a_scr):
        t = pl.program_id(1)
        i, j = ti_ref[t], tj_ref[t]

        @pl.when(j == 0)
        def _():
            m_scr[...] = jnp.full(m_scr.shape, -jnp.inf, F32)
            l_scr[...] = jnp.zeros(l_scr.shape, F32)
            acc_scr[...] = jnp.zeros(acc_scr.shape, F32)

        def compute(diag):
            lo = _iota((1, LANES), 1) < FOX_DH
            v = v_ref[...]
            lane = _iota((1, LANES), 1)
            blocks = [(hh, r) for hh in range(2) for r in range(tb // rb)]
            masks = [lo, jnp.logical_not(lo)]

            def visible(r):
                return ((r + 1) * rb - 1) // LANES + 1 if diag else tb // LANES

            for b, (hh, r) in enumerate(blocks):
                rows = slice(r * rb, (r + 1) * rb)
                qr = q_ref[rows, :]
                qh = jnp.where(masks[hh], qr * scale, jnp.zeros_like(qr))
                ctb = (cb0_ref if hh == 0 else cb1_ref)[rows, :]
                mx = None
                for c in range(tb // cb):
                    if c * cb // LANES >= visible(r):
                        continue
                    s2 = lax.dot_general(qh, k_ref[c * cb:(c + 1) * cb, :], _DIMS["nt"], preferred_element_type=F32)
                    for piece in range(c * cb // LANES, min((c + 1) * cb // LANES, visible(r))):
                        cols = slice(piece * LANES, (piece + 1) * LANES)
                        s = s2[:, piece * LANES - c * cb:(piece + 1) * LANES - c * cb] + ctb - cr_ref[hh:hh + 1, cols]
                        if diag and (piece + 1) * LANES - 1 > r * rb:
                            keep = piece * LANES + _iota((rb, LANES), 1) <= r * rb + _iota((rb, LANES), 0)
                            s = jnp.where(keep, s, -jnp.inf)
                        s_scr[b, :, cols] = s
                        mx = s if mx is None else jnp.maximum(mx, s)
                m_prev = m_scr[hh, rows, :]
                m_new = jnp.maximum(m_prev, jnp.broadcast_to(jnp.max(mx, -1, keepdims=True), (rb, LANES)))
                a_scr[b] = jnp.exp(m_prev - m_new)
                m_scr[hh, rows, :] = m_new
            for b, (hh, r) in enumerate(blocks):
                m_new = m_scr[hh, r * rb:(r + 1) * rb, :]
                for piece in range(visible(r)):
                    cols = slice(piece * LANES, (piece + 1) * LANES)
                    p_scr[b, :, cols] = jnp.exp(s_scr[b, :, cols] - m_new).astype(BF16)
            for b, (hh, r) in enumerate(blocks):
                rows = slice(r * rb, (r + 1) * rb)
                nkv = visible(r) * LANES
                pb = p_scr[b, :, :nkv]
                spare = (1 - hh) * FOX_DH
                vh = jnp.where(masks[hh], v[:nkv], (lane == spare).astype(BF16))
                pv = lax.dot_general(pb, vh, _DIMS["nn"], preferred_element_type=F32)
                psum = jnp.broadcast_to(pv[:, spare:spare + 1], (rb, LANES))
                alpha = a_scr[b]
                l_scr[hh, rows, :] = alpha * l_scr[hh, rows, :] + psum
                acc = acc_scr[rows, :]
                acc_scr[rows, :] = jnp.where(masks[hh], acc * alpha + pv, acc)

        @pl.when(j < i)
        def _():
            compute(False)

        @pl.when(j == i)
        def _():
            compute(True)
            lo = _iota((1, LANES), 1) < FOX_DH
            o_ref[...] = acc_scr[...] / jnp.where(lo, l_scr[0], l_scr[1])
            lse_ref[...] = m_scr[...] + jnp.log(l_scr[...])

    qs = pl.BlockSpec((tb, LANES), lambda h, t, ti_, tj_: (ti_[t], h))
    qs1 = pl.BlockSpec((tb, LANES), lambda h, t, ti_, tj_: (ti_[t], HP + h))
    ks = pl.BlockSpec((tb, LANES), lambda h, t, ti_, tj_: (tj_[t], h))
    crs = pl.BlockSpec((None, SUBLANES, tb), lambda h, t, ti_, tj_: (h, 0, tj_[t]))
    return pl.pallas_call(
        body, name=name,
        grid_spec=pltpu.PrefetchScalarGridSpec(
            num_scalar_prefetch=2, grid=(HP, len(steps)),
            in_specs=[qs, ks, ks, qs, qs1, crs],
            out_specs=[qs, pl.BlockSpec((2, tb, LANES), lambda h, t, ti_, tj_: (0, ti_[t], h))],
            scratch_shapes=[pltpu.VMEM((2, tb, LANES), F32), pltpu.VMEM((2, tb, LANES), F32),
                            pltpu.VMEM((tb, LANES), F32), pltpu.VMEM((nblk, rb, tb), F32),
                            pltpu.VMEM((nblk, rb, tb), BF16), pltpu.VMEM((nblk, rb, LANES), F32)]),
        out_shape=[jax.ShapeDtypeStruct((S_len, W), F32), jax.ShapeDtypeStruct((2, S_len, W), F32)],
        compiler_params=_params("parallel", "arbitrary"),
    )(ti, tj, qn, kn, vb, c_b, c_b, c_rowp)


def _fox_attn_bwd(qn, kn, vb, c_b, c_rowp, lse_b, delta_b, do, *, tb, name):
    S_len, W = qn.shape
    HP = W // LANES
    tb = min(tb, S_len)
    nb = S_len // tb
    scale = FOX_DH ** -0.5
    rb, cb = min(ATTN_ROWS, tb), min(ATTN_COLS, tb)

    steps = [(j, i) for j in range(nb) for i in range(j, nb)]
    tj = jnp.asarray([s[0] for s in steps], jnp.int32)
    ti = jnp.asarray([s[1] for s in steps], jnp.int32)

    def body(tj_ref, ti_ref, q_ref, k_ref, v_ref, cb0_ref, cb1_ref, cr_ref, lse_ref, dl0_ref, dl1_ref, do_ref,
             dq_ref, dk_ref, dv_ref, dcr_ref, dct_ref, dk_scr, dv_scr, dc_scr, p_scr, ds_scr):
        t = pl.program_id(1)
        j, i = tj_ref[t], ti_ref[t]

        @pl.when(t == 0)
        def _():
            dq_ref[...] = jnp.zeros(dq_ref.shape, F32)
            dct_ref[...] = jnp.zeros(dct_ref.shape, F32)

        @pl.when(i == j)
        def _():
            dk_scr[...] = jnp.zeros(dk_scr.shape, F32)
            dv_scr[...] = jnp.zeros(dv_scr.shape, F32)
            dc_scr[...] = jnp.zeros(dc_scr.shape, F32)

        def compute(diag):
            lo = _iota((1, LANES), 1) < FOX_DH
            masks = [lo, jnp.logical_not(lo)]
            row0 = pl.multiple_of(i * tb, tb)
            npiece = tb // LANES
            colsum = [[None] * npiece for _ in range(2)]

            def visible(r):
                return ((r + 1) * rb - 1) // LANES + 1 if diag else npiece

            for hh in range(2):
                for r in range(tb // rb):
                    rows = slice(r * rb, (r + 1) * rb)
                    qr = q_ref[rows, :]
                    qh = jnp.where(masks[hh], qr * scale, jnp.zeros_like(qr))
                    doh = jnp.where(masks[hh], do_ref[rows, :], 0.0).astype(BF16)
                    bq = (cb0_ref if hh == 0 else cb1_ref)[rows, :] - lse_ref[hh, rows, :]
                    dlt = (dl0_ref if hh == 0 else dl1_ref)[rows, :]
                    rsum = None
                    for c in range(tb // cb):
                        first, last = c * cb // LANES, min((c + 1) * cb // LANES, visible(r))
                        for piece in range(max(first, last), (c + 1) * cb // LANES):
                            cols = slice(piece * LANES, (piece + 1) * LANES)
                            p_scr[hh, rows, cols] = jnp.zeros((rb, LANES), BF16)
                            ds_scr[hh, rows, cols] = jnp.zeros((rb, LANES), BF16)
                        if first >= last:
                            continue
                        s2 = lax.dot_general(qh, k_ref[c * cb:(c + 1) * cb, :], _DIMS["nt"], preferred_element_type=F32)
                        dp2 = lax.dot_general(doh, v_ref[c * cb:(c + 1) * cb, :], _DIMS["nt"], preferred_element_type=F32)
                        for piece in range(first, last):
                            cols = slice(piece * LANES, (piece + 1) * LANES)
                            sub = slice(piece * LANES - c * cb, (piece + 1) * LANES - c * cb)
                            s = s2[:, sub] + bq - cr_ref[hh:hh + 1, cols]
                            if diag and (piece + 1) * LANES - 1 > r * rb:
                                keep = piece * LANES + _iota((rb, LANES), 1) <= r * rb + _iota((rb, LANES), 0)
                                s = jnp.where(keep, s, -jnp.inf)
                            p = jnp.exp(s)
                            ds = p * (dp2[:, sub] - dlt)
                            p_scr[hh, rows, cols] = p.astype(BF16)
                            ds_scr[hh, rows, cols] = ds.astype(BF16)
                            rsum = ds if rsum is None else rsum + ds
                            csum = jnp.sum(ds, axis=0, keepdims=True)
                            colsum[hh][piece] = csum if colsum[hh][piece] is None else colsum[hh][piece] + csum
                    grow = pl.ds(row0 + r * rb, rb)
                    dct_ref[grow, :] += jnp.where(_iota((1, SUBLANES), 1) == hh, jnp.sum(rsum, -1, keepdims=True), 0.0)
            k = k_ref[...]
            qf = q_ref[...]
            dof = do_ref[...]
            dq_part = jnp.zeros((tb, LANES), F32)
            for hh in range(2):
                kh = jnp.where(masks[hh], k, jnp.zeros_like(k))
                qhf = jnp.where(masks[hh], qf * scale, jnp.zeros_like(qf))
                dohf = jnp.where(masks[hh], dof, 0.0).astype(BF16)
                dv_scr[...] += lax.dot_general(p_scr[hh], dohf, _DIMS["tn"], preferred_element_type=F32)
                dk_scr[...] += lax.dot_general(ds_scr[hh], qhf, _DIMS["tn"], preferred_element_type=F32)
                dq_part = dq_part + lax.dot_general(ds_scr[hh], kh, _DIMS["nn"], preferred_element_type=F32)
                for piece in range(npiece):
                    if colsum[hh][piece] is not None:
                        dc_scr[hh:hh + 1, piece * LANES:(piece + 1) * LANES] -= colsum[hh][piece]
            dq_ref[pl.ds(row0, tb), :] += dq_part * scale

        @pl.when(i > j)
        def _():
            compute(False)

        @pl.when(i == j)
        def _():
            compute(True)

        @pl.when(i == nb - 1)
        def _():
            dk_ref[...] = dk_scr[...]
            dv_ref[...] = dv_scr[...]
            dcr_ref[...] = dc_scr[...]

    qs = pl.BlockSpec((tb, LANES), lambda h, t, tj_, ti_: (ti_[t], h))
    qs1 = pl.BlockSpec((tb, LANES), lambda h, t, tj_, ti_: (ti_[t], HP + h))
    ks = pl.BlockSpec((tb, LANES), lambda h, t, tj_, ti_: (tj_[t], h))
    crs = pl.BlockSpec((None, SUBLANES, tb), lambda h, t, tj_, ti_: (h, 0, tj_[t]))
    whole = pl.BlockSpec((S_len, LANES), lambda h, t, tj_, ti_: (0, h))
    return pl.pallas_call(
        body, name=name,
        grid_spec=pltpu.PrefetchScalarGridSpec(
            num_scalar_prefetch=2, grid=(HP, len(steps)),
            in_specs=[qs, ks, ks, qs, qs1, crs, pl.BlockSpec((2, tb, LANES), lambda h, t, tj_, ti_: (0, ti_[t], h)),
                      qs, qs1, qs],
            out_specs=[whole, ks, ks, crs, pl.BlockSpec((None, S_len, SUBLANES), lambda h, t, tj_, ti_: (h, 0, 0))],
            scratch_shapes=[pltpu.VMEM((tb, LANES), F32), pltpu.VMEM((tb, LANES), F32),
                            pltpu.VMEM((SUBLANES, tb), F32), pltpu.VMEM((2, tb, tb), BF16),
                            pltpu.VMEM((2, tb, tb), BF16)]),
        out_shape=[jax.ShapeDtypeStruct((S_len, W), F32)] * 3 + [jax.ShapeDtypeStruct((HP, SUBLANES, S_len), F32),
                                                                 jax.ShapeDtypeStruct((HP, S_len, SUBLANES), F32)],
        compiler_params=_params("parallel", "arbitrary"),
    )(tj, ti, qn, kn, vb, c_b, c_b, c_rowp, lse_b, delta_b, delta_b, do)


def _fox_post_tile(o, z):
    return o * _silu(z)


def _fox_post_fwd(o, h_main, *, tr, name):
    W = o.shape[1]

    def body(rows, consts, orows, oaccs, scr, step, blk):
        orows[0][...] = _fox_post_tile(rows[0][...], rows[1][...]).astype(BF16)

    z = (h_main, lambda tr_, n: pl.BlockSpec((tr_, W), lambda i: (i, 3)))
    return _rowcall(body, [o, z], [], [(W, BF16)], [], tr=tr, name=name)[0]


def _fox_post_bwd(o, h_main, du, w_out, *, tr, name):
    W = o.shape[1]

    def body(rows, consts, orows, oaccs, scr, step, blk):
        d_og = lax.dot_general(rows[2][...].astype(BF16), consts[0][...], _DIMS["nt"], preferred_element_type=F32)
        _, vjp = jax.vjp(_fox_post_tile, rows[0][...], rows[1][...])
        d_o, d_z = vjp(d_og)
        orows[0][...] = d_o
        orows[1][...] = d_z.astype(BF16)
        lo_rows = (_iota((LANES, LANES), 0) < FOX_DH)
        for s in range(W // LANES):
            ls = slice(s * LANES, (s + 1) * LANES)
            prod = d_o[:, ls] * rows[0][:, ls]
            orows[2][:, ls] = _raw_mm(prod, lo_rows.astype(F32), "nn", "cb")
            orows[2][:, W + s * LANES:W + (s + 1) * LANES] = _raw_mm(prod, jnp.logical_not(lo_rows).astype(F32), "nn", "cb")

    z = (h_main, lambda tr_, n: pl.BlockSpec((tr_, W), lambda i: (i, 3)))
    return _rowcall(body, [o, z, du], [w_out], [(W, F32), (W, BF16), (2 * W, F32)], [], tr=tr, name=name)


MESH_IDS = pl.DeviceIdType.MESH
N_CHIPS = 4
N_DEV = 8
_ANY = pl.BlockSpec(memory_space=pl.ANY)


def _xy_exchange(src, *, gather, name):
    def body(src_ref, out_ref, send_sems, recv_sems, local_sem):
        _xy_start(src_ref, out_ref, send_sems, recv_sems, local_sem, gather)
        _xy_wait(src_ref, out_ref, send_sems, recv_sems, local_sem, gather)

    return pl.pallas_call(
        body, name=name, in_specs=[_ANY], out_specs=_ANY,
        out_shape=_xy_out_shape(src), scratch_shapes=_xy_sems(),
    )(src)


def _xy_out_shape(src):
    return jax.ShapeDtypeStruct((N_CHIPS,) + tuple(src.shape[-2:]), src.dtype)


def _xy_sems():
    return [pltpu.SemaphoreType.DMA((N_CHIPS - 1,)), pltpu.SemaphoreType.DMA((N_CHIPS - 1,)), pltpu.SemaphoreType.DMA]


def _xy_copies(src_ref, out_ref, send_sems, recv_sems, local_sem, gather, with_arrivals=True):
    x, y, c = lax.axis_index("x"), lax.axis_index("y"), lax.axis_index("c")
    me = 2 * x + y
    peers = [(1 - x, y), (x, 1 - y), (1 - x, 1 - y)]

    def outgoing(px, py):
        return src_ref if gather else src_ref.at[2 * px + py]

    def copy(j, px, py, slot):
        return pltpu.make_async_remote_copy(
            src_ref=outgoing(px, py), dst_ref=out_ref.at[slot], send_sem=send_sems.at[j], recv_sem=recv_sems.at[j],
            device_id=(px, py, c), device_id_type=MESH_IDS)

    mine = pltpu.make_async_copy(outgoing(x, y), out_ref.at[me], local_sem)
    sends = [copy(j, px, py, me) for j, (px, py) in enumerate(peers)]
    arrivals = [copy(j, px, py, 2 * px + py) for j, (px, py) in enumerate(peers)] if with_arrivals else []
    return mine, sends, arrivals


def _xy_start(src_ref, out_ref, send_sems, recv_sems, local_sem, gather):
    mine, sends, _ = _xy_copies(src_ref, out_ref, send_sems, recv_sems, local_sem, gather, with_arrivals=False)
    mine.start()
    for cp in sends:
        cp.start()


def _xy_wait(src_ref, out_ref, send_sems, recv_sems, local_sem, gather):
    mine, sends, arrivals = _xy_copies(src_ref, out_ref, send_sems, recv_sems, local_sem, gather)
    for cp in arrivals:
        cp.wait_recv()
    for cp in sends:
        cp.wait_send()
    mine.wait()


def _c_swap(src, *, name):
    def body(src_ref, out_ref, send_sem, recv_sem):
        x, y, c = lax.axis_index("x"), lax.axis_index("y"), lax.axis_index("c")
        cp = pltpu.make_async_remote_copy(
            src_ref=src_ref, dst_ref=out_ref, send_sem=send_sem, recv_sem=recv_sem,
            device_id=(x, y, 1 - c), device_id_type=MESH_IDS)
        cp.start()
        cp.wait()

    return pl.pallas_call(
        body, name=name, in_specs=[_ANY], out_specs=_ANY,
        out_shape=jax.ShapeDtypeStruct(src.shape, src.dtype),
        scratch_shapes=[pltpu.SemaphoreType.DMA, pltpu.SemaphoreType.DMA],
    )(src)


def _all_gather8(blk, *, name):
    m_per, n = blk.shape

    def body(x_ref, out_ref, send_sems, recv_sems, local_sem):
        x, y, c = lax.axis_index("x"), lax.axis_index("y"), lax.axis_index("c")
        me, sibling = (x, y, c), (x, y, 1 - c)
        chips = [(1 - x, y), (x, 1 - y), (1 - x, 1 - y)]

        def rows(px, py, pc):
            return out_ref.at[pl.ds((4 * px + 2 * py + pc) * m_per, m_per), :]

        def copy(k, block, to, src=None):
            return pltpu.make_async_remote_copy(
                src_ref=rows(*block) if src is None else src, dst_ref=rows(*block),
                send_sem=send_sems.at[k], recv_sem=recv_sems.at[k], device_id=to, device_id_type=MESH_IDS)

        mine = pltpu.make_async_copy(x_ref, rows(*me), local_sem)
        mine.start()
        first = [copy(0, me, sibling, src=x_ref)]
        first += [copy(1 + j, me, (*chip, c), src=x_ref) for j, chip in enumerate(chips)]
        for cp in first:
            cp.start()
        passed = [copy(4 + j, (*chip, c), sibling) for j, chip in enumerate(chips)]
        for j, chip in enumerate(chips):
            copy(1 + j, (*chip, c), me).wait_recv()
            passed[j].start()
        copy(0, sibling, me).wait_recv()
        for j, chip in enumerate(chips):
            copy(4 + j, (*chip, 1 - c), me).wait_recv()
        for cp in first + passed:
            cp.wait_send()
        mine.wait()

    return pl.pallas_call(
        body, name=name,
        out_shape=jax.ShapeDtypeStruct((N_DEV * m_per, n), blk.dtype),
        in_specs=[pl.BlockSpec(memory_space=pltpu.VMEM)], out_specs=pl.BlockSpec(memory_space=pltpu.VMEM),
        scratch_shapes=[pltpu.SemaphoreType.DMA((7,)), pltpu.SemaphoreType.DMA((7,)), pltpu.SemaphoreType.DMA],
    )(blk)


def _sum_slots(parts, *, tr, name):
    n, R, _ = parts.shape
    pack = 2 * SUBLANES
    tr = max(t for t in range(pack, min(tr, R) + 1, pack) if R % t == 0) if R % pack == 0 else R

    def body(p_ref, o_ref):
        tot = p_ref[0].astype(F32)
        for s in range(1, n):
            tot = tot + p_ref[s].astype(F32)
        o_ref[...] = tot

    return pl.pallas_call(
        body, name=name, grid=(R // tr,),
        in_specs=[pl.BlockSpec((n, tr, LANES), lambda i: (0, i, 0))], out_specs=pl.BlockSpec((tr, LANES), lambda i: (i, 0)),
        out_shape=jax.ShapeDtypeStruct((R, LANES), F32), compiler_params=_params("parallel"),
    )(parts)


def _adamw(w, g_parts, m, v, *, name):
    shape = w.shape
    as2d = lambda a: a.reshape(-1, shape[-1])
    w2, m2, v2 = as2d(w), as2d(m), as2d(v)
    gs = [as2d(g) for g in g_parts]
    R, C = w2.shape
    tr = R
    while tr * C * 4 > (1 << 20) and tr % 2 == 0 and (tr // 2) % SUBLANES == 0:
        tr //= 2
    ng = len(gs)

    def body(*refs):
        w_ref, m_ref, v_ref = refs[:3]
        g_refs = refs[3:3 + ng]
        go_ref, d_ref, mo_ref, vo_ref = refs[3 + ng:]
        g = g_refs[0][...]
        for r in g_refs[1:]:
            g = g + r[...]
        mn = ADAM_B1 * m_ref[...] + (1.0 - ADAM_B1) * g
        vn = ADAM_B2 * v_ref[...] + (1.0 - ADAM_B2) * jnp.square(g)
        m_hat = mn / (1.0 - ADAM_B1 ** ADAM_STEP)
        v_hat = vn / (1.0 - ADAM_B2 ** ADAM_STEP)
        go_ref[...] = g
        d_ref[...] = -ADAM_LR * (m_hat / (jnp.sqrt(v_hat) + ADAM_EPS) + ADAM_WD * w_ref[...])
        mo_ref[...] = mn
        vo_ref[...] = vn

    spec = pl.BlockSpec((tr, C), lambda i: (i, 0))
    outs = pl.pallas_call(
        body, name=name, grid=(R // tr,), in_specs=[spec] * (3 + ng), out_specs=[spec] * 4,
        out_shape=[jax.ShapeDtypeStruct((R, C), F32)] * 4, compiler_params=_params("parallel"),
    )(w2, m2, v2, *gs)
    return tuple(o.reshape(shape) for o in outs)


TR = 256
ATTN_TILE = 1024
ATTN_ROWS = 256
ATTN_COLS = 256


def _mm_nn(a, b, name, **kw):
    return _matmul(a, b, tm=2048, tn=1024, tk=1024, name=name, **kw)


def _mm_tn(a, b, name, **kw):
    return _matmul(a, b, ta=True, tm=1024, tn=2048, tk=512, name=name, **kw)


def _c_rows(c, Hf):
    S_len = c.shape[0]
    ct = c[:, :Hf].T.reshape(Hf // 2, 2, S_len)
    return jnp.pad(ct, ((0, 0), (0, SUBLANES - 2), (0, 0)))


def _local_step(x, p, target, wts, late_weights=None, early_grads=None):
    L = len(wts["ln_g"])
    alpha = (2 * L) ** 0.25
    Hg = wts["gdn_a_log"][0].shape[-1]
    Hf = wts["fox_b_f"][0].shape[-1]
    Wg_ = Hg * GDN_DK
    Wf_ = Hf * FOX_DH
    saved = []
    for i in range(L):
        j = i // 2
        sv = {"x": x}
        if i % 2 == 0:
            w_in = wts["gdn_w_in"][j]
            wm, ws = w_in[:, :4 * Wg_], _pad_lanes(w_in[:, 4 * Wg_:])
            cw8 = jnp.pad(wts["gdn_conv_w"][j], ((0, SUBLANES - GDN_CONV), (0, 0)))
            alog = _pad_lanes(_row(wts["gdn_a_log"][j]), offset=Hg)
            dtb = _pad_lanes(_row(wts["gdn_dt_bias"][j]), offset=Hg)
            hm, hs = _in_proj_fwd(x, wm, ws, tr=2 * TR, name=f"gdn{j}_in")
            q, k, v, bg = _gdn_pre_fwd(hm, hs, cw8, alog, dtb, Hg, tr=TR, name=f"gdn{j}_pre")
            if i == 0 and late_weights is not None:
                o, states, tinv, got = _gdn_rule_fwd(q, k, v, bg, Hg, name=f"gdn{j}_rule", carry=(late_weights[0], True))
                for (wname_, idx), arr in late_weights[1](got).items():
                    wts[wname_][idx] = arr
            else:
                o, states, tinv = _gdn_rule_fwd(q, k, v, bg, Hg, name=f"gdn{j}_rule")
            on = _gdn_post_fwd(o, hm, wts["gdn_norm_g"][j], Hg, tr=TR, name=f"gdn{j}_post")
            w_out = wts["gdn_w_out"][j]
            sv.update(wm=wm, ws=ws, cw8=cw8, alog=alog, dtb=dtb, hm=hm, hs=hs, q=q, k=k, v=v, bg=bg, o=o,
                      states=states, tinv=tinv, on=on)
        else:
            w_in = wts["fox_w_in"][j]
            wm, ws = w_in[:, :4 * Wf_], _pad_lanes(w_in[:, 4 * Wf_:])
            gq2 = _row(jnp.tile(wts["fox_q_norm_g"][j], 2))
            gk2 = _row(jnp.tile(wts["fox_k_norm_g"][j], 2))
            bf = _pad_lanes(_row(wts["fox_b_f"][j]))
            hm, hs = _in_proj_fwd(x, wm, ws, tr=2 * TR, name=f"fox{j}_in")
            qn, kn, vb, c, c_b = _fox_pre_fwd(hm, hs, gq2, gk2, bf, Hf, tr=TR, name=f"fox{j}_pre")
            c_rowp = _c_rows(c, Hf)
            o, lse_b = _fox_attn_fwd(qn, kn, vb, c_b, c_rowp, tb=ATTN_TILE, name=f"fox{j}_attn")
            on = _fox_post_fwd(o, hm, tr=TR, name=f"fox{j}_post")
            w_out = wts["fox_w_out"][j]
            sv.update(wm=wm, ws=ws, gq2=gq2, gk2=gk2, bf=bf, hm=hm, hs=hs, qn=qn, kn=kn, vb=vb, c_b=c_b,
                      c_rowp=c_rowp, o=o, lse_b=lse_b, on=on)
        y, x_ln = _out_ln_fwd(on, w_out, x, wts["ln_g"][i], wts["ln_b"][i], alpha, tr=2 * TR, name=f"out_ln{i}")
        pp = _mm_nn(p[i], wts["ple_w_proj"][i], f"ple{i}_proj")
        gp, x_out = _gate_mix_fwd(x_ln, wts["ple_w_gate"][i], pp, tr=2 * TR, name=f"ple{i}_gate_mix")
        sv.update(y=y, x_ln=x_ln, gp=gp, pp=pp)
        saved.append(sv)
        x = x_out

    loss_row, dx = _loss_fwd_bwd(x, target, tr=TR, name="loss")

    g = {n: [None] * len(wts[n]) for n in wts}
    got_early = None
    for i in reversed(range(L)):
        j = i // 2
        sv = saved[i]
        d_pre, d_pp, du, g["ln_g"][i], g["ln_b"][i] = _ple_ln_bwd(
            dx, sv["gp"], sv["pp"], wts["ple_w_gate"][i], sv["x"], sv["y"], wts["ln_g"][i], wts["ln_b"][i], alpha,
            tr=2 * TR, name=f"ple_ln{i}_bwd")
        g["ple_w_gate"][i] = _mm_tn(sv["x_ln"], d_pre, f"ple{i}_gate_dw")
        g["ple_w_proj"][i] = _mm_tn(p[i], d_pp, f"ple{i}_proj_dw")
        if i % 2 == 0:
            g["gdn_w_out"][j] = _mm_tn(sv["on"], du, f"gdn{j}_out_dw")
            d_o, d_z, d_ng = _gdn_post_bwd(sv["o"], sv["hm"], wts["gdn_norm_g"][j], du, wts["gdn_w_out"][j], Hg,
                                           tr=TR, name=f"gdn{j}_post_bwd")
            rule_args = (sv["q"], sv["k"], sv["v"], sv["bg"], sv["states"], sv["tinv"], d_o, Hg)
            if i == 0 and early_grads is not None:
                dq, dk, dv, dbg, got_early = _gdn_rule_bwd(*rule_args, name=f"gdn{j}_rule_bwd",
                                                           carry=(early_grads(g), False))
            else:
                dq, dk, dv, dbg = _gdn_rule_bwd(*rule_args, name=f"gdn{j}_rule_bwd")
            d_hm, d_hs, d_cw, d_al, d_dtb = _gdn_pre_bwd(sv["hm"], sv["hs"], sv["cw8"], sv["alog"], sv["dtb"],
                                                         dq, dk, dv, dbg, d_z, Hg, tr=TR, name=f"gdn{j}_pre_bwd")
            g["gdn_norm_g"][j] = d_ng[0]
            g["gdn_conv_w"][j] = d_cw[:GDN_CONV]
            g["gdn_a_log"][j] = d_al[0, Hg:2 * Hg]
            g["gdn_dt_bias"][j] = d_dtb[0, Hg:2 * Hg]
            wname, nsmall = "gdn_w_in", 2 * Hg
        else:
            g["fox_w_out"][j] = _mm_tn(sv["on"], du, f"fox{j}_out_dw")
            d_o, d_z, delta_b = _fox_post_bwd(sv["o"], sv["hm"], du, wts["fox_w_out"][j], tr=TR, name=f"fox{j}_post_bwd")
            dqn, dkn, dvv, dcr, dct = _fox_attn_bwd(sv["qn"], sv["kn"], sv["vb"], sv["c_b"], sv["c_rowp"], sv["lse_b"],
                                                    delta_b, d_o, tb=ATTN_TILE, name=f"fox{j}_attn_bwd")
            dc = _pad_lanes(dcr[:, :2, :].reshape(Hf, -1).T + dct[:, :, :2].transpose(1, 0, 2).reshape(-1, Hf))
            d_hm, d_hs, d_gq, d_gk, d_bf = _fox_pre_bwd(sv["hm"], sv["hs"], sv["gq2"], sv["gk2"], sv["bf"],
                                                        dqn, dkn, dvv, d_z, dc, Hf, tr=TR, name=f"fox{j}_pre_bwd")
            g["fox_q_norm_g"][j] = d_gq[0, :FOX_DH] + d_gq[0, FOX_DH:]
            g["fox_k_norm_g"][j] = d_gk[0, :FOX_DH] + d_gk[0, FOX_DH:]
            g["fox_b_f"][j] = d_bf[0, :Hf]
            wname, nsmall = "fox_w_in", Hf
        dwm = _mm_tn(sv["x"], d_hm, f"{wname}{j}_main_dw")
        dws = _mm_tn(sv["x"], d_hs, f"{wname}{j}_small_dw")
        g[wname][j] = jnp.concatenate([dwm, dws[:, :nsmall]], axis=1)
        dx = _in_proj_bwd_x(d_hm, d_hs, du, sv["wm"], sv["ws"], alpha, tr=2 * TR, name=f"{wname}{j}_dx")
    return loss_row, dx, g, got_early


_SHARDED = (("ple_w_gate", 1), ("ple_w_proj", 2), ("gdn_w_in", 2), ("gdn_conv_w", 2), ("gdn_w_out", 1),
            ("fox_w_in", 2), ("fox_w_out", 1))
_SHARD_AXIS = dict(_SHARDED)
_REPLICATED = ("ln_g", "ln_b", "gdn_a_log", "gdn_dt_bias", "gdn_norm_g", "fox_b_f", "fox_q_norm_g", "fox_k_norm_g")
_EXACT = ("gdn_conv_w",)
_ORDER = ("ln_g", "ln_b", "ple_w_gate", "ple_w_proj", "gdn_w_in", "gdn_conv_w", "gdn_a_log", "gdn_dt_bias",
          "gdn_norm_g", "gdn_w_out", "fox_w_in", "fox_b_f", "fox_q_norm_g", "fox_k_norm_g", "fox_w_out")
_FIRST_WEIGHTS = (("gdn_w_in", 0), ("gdn_conv_w", 0))
_LAST_GRADS = (("ple_w_gate", 0), ("ple_w_proj", 0), ("gdn_w_in", 0), ("gdn_conv_w", 0), ("gdn_w_out", 0))
PACK_ROWS = 2 * SUBLANES


def _as_rows(a):
    rows = a.reshape(-1, LANES)
    return jnp.pad(rows, ((0, -rows.shape[0] % PACK_ROWS), (0, 0)))


def _n_elements(shape):
    n = 1
    for d in shape:
        n *= d
    return n


def _pack_weights(local, items):
    parts = []
    for name, idx in items:
        w = local[name][idx]
        parts.append(_as_rows(lax.bitcast_convert_type(w, BF16) if name in _EXACT else w.astype(BF16)))
    return jnp.concatenate(parts, axis=0), [q.shape[0] for q in parts]


def _unpack_weights(got, local, items, sizes):
    out, r0 = {}, 0
    for (name, idx), nrow in zip(items, sizes):
        shp = tuple(local[name].shape[1:])
        n_el = _n_elements(shp) * (2 if name in _EXACT else 1)
        seg = got[:, r0:r0 + nrow].reshape(N_CHIPS, -1)[:, :n_el]
        r0 += nrow
        if name in _EXACT:
            blocks = lax.bitcast_convert_type(seg.reshape((N_CHIPS,) + shp + (2,)), F32)
        else:
            blocks = seg.reshape((N_CHIPS,) + shp)
        axis = _SHARD_AXIS[name] - 1
        joined = shp[:axis] + (N_CHIPS * shp[axis],) + shp[axis + 1:]
        out[(name, idx)] = jnp.moveaxis(blocks, 0, axis).reshape(joined)
    return out


def _pack_grads(g, items):
    parts = []
    for name, idx in items:
        gfull = g[name][idx]
        axis = _SHARD_AXIS[name] - 1
        shp = gfull.shape
        split = shp[:axis] + (N_CHIPS, shp[axis] // N_CHIPS) + shp[axis + 1:]
        rows = jnp.moveaxis(gfull.reshape(split), axis, 0).astype(BF16).reshape(N_CHIPS, -1, LANES)
        parts.append(jnp.pad(rows, ((0, 0), (0, -rows.shape[1] % PACK_ROWS), (0, 0))))
    return jnp.concatenate(parts, axis=1), [q.shape[1] for q in parts]


def _unpack_grads(flat, local, items, sizes):
    out, r0 = {}, 0
    for (name, idx), nrow in zip(items, sizes):
        shp = tuple(local[name].shape[1:])
        out[(name, idx)] = flat[r0:r0 + nrow].reshape(-1)[:_n_elements(shp)].reshape(shp)
        r0 += nrow
    return out


def _reduce_replicated(grads, loss_part):
    rows = [_pad_lanes(jnp.reshape(loss_part, (1, 1)))]
    for name in _REPLICATED:
        gr = grads[name]
        rows.append(gr.reshape(-1, LANES) if gr.shape[-1] % LANES == 0 else _pad_lanes(gr))
    sizes = [r.shape[0] for r in rows]
    blk = jnp.concatenate(rows, axis=0)
    nrow = blk.shape[0]
    npad = -nrow % SUBLANES
    blk = jnp.pad(blk, ((0, npad), (0, 0)))
    allb = _all_gather8(blk, name="gather_small_grads").reshape(N_DEV, nrow + npad, LANES)
    tot = _sum_slots(allb, tr=nrow + npad, name="sum_small_grads")
    out, r0 = {}, sizes[0]
    loss = tot[0, 0]
    for name, n in zip(_REPLICATED, sizes[1:]):
        gr = grads[name]
        seg = tot[r0:r0 + n]
        out[name] = seg.reshape(gr.shape) if gr.shape[-1] % LANES == 0 else seg[:, :gr.shape[-1]]
        r0 += n
    return loss, out


def kernel(x, p, ln_g, ln_b, ple_w_gate, ple_w_proj, gdn_w_in, gdn_conv_w, gdn_a_log, gdn_dt_bias, gdn_norm_g, gdn_w_out, fox_w_in, fox_b_f, fox_q_norm_g, fox_k_norm_g, fox_w_out, loss_target, m_ln_g, m_ln_b, m_ple_w_gate, m_ple_w_proj, m_gdn_w_in, m_gdn_conv_w, m_gdn_a_log, m_gdn_dt_bias, m_gdn_norm_g, m_gdn_w_out, m_fox_w_in, m_fox_b_f, m_fox_q_norm_g, m_fox_k_norm_g, m_fox_w_out, v_ln_g, v_ln_b, v_ple_w_gate, v_ple_w_proj, v_gdn_w_in, v_gdn_conv_w, v_gdn_a_log, v_gdn_dt_bias, v_gdn_norm_g, v_gdn_w_out, v_fox_w_in, v_fox_b_f, v_fox_q_norm_g, v_fox_k_norm_g, v_fox_w_out):
    local = dict(ln_g=ln_g, ln_b=ln_b, ple_w_gate=ple_w_gate, ple_w_proj=ple_w_proj, gdn_w_in=gdn_w_in,
                 gdn_conv_w=gdn_conv_w, gdn_a_log=gdn_a_log, gdn_dt_bias=gdn_dt_bias, gdn_norm_g=gdn_norm_g,
                 gdn_w_out=gdn_w_out, fox_w_in=fox_w_in, fox_b_f=fox_b_f, fox_q_norm_g=fox_q_norm_g,
                 fox_k_norm_g=fox_k_norm_g, fox_w_out=fox_w_out)
    mom_m = dict(ln_g=m_ln_g, ln_b=m_ln_b, ple_w_gate=m_ple_w_gate, ple_w_proj=m_ple_w_proj, gdn_w_in=m_gdn_w_in,
                 gdn_conv_w=m_gdn_conv_w, gdn_a_log=m_gdn_a_log, gdn_dt_bias=m_gdn_dt_bias, gdn_norm_g=m_gdn_norm_g,
                 gdn_w_out=m_gdn_w_out, fox_w_in=m_fox_w_in, fox_b_f=m_fox_b_f, fox_q_norm_g=m_fox_q_norm_g,
                 fox_k_norm_g=m_fox_k_norm_g, fox_w_out=m_fox_w_out)
    mom_v = dict(ln_g=v_ln_g, ln_b=v_ln_b, ple_w_gate=v_ple_w_gate, ple_w_proj=v_ple_w_proj, gdn_w_in=v_gdn_w_in,
                 gdn_conv_w=v_gdn_conv_w, gdn_a_log=v_gdn_a_log, gdn_dt_bias=v_gdn_dt_bias, gdn_norm_g=v_gdn_norm_g,
                 gdn_w_out=v_gdn_w_out, fox_w_in=v_fox_w_in, fox_b_f=v_fox_b_f, fox_q_norm_g=v_fox_q_norm_g,
                 fox_k_norm_g=v_fox_k_norm_g, fox_w_out=v_fox_w_out)

    items = [(name, idx) for name, _ in _SHARDED for idx in range(local[name].shape[0])]
    w_first = [it for it in items if it in _FIRST_WEIGHTS]
    w_later = [it for it in items if it not in _FIRST_WEIGHTS]
    g_early = [it for it in items if it not in _LAST_GRADS]
    g_last = [it for it in items if it in _LAST_GRADS]

    wts = {name: [None] * local[name].shape[0] for name, _ in _SHARDED}
    for name in _REPLICATED:
        wts[name] = local[name]
    packed, sizes = _pack_weights(local, w_first)
    got = _xy_exchange(packed, gather=True, name="gather_weights_first")
    for (name, idx), arr in _unpack_weights(got, local, w_first, sizes).items():
        wts[name][idx] = arr
    packed_later, sizes_later = _pack_weights(local, w_later)
    early_sizes = []

    def pack_early(g):
        packed_g, sz = _pack_grads(g, g_early)
        early_sizes.extend(sz)
        return packed_g

    loss_row, dx, g, got_early = _local_step(
        x[0], p[:, 0], loss_target[0], wts,
        late_weights=(packed_later, lambda res: _unpack_weights(res, local, w_later, sizes_later)),
        early_grads=pack_early)

    loss, small = _reduce_replicated({name: jnp.stack(g[name]) for name in _REPLICATED}, jnp.sum(loss_row))
    packed_last, last_sizes = _pack_grads(g, g_last)
    got_last = _xy_exchange(packed_last, gather=False, name="exchange_grads_last")
    mine, other = {}, {}
    for tag, res, its, szs in (("early", got_early, g_early, early_sizes), ("last", got_last, g_last, last_sizes)):
        part = _sum_slots(res, tr=4096, name=f"sum_grads_{tag}")
        sib = _c_swap(part, name=f"swap_grads_{tag}")
        mine.update(_unpack_grads(part, local, its, szs))
        other.update(_unpack_grads(sib, local, its, szs))

    outs = {}
    for name in _ORDER:
        if name in _SHARD_AXIS:
            n_layers = local[name].shape[0]
            parts = [jnp.stack([mine[(name, i)] for i in range(n_layers)]),
                     jnp.stack([other[(name, i)] for i in range(n_layers)])]
        else:
            parts = [small[name]]
        outs[name] = _adamw(local[name], parts, mom_m[name], mom_v[name], name=f"adamw_{name}")
    return (loss, dx[None], *[outs[n][0] for n in _ORDER], *[outs[n][1] for n in _ORDER],
            *[outs[n][2] for n in _ORDER], *[outs[n][3] for n in _ORDER])
```

```python
import functools

import jax
import jax.numpy as jnp
from jax import lax
from jax.experimental import pallas as pl
from jax.experimental.pallas import tpu as pltpu

F32 = jnp.float32
BF16 = jnp.bfloat16

LANES = 128
SUBLANES = 8
VMEM_LIMIT_BYTES = 56 * 1024 * 1024

GDN_DK = 128
GDN_CHUNK = 64
GDN_CONV = 4
FOX_DH = 64
LN_EPS = 1e-5
RMS_EPS = 1e-6

ADAM_LR = 0.001
ADAM_B1 = 0.9
ADAM_B2 = 0.999
ADAM_EPS = 1e-08
ADAM_WD = 0.01
ADAM_STEP = 10

_DIMS = {"nn": (((1,), (0,)), ((), ())), "nt": (((1,), (1,)), ((), ())), "tn": (((0,), (0,)), ((), ()))}


def _params(*sem):
    return pltpu.CompilerParams(dimension_semantics=sem, vmem_limit_bytes=VMEM_LIMIT_BYTES)


def _split(a, terms):
    out = []
    rest = a.astype(F32)
    for t in range(terms):
        piece = rest.astype(BF16)
        out.append(piece)
        if t + 1 < terms:
            rest = rest - piece.astype(F32)
    return out


def _raw_mm(a, b, form, mode):
    dot = lambda x, y: lax.dot_general(x, y, _DIMS[form], preferred_element_type=F32)
    if mode == "b":
        return dot(a.astype(BF16), b.astype(BF16))
    if mode == "x3":
        (ah, al), (bh, bl) = _split(a, 2), _split(b, 2)
        return dot(ah, bh) + (dot(ah, bl) + dot(al, bh))
    if mode == "ca":
        ac = a.astype(BF16)
        b1, b2, b3 = _split(b, 3)
        return dot(ac, b1) + (dot(ac, b2) + dot(ac, b3))
    assert mode == "cb", mode
    bc = b.astype(BF16)
    a1, a2, a3 = _split(a, 3)
    return dot(a1, bc) + (dot(a2, bc) + dot(a3, bc))


@functools.partial(jax.custom_vjp, nondiff_argnums=(2, 3))
def _mm(a, b, form, mode):
    return _raw_mm(a, b, form, mode)


def _mm_fwd(a, b, form, mode):
    return _raw_mm(a, b, form, mode), (a, b)


def _mm_bwd(form, mode, res, g):
    a, b = res
    flip = {"b": "b", "x3": "x3", "ca": "cb", "cb": "ca"}[mode]
    if form == "nn":
        da, db = (lambda: _mm(g, b, "nt", mode)), (lambda: _mm(a, g, "tn", mode))
    elif form == "nt":
        da, db = (lambda: _mm(g, b, "nn", mode)), (lambda: _mm(g, a, "tn", flip))
    else:
        da, db = (lambda: _mm(b, g, "nt", flip)), (lambda: _mm(a, g, "nn", mode))
    return (jnp.zeros_like(a) if mode == "ca" else da()), (jnp.zeros_like(b) if mode == "cb" else db())


_mm.defvjp(_mm_fwd, _mm_bwd)


@jax.custom_vjp
def _tri_inv(Ls):
    C = Ls[0].shape[0]
    eye = (_iota((C, C), 0) == _iota((C, C), 1)).astype(F32)
    X = [eye - L for L in Ls]
    P = [_raw_mm(L, L, "nn", "x3") for L in Ls]
    n_sq = max(1, (C - 1).bit_length() - 1)
    for it in range(n_sq):
        XP = [_raw_mm(x, p, "nn", "x3") for x, p in zip(X, P)]
        if it < n_sq - 1:
            P = [_raw_mm(p, p, "nn", "x3") for p in P]
        X = [x + xp for x, xp in zip(X, XP)]
    return X


def _tri_inv_fwd(Ls):
    Ts = _tri_inv(Ls)
    return Ts, Ts


def _tri_inv_bwd(Ts, dTs):
    Ms = [_raw_mm(dT, T, "nt", "x3") for dT, T in zip(dTs, Ts)]
    return ([-_raw_mm(T, M, "tn", "x3") for T, M in zip(Ts, Ms)],)


_tri_inv.defvjp(_tri_inv_fwd, _tri_inv_bwd)


@jax.custom_vjp
def _tri_inv_known(Ls, Ts):
    return Ts


def _tri_inv_known_fwd(Ls, Ts):
    return Ts, Ts


def _tri_inv_known_bwd(Ts, dTs):
    return _tri_inv_bwd(Ts, dTs)[0], [jnp.zeros_like(T) for T in Ts]


_tri_inv_known.defvjp(_tri_inv_known_fwd, _tri_inv_known_bwd)


def _silu(x):
    return x * jax.nn.sigmoid(x)


def _softplus(x):
    return jnp.maximum(x, 0.0) + jnp.log1p(jnp.exp(-jnp.abs(x)))


def _iota(shape, dim):
    return lax.broadcasted_iota(jnp.int32, shape, dim)


def _matmul(a, b, *, ta=False, tb=False, out_dtype=F32, add=None, add_scale=1.0, tm=512, tn=512, tk=512, name):
    if ta:
        K, M = a.shape
    else:
        M, K = a.shape
    if tb:
        N, K2 = b.shape
    else:
        K2, N = b.shape
    assert K == K2, (a.shape, b.shape, ta, tb)
    tm, tn, tk = min(tm, M), min(tn, N), min(tk, K)
    assert M % tm == 0 and N % tn == 0 and K % tk == 0, (M, N, K, tm, tn, tk)
    nk = K // tk
    form = ("t" if ta else "n") + ("t" if tb else "n")
    dims = (((0 if ta else 1,), (1 if tb else 0,)), ((), ()))
    del form
    a_spec = pl.BlockSpec((tk, tm), lambda i, j, k: (k, i)) if ta else pl.BlockSpec((tm, tk), lambda i, j, k: (i, k))
    b_spec = pl.BlockSpec((tn, tk), lambda i, j, k: (j, k)) if tb else pl.BlockSpec((tk, tn), lambda i, j, k: (k, j))
    o_spec = pl.BlockSpec((tm, tn), lambda i, j, k: (i, j))
    has_add = add is not None

    def body(*refs):
        a_ref, b_ref = refs[:2]
        add_ref = refs[2] if has_add else None
        o_ref = refs[3] if has_add else refs[2]
        acc_ref = refs[-1] if nk > 1 else None
        k = pl.program_id(2)
        part = lax.dot_general(a_ref[...].astype(BF16), b_ref[...].astype(BF16), dims, preferred_element_type=F32)

        def finish(total):
            if has_add:
                total = total + add_scale * add_ref[...].astype(F32)
            o_ref[...] = total.astype(o_ref.dtype)

        if nk == 1:
            finish(part)
        else:
            @pl.when(k == 0)
            def _():
                acc_ref[...] = part

            @pl.when(jnp.logical_and(k > 0, k < nk - 1))
            def _():
                acc_ref[...] += part

            @pl.when(k == nk - 1)
            def _():
                finish(acc_ref[...] + part)

    in_specs = [a_spec, b_spec] + ([o_spec] if has_add else [])
    args = (a, b) + ((add,) if has_add else ())
    return pl.pallas_call(
        body, name=name, grid=(M // tm, N // tn, nk),
        in_specs=in_specs, out_specs=o_spec,
        out_shape=jax.ShapeDtypeStruct((M, N), out_dtype),
        scratch_shapes=[pltpu.VMEM((tm, tn), F32)] if nk > 1 else [],
        compiler_params=_params("parallel", "parallel", "arbitrary"),
    )(*args)


def _rowcall(body_fn, rows, consts, out_rows, out_accs, *, tr, name, reverse=False, scratch=()):
    def arr_spec(r):
        return r if isinstance(r, tuple) else (r, None)

    S = arr_spec(rows[0])[0].shape[0]
    tr = min(tr, S)
    assert S % tr == 0
    n = S // tr
    ridx = (lambda i: (n - 1 - i, 0)) if reverse else (lambda i: (i, 0))
    in_specs, args = [], []
    for r in rows:
        arr, spec = arr_spec(r)
        args.append(arr)
        in_specs.append(spec(tr, n) if spec is not None else pl.BlockSpec((tr, arr.shape[1]), ridx))
    for c in consts:
        args.append(c)
        in_specs.append(pl.BlockSpec(c.shape, lambda i: (0, 0)))
    out_specs, out_shape = [], []
    for (ncol, dt) in out_rows:
        out_specs.append(pl.BlockSpec((tr, ncol), ridx))
        out_shape.append(jax.ShapeDtypeStruct((S, ncol), dt))
    for shp in out_accs:
        out_specs.append(pl.BlockSpec(shp, lambda i: (0, 0)))
        out_shape.append(jax.ShapeDtypeStruct(shp, F32))
    nr, nc, no, na = len(rows), len(consts), len(out_rows), len(out_accs)

    def kernel(*refs):
        row_refs = refs[:nr]
        const_refs = refs[nr:nr + nc]
        orow_refs = refs[nr + nc:nr + nc + no]
        oacc_refs = refs[nr + nc + no:nr + nc + no + na]
        scr = refs[nr + nc + no + na:]
        step = pl.program_id(0)
        blk = (n - 1 - step) if reverse else step

        @pl.when(step == 0)
        def _():
            for acc in oacc_refs:
                acc[...] = jnp.zeros(acc.shape, F32)

        body_fn(row_refs, const_refs, orow_refs, oacc_refs, scr, step, blk)

    outs = pl.pallas_call(
        kernel, name=name, grid=(n,), in_specs=in_specs, out_specs=out_specs, out_shape=out_shape,
        scratch_shapes=list(scratch), compiler_params=_params("arbitrary"),
    )(*args)
    return outs


def _row(v):
    return v.astype(F32).reshape(1, -1)


def _pad_lanes(v, width=LANES, offset=0):
    pad = [(0, 0)] * (v.ndim - 1) + [(offset, width - offset - v.shape[-1])]
    return jnp.pad(v, pad)


def _ln_tile(x, y, g, b, alpha):
    u = alpha * x + y
    mu = jnp.mean(u, -1, keepdims=True)
    d = u - mu
    var = jnp.mean(d * d, -1, keepdims=True)
    return d * lax.rsqrt(var + LN_EPS) * g + b


def _in_proj_fwd(x, wm, ws, *, tr, name):
    def body(rows, consts, orows, oaccs, scr, step, blk):
        xb = rows[0][...].astype(BF16)
        orows[0][...] = lax.dot_general(xb, consts[0][...], _DIMS["nn"], preferred_element_type=F32)
        orows[1][...] = lax.dot_general(xb, consts[1][...], _DIMS["nn"], preferred_element_type=F32)

    return _rowcall(body, [x], [wm, ws], [(wm.shape[1], F32), (ws.shape[1], F32)], [], tr=tr, name=name)


def _in_proj_bwd_x(d_hm, d_hs, du, wm, ws, alpha, *, tr, name):
    D = du.shape[1]

    def body(rows, consts, orows, oaccs, scr, step, blk):
        acc = lax.dot_general(rows[0][...], consts[0][...], _DIMS["nt"], preferred_element_type=F32)
        acc = acc + lax.dot_general(rows[1][...], consts[1][...], _DIMS["nt"], preferred_element_type=F32)
        orows[0][...] = acc + alpha * rows[2][...]

    return _rowcall(body, [d_hm, d_hs, du], [wm, ws], [(D, F32)], [], tr=tr, name=name)[0]


def _out_ln_fwd(on, w_out, x, g, b, alpha, *, tr, name):
    D = x.shape[1]

    def body(rows, consts, orows, oaccs, scr, step, blk):
        y = lax.dot_general(rows[0][...].astype(BF16), consts[0][...], _DIMS["nn"], preferred_element_type=F32)
        orows[0][...] = y
        orows[1][...] = _ln_tile(rows[1][...], y, consts[1][...], consts[2][...], alpha)

    return _rowcall(body, [on, x], [w_out, _row(g), _row(b)], [(D, F32), (D, F32)], [], tr=tr, name=name)


def _gate_mix_fwd(x_ln, w_gate, pp, *, tr, name):
    D = x_ln.shape[1]

    def body(rows, consts, orows, oaccs, scr, step, blk):
        a = rows[0][...]
        gp = lax.dot_general(a.astype(BF16), consts[0][...], _DIMS["nn"], preferred_element_type=F32)
        orows[0][...] = gp
        orows[1][...] = a + jax.nn.sigmoid(gp) * rows[1][...]

    return _rowcall(body, [x_ln, pp], [w_gate], [(D, F32), (D, F32)], [], tr=tr, name=name)


def _ple_ln_bwd(dxo, gp, pp, w_gate, x, y, g, b, alpha, *, tr, name):
    D = x.shape[1]

    def body(rows, consts, orows, oaccs, scr, step, blk):
        d = rows[0][...]
        s = jax.nn.sigmoid(rows[1][...])
        d_pre = (d * rows[2][...] * s * (1.0 - s)).astype(BF16)
        orows[0][...] = d_pre
        orows[1][...] = (d * s).astype(BF16)
        ct = d + lax.dot_general(d_pre, consts[0][...], _DIMS["nt"], preferred_element_type=F32)
        xv, yv = rows[3][...], rows[4][...]
        _, vjp = jax.vjp(lambda yy, gg, bb: _ln_tile(xv, yy, gg, bb, alpha), yv, consts[1][...], consts[2][...])
        du, dg, db = vjp(ct)
        orows[2][...] = du
        oaccs[0][...] += dg
        oaccs[1][...] += db

    d_pre, d_pp, du, dg, db = _rowcall(body, [dxo, gp, pp, x, y], [w_gate, _row(g), _row(b)],
                                       [(D, BF16), (D, BF16), (D, F32)], [(1, D), (1, D)], tr=tr, name=name)
    return d_pre, d_pp, du, dg[0], db[0]


def _loss_fwd_bwd(xf, target, *, tr, name):
    D = xf.shape[1]

    def body(rows, consts, orows, oaccs, scr, step, blk):
        err = rows[0][...] - rows[1][...]
        orows[0][...] = err * (1.0 / D)
        part = jnp.sum(err * err, axis=0, keepdims=True) * (0.5 / D)
        oaccs[0][...] += part

    dx, lrow = _rowcall(body, [xf, target], [], [(D, F32)], [(1, D)], tr=tr, name=name)
    return lrow, dx


def _gdn_qk_tile(c):
    y = _silu(c)
    return y * lax.rsqrt(jnp.sum(y * y, -1, keepdims=True) + RMS_EPS)


def _make_bg_fn(H):
    def fn(hs, alog, dtb):
        lane = _iota((1, LANES), 1)
        beta = jax.nn.sigmoid(hs)
        g = -jnp.exp(alog) * _softplus(hs + dtb)
        return jnp.where(lane < H, beta, jnp.where(lane < 2 * H, g, 0.0))
    return fn


def _halo_spec(ncol):
    def make(tr, n):
        per = tr // SUBLANES
        return pl.BlockSpec((SUBLANES, ncol), lambda i: (jnp.maximum(i * per - 1, 0), 0))
    return make


def _halo_spec_rev(ncol):
    def make(tr, n):
        per = tr // SUBLANES
        return pl.BlockSpec((SUBLANES, ncol), lambda i: (jnp.maximum((n - 1 - i) * per - 1, 0), 0))
    return make


def _gdn_pre_fwd(h_main, h_small, conv_w8, alog_row, dtb_row, H, *, tr, name):
    W = H * GDN_DK
    C3 = 3 * W
    bg_fn = _make_bg_fn(H)

    def body(rows, consts, orows, oaccs, scr, step, blk):
        main_ref, halo_ref, hs_ref = rows
        w_ref, alog_ref, dtb_ref = consts
        q_ref, k_ref, v_ref, bg_ref = orows
        xs = scr[0]
        trr = main_ref.shape[0]
        xs[pl.ds(SUBLANES, trr), :] = main_ref[...]
        xs[pl.ds(0, SUBLANES), :] = jnp.where(blk > 0, halo_ref[...], 0.0)
        for s in range(C3 // LANES):
            ls = slice(s * LANES, (s + 1) * LANES)
            c = jnp.zeros((trr, LANES), F32)
            for j in range(GDN_CONV):
                c = c + w_ref[GDN_CONV - 1 - j:GDN_CONV - j, ls] * xs[pl.ds(SUBLANES - j, trr), ls]
            if s < 2 * H:
                out = _gdn_qk_tile(c)
                (q_ref if s < H else k_ref)[:, (s % H) * LANES:(s % H + 1) * LANES] = out
            else:
                v_ref[:, (s - 2 * H) * LANES:(s - 2 * H + 1) * LANES] = _silu(c)
        bg_ref[...] = bg_fn(hs_ref[...], alog_ref[...], dtb_ref[...])

    main = (h_main, lambda tr_, n: pl.BlockSpec((tr_, C3), lambda i: (i, 0)))
    halo = (h_main, _halo_spec(C3))
    trr = min(tr, h_main.shape[0])
    return _rowcall(body, [main, halo, h_small], [conv_w8, alog_row, dtb_row],
                    [(W, F32), (W, F32), (W, F32), (LANES, F32)], [], tr=tr, name=name,
                    scratch=[pltpu.VMEM((trr + SUBLANES, C3), F32)])


def _gdn_pre_bwd(h_main, h_small, conv_w8, alog_row, dtb_row, dq, dk, dv, dbg, dz, H, *, tr, name):
    W = H * GDN_DK
    C3 = 3 * W
    bg_fn = _make_bg_fn(H)

    def body(rows, consts, orows, oaccs, scr, step, blk):
        main_ref, halo_ref, hs_ref, dq_ref, dk_ref, dv_ref, dbg_ref, dz_ref = rows
        w_ref, alog_ref, dtb_ref = consts
        dmain_ref, dhs_ref = orows
        dw_ref, dalog_ref, ddtb_ref = oaccs
        xs, dcs = scr
        trr = main_ref.shape[0]
        xs[pl.ds(SUBLANES, trr), :] = main_ref[...]
        xs[pl.ds(0, SUBLANES), :] = jnp.where(blk > 0, halo_ref[...], 0.0)

        @pl.when(step == 0)
        def _():
            dcs[pl.ds(trr, SUBLANES), :] = jnp.zeros((SUBLANES, C3), F32)

        for s in range(C3 // LANES):
            ls = slice(s * LANES, (s + 1) * LANES)
            c = jnp.zeros((trr, LANES), F32)
            for j in range(GDN_CONV):
                c = c + w_ref[GDN_CONV - 1 - j:GDN_CONV - j, ls] * xs[pl.ds(SUBLANES - j, trr), ls]
            if s < 2 * H:
                src = dq_ref if s < H else dk_ref
                ct = src[:, (s % H) * LANES:(s % H + 1) * LANES]
                _, vjp = jax.vjp(_gdn_qk_tile, c)
            else:
                ct = dv_ref[:, (s - 2 * H) * LANES:(s - 2 * H + 1) * LANES]
                _, vjp = jax.vjp(_silu, c)
            dcs[pl.ds(0, trr), ls] = vjp(ct)[0]
        for s in range(C3 // LANES):
            ls = slice(s * LANES, (s + 1) * LANES)
            dx = jnp.zeros((trr, LANES), F32)
            dc0 = dcs[pl.ds(0, trr), ls]
            for j in range(GDN_CONV):
                wrow = w_ref[GDN_CONV - 1 - j:GDN_CONV - j, ls]
                dx = dx + wrow * dcs[pl.ds(j, trr), ls]
                dw_ref[GDN_CONV - 1 - j:GDN_CONV - j, ls] += jnp.sum(dc0 * xs[pl.ds(SUBLANES - j, trr), ls], axis=0, keepdims=True)
            dmain_ref[:, ls] = dx.astype(BF16)
        dmain_ref[:, C3:] = dz_ref[...]
        dcs[pl.ds(trr, SUBLANES), :] = dcs[pl.ds(0, SUBLANES), :]
        _, vjp = jax.vjp(bg_fn, hs_ref[...], alog_ref[...], dtb_ref[...])
        dhs, dalog, ddtb = vjp(dbg_ref[...])
        dhs_ref[...] = dhs.astype(BF16)
        dalog_ref[...] += dalog
        ddtb_ref[...] += ddtb

    trr = min(tr, h_main.shape[0])
    main = (h_main, lambda tr_, n: pl.BlockSpec((tr_, C3), lambda i: (n - 1 - i, 0)))
    halo = (h_main, _halo_spec_rev(C3))
    return _rowcall(body, [main, halo, h_small, dq, dk, dv, dbg, dz], [conv_w8, alog_row, dtb_row],
                    [(4 * W, BF16), (LANES, BF16)], [(SUBLANES, C3), (1, LANES), (1, LANES)],
                    tr=tr, name=name, reverse=True,
                    scratch=[pltpu.VMEM((trr + SUBLANES, C3), F32), pltpu.VMEM((trr + SUBLANES, C3), F32)])


def _gdn_chunk(qs, ks, vs, betas, gs, Ss, Ts=None, with_inverse=False):
    C, dk = qs[0].shape
    dv = vs[0].shape[1]
    ri, ci = _iota((C, C), 0), _iota((C, C), 1)
    causal, strict = ri >= ci, ri > ci
    tril = causal.astype(F32)
    lane0 = (_iota((1, LANES), 1) == 0).astype(F32)
    e0 = jnp.ones((C, 1), F32) * lane0
    last = (_iota((C, 1), 0) == C - 1).astype(F32)

    def each(f, *lists):
        return [f(*a) for a in zip(*lists)]

    G = each(lambda g: g * jnp.ones((1, LANES), F32), gs)
    gcB = each(lambda x: _mm(tril, x, "nn", "ca"), G)
    gc = each(lambda x: jnp.sum(x * lane0, -1, keepdims=True), gcB)
    gc_row = each(lambda x: _mm(e0, x, "nt", "ca"), gcB)
    decay = each(lambda a, b: jnp.where(causal, jnp.exp(jnp.where(causal, a - b, 0.0)), 0.0), gc, gc_row)
    kb = each(lambda k, b: k * b, ks, betas)
    kk = each(lambda a, k: _mm(a, k, "nt", "b"), kb, ks)
    L = each(lambda a, d: jnp.where(strict, a * d, 0.0), kk, decay)
    X = _tri_inv(L) if Ts is None else _tri_inv_known(L, Ts)
    egc = each(jnp.exp, gc)
    u = each(lambda x, v, b: _mm(x, v * b, "nn", "x3"), X, vs, betas)
    w = each(lambda x, a, e: _mm(x, a * e, "nn", "x3"), X, kb, egc)
    qsc = each(lambda q: q * (dk ** -0.5), qs)
    qk = each(lambda q, k: _mm(q, k, "nt", "b"), qsc, ks)
    A = each(lambda a, d: jnp.where(causal, a * d, 0.0), qk, decay)
    q_dec = each(lambda q, e: q * e, qsc, egc)
    gl = each(lambda x: jnp.sum(x * last, keepdims=True), gc)
    k_dec = each(lambda k, a, b: k * jnp.exp(a - b), ks, gl, gc)
    wS = each(lambda a, s: _mm(a, s, "nn", "b"), w, Ss)
    qS = each(lambda a, s: _mm(a, s, "nn", "b"), q_dec, Ss)
    v_new = each(lambda a, b: a - b, u, wS)
    Av = each(lambda a, b: _mm(a, b, "nn", "b"), A, v_new)
    kv = each(lambda a, b: _mm(a, b, "tn", "b"), k_dec, v_new)
    o = each(lambda a, b: a + b, qS, Av)
    S_new = each(lambda s, e, x: s * jnp.exp(e) + x, Ss, gl, kv)
    return (o, S_new, X) if with_inverse else (o, S_new)


def _gdn_rule_fwd(q, k, v, bg, H, *, name, carry=None):
    S_len = q.shape[0]
    C = min(GDN_CHUNK, S_len)
    N = S_len // C
    dk = dv = GDN_DK

    def body(*refs):
        if carry is None:
            q_ref, k_ref, v_ref, bg_ref, o_ref, st_ref, ti_ref, s_scr = refs
        else:
            q_ref, k_ref, v_ref, bg_ref, src_ref, o_ref, st_ref, ti_ref, got_ref, s_scr = refs[:10]
            xy = (src_ref, got_ref) + tuple(refs[10:]) + (carry[1],)
        n = pl.program_id(0)

        @pl.when(n == 0)
        def _():
            s_scr[...] = jnp.zeros(s_scr.shape, F32)
            if carry is not None:
                _xy_start(*xy)

        bgt = bg_ref[...]
        sl = [slice(h * dk, (h + 1) * dk) for h in range(H)]
        Ss = [s_scr[h] for h in range(H)]
        for h in range(H):
            st_ref[h] = Ss[h]
        os_, S_new, Ts = _gdn_chunk([q_ref[:, s] for s in sl], [k_ref[:, s] for s in sl], [v_ref[:, s] for s in sl],
                                    [bgt[:, h:h + 1] for h in range(H)], [bgt[:, H + h:H + h + 1] for h in range(H)],
                                    Ss, with_inverse=True)
        for h in range(H):
            o_ref[:, sl[h]] = os_[h]
            s_scr[h] = S_new[h]
            ti_ref[h] = Ts[h]

        if carry is not None:
            @pl.when(n == N - 1)
            def _():
                _xy_wait(*xy)

    rows = pl.BlockSpec((C, H * dk), lambda n: (n, 0))
    extra = carry is not None
    return pl.pallas_call(
        body, name=name, grid=(N,),
        in_specs=[rows, rows, rows, pl.BlockSpec((C, LANES), lambda n: (n, 0))] + ([_ANY] if extra else []),
        out_specs=[rows, pl.BlockSpec((H, dk, dv), lambda n: (n, 0, 0)), pl.BlockSpec((H, C, C), lambda n: (n, 0, 0))]
        + ([_ANY] if extra else []),
        out_shape=[jax.ShapeDtypeStruct((S_len, H * dv), F32), jax.ShapeDtypeStruct((N * H, dk, dv), F32),
                   jax.ShapeDtypeStruct((N * H, C, C), F32)] + ([_xy_out_shape(carry[0])] if extra else []),
        scratch_shapes=[pltpu.VMEM((H, dk, dv), F32)] + (_xy_sems() if extra else []),
        compiler_params=_params("arbitrary"),
    )(q, k, v, bg, *((carry[0],) if extra else ()))


def _gdn_rule_bwd(q, k, v, bg, states, tinv, do, H, *, name, carry=None):
    S_len = q.shape[0]
    C = min(GDN_CHUNK, S_len)
    N = S_len // C
    dk = dv = GDN_DK

    def body(*refs):
        if carry is None:
            q_ref, k_ref, v_ref, bg_ref, st_ref, ti_ref, do_ref, dq_ref, dk_ref, dv_ref, dbg_ref, ds_scr = refs
        else:
            (q_ref, k_ref, v_ref, bg_ref, st_ref, ti_ref, do_ref, src_ref,
             dq_ref, dk_ref, dv_ref, dbg_ref, got_ref, ds_scr) = refs[:14]
            xy = (src_ref, got_ref) + tuple(refs[14:]) + (carry[1],)
        step = pl.program_id(0)

        @pl.when(step == 0)
        def _():
            ds_scr[...] = jnp.zeros(ds_scr.shape, F32)
            if carry is not None:
                _xy_start(*xy)

        bgt = bg_ref[...]
        lane = _iota((1, LANES), 1)
        dbg = jnp.zeros((C, LANES), F32)
        sl = [slice(h * dk, (h + 1) * dk) for h in range(H)]
        Ts = [ti_ref[h] for h in range(H)]
        _, vjp = jax.vjp(lambda *a: _gdn_chunk(*a, Ts=Ts),
                         [q_ref[:, s] for s in sl], [k_ref[:, s] for s in sl], [v_ref[:, s] for s in sl],
                         [bgt[:, h:h + 1] for h in range(H)], [bgt[:, H + h:H + h + 1] for h in range(H)],
                         [st_ref[h] for h in range(H)])
        dq, dkk, dvv, dbeta, dg, dS = vjp(([do_ref[:, s] for s in sl], [ds_scr[h] for h in range(H)]))
        for h in range(H):
            dq_ref[:, sl[h]] = dq[h]
            dk_ref[:, sl[h]] = dkk[h]
            dv_ref[:, sl[h]] = dvv[h]
            dbg = dbg + jnp.where(lane == h, dbeta[h], 0.0) + jnp.where(lane == h + H, dg[h], 0.0)
            ds_scr[h] = dS[h]
        dbg_ref[...] = dbg

        if carry is not None:
            @pl.when(step == N - 1)
            def _():
                _xy_wait(*xy)

    rows = pl.BlockSpec((C, H * dk), lambda s: (N - 1 - s, 0))
    bgs = pl.BlockSpec((C, LANES), lambda s: (N - 1 - s, 0))
    extra = carry is not None
    return pl.pallas_call(
        body, name=name, grid=(N,),
        in_specs=[rows, rows, rows, bgs, pl.BlockSpec((H, dk, dv), lambda s: (N - 1 - s, 0, 0)),
                  pl.BlockSpec((H, C, C), lambda s: (N - 1 - s, 0, 0)), rows] + ([_ANY] if extra else []),
        out_specs=[rows, rows, rows, bgs] + ([_ANY] if extra else []),
        out_shape=[jax.ShapeDtypeStruct((S_len, H * dk), F32)] * 3 + [jax.ShapeDtypeStruct((S_len, LANES), F32)]
        + ([_xy_out_shape(carry[0])] if extra else []),
        scratch_shapes=[pltpu.VMEM((H, dk, dv), F32)] + (_xy_sems() if extra else []),
        compiler_params=_params("arbitrary"),
    )(q, k, v, bg, states, tinv, do, *((carry[0],) if extra else ()))


def _gdn_post_tile(o, z, g):
    return o * lax.rsqrt(jnp.mean(o * o, -1, keepdims=True) + RMS_EPS) * g * _silu(z)


def _gdn_post_fwd(o, h_main, norm_g, H, *, tr, name):
    W = H * GDN_DK

    def body(rows, consts, orows, oaccs, scr, step, blk):
        for h in range(H):
            ls = slice(h * LANES, (h + 1) * LANES)
            orows[0][:, ls] = _gdn_post_tile(rows[0][:, ls], rows[1][:, ls], consts[0][...]).astype(BF16)

    z = (h_main, lambda tr_, n: pl.BlockSpec((tr_, W), lambda i: (i, 3)))
    return _rowcall(body, [o, z], [_row(norm_g)], [(W, BF16)], [], tr=tr, name=name)[0]


def _gdn_post_bwd(o, h_main, norm_g, du, w_out, H, *, tr, name):
    W = H * GDN_DK

    def body(rows, consts, orows, oaccs, scr, step, blk):
        d_on = lax.dot_general(rows[2][...].astype(BF16), consts[1][...], _DIMS["nt"], preferred_element_type=F32)
        for h in range(H):
            ls = slice(h * LANES, (h + 1) * LANES)
            _, vjp = jax.vjp(_gdn_post_tile, rows[0][:, ls], rows[1][:, ls], consts[0][...])
            d_o, d_z, d_g = vjp(d_on[:, ls])
            orows[0][:, ls] = d_o
            orows[1][:, ls] = d_z.astype(BF16)
            oaccs[0][...] += d_g

    z = (h_main, lambda tr_, n: pl.BlockSpec((tr_, W), lambda i: (i, 3)))
    return _rowcall(body, [o, z, du], [_row(norm_g), w_out], [(W, F32), (W, BF16)], [(1, LANES)], tr=tr, name=name)


def _seg_ones():
    ri, ci = _iota((LANES, LANES), 0), _iota((LANES, LANES), 1)
    return ((ri < FOX_DH) == (ci < FOX_DH)).astype(F32)


def _fox_qk_tile(x, g2):
    ms = _mm(x * x, _seg_ones(), "nn", "cb") * (1.0 / FOX_DH)
    return x * lax.rsqrt(ms + RMS_EPS) * g2


def _make_lf_fn(Hf):
    def fn(hs, bf):
        lane = _iota((1, LANES), 1)
        return jnp.where(lane < Hf, -_softplus(-(hs + bf)), 0.0)
    return fn


def _fox_pre_fwd(h_main, h_small, gq2, gk2, bf_row, Hf, *, tr, name):
    W = Hf * FOX_DH
    lf_fn = _make_lf_fn(Hf)

    def body(rows, consts, orows, oaccs, scr, step, blk):
        qk_ref, v_ref, hs_ref = rows
        gq_ref, gk_ref, bf_ref = consts
        qn_ref, kn_ref, vb_ref, c_ref, cb_ref = orows
        carry = scr[0]
        trr = qk_ref.shape[0]

        @pl.when(step == 0)
        def _():
            carry[...] = jnp.zeros(carry.shape, F32)

        for s in range(W // LANES):
            ls = slice(s * LANES, (s + 1) * LANES)
            qn_ref[:, ls] = _fox_qk_tile(qk_ref[:, ls], gq_ref[...]).astype(BF16)
            kn_ref[:, ls] = _fox_qk_tile(qk_ref[:, W + s * LANES:W + (s + 1) * LANES], gk_ref[...]).astype(BF16)
        vb_ref[...] = v_ref[...].astype(BF16)
        lf = lf_fn(hs_ref[...], bf_ref[...])
        tril = (_iota((trr, trr), 0) >= _iota((trr, trr), 1)).astype(F32)
        c = _raw_mm(tril, lf, "nn", "ca") + carry[0:1, :]
        c_ref[...] = c
        carry[0:1, :] = c[trr - 1:trr, :]
        col = _iota((LANES, 2 * W), 1)
        parity = (col >= W).astype(jnp.int32)
        slab = jnp.right_shift(col - parity * W, 7)
        expand = (2 * slab + parity == _iota((LANES, 2 * W), 0)).astype(F32)
        cb_ref[...] = _raw_mm(c, expand, "nn", "cb")

    qk = (h_main, lambda tr_, n: pl.BlockSpec((tr_, 2 * W), lambda i: (i, 0)))
    vv = (h_main, lambda tr_, n: pl.BlockSpec((tr_, W), lambda i: (i, 2)))
    return _rowcall(body, [qk, vv, h_small], [gq2, gk2, bf_row],
                    [(W, BF16), (W, BF16), (W, BF16), (LANES, F32), (2 * W, F32)], [], tr=tr, name=name,
                    scratch=[pltpu.VMEM((SUBLANES, LANES), F32)])


def _fox_pre_bwd(h_main, h_small, gq2, gk2, bf_row, dqn, dkn, dvv, dz, dc, Hf, *, tr, name):
    W = Hf * FOX_DH
    lf_fn = _make_lf_fn(Hf)

    def body(rows, consts, orows, oaccs, scr, step, blk):
        qk_ref, hs_ref, dqn_ref, dkn_ref, dvv_ref, dz_ref, dc_ref = rows
        gq_ref, gk_ref, bf_ref = consts
        dmain_ref, dhs_ref = orows
        dgq_ref, dgk_ref, dbf_ref = oaccs
        carry = scr[0]
        trr = qk_ref.shape[0]

        @pl.when(step == 0)
        def _():
            carry[...] = jnp.zeros(carry.shape, F32)

        for s in range(W // LANES):
            ls = slice(s * LANES, (s + 1) * LANES)
            lk = slice(W + s * LANES, W + (s + 1) * LANES)
            _, vjp = jax.vjp(_fox_qk_tile, qk_ref[:, ls], gq_ref[...])
            dx, dg = vjp(dqn_ref[:, ls])
            dmain_ref[:, ls] = dx.astype(BF16)
            dgq_ref[...] += dg
            _, vjp = jax.vjp(_fox_qk_tile, qk_ref[:, lk], gk_ref[...])
            dx, dg = vjp(dkn_ref[:, ls])
            dmain_ref[:, lk] = dx.astype(BF16)
            dgk_ref[...] += dg
        dmain_ref[:, 2 * W:3 * W] = dvv_ref[...].astype(BF16)
        dmain_ref[:, 3 * W:] = dz_ref[...]
        dcv = dc_ref[...]
        triu = (_iota((trr, trr), 0) <= _iota((trr, trr), 1)).astype(F32)
        dlf = _raw_mm(triu, dcv, "nn", "ca") + carry[0:1, :]
        carry[0:1, :] = dlf[0:1, :]
        _, vjp = jax.vjp(lf_fn, hs_ref[...], bf_ref[...])
        dhs, dbf = vjp(dlf)
        dhs_ref[...] = dhs.astype(BF16)
        dbf_ref[...] += dbf

    qk = (h_main, lambda tr_, n: pl.BlockSpec((tr_, 2 * W), lambda i: (n - 1 - i, 0)))
    return _rowcall(body, [qk, h_small, dqn, dkn, dvv, dz, dc], [gq2, gk2, bf_row],
                    [(4 * W, BF16), (LANES, BF16)], [(1, LANES), (1, LANES), (1, LANES)],
                    tr=tr, name=name, reverse=True, scratch=[pltpu.VMEM((SUBLANES, LANES), F32)])


def _fox_attn_fwd(qn, kn, vb, c_b, c_rowp, *, tb, name, carry=None):
    S_len, W = qn.shape
    HP = W // LANES
    tb = min(tb, S_len)
    nb = S_len // tb
    scale = FOX_DH ** -0.5
    rb, cb = min(ATTN_ROWS, tb), min(ATTN_COLS, tb)
    nblk = 2 * (tb // rb)

    steps = [(i, j) for i in range(nb) for j in range(i + 1)]
    ti = jnp.asarray([s[0] for s in steps], jnp.int32)
    tj = jnp.asarray([s[1] for s in steps], jnp.int32)

    def body(*refs):
        if carry is None:
            (ti_ref, tj_ref, q_ref, k_ref, v_ref, cb0_ref, cb1_ref, cr_ref, o_ref, lse_ref,
             m_scr, l_scr, acc_scr, s_scr, p_scr, a_scr) = refs
        else:
            (ti_ref, tj_ref, q_ref, k_ref, v_ref, cb0_ref, cb1_ref, cr_ref, src_ref, o_ref, lse_ref, got_ref,
             m_scr, l_scr, acc_scr, s_scr, p_scr, a_scr) = refs[:18]
            xy = (src_ref, got_ref) + tuple(refs[18:]) + (carry[1],)
        t = pl.program_id(1)
        i, j = ti_ref[t], tj_ref[t]

        if carry is not None:
            @pl.when(jnp.logical_and(pl.program_id(0) == 0, t == 0))
            def _():
                _xy_start(*xy)

        @pl.when(j == 0)
        def _():
            m_scr[...] = jnp.full(m_scr.shape, -jnp.inf, F32)
            l_scr[...] = jnp.zeros(l_scr.shape, F32)
            acc_scr[...] = jnp.zeros(acc_scr.shape, F32)

        def compute(diag):
            lo = _iota((1, LANES), 1) < FOX_DH
            v = v_ref[...]
            lane = _iota((1, LANES), 1)
            blocks = [(hh, r) for hh in range(2) for r in range(tb // rb)]
            masks = [lo, jnp.logical_not(lo)]

            def visible(r):
                return ((r + 1) * rb - 1) // LANES + 1 if diag else tb // LANES

            for b, (hh, r) in enumerate(blocks):
                rows = slice(r * rb, (r + 1) * rb)
                qr = q_ref[rows, :]
                qh = jnp.where(masks[hh], qr * scale, jnp.zeros_like(qr))
                ctb = (cb0_ref if hh == 0 else cb1_ref)[rows, :]
                mx = None
                for c in range(tb // cb):
                    if c * cb // LANES >= visible(r):
                        continue
                    s2 = lax.dot_general(qh, k_ref[c * cb:(c + 1) * cb, :], _DIMS["nt"], preferred_element_type=F32)
                    for piece in range(c * cb // LANES, min((c + 1) * cb // LANES, visible(r))):
                        cols = slice(piece * LANES, (piece + 1) * LANES)
                        s = s2[:, piece * LANES - c * cb:(piece + 1) * LANES - c * cb] + ctb - cr_ref[hh:hh + 1, cols]
                        if diag and (piece + 1) * LANES - 1 > r * rb:
                            keep = piece * LANES + _iota((rb, LANES), 1) <= r * rb + _iota((rb, LANES), 0)
                            s = jnp.where(keep, s, -jnp.inf)
                        s_scr[b, :, cols] = s
                        mx = s if mx is None else jnp.maximum(mx, s)
                m_prev = m_scr[hh, rows, :]
                m_new = jnp.maximum(m_prev, jnp.broadcast_to(jnp.max(mx, -1, keepdims=True), (rb, LANES)))
                a_scr[b] = jnp.exp(m_prev - m_new)
                m_scr[hh, rows, :] = m_new
            for b, (hh, r) in enumerate(blocks):
                m_new = m_scr[hh, r * rb:(r + 1) * rb, :]
                for piece in range(visible(r)):
                    cols = slice(piece * LANES, (piece + 1) * LANES)
                    p_scr[b, :, cols] = jnp.exp(s_scr[b, :, cols] - m_new).astype(BF16)
            for b, (hh, r) in enumerate(blocks):
                rows = slice(r * rb, (r + 1) * rb)
                nkv = visible(r) * LANES
                pb = p_scr[b, :, :nkv]
                spare = (1 - hh) * FOX_DH
                vh = jnp.where(masks[hh], v[:nkv], (lane == spare).astype(BF16))
                pv = lax.dot_general(pb, vh, _DIMS["nn"], preferred_element_type=F32)
                psum = jnp.broadcast_to(pv[:, spare:spare + 1], (rb, LANES))
                alpha = a_scr[b]
                l_scr[hh, rows, :] = alpha * l_scr[hh, rows, :] + psum
                acc = acc_scr[rows, :]
                acc_scr[rows, :] = jnp.where(masks[hh], acc * alpha + pv, acc)

        @pl.when(j < i)
        def _():
            compute(False)

        @pl.when(j == i)
        def _():
            compute(True)
            lo = _iota((1, LANES), 1) < FOX_DH
            o_ref[...] = acc_scr[...] / jnp.where(lo, l_scr[0], l_scr[1])
            lse_ref[...] = m_scr[...] + jnp.log(l_scr[...])

        if carry is not None:
            @pl.when(jnp.logical_and(pl.program_id(0) == HP - 1, t == len(steps) - 1))
            def _():
                _xy_wait(*xy)

    qs = pl.BlockSpec((tb, LANES), lambda h, t, ti_, tj_: (ti_[t], h))
    qs1 = pl.BlockSpec((tb, LANES), lambda h, t, ti_, tj_: (ti_[t], HP + h))
    ks = pl.BlockSpec((tb, LANES), lambda h, t, ti_, tj_: (tj_[t], h))
    crs = pl.BlockSpec((None, SUBLANES, tb), lambda h, t, ti_, tj_: (h, 0, tj_[t]))
    extra = carry is not None
    return pl.pallas_call(
        body, name=name,
        grid_spec=pltpu.PrefetchScalarGridSpec(
            num_scalar_prefetch=2, grid=(HP, len(steps)),
            in_specs=[qs, ks, ks, qs, qs1, crs] + ([_ANY] if extra else []),
            out_specs=[qs, pl.BlockSpec((2, tb, LANES), lambda h, t, ti_, tj_: (0, ti_[t], h))]
            + ([_ANY] if extra else []),
            scratch_shapes=[pltpu.VMEM((2, tb, LANES), F32), pltpu.VMEM((2, tb, LANES), F32),
                            pltpu.VMEM((tb, LANES), F32), pltpu.VMEM((nblk, rb, tb), F32),
                            pltpu.VMEM((nblk, rb, tb), BF16), pltpu.VMEM((nblk, rb, LANES), F32)]
            + (_xy_sems() if extra else [])),
        out_shape=[jax.ShapeDtypeStruct((S_len, W), F32), jax.ShapeDtypeStruct((2, S_len, W), F32)]
        + ([_xy_out_shape(carry[0])] if extra else []),
        compiler_params=_params("arbitrary" if extra else "parallel", "arbitrary"),
    )(ti, tj, qn, kn, vb, c_b, c_b, c_rowp, *((carry[0],) if extra else ()))


def _fox_attn_bwd(qn, kn, vb, c_b, c_rowp, lse_b, delta_b, do, *, tb, name):
    S_len, W = qn.shape
    HP = W // LANES
    tb = min(tb, S_len)
    nb = S_len // tb
    scale = FOX_DH ** -0.5
    rb, cb = min(ATTN_ROWS, tb), min(ATTN_COLS, tb)

    steps = [(j, i) for j in range(nb) for i in range(j, nb)]
    tj = jnp.asarray([s[0] for s in steps], jnp.int32)
    ti = jnp.asarray([s[1] for s in steps], jnp.int32)

    def body(tj_ref, ti_ref, q_ref, k_ref, v_ref, cb0_ref, cb1_ref, cr_ref, lse_ref, dl0_ref, dl1_ref, do_ref,
             dq_ref, dk_ref, dv_ref, dcr_ref, dct_ref, dk_scr, dv_scr, dc_scr, p_scr, ds_scr):
        t = pl.program_id(1)
        j, i = tj_ref[t], ti_ref[t]

        @pl.when(t == 0)
        def _():
            dq_ref[...] = jnp.zeros(dq_ref.shape, F32)
            dct_ref[...] = jnp.zeros(dct_ref.shape, F32)

        @pl.when(i == j)
        def _():
            dk_scr[...] = jnp.zeros(dk_scr.shape, F32)
            dv_scr[...] = jnp.zeros(dv_scr.shape, F32)
            dc_scr[...] = jnp.zeros(dc_scr.shape, F32)

        def compute(diag):
            lo = _iota((1, LANES), 1) < FOX_DH
            masks = [lo, jnp.logical_not(lo)]
            row0 = pl.multiple_of(i * tb, tb)
            npiece = tb // LANES
            colsum = [[None] * npiece for _ in range(2)]

            def visible(r):
                return ((r + 1) * rb - 1) // LANES + 1 if diag else npiece

            for hh in range(2):
                for r in range(tb // rb):
                    rows = slice(r * rb, (r + 1) * rb)
                    qr = q_ref[rows, :]
                    qh = jnp.where(masks[hh], qr * scale, jnp.zeros_like(qr))
                    doh = jnp.where(masks[hh], do_ref[rows, :], 0.0).astype(BF16)
                    bq = (cb0_ref if hh == 0 else cb1_ref)[rows, :] - lse_ref[hh, rows, :]
                    dlt = (dl0_ref if hh == 0 else dl1_ref)[rows, :]
                    rsum = None
                    for c in range(tb // cb):
                        first, last = c * cb // LANES, min((c + 1) * cb // LANES, visible(r))
                        for piece in range(max(first, last), (c + 1) * cb // LANES):
                            cols = slice(piece * LANES, (piece + 1) * LANES)
                            p_scr[hh, rows, cols] = jnp.zeros((rb, LANES), BF16)
                            ds_scr[hh, rows, cols] = jnp.zeros((rb, LANES), BF16)
                        if first >= last:
                            continue
                        s2 = lax.dot_general(qh, k_ref[c * cb:(c + 1) * cb, :], _DIMS["nt"], preferred_element_type=F32)
                        dp2 = lax.dot_general(doh, v_ref[c * cb:(c + 1) * cb, :], _DIMS["nt"], preferred_element_type=F32)
                        for piece in range(first, last):
                            cols = slice(piece * LANES, (piece + 1) * LANES)
                            sub = slice(piece * LANES - c * cb, (piece + 1) * LANES - c * cb)
                            s = s2[:, sub] + bq - cr_ref[hh:hh + 1, cols]
                            if diag and (piece + 1) * LANES - 1 > r * rb:
                                keep = piece * LANES + _iota((rb, LANES), 1) <= r * rb + _iota((rb, LANES), 0)
                                s = jnp.where(keep, s, -jnp.inf)
                            p = jnp.exp(s)
                            ds = p * (dp2[:, sub] - dlt)
                            p_scr[hh, rows, cols] = p.astype(BF16)
                            ds_scr[hh, rows, cols] = ds.astype(BF16)
                            rsum = ds if rsum is None else rsum + ds
                            csum = jnp.sum(ds, axis=0, keepdims=True)
                            colsum[hh][piece] = csum if colsum[hh][piece] is None else colsum[hh][piece] + csum
                    grow = pl.ds(row0 + r * rb, rb)
                    dct_ref[grow, :] += jnp.where(_iota((1, SUBLANES), 1) == hh, jnp.sum(rsum, -1, keepdims=True), 0.0)
            k = k_ref[...]
            qf = q_ref[...]
            dof = do_ref[...]
            dq_part = jnp.zeros((tb, LANES), F32)
            for hh in range(2):
                kh = jnp.where(masks[hh], k, jnp.zeros_like(k))
                qhf = jnp.where(masks[hh], qf * scale, jnp.zeros_like(qf))
                dohf = jnp.where(masks[hh], dof, 0.0).astype(BF16)
                dv_scr[...] += lax.dot_general(p_scr[hh], dohf, _DIMS["tn"], preferred_element_type=F32)
                dk_scr[...] += lax.dot_general(ds_scr[hh], qhf, _DIMS["tn"], preferred_element_type=F32)
                dq_part = dq_part + lax.dot_general(ds_scr[hh], kh, _DIMS["nn"], preferred_element_type=F32)
                for piece in range(npiece):
                    if colsum[hh][piece] is not None:
                        dc_scr[hh:hh + 1, piece * LANES:(piece + 1) * LANES] -= colsum[hh][piece]
            dq_ref[pl.ds(row0, tb), :] += dq_part * scale

        @pl.when(i > j)
        def _():
            compute(False)

        @pl.when(i == j)
        def _():
            compute(True)

        @pl.when(i == nb - 1)
        def _():
            dk_ref[...] = dk_scr[...]
            dv_ref[...] = dv_scr[...]
            dcr_ref[...] = dc_scr[...]

    qs = pl.BlockSpec((tb, LANES), lambda h, t, tj_, ti_: (ti_[t], h))
    qs1 = pl.BlockSpec((tb, LANES), lambda h, t, tj_, ti_: (ti_[t], HP + h))
    ks = pl.BlockSpec((tb, LANES), lambda h, t, tj_, ti_: (tj_[t], h))
    crs = pl.BlockSpec((None, SUBLANES, tb), lambda h, t, tj_, ti_: (h, 0, tj_[t]))
    whole = pl.BlockSpec((S_len, LANES), lambda h, t, tj_, ti_: (0, h))
    return pl.pallas_call(
        body, name=name,
        grid_spec=pltpu.PrefetchScalarGridSpec(
            num_scalar_prefetch=2, grid=(HP, len(steps)),
            in_specs=[qs, ks, ks, qs, qs1, crs, pl.BlockSpec((2, tb, LANES), lambda h, t, tj_, ti_: (0, ti_[t], h)),
                      qs, qs1, qs],
            out_specs=[whole, ks, ks, crs, pl.BlockSpec((None, S_len, SUBLANES), lambda h, t, tj_, ti_: (h, 0, 0))],
            scratch_shapes=[pltpu.VMEM((tb, LANES), F32), pltpu.VMEM((tb, LANES), F32),
                            pltpu.VMEM((SUBLANES, tb), F32), pltpu.VMEM((2, tb, tb), BF16),
                            pltpu.VMEM((2, tb, tb), BF16)]),
        out_shape=[jax.ShapeDtypeStruct((S_len, W), F32)] * 3 + [jax.ShapeDtypeStruct((HP, SUBLANES, S_len), F32),
                                                                 jax.ShapeDtypeStruct((HP, S_len, SUBLANES), F32)],
        compiler_params=_params("parallel", "arbitrary"),
    )(tj, ti, qn, kn, vb, c_b, c_b, c_rowp, lse_b, delta_b, delta_b, do)


def _fox_post_tile(o, z):
    return o * _silu(z)


def _fox_post_fwd(o, h_main, *, tr, name):
    W = o.shape[1]

    def body(rows, consts, orows, oaccs, scr, step, blk):
        orows[0][...] = _fox_post_tile(rows[0][...], rows[1][...]).astype(BF16)

    z = (h_main, lambda tr_, n: pl.BlockSpec((tr_, W), lambda i: (i, 3)))
    return _rowcall(body, [o, z], [], [(W, BF16)], [], tr=tr, name=name)[0]


def _fox_post_bwd(o, h_main, du, w_out, *, tr, name):
    W = o.shape[1]

    def body(rows, consts, orows, oaccs, scr, step, blk):
        d_og = lax.dot_general(rows[2][...].astype(BF16), consts[0][...], _DIMS["nt"], preferred_element_type=F32)
        _, vjp = jax.vjp(_fox_post_tile, rows[0][...], rows[1][...])
        d_o, d_z = vjp(d_og)
        orows[0][...] = d_o
        orows[1][...] = d_z.astype(BF16)
        lo_rows = (_iota((LANES, LANES), 0) < FOX_DH)
        for s in range(W // LANES):
            ls = slice(s * LANES, (s + 1) * LANES)
            prod = d_o[:, ls] * rows[0][:, ls]
            orows[2][:, ls] = _raw_mm(prod, lo_rows.astype(F32), "nn", "cb")
            orows[2][:, W + s * LANES:W + (s + 1) * LANES] = _raw_mm(prod, jnp.logical_not(lo_rows).astype(F32), "nn", "cb")

    z = (h_main, lambda tr_, n: pl.BlockSpec((tr_, W), lambda i: (i, 3)))
    return _rowcall(body, [o, z, du], [w_out], [(W, F32), (W, BF16), (2 * W, F32)], [], tr=tr, name=name)


MESH_IDS = pl.DeviceIdType.MESH
N_CHIPS = 4
N_DEV = 8
_ANY = pl.BlockSpec(memory_space=pl.ANY)


def _xy_exchange(src, *, gather, name):
    def body(src_ref, out_ref, send_sems, recv_sems, local_sem):
        _xy_start(src_ref, out_ref, send_sems, recv_sems, local_sem, gather)
        _xy_wait(src_ref, out_ref, send_sems, recv_sems, local_sem, gather)

    return pl.pallas_call(
        body, name=name, in_specs=[_ANY], out_specs=_ANY,
        out_shape=_xy_out_shape(src), scratch_shapes=_xy_sems(),
    )(src)


def _xy_out_shape(src):
    return jax.ShapeDtypeStruct((N_CHIPS,) + tuple(src.shape[-2:]), src.dtype)


def _xy_sems():
    return [pltpu.SemaphoreType.DMA((N_CHIPS - 1,)), pltpu.SemaphoreType.DMA((N_CHIPS - 1,)), pltpu.SemaphoreType.DMA]


def _xy_copies(src_ref, out_ref, send_sems, recv_sems, local_sem, gather, with_arrivals=True):
    x, y, c = lax.axis_index("x"), lax.axis_index("y"), lax.axis_index("c")
    me = 2 * x + y
    peers = [(1 - x, y), (x, 1 - y), (1 - x, 1 - y)]

    def outgoing(px, py):
        return src_ref if gather else src_ref.at[2 * px + py]

    def copy(j, px, py, slot):
        return pltpu.make_async_remote_copy(
            src_ref=outgoing(px, py), dst_ref=out_ref.at[slot], send_sem=send_sems.at[j], recv_sem=recv_sems.at[j],
            device_id=(px, py, c), device_id_type=MESH_IDS)

    mine = pltpu.make_async_copy(outgoing(x, y), out_ref.at[me], local_sem)
    sends = [copy(j, px, py, me) for j, (px, py) in enumerate(peers)]
    arrivals = [copy(j, px, py, 2 * px + py) for j, (px, py) in enumerate(peers)] if with_arrivals else []
    return mine, sends, arrivals


def _xy_start(src_ref, out_ref, send_sems, recv_sems, local_sem, gather):
    mine, sends, _ = _xy_copies(src_ref, out_ref, send_sems, recv_sems, local_sem, gather, with_arrivals=False)
    mine.start()
    for cp in sends:
        cp.start()


def _xy_wait(src_ref, out_ref, send_sems, recv_sems, local_sem, gather):
    mine, sends, arrivals = _xy_copies(src_ref, out_ref, send_sems, recv_sems, local_sem, gather)
    for cp in arrivals:
        cp.wait_recv()
    for cp in sends:
        cp.wait_send()
    mine.wait()


def _c_swap(src, *, name):
    def body(src_ref, out_ref, send_sem, recv_sem):
        x, y, c = lax.axis_index("x"), lax.axis_index("y"), lax.axis_index("c")
        cp = pltpu.make_async_remote_copy(
            src_ref=src_ref, dst_ref=out_ref, send_sem=send_sem, recv_sem=recv_sem,
            device_id=(x, y, 1 - c), device_id_type=MESH_IDS)
        cp.start()
        cp.wait()

    return pl.pallas_call(
        body, name=name, in_specs=[_ANY], out_specs=_ANY,
        out_shape=jax.ShapeDtypeStruct(src.shape, src.dtype),
        scratch_shapes=[pltpu.SemaphoreType.DMA, pltpu.SemaphoreType.DMA],
    )(src)


def _all_gather8(blk, *, name):
    m_per, n = blk.shape

    def body(x_ref, out_ref, send_sems, recv_sems, local_sem):
        x, y, c = lax.axis_index("x"), lax.axis_index("y"), lax.axis_index("c")
        me, sibling = (x, y, c), (x, y, 1 - c)
        chips = [(1 - x, y), (x, 1 - y), (1 - x, 1 - y)]

        def rows(px, py, pc):
            return out_ref.at[pl.ds((4 * px + 2 * py + pc) * m_per, m_per), :]

        def copy(k, block, to, src=None):
            return pltpu.make_async_remote_copy(
                src_ref=rows(*block) if src is None else src, dst_ref=rows(*block),
                send_sem=send_sems.at[k], recv_sem=recv_sems.at[k], device_id=to, device_id_type=MESH_IDS)

        mine = pltpu.make_async_copy(x_ref, rows(*me), local_sem)
        mine.start()
        first = [copy(0, me, sibling, src=x_ref)]
        first += [copy(1 + j, me, (*chip, c), src=x_ref) for j, chip in enumerate(chips)]
        for cp in first:
            cp.start()
        passed = [copy(4 + j, (*chip, c), sibling) for j, chip in enumerate(chips)]
        for j, chip in enumerate(chips):
            copy(1 + j, (*chip, c), me).wait_recv()
            passed[j].start()
        copy(0, sibling, me).wait_recv()
        for j, chip in enumerate(chips):
            copy(4 + j, (*chip, 1 - c), me).wait_recv()
        for cp in first + passed:
            cp.wait_send()
        mine.wait()

    return pl.pallas_call(
        body, name=name,
        out_shape=jax.ShapeDtypeStruct((N_DEV * m_per, n), blk.dtype),
        in_specs=[pl.BlockSpec(memory_space=pltpu.VMEM)], out_specs=pl.BlockSpec(memory_space=pltpu.VMEM),
        scratch_shapes=[pltpu.SemaphoreType.DMA((7,)), pltpu.SemaphoreType.DMA((7,)), pltpu.SemaphoreType.DMA],
    )(blk)


def _sum_slots(parts, *, tr, name):
    n, R, _ = parts.shape
    pack = 2 * SUBLANES
    tr = max(t for t in range(pack, min(tr, R) + 1, pack) if R % t == 0) if R % pack == 0 else R

    def body(p_ref, o_ref):
        tot = p_ref[0].astype(F32)
        for s in range(1, n):
            tot = tot + p_ref[s].astype(F32)
        o_ref[...] = tot

    return pl.pallas_call(
        body, name=name, grid=(R // tr,),
        in_specs=[pl.BlockSpec((n, tr, LANES), lambda i: (0, i, 0))], out_specs=pl.BlockSpec((tr, LANES), lambda i: (i, 0)),
        out_shape=jax.ShapeDtypeStruct((R, LANES), F32), compiler_params=_params("parallel"),
    )(parts)


def _adamw(w, g_parts, m, v, *, name):
    shape = w.shape
    as2d = lambda a: a.reshape(-1, shape[-1])
    w2, m2, v2 = as2d(w), as2d(m), as2d(v)
    gs = [as2d(g) for g in g_parts]
    R, C = w2.shape
    tr = R
    while tr * C * 4 > (1 << 20) and tr % 2 == 0 and (tr // 2) % SUBLANES == 0:
        tr //= 2
    ng = len(gs)

    def body(*refs):
        w_ref, m_ref, v_ref = refs[:3]
        g_refs = refs[3:3 + ng]
        go_ref, d_ref, mo_ref, vo_ref = refs[3 + ng:]
        g = g_refs[0][...]
        for r in g_refs[1:]:
            g = g + r[...]
        mn = ADAM_B1 * m_ref[...] + (1.0 - ADAM_B1) * g
        vn = ADAM_B2 * v_ref[...] + (1.0 - ADAM_B2) * jnp.square(g)
        m_hat = mn / (1.0 - ADAM_B1 ** ADAM_STEP)
        v_hat = vn / (1.0 - ADAM_B2 ** ADAM_STEP)
        go_ref[...] = g
        d_ref[...] = -ADAM_LR * (m_hat / (jnp.sqrt(v_hat) + ADAM_EPS) + ADAM_WD * w_ref[...])
        mo_ref[...] = mn
        vo_ref[...] = vn

    spec = pl.BlockSpec((tr, C), lambda i: (i, 0))
    outs = pl.pallas_call(
        body, name=name, grid=(R // tr,), in_specs=[spec] * (3 + ng), out_specs=[spec] * 4,
        out_shape=[jax.ShapeDtypeStruct((R, C), F32)] * 4, compiler_params=_params("parallel"),
    )(w2, m2, v2, *gs)
    return tuple(o.reshape(shape) for o in outs)


TR = 256
ATTN_TILE = 1024
ATTN_ROWS = 256
ATTN_COLS = 256


def _mm_nn(a, b, name, **kw):
    return _matmul(a, b, tm=2048, tn=1024, tk=1024, name=name, **kw)


def _mm_tn(a, b, name, **kw):
    return _matmul(a, b, ta=True, tm=1024, tn=2048, tk=512, name=name, **kw)


def _c_rows(c, Hf):
    S_len = c.shape[0]
    ct = c[:, :Hf].T.reshape(Hf // 2, 2, S_len)
    return jnp.pad(ct, ((0, 0), (0, SUBLANES - 2), (0, 0)))


def _local_step(x, p, target, wts, late_weights=None, early_grads=None):
    L = len(wts["ln_g"])
    alpha = (2 * L) ** 0.25
    Hg = wts["gdn_a_log"][0].shape[-1]
    Hf = wts["fox_b_f"][0].shape[-1]
    Wg_ = Hg * GDN_DK
    Wf_ = Hf * FOX_DH
    saved = []
    for i in range(L):
        j = i // 2
        sv = {"x": x}
        if i % 2 == 0:
            w_in = wts["gdn_w_in"][j]
            wm, ws = w_in[:, :4 * Wg_], _pad_lanes(w_in[:, 4 * Wg_:])
            cw8 = jnp.pad(wts["gdn_conv_w"][j], ((0, SUBLANES - GDN_CONV), (0, 0)))
            alog = _pad_lanes(_row(wts["gdn_a_log"][j]), offset=Hg)
            dtb = _pad_lanes(_row(wts["gdn_dt_bias"][j]), offset=Hg)
            hm, hs = _in_proj_fwd(x, wm, ws, tr=2 * TR, name=f"gdn{j}_in")
            q, k, v, bg = _gdn_pre_fwd(hm, hs, cw8, alog, dtb, Hg, tr=TR, name=f"gdn{j}_pre")
            if i == 0 and late_weights is not None:
                packed_w, unpack_w = late_weights[0]
                o, states, tinv, got = _gdn_rule_fwd(q, k, v, bg, Hg, name=f"gdn{j}_rule", carry=(packed_w, True))
                for (wname_, idx), arr in unpack_w(got).items():
                    wts[wname_][idx] = arr
            else:
                o, states, tinv = _gdn_rule_fwd(q, k, v, bg, Hg, name=f"gdn{j}_rule")
            on = _gdn_post_fwd(o, hm, wts["gdn_norm_g"][j], Hg, tr=TR, name=f"gdn{j}_post")
            w_out = wts["gdn_w_out"][j]
            sv.update(wm=wm, ws=ws, cw8=cw8, alog=alog, dtb=dtb, hm=hm, hs=hs, q=q, k=k, v=v, bg=bg, o=o,
                      states=states, tinv=tinv, on=on)
        else:
            w_in = wts["fox_w_in"][j]
            wm, ws = w_in[:, :4 * Wf_], _pad_lanes(w_in[:, 4 * Wf_:])
            gq2 = _row(jnp.tile(wts["fox_q_norm_g"][j], 2))
            gk2 = _row(jnp.tile(wts["fox_k_norm_g"][j], 2))
            bf = _pad_lanes(_row(wts["fox_b_f"][j]))
            hm, hs = _in_proj_fwd(x, wm, ws, tr=2 * TR, name=f"fox{j}_in")
            qn, kn, vb, c, c_b = _fox_pre_fwd(hm, hs, gq2, gk2, bf, Hf, tr=TR, name=f"fox{j}_pre")
            c_rowp = _c_rows(c, Hf)
            if i == 1 and late_weights is not None:
                packed_w, unpack_w = late_weights[1]
                o, lse_b, got = _fox_attn_fwd(qn, kn, vb, c_b, c_rowp, tb=ATTN_TILE, name=f"fox{j}_attn",
                                              carry=(packed_w, True))
                for (wname_, idx), arr in unpack_w(got).items():
                    wts[wname_][idx] = arr
            else:
                o, lse_b = _fox_attn_fwd(qn, kn, vb, c_b, c_rowp, tb=ATTN_TILE, name=f"fox{j}_attn")
            on = _fox_post_fwd(o, hm, tr=TR, name=f"fox{j}_post")
            w_out = wts["fox_w_out"][j]
            sv.update(wm=wm, ws=ws, gq2=gq2, gk2=gk2, bf=bf, hm=hm, hs=hs, qn=qn, kn=kn, vb=vb, c_b=c_b,
                      c_rowp=c_rowp, o=o, lse_b=lse_b, on=on)
        y, x_ln = _out_ln_fwd(on, w_out, x, wts["ln_g"][i], wts["ln_b"][i], alpha, tr=2 * TR, name=f"out_ln{i}")
        pp = _mm_nn(p[i], wts["ple_w_proj"][i], f"ple{i}_proj")
        gp, x_out = _gate_mix_fwd(x_ln, wts["ple_w_gate"][i], pp, tr=2 * TR, name=f"ple{i}_gate_mix")
        sv.update(y=y, x_ln=x_ln, gp=gp, pp=pp)
        saved.append(sv)
        x = x_out

    loss_row, dx = _loss_fwd_bwd(x, target, tr=TR, name="loss")

    g = {n: [None] * len(wts[n]) for n in wts}
    got_early = None
    for i in reversed(range(L)):
        j = i // 2
        sv = saved[i]
        d_pre, d_pp, du, g["ln_g"][i], g["ln_b"][i] = _ple_ln_bwd(
            dx, sv["gp"], sv["pp"], wts["ple_w_gate"][i], sv["x"], sv["y"], wts["ln_g"][i], wts["ln_b"][i], alpha,
            tr=2 * TR, name=f"ple_ln{i}_bwd")
        g["ple_w_gate"][i] = _mm_tn(sv["x_ln"], d_pre, f"ple{i}_gate_dw")
        g["ple_w_proj"][i] = _mm_tn(p[i], d_pp, f"ple{i}_proj_dw")
        if i % 2 == 0:
            g["gdn_w_out"][j] = _mm_tn(sv["on"], du, f"gdn{j}_out_dw")
            d_o, d_z, d_ng = _gdn_post_bwd(sv["o"], sv["hm"], wts["gdn_norm_g"][j], du, wts["gdn_w_out"][j], Hg,
                                           tr=TR, name=f"gdn{j}_post_bwd")
            rule_args = (sv["q"], sv["k"], sv["v"], sv["bg"], sv["states"], sv["tinv"], d_o, Hg)
            if i == 0 and early_grads is not None:
                dq, dk, dv, dbg, got_early = _gdn_rule_bwd(*rule_args, name=f"gdn{j}_rule_bwd",
                                                           carry=(early_grads(g), False))
            else:
                dq, dk, dv, dbg = _gdn_rule_bwd(*rule_args, name=f"gdn{j}_rule_bwd")
            d_hm, d_hs, d_cw, d_al, d_dtb = _gdn_pre_bwd(sv["hm"], sv["hs"], sv["cw8"], sv["alog"], sv["dtb"],
                                                         dq, dk, dv, dbg, d_z, Hg, tr=TR, name=f"gdn{j}_pre_bwd")
            g["gdn_norm_g"][j] = d_ng[0]
            g["gdn_conv_w"][j] = d_cw[:GDN_CONV]
            g["gdn_a_log"][j] = d_al[0, Hg:2 * Hg]
            g["gdn_dt_bias"][j] = d_dtb[0, Hg:2 * Hg]
            wname, nsmall = "gdn_w_in", 2 * Hg
        else:
            g["fox_w_out"][j] = _mm_tn(sv["on"], du, f"fox{j}_out_dw")
            d_o, d_z, delta_b = _fox_post_bwd(sv["o"], sv["hm"], du, wts["fox_w_out"][j], tr=TR, name=f"fox{j}_post_bwd")
            dqn, dkn, dvv, dcr, dct = _fox_attn_bwd(sv["qn"], sv["kn"], sv["vb"], sv["c_b"], sv["c_rowp"], sv["lse_b"],
                                                    delta_b, d_o, tb=ATTN_TILE, name=f"fox{j}_attn_bwd")
            dc = _pad_lanes(dcr[:, :2, :].reshape(Hf, -1).T + dct[:, :, :2].transpose(1, 0, 2).reshape(-1, Hf))
            d_hm, d_hs, d_gq, d_gk, d_bf = _fox_pre_bwd(sv["hm"], sv["hs"], sv["gq2"], sv["gk2"], sv["bf"],
                                                        dqn, dkn, dvv, d_z, dc, Hf, tr=TR, name=f"fox{j}_pre_bwd")
            g["fox_q_norm_g"][j] = d_gq[0, :FOX_DH] + d_gq[0, FOX_DH:]
            g["fox_k_norm_g"][j] = d_gk[0, :FOX_DH] + d_gk[0, FOX_DH:]
            g["fox_b_f"][j] = d_bf[0, :Hf]
            wname, nsmall = "fox_w_in", Hf
        dwm = _mm_tn(sv["x"], d_hm, f"{wname}{j}_main_dw")
        dws = _mm_tn(sv["x"], d_hs, f"{wname}{j}_small_dw")
        g[wname][j] = jnp.concatenate([dwm, dws[:, :nsmall]], axis=1)
        dx = _in_proj_bwd_x(d_hm, d_hs, du, sv["wm"], sv["ws"], alpha, tr=2 * TR, name=f"{wname}{j}_dx")
    return loss_row, dx, g, got_early


_SHARDED = (("ple_w_gate", 1), ("ple_w_proj", 2), ("gdn_w_in", 2), ("gdn_conv_w", 2), ("gdn_w_out", 1),
            ("fox_w_in", 2), ("fox_w_out", 1))
_SHARD_AXIS = dict(_SHARDED)
_REPLICATED = ("ln_g", "ln_b", "gdn_a_log", "gdn_dt_bias", "gdn_norm_g", "fox_b_f", "fox_q_norm_g", "fox_k_norm_g")
_EXACT = ("gdn_conv_w",)
_ORDER = ("ln_g", "ln_b", "ple_w_gate", "ple_w_proj", "gdn_w_in", "gdn_conv_w", "gdn_a_log", "gdn_dt_bias",
          "gdn_norm_g", "gdn_w_out", "fox_w_in", "fox_b_f", "fox_q_norm_g", "fox_k_norm_g", "fox_w_out")
_FIRST_WEIGHTS = (("gdn_w_in", 0), ("gdn_conv_w", 0))
_LAST_GRADS = (("gdn_w_in", 0), ("gdn_conv_w", 0))
PACK_ROWS = 2 * SUBLANES


def _as_rows(a):
    rows = a.reshape(-1, LANES)
    return jnp.pad(rows, ((0, -rows.shape[0] % PACK_ROWS), (0, 0)))


def _n_elements(shape):
    n = 1
    for d in shape:
        n *= d
    return n


def _pack_weights(local, items):
    parts = []
    for name, idx in items:
        w = local[name][idx]
        parts.append(_as_rows(lax.bitcast_convert_type(w, BF16) if name in _EXACT else w.astype(BF16)))
    return jnp.concatenate(parts, axis=0), [q.shape[0] for q in parts]


def _unpack_weights(got, local, items, sizes):
    out, r0 = {}, 0
    for (name, idx), nrow in zip(items, sizes):
        shp = tuple(local[name].shape[1:])
        n_el = _n_elements(shp) * (2 if name in _EXACT else 1)
        seg = got[:, r0:r0 + nrow].reshape(N_CHIPS, -1)[:, :n_el]
        r0 += nrow
        if name in _EXACT:
            blocks = lax.bitcast_convert_type(seg.reshape((N_CHIPS,) + shp + (2,)), F32)
        else:
            blocks = seg.reshape((N_CHIPS,) + shp)
        axis = _SHARD_AXIS[name] - 1
        joined = shp[:axis] + (N_CHIPS * shp[axis],) + shp[axis + 1:]
        out[(name, idx)] = jnp.moveaxis(blocks, 0, axis).reshape(joined)
    return out


def _pack_grads(g, items):
    parts = []
    for name, idx in items:
        gfull = g[name][idx]
        axis = _SHARD_AXIS[name] - 1
        shp = gfull.shape
        split = shp[:axis] + (N_CHIPS, shp[axis] // N_CHIPS) + shp[axis + 1:]
        rows = jnp.moveaxis(gfull.reshape(split), axis, 0).astype(BF16).reshape(N_CHIPS, -1, LANES)
        parts.append(jnp.pad(rows, ((0, 0), (0, -rows.shape[1] % PACK_ROWS), (0, 0))))
    return jnp.concatenate(parts, axis=1), [q.shape[1] for q in parts]


def _unpack_grads(flat, local, items, sizes):
    out, r0 = {}, 0
    for (name, idx), nrow in zip(items, sizes):
        shp = tuple(local[name].shape[1:])
        out[(name, idx)] = flat[r0:r0 + nrow].reshape(-1)[:_n_elements(shp)].reshape(shp)
        r0 += nrow
    return out


def _reduce_replicated(grads, loss_part):
    rows = [_pad_lanes(jnp.reshape(loss_part, (1, 1)))]
    for name in _REPLICATED:
        gr = grads[name]
        rows.append(gr.reshape(-1, LANES) if gr.shape[-1] % LANES == 0 else _pad_lanes(gr))
    sizes = [r.shape[0] for r in rows]
    blk = jnp.concatenate(rows, axis=0)
    nrow = blk.shape[0]
    npad = -nrow % SUBLANES
    blk = jnp.pad(blk, ((0, npad), (0, 0)))
    allb = _all_gather8(blk, name="gather_small_grads").reshape(N_DEV, nrow + npad, LANES)
    tot = _sum_slots(allb, tr=nrow + npad, name="sum_small_grads")
    out, r0 = {}, sizes[0]
    loss = tot[0, 0]
    for name, n in zip(_REPLICATED, sizes[1:]):
        gr = grads[name]
        seg = tot[r0:r0 + n]
        out[name] = seg.reshape(gr.shape) if gr.shape[-1] % LANES == 0 else seg[:, :gr.shape[-1]]
        r0 += n
    return loss, out


def kernel(x, p, ln_g, ln_b, ple_w_gate, ple_w_proj, gdn_w_in, gdn_conv_w, gdn_a_log, gdn_dt_bias, gdn_norm_g, gdn_w_out, fox_w_in, fox_b_f, fox_q_norm_g, fox_k_norm_g, fox_w_out, loss_target, m_ln_g, m_ln_b, m_ple_w_gate, m_ple_w_proj, m_gdn_w_in, m_gdn_conv_w, m_gdn_a_log, m_gdn_dt_bias, m_gdn_norm_g, m_gdn_w_out, m_fox_w_in, m_fox_b_f, m_fox_q_norm_g, m_fox_k_norm_g, m_fox_w_out, v_ln_g, v_ln_b, v_ple_w_gate, v_ple_w_proj, v_gdn_w_in, v_gdn_conv_w, v_gdn_a_log, v_gdn_dt_bias, v_gdn_norm_g, v_gdn_w_out, v_fox_w_in, v_fox_b_f, v_fox_q_norm_g, v_fox_k_norm_g, v_fox_w_out):
    local = dict(ln_g=ln_g, ln_b=ln_b, ple_w_gate=ple_w_gate, ple_w_proj=ple_w_proj, gdn_w_in=gdn_w_in,
                 gdn_conv_w=gdn_conv_w, gdn_a_log=gdn_a_log, gdn_dt_bias=gdn_dt_bias, gdn_norm_g=gdn_norm_g,
                 gdn_w_out=gdn_w_out, fox_w_in=fox_w_in, fox_b_f=fox_b_f, fox_q_norm_g=fox_q_norm_g,
                 fox_k_norm_g=fox_k_norm_g, fox_w_out=fox_w_out)
    mom_m = dict(ln_g=m_ln_g, ln_b=m_ln_b, ple_w_gate=m_ple_w_gate, ple_w_proj=m_ple_w_proj, gdn_w_in=m_gdn_w_in,
                 gdn_conv_w=m_gdn_conv_w, gdn_a_log=m_gdn_a_log, gdn_dt_bias=m_gdn_dt_bias, gdn_norm_g=m_gdn_norm_g,
                 gdn_w_out=m_gdn_w_out, fox_w_in=m_fox_w_in, fox_b_f=m_fox_b_f, fox_q_norm_g=m_fox_q_norm_g,
                 fox_k_norm_g=m_fox_k_norm_g, fox_w_out=m_fox_w_out)
    mom_v = dict(ln_g=v_ln_g, ln_b=v_ln_b, ple_w_gate=v_ple_w_gate, ple_w_proj=v_ple_w_proj, gdn_w_in=v_gdn_w_in,
                 gdn_conv_w=v_gdn_conv_w, gdn_a_log=v_gdn_a_log, gdn_dt_bias=v_gdn_dt_bias, gdn_norm_g=v_gdn_norm_g,
                 gdn_w_out=v_gdn_w_out, fox_w_in=v_fox_w_in, fox_b_f=v_fox_b_f, fox_q_norm_g=v_fox_q_norm_g,
                 fox_k_norm_g=v_fox_k_norm_g, fox_w_out=v_fox_w_out)

    items = [(name, idx) for name, _ in _SHARDED for idx in range(local[name].shape[0])]
    w_first = [it for it in items if it in _FIRST_WEIGHTS]
    w_later = [it for it in items if it not in _FIRST_WEIGHTS]
    g_early = [it for it in items if it not in _LAST_GRADS]
    g_last = [it for it in items if it in _LAST_GRADS]

    wts = {name: [None] * local[name].shape[0] for name, _ in _SHARDED}
    for name in _REPLICATED:
        wts[name] = local[name]
    packed, sizes = _pack_weights(local, w_first)
    got = _xy_exchange(packed, gather=True, name="gather_weights_first")
    for (name, idx), arr in _unpack_weights(got, local, w_first, sizes).items():
        wts[name][idx] = arr
    def layer_of(item):
        name, idx = item
        return idx if name.startswith("ple") else 2 * idx + (1 if name.startswith("fox") else 0)

    late_weights = []
    for group in ([it for it in w_later if layer_of(it) < 2], [it for it in w_later if layer_of(it) >= 2]):
        packed_w, sizes_w = _pack_weights(local, group)
        late_weights.append((packed_w, functools.partial(_unpack_weights, local=local, items=group, sizes=sizes_w)))
    early_sizes = []

    def pack_early(g):
        packed_g, sz = _pack_grads(g, g_early)
        early_sizes.extend(sz)
        return packed_g

    loss_row, dx, g, got_early = _local_step(
        x[0], p[:, 0], loss_target[0], wts,
        late_weights=late_weights,
        early_grads=pack_early)

    loss, small = _reduce_replicated({name: jnp.stack(g[name]) for name in _REPLICATED}, jnp.sum(loss_row))
    packed_last, last_sizes = _pack_grads(g, g_last)
    got_last = _xy_exchange(packed_last, gather=False, name="exchange_grads_last")
    mine, other = {}, {}
    for tag, res, its, szs in (("early", got_early, g_early, early_sizes), ("last", got_last, g_last, last_sizes)):
        part = _sum_slots(res, tr=4096, name=f"sum_grads_{tag}")
        sib = _c_swap(part, name=f"swap_grads_{tag}")
        mine.update(_unpack_grads(part, local, its, szs))
        other.update(_unpack_grads(sib, local, its, szs))

    outs = {}
    for name in _ORDER:
        if name in _SHARD_AXIS:
            n_layers = local[name].shape[0]
            parts = [jnp.stack([mine[(name, i)] for i in range(n_layers)]),
                     jnp.stack([other[(name, i)] for i in range(n_layers)])]
        else:
            parts = [small[name]]
        outs[name] = _adamw(local[name], parts, mom_m[name], mom_v[name], name=f"adamw_{name}")
    return (loss, dx[None], *[outs[n][0] for n in _ORDER], *[outs[n][1] for n in _ORDER],
            *[outs[n][2] for n in _ORDER], *[outs[n][3] for n in _ORDER])
```

```python
import functools

import jax
import jax.numpy as jnp
from jax import lax
from jax.experimental import pallas as pl
from jax.experimental.pallas import tpu as pltpu

F32 = jnp.float32
BF16 = jnp.bfloat16

LANES = 128
SUBLANES = 8
VMEM_LIMIT_BYTES = 56 * 1024 * 1024

GDN_DK = 128
GDN_CHUNK = 64
GDN_CONV = 4
FOX_DH = 64
LN_EPS = 1e-5
RMS_EPS = 1e-6

ADAM_LR = 0.001
ADAM_B1 = 0.9
ADAM_B2 = 0.999
ADAM_EPS = 1e-08
ADAM_WD = 0.01
ADAM_STEP = 10

_DIMS = {"nn": (((1,), (0,)), ((), ())), "nt": (((1,), (1,)), ((), ())), "tn": (((0,), (0,)), ((), ()))}


def _params(*sem):
    return pltpu.CompilerParams(dimension_semantics=sem, vmem_limit_bytes=VMEM_LIMIT_BYTES)


def _split(a, terms):
    out = []
    rest = a.astype(F32)
    for t in range(terms):
        piece = rest.astype(BF16)
        out.append(piece)
        if t + 1 < terms:
            rest = rest - piece.astype(F32)
    return out


def _raw_mm(a, b, form, mode):
    dot = lambda x, y: lax.dot_general(x, y, _DIMS[form], preferred_element_type=F32)
    if mode == "b":
        return dot(a.astype(BF16), b.astype(BF16))
    if mode == "x3":
        (ah, al), (bh, bl) = _split(a, 2), _split(b, 2)
        return dot(ah, bh) + (dot(ah, bl) + dot(al, bh))
    if mode == "ca":
        ac = a.astype(BF16)
        b1, b2, b3 = _split(b, 3)
        return dot(ac, b1) + (dot(ac, b2) + dot(ac, b3))
    assert mode == "cb", mode
    bc = b.astype(BF16)
    a1, a2, a3 = _split(a, 3)
    return dot(a1, bc) + (dot(a2, bc) + dot(a3, bc))


@functools.partial(jax.custom_vjp, nondiff_argnums=(2, 3))
def _mm(a, b, form, mode):
    return _raw_mm(a, b, form, mode)


def _mm_fwd(a, b, form, mode):
    return _raw_mm(a, b, form, mode), (a, b)


def _mm_bwd(form, mode, res, g):
    a, b = res
    flip = {"b": "b", "x3": "x3", "ca": "cb", "cb": "ca"}[mode]
    if form == "nn":
        da, db = (lambda: _mm(g, b, "nt", mode)), (lambda: _mm(a, g, "tn", mode))
    elif form == "nt":
        da, db = (lambda: _mm(g, b, "nn", mode)), (lambda: _mm(g, a, "tn", flip))
    else:
        da, db = (lambda: _mm(b, g, "nt", flip)), (lambda: _mm(a, g, "nn", mode))
    return (jnp.zeros_like(a) if mode == "ca" else da()), (jnp.zeros_like(b) if mode == "cb" else db())


_mm.defvjp(_mm_fwd, _mm_bwd)


@jax.custom_vjp
def _tri_inv(Ls):
    C = Ls[0].shape[0]
    eye = (_iota((C, C), 0) == _iota((C, C), 1)).astype(F32)
    X = [eye - L for L in Ls]
    P = [_raw_mm(L, L, "nn", "x3") for L in Ls]
    n_sq = max(1, (C - 1).bit_length() - 1)
    for it in range(n_sq):
        XP = [_raw_mm(x, p, "nn", "x3") for x, p in zip(X, P)]
        if it < n_sq - 1:
            P = [_raw_mm(p, p, "nn", "x3") for p in P]
        X = [x + xp for x, xp in zip(X, XP)]
    return X


def _tri_inv_fwd(Ls):
    Ts = _tri_inv(Ls)
    return Ts, Ts


def _tri_inv_bwd(Ts, dTs):
    Ms = [_raw_mm(dT, T, "nt", "x3") for dT, T in zip(dTs, Ts)]
    return ([-_raw_mm(T, M, "tn", "x3") for T, M in zip(Ts, Ms)],)


_tri_inv.defvjp(_tri_inv_fwd, _tri_inv_bwd)


@jax.custom_vjp
def _tri_inv_known(Ls, Ts):
    return Ts


def _tri_inv_known_fwd(Ls, Ts):
    return Ts, Ts


def _tri_inv_known_bwd(Ts, dTs):
    return _tri_inv_bwd(Ts, dTs)[0], [jnp.zeros_like(T) for T in Ts]


_tri_inv_known.defvjp(_tri_inv_known_fwd, _tri_inv_known_bwd)


def _silu(x):
    return x * jax.nn.sigmoid(x)


def _softplus(x):
    return jnp.maximum(x, 0.0) + jnp.log1p(jnp.exp(-jnp.abs(x)))


def _iota(shape, dim):
    return lax.broadcasted_iota(jnp.int32, shape, dim)


def _matmul(a, b, *, ta=False, tb=False, out_dtype=F32, add=None, add_scale=1.0, tm=512, tn=512, tk=512, name):
    if ta:
        K, M = a.shape
    else:
        M, K = a.shape
    if tb:
        N, K2 = b.shape
    else:
        K2, N = b.shape
    assert K == K2, (a.shape, b.shape, ta, tb)
    tm, tn, tk = min(tm, M), min(tn, N), min(tk, K)
    assert M % tm == 0 and N % tn == 0 and K % tk == 0, (M, N, K, tm, tn, tk)
    nk = K // tk
    form = ("t" if ta else "n") + ("t" if tb else "n")
    dims = (((0 if ta else 1,), (1 if tb else 0,)), ((), ()))
    del form
    a_spec = pl.BlockSpec((tk, tm), lambda i, j, k: (k, i)) if ta else pl.BlockSpec((tm, tk), lambda i, j, k: (i, k))
    b_spec = pl.BlockSpec((tn, tk), lambda i, j, k: (j, k)) if tb else pl.BlockSpec((tk, tn), lambda i, j, k: (k, j))
    o_spec = pl.BlockSpec((tm, tn), lambda i, j, k: (i, j))
    has_add = add is not None

    def body(*refs):
        a_ref, b_ref = refs[:2]
        add_ref = refs[2] if has_add else None
        o_ref = refs[3] if has_add else refs[2]
        acc_ref = refs[-1] if nk > 1 else None
        k = pl.program_id(2)
        part = lax.dot_general(a_ref[...].astype(BF16), b_ref[...].astype(BF16), dims, preferred_element_type=F32)

        def finish(total):
            if has_add:
                total = total + add_scale * add_ref[...].astype(F32)
            o_ref[...] = total.astype(o_ref.dtype)

        if nk == 1:
            finish(part)
        else:
            @pl.when(k == 0)
            def _():
                acc_ref[...] = part

            @pl.when(jnp.logical_and(k > 0, k < nk - 1))
            def _():
                acc_ref[...] += part

            @pl.when(k == nk - 1)
            def _():
                finish(acc_ref[...] + part)

    in_specs = [a_spec, b_spec] + ([o_spec] if has_add else [])
    args = (a, b) + ((add,) if has_add else ())
    return pl.pallas_call(
        body, name=name, grid=(M // tm, N // tn, nk),
        in_specs=in_specs, out_specs=o_spec,
        out_shape=jax.ShapeDtypeStruct((M, N), out_dtype),
        scratch_shapes=[pltpu.VMEM((tm, tn), F32)] if nk > 1 else [],
        compiler_params=_params("parallel", "parallel", "arbitrary"),
    )(*args)


def _rowcall(body_fn, rows, consts, out_rows, out_accs, *, tr, name, reverse=False, scratch=()):
    def arr_spec(r):
        return r if isinstance(r, tuple) else (r, None)

    S = arr_spec(rows[0])[0].shape[0]
    tr = min(tr, S)
    assert S % tr == 0
    n = S // tr
    ridx = (lambda i: (n - 1 - i, 0)) if reverse else (lambda i: (i, 0))
    in_specs, args = [], []
    for r in rows:
        arr, spec = arr_spec(r)
        args.append(arr)
        in_specs.append(spec(tr, n) if spec is not None else pl.BlockSpec((tr, arr.shape[1]), ridx))
    for c in consts:
        args.append(c)
        in_specs.append(pl.BlockSpec(c.shape, lambda i: (0, 0)))
    out_specs, out_shape = [], []
    for (ncol, dt) in out_rows:
        out_specs.append(pl.BlockSpec((tr, ncol), ridx))
        out_shape.append(jax.ShapeDtypeStruct((S, ncol), dt))
    for shp in out_accs:
        out_specs.append(pl.BlockSpec(shp, lambda i: (0, 0)))
        out_shape.append(jax.ShapeDtypeStruct(shp, F32))
    nr, nc, no, na = len(rows), len(consts), len(out_rows), len(out_accs)

    def kernel(*refs):
        row_refs = refs[:nr]
        const_refs = refs[nr:nr + nc]
        orow_refs = refs[nr + nc:nr + nc + no]
        oacc_refs = refs[nr + nc + no:nr + nc + no + na]
        scr = refs[nr + nc + no + na:]
        step = pl.program_id(0)
        blk = (n - 1 - step) if reverse else step

        @pl.when(step == 0)
        def _():
            for acc in oacc_refs:
                acc[...] = jnp.zeros(acc.shape, F32)

        body_fn(row_refs, const_refs, orow_refs, oacc_refs, scr, step, blk)

    outs = pl.pallas_call(
        kernel, name=name, grid=(n,), in_specs=in_specs, out_specs=out_specs, out_shape=out_shape,
        scratch_shapes=list(scratch), compiler_params=_params("arbitrary"),
    )(*args)
    return outs


def _row(v):
    return v.astype(F32).reshape(1, -1)


def _pad_lanes(v, width=LANES, offset=0):
    pad = [(0, 0)] * (v.ndim - 1) + [(offset, width - offset - v.shape[-1])]
    return jnp.pad(v, pad)


def _ln_tile(x, y, g, b, alpha):
    u = alpha * x + y
    mu = jnp.mean(u, -1, keepdims=True)
    d = u - mu
    var = jnp.mean(d * d, -1, keepdims=True)
    return d * lax.rsqrt(var + LN_EPS) * g + b


def _in_proj_fwd(x, wm, ws, *, tr, name):
    def body(rows, consts, orows, oaccs, scr, step, blk):
        xb = rows[0][...].astype(BF16)
        orows[0][...] = lax.dot_general(xb, consts[0][...], _DIMS["nn"], preferred_element_type=F32)
        orows[1][...] = lax.dot_general(xb, consts[1][...], _DIMS["nn"], preferred_element_type=F32)

    return _rowcall(body, [x], [wm, ws], [(wm.shape[1], F32), (ws.shape[1], F32)], [], tr=tr, name=name)


def _in_proj_bwd_x(d_hm, d_hs, du, wm, ws, alpha, *, tr, name):
    D = du.shape[1]

    def body(rows, consts, orows, oaccs, scr, step, blk):
        acc = lax.dot_general(rows[0][...], consts[0][...], _DIMS["nt"], preferred_element_type=F32)
        acc = acc + lax.dot_general(rows[1][...], consts[1][...], _DIMS["nt"], preferred_element_type=F32)
        orows[0][...] = acc + alpha * rows[2][...]

    return _rowcall(body, [d_hm, d_hs, du], [wm, ws], [(D, F32)], [], tr=tr, name=name)[0]


def _out_ln_fwd(on, w_out, x, g, b, alpha, *, tr, name):
    D = x.shape[1]

    def body(rows, consts, orows, oaccs, scr, step, blk):
        y = lax.dot_general(rows[0][...].astype(BF16), consts[0][...], _DIMS["nn"], preferred_element_type=F32)
        orows[0][...] = y
        orows[1][...] = _ln_tile(rows[1][...], y, consts[1][...], consts[2][...], alpha)

    return _rowcall(body, [on, x], [w_out, _row(g), _row(b)], [(D, F32), (D, F32)], [], tr=tr, name=name)


def _gate_mix_fwd(x_ln, w_gate, pp, *, tr, name):
    D = x_ln.shape[1]

    def body(rows, consts, orows, oaccs, scr, step, blk):
        a = rows[0][...]
        gp = lax.dot_general(a.astype(BF16), consts[0][...], _DIMS["nn"], preferred_element_type=F32)
        orows[0][...] = gp
        orows[1][...] = a + jax.nn.sigmoid(gp) * rows[1][...]

    return _rowcall(body, [x_ln, pp], [w_gate], [(D, F32), (D, F32)], [], tr=tr, name=name)


def _ple_ln_bwd(dxo, gp, pp, w_gate, x, y, g, b, alpha, *, tr, name):
    D = x.shape[1]

    def body(rows, consts, orows, oaccs, scr, step, blk):
        d = rows[0][...]
        s = jax.nn.sigmoid(rows[1][...])
        d_pre = (d * rows[2][...] * s * (1.0 - s)).astype(BF16)
        orows[0][...] = d_pre
        orows[1][...] = (d * s).astype(BF16)
        ct = d + lax.dot_general(d_pre, consts[0][...], _DIMS["nt"], preferred_element_type=F32)
        xv, yv = rows[3][...], rows[4][...]
        _, vjp = jax.vjp(lambda yy, gg, bb: _ln_tile(xv, yy, gg, bb, alpha), yv, consts[1][...], consts[2][...])
        du, dg, db = vjp(ct)
        orows[2][...] = du
        oaccs[0][...] += dg
        oaccs[1][...] += db

    d_pre, d_pp, du, dg, db = _rowcall(body, [dxo, gp, pp, x, y], [w_gate, _row(g), _row(b)],
                                       [(D, BF16), (D, BF16), (D, F32)], [(1, D), (1, D)], tr=tr, name=name)
    return d_pre, d_pp, du, dg[0], db[0]


def _loss_fwd_bwd(xf, target, *, tr, name):
    D = xf.shape[1]

    def body(rows, consts, orows, oaccs, scr, step, blk):
        err = rows[0][...] - rows[1][...]
        orows[0][...] = err * (1.0 / D)
        part = jnp.sum(err * err, axis=0, keepdims=True) * (0.5 / D)
        oaccs[0][...] += part

    dx, lrow = _rowcall(body, [xf, target], [], [(D, F32)], [(1, D)], tr=tr, name=name)
    return lrow, dx


def _gdn_qk_tile(c):
    y = _silu(c)
    return y * lax.rsqrt(jnp.sum(y * y, -1, keepdims=True) + RMS_EPS)


def _make_bg_fn(H):
    def fn(hs, alog, dtb):
        lane = _iota((1, LANES), 1)
        beta = jax.nn.sigmoid(hs)
        g = -jnp.exp(alog) * _softplus(hs + dtb)
        return jnp.where(lane < H, beta, jnp.where(lane < 2 * H, g, 0.0))
    return fn


def _halo_spec(ncol):
    def make(tr, n):
        per = tr // SUBLANES
        return pl.BlockSpec((SUBLANES, ncol), lambda i: (jnp.maximum(i * per - 1, 0), 0))
    return make


def _halo_spec_rev(ncol):
    def make(tr, n):
        per = tr // SUBLANES
        return pl.BlockSpec((SUBLANES, ncol), lambda i: (jnp.maximum((n - 1 - i) * per - 1, 0), 0))
    return make


def _gdn_pre_fwd(h_main, h_small, conv_w8, alog_row, dtb_row, H, *, tr, name):
    W = H * GDN_DK
    C3 = 3 * W
    bg_fn = _make_bg_fn(H)

    def body(rows, consts, orows, oaccs, scr, step, blk):
        main_ref, halo_ref, hs_ref = rows
        w_ref, alog_ref, dtb_ref = consts
        q_ref, k_ref, v_ref, bg_ref = orows
        xs = scr[0]
        trr = main_ref.shape[0]
        xs[pl.ds(SUBLANES, trr), :] = main_ref[...]
        xs[pl.ds(0, SUBLANES), :] = jnp.where(blk > 0, halo_ref[...], 0.0)
        for s in range(C3 // LANES):
            ls = slice(s * LANES, (s + 1) * LANES)
            c = jnp.zeros((trr, LANES), F32)
            for j in range(GDN_CONV):
                c = c + w_ref[GDN_CONV - 1 - j:GDN_CONV - j, ls] * xs[pl.ds(SUBLANES - j, trr), ls]
            if s < 2 * H:
                out = _gdn_qk_tile(c)
                (q_ref if s < H else k_ref)[:, (s % H) * LANES:(s % H + 1) * LANES] = out
            else:
                v_ref[:, (s - 2 * H) * LANES:(s - 2 * H + 1) * LANES] = _silu(c)
        bg_ref[...] = bg_fn(hs_ref[...], alog_ref[...], dtb_ref[...])

    main = (h_main, lambda tr_, n: pl.BlockSpec((tr_, C3), lambda i: (i, 0)))
    halo = (h_main, _halo_spec(C3))
    trr = min(tr, h_main.shape[0])
    return _rowcall(body, [main, halo, h_small], [conv_w8, alog_row, dtb_row],
                    [(W, F32), (W, F32), (W, F32), (LANES, F32)], [], tr=tr, name=name,
                    scratch=[pltpu.VMEM((trr + SUBLANES, C3), F32)])


def _gdn_pre_bwd(h_main, h_small, conv_w8, alog_row, dtb_row, dq, dk, dv, dbg, dz, H, *, tr, name):
    W = H * GDN_DK
    C3 = 3 * W
    bg_fn = _make_bg_fn(H)

    def body(rows, consts, orows, oaccs, scr, step, blk):
        main_ref, halo_ref, hs_ref, dq_ref, dk_ref, dv_ref, dbg_ref, dz_ref = rows
        w_ref, alog_ref, dtb_ref = consts
        dmain_ref, dhs_ref = orows
        dw_ref, dalog_ref, ddtb_ref = oaccs
        xs, dcs = scr
        trr = main_ref.shape[0]
        xs[pl.ds(SUBLANES, trr), :] = main_ref[...]
        xs[pl.ds(0, SUBLANES), :] = jnp.where(blk > 0, halo_ref[...], 0.0)

        @pl.when(step == 0)
        def _():
            dcs[pl.ds(trr, SUBLANES), :] = jnp.zeros((SUBLANES, C3), F32)

        for s in range(C3 // LANES):
            ls = slice(s * LANES, (s + 1) * LANES)
            c = jnp.zeros((trr, LANES), F32)
            for j in range(GDN_CONV):
                c = c + w_ref[GDN_CONV - 1 - j:GDN_CONV - j, ls] * xs[pl.ds(SUBLANES - j, trr), ls]
            if s < 2 * H:
                src = dq_ref if s < H else dk_ref
                ct = src[:, (s % H) * LANES:(s % H + 1) * LANES]
                _, vjp = jax.vjp(_gdn_qk_tile, c)
            else:
                ct = dv_ref[:, (s - 2 * H) * LANES:(s - 2 * H + 1) * LANES]
                _, vjp = jax.vjp(_silu, c)
            dcs[pl.ds(0, trr), ls] = vjp(ct)[0]
        for s in range(C3 // LANES):
            ls = slice(s * LANES, (s + 1) * LANES)
            dx = jnp.zeros((trr, LANES), F32)
            dc0 = dcs[pl.ds(0, trr), ls]
            for j in range(GDN_CONV):
                wrow = w_ref[GDN_CONV - 1 - j:GDN_CONV - j, ls]
                dx = dx + wrow * dcs[pl.ds(j, trr), ls]
                dw_ref[GDN_CONV - 1 - j:GDN_CONV - j, ls] += jnp.sum(dc0 * xs[pl.ds(SUBLANES - j, trr), ls], axis=0, keepdims=True)
            dmain_ref[:, ls] = dx.astype(BF16)
        dmain_ref[:, C3:] = dz_ref[...]
        dcs[pl.ds(trr, SUBLANES), :] = dcs[pl.ds(0, SUBLANES), :]
        _, vjp = jax.vjp(bg_fn, hs_ref[...], alog_ref[...], dtb_ref[...])
        dhs, dalog, ddtb = vjp(dbg_ref[...])
        dhs_ref[...] = dhs.astype(BF16)
        dalog_ref[...] += dalog
        ddtb_ref[...] += ddtb

    trr = min(tr, h_main.shape[0])
    main = (h_main, lambda tr_, n: pl.BlockSpec((tr_, C3), lambda i: (n - 1 - i, 0)))
    halo = (h_main, _halo_spec_rev(C3))
    return _rowcall(body, [main, halo, h_small, dq, dk, dv, dbg, dz], [conv_w8, alog_row, dtb_row],
                    [(4 * W, BF16), (LANES, BF16)], [(SUBLANES, C3), (1, LANES), (1, LANES)],
                    tr=tr, name=name, reverse=True,
                    scratch=[pltpu.VMEM((trr + SUBLANES, C3), F32), pltpu.VMEM((trr + SUBLANES, C3), F32)])


def _gdn_chunk(qs, ks, vs, betas, gs, Ss, Ts=None, with_inverse=False):
    C, dk = qs[0].shape
    dv = vs[0].shape[1]
    ri, ci = _iota((C, C), 0), _iota((C, C), 1)
    causal, strict = ri >= ci, ri > ci
    tril = causal.astype(F32)
    lane0 = (_iota((1, LANES), 1) == 0).astype(F32)
    e0 = jnp.ones((C, 1), F32) * lane0
    last = (_iota((C, 1), 0) == C - 1).astype(F32)

    def each(f, *lists):
        return [f(*a) for a in zip(*lists)]

    G = each(lambda g: g * jnp.ones((1, LANES), F32), gs)
    gcB = each(lambda x: _mm(tril, x, "nn", "ca"), G)
    gc = each(lambda x: jnp.sum(x * lane0, -1, keepdims=True), gcB)
    gc_row = each(lambda x: _mm(e0, x, "nt", "ca"), gcB)
    decay = each(lambda a, b: jnp.where(causal, jnp.exp(jnp.where(causal, a - b, 0.0)), 0.0), gc, gc_row)
    kb = each(lambda k, b: k * b, ks, betas)
    kk = each(lambda a, k: _mm(a, k, "nt", "b"), kb, ks)
    L = each(lambda a, d: jnp.where(strict, a * d, 0.0), kk, decay)
    X = _tri_inv(L) if Ts is None else _tri_inv_known(L, Ts)
    egc = each(jnp.exp, gc)
    u = each(lambda x, v, b: _mm(x, v * b, "nn", "x3"), X, vs, betas)
    w = each(lambda x, a, e: _mm(x, a * e, "nn", "x3"), X, kb, egc)
    qsc = each(lambda q: q * (dk ** -0.5), qs)
    qk = each(lambda q, k: _mm(q, k, "nt", "b"), qsc, ks)
    A = each(lambda a, d: jnp.where(causal, a * d, 0.0), qk, decay)
    q_dec = each(lambda q, e: q * e, qsc, egc)
    gl = each(lambda x: jnp.sum(x * last, keepdims=True), gc)
    k_dec = each(lambda k, a, b: k * jnp.exp(a - b), ks, gl, gc)
    wS = each(lambda a, s: _mm(a, s, "nn", "b"), w, Ss)
    qS = each(lambda a, s: _mm(a, s, "nn", "b"), q_dec, Ss)
    v_new = each(lambda a, b: a - b, u, wS)
    Av = each(lambda a, b: _mm(a, b, "nn", "b"), A, v_new)
    kv = each(lambda a, b: _mm(a, b, "tn", "b"), k_dec, v_new)
    o = each(lambda a, b: a + b, qS, Av)
    S_new = each(lambda s, e, x: s * jnp.exp(e) + x, Ss, gl, kv)
    return (o, S_new, X) if with_inverse else (o, S_new)


def _gdn_rule_fwd(q, k, v, bg, H, *, name, carry=None):
    S_len = q.shape[0]
    C = min(GDN_CHUNK, S_len)
    N = S_len // C
    dk = dv = GDN_DK

    def body(*refs):
        if carry is None:
            q_ref, k_ref, v_ref, bg_ref, o_ref, st_ref, ti_ref, s_scr = refs
        else:
            q_ref, k_ref, v_ref, bg_ref, src_ref, o_ref, st_ref, ti_ref, got_ref, s_scr = refs[:10]
            xy = (src_ref, got_ref) + tuple(refs[10:]) + (carry[1],)
        n = pl.program_id(0)

        @pl.when(n == 0)
        def _():
            s_scr[...] = jnp.zeros(s_scr.shape, F32)
            if carry is not None:
                _xy_start(*xy)

        bgt = bg_ref[...]
        sl = [slice(h * dk, (h + 1) * dk) for h in range(H)]
        Ss = [s_scr[h] for h in range(H)]
        for h in range(H):
            st_ref[h] = Ss[h]
        os_, S_new, Ts = _gdn_chunk([q_ref[:, s] for s in sl], [k_ref[:, s] for s in sl], [v_ref[:, s] for s in sl],
                                    [bgt[:, h:h + 1] for h in range(H)], [bgt[:, H + h:H + h + 1] for h in range(H)],
                                    Ss, with_inverse=True)
        for h in range(H):
            o_ref[:, sl[h]] = os_[h]
            s_scr[h] = S_new[h]
            ti_ref[h] = Ts[h]

        if carry is not None:
            @pl.when(n == N - 1)
            def _():
                _xy_wait(*xy)

    rows = pl.BlockSpec((C, H * dk), lambda n: (n, 0))
    extra = carry is not None
    return pl.pallas_call(
        body, name=name, grid=(N,),
        in_specs=[rows, rows, rows, pl.BlockSpec((C, LANES), lambda n: (n, 0))] + ([_ANY] if extra else []),
        out_specs=[rows, pl.BlockSpec((H, dk, dv), lambda n: (n, 0, 0)), pl.BlockSpec((H, C, C), lambda n: (n, 0, 0))]
        + ([_ANY] if extra else []),
        out_shape=[jax.ShapeDtypeStruct((S_len, H * dv), F32), jax.ShapeDtypeStruct((N * H, dk, dv), F32),
                   jax.ShapeDtypeStruct((N * H, C, C), F32)] + ([_xy_out_shape(carry[0])] if extra else []),
        scratch_shapes=[pltpu.VMEM((H, dk, dv), F32)] + (_xy_sems() if extra else []),
        compiler_params=_params("arbitrary"),
    )(q, k, v, bg, *((carry[0],) if extra else ()))


def _gdn_rule_bwd(q, k, v, bg, states, tinv, do, H, *, name, carry=None):
    S_len = q.shape[0]
    C = min(GDN_CHUNK, S_len)
    N = S_len // C
    dk = dv = GDN_DK

    def body(*refs):
        if carry is None:
            q_ref, k_ref, v_ref, bg_ref, st_ref, ti_ref, do_ref, dq_ref, dk_ref, dv_ref, dbg_ref, ds_scr = refs
        else:
            (q_ref, k_ref, v_ref, bg_ref, st_ref, ti_ref, do_ref, src_ref,
             dq_ref, dk_ref, dv_ref, dbg_ref, got_ref, ds_scr) = refs[:14]
            xy = (src_ref, got_ref) + tuple(refs[14:]) + (carry[1],)
        step = pl.program_id(0)

        @pl.when(step == 0)
        def _():
            ds_scr[...] = jnp.zeros(ds_scr.shape, F32)
            if carry is not None:
                _xy_start(*xy)

        bgt = bg_ref[...]
        lane = _iota((1, LANES), 1)
        dbg = jnp.zeros((C, LANES), F32)
        sl = [slice(h * dk, (h + 1) * dk) for h in range(H)]
        Ts = [ti_ref[h] for h in range(H)]
        _, vjp = jax.vjp(lambda *a: _gdn_chunk(*a, Ts=Ts),
                         [q_ref[:, s] for s in sl], [k_ref[:, s] for s in sl], [v_ref[:, s] for s in sl],
                         [bgt[:, h:h + 1] for h in range(H)], [bgt[:, H + h:H + h + 1] for h in range(H)],
                         [st_ref[h] for h in range(H)])
        dq, dkk, dvv, dbeta, dg, dS = vjp(([do_ref[:, s] for s in sl], [ds_scr[h] for h in range(H)]))
        for h in range(H):
            dq_ref[:, sl[h]] = dq[h]
            dk_ref[:, sl[h]] = dkk[h]
            dv_ref[:, sl[h]] = dvv[h]
            dbg = dbg + jnp.where(lane == h, dbeta[h], 0.0) + jnp.where(lane == h + H, dg[h], 0.0)
            ds_scr[h] = dS[h]
        dbg_ref[...] = dbg

        if carry is not None:
            @pl.when(step == N - 1)
            def _():
                _xy_wait(*xy)

    rows = pl.BlockSpec((C, H * dk), lambda s: (N - 1 - s, 0))
    bgs = pl.BlockSpec((C, LANES), lambda s: (N - 1 - s, 0))
    extra = carry is not None
    return pl.pallas_call(
        body, name=name, grid=(N,),
        in_specs=[rows, rows, rows, bgs, pl.BlockSpec((H, dk, dv), lambda s: (N - 1 - s, 0, 0)),
                  pl.BlockSpec((H, C, C), lambda s: (N - 1 - s, 0, 0)), rows] + ([_ANY] if extra else []),
        out_specs=[rows, rows, rows, bgs] + ([_ANY] if extra else []),
        out_shape=[jax.ShapeDtypeStruct((S_len, H * dk), F32)] * 3 + [jax.ShapeDtypeStruct((S_len, LANES), F32)]
        + ([_xy_out_shape(carry[0])] if extra else []),
        scratch_shapes=[pltpu.VMEM((H, dk, dv), F32)] + (_xy_sems() if extra else []),
        compiler_params=_params("arbitrary"),
    )(q, k, v, bg, states, tinv, do, *((carry[0],) if extra else ()))


def _gdn_post_tile(o, z, g):
    return o * lax.rsqrt(jnp.mean(o * o, -1, keepdims=True) + RMS_EPS) * g * _silu(z)


def _gdn_post_fwd(o, h_main, norm_g, H, *, tr, name):
    W = H * GDN_DK

    def body(rows, consts, orows, oaccs, scr, step, blk):
        for h in range(H):
            ls = slice(h * LANES, (h + 1) * LANES)
            orows[0][:, ls] = _gdn_post_tile(rows[0][:, ls], rows[1][:, ls], consts[0][...]).astype(BF16)

    z = (h_main, lambda tr_, n: pl.BlockSpec((tr_, W), lambda i: (i, 3)))
    return _rowcall(body, [o, z], [_row(norm_g)], [(W, BF16)], [], tr=tr, name=name)[0]


def _gdn_post_bwd(o, h_main, norm_g, du, w_out, H, *, tr, name):
    W = H * GDN_DK

    def body(rows, consts, orows, oaccs, scr, step, blk):
        d_on = lax.dot_general(rows[2][...].astype(BF16), consts[1][...], _DIMS["nt"], preferred_element_type=F32)
        for h in range(H):
            ls = slice(h * LANES, (h + 1) * LANES)
            _, vjp = jax.vjp(_gdn_post_tile, rows[0][:, ls], rows[1][:, ls], consts[0][...])
            d_o, d_z, d_g = vjp(d_on[:, ls])
            orows[0][:, ls] = d_o
            orows[1][:, ls] = d_z.astype(BF16)
            oaccs[0][...] += d_g

    z = (h_main, lambda tr_, n: pl.BlockSpec((tr_, W), lambda i: (i, 3)))
    return _rowcall(body, [o, z, du], [_row(norm_g), w_out], [(W, F32), (W, BF16)], [(1, LANES)], tr=tr, name=name)


def _seg_ones():
    ri, ci = _iota((LANES, LANES), 0), _iota((LANES, LANES), 1)
    return ((ri < FOX_DH) == (ci < FOX_DH)).astype(F32)


def _fox_qk_tile(x, g2):
    ms = _mm(x * x, _seg_ones(), "nn", "cb") * (1.0 / FOX_DH)
    return x * lax.rsqrt(ms + RMS_EPS) * g2


def _make_lf_fn(Hf):
    def fn(hs, bf):
        lane = _iota((1, LANES), 1)
        return jnp.where(lane < Hf, -_softplus(-(hs + bf)), 0.0)
    return fn


def _fox_pre_fwd(h_main, h_small, gq2, gk2, bf_row, Hf, *, tr, name):
    W = Hf * FOX_DH
    lf_fn = _make_lf_fn(Hf)

    def body(rows, consts, orows, oaccs, scr, step, blk):
        qk_ref, v_ref, hs_ref = rows
        gq_ref, gk_ref, bf_ref = consts
        qn_ref, kn_ref, vb_ref, c_ref, cb_ref = orows
        carry = scr[0]
        trr = qk_ref.shape[0]

        @pl.when(step == 0)
        def _():
            carry[...] = jnp.zeros(carry.shape, F32)

        for s in range(W // LANES):
            ls = slice(s * LANES, (s + 1) * LANES)
            qn_ref[:, ls] = _fox_qk_tile(qk_ref[:, ls], gq_ref[...]).astype(BF16)
            kn_ref[:, ls] = _fox_qk_tile(qk_ref[:, W + s * LANES:W + (s + 1) * LANES], gk_ref[...]).astype(BF16)
        vb_ref[...] = v_ref[...].astype(BF16)
        lf = lf_fn(hs_ref[...], bf_ref[...])
        tril = (_iota((trr, trr), 0) >= _iota((trr, trr), 1)).astype(F32)
        c = _raw_mm(tril, lf, "nn", "ca") + carry[0:1, :]
        c_ref[...] = c
        carry[0:1, :] = c[trr - 1:trr, :]
        col = _iota((LANES, 2 * W), 1)
        parity = (col >= W).astype(jnp.int32)
        slab = jnp.right_shift(col - parity * W, 7)
        expand = (2 * slab + parity == _iota((LANES, 2 * W), 0)).astype(F32)
        cb_ref[...] = _raw_mm(c, expand, "nn", "cb")

    qk = (h_main, lambda tr_, n: pl.BlockSpec((tr_, 2 * W), lambda i: (i, 0)))
    vv = (h_main, lambda tr_, n: pl.BlockSpec((tr_, W), lambda i: (i, 2)))
    return _rowcall(body, [qk, vv, h_small], [gq2, gk2, bf_row],
                    [(W, BF16), (W, BF16), (W, BF16), (LANES, F32), (2 * W, F32)], [], tr=tr, name=name,
                    scratch=[pltpu.VMEM((SUBLANES, LANES), F32)])


def _fox_pre_bwd(h_main, h_small, gq2, gk2, bf_row, dqn, dkn, dvv, dz, dc, Hf, *, tr, name):
    W = Hf * FOX_DH
    lf_fn = _make_lf_fn(Hf)

    def body(rows, consts, orows, oaccs, scr, step, blk):
        qk_ref, hs_ref, dqn_ref, dkn_ref, dvv_ref, dz_ref, dc_ref = rows
        gq_ref, gk_ref, bf_ref = consts
        dmain_ref, dhs_ref = orows
        dgq_ref, dgk_ref, dbf_ref = oaccs
        carry = scr[0]
        trr = qk_ref.shape[0]

        @pl.when(step == 0)
        def _():
            carry[...] = jnp.zeros(carry.shape, F32)

        for s in range(W // LANES):
            ls = slice(s * LANES, (s + 1) * LANES)
            lk = slice(W + s * LANES, W + (s + 1) * LANES)
            _, vjp = jax.vjp(_fox_qk_tile, qk_ref[:, ls], gq_ref[...])
            dx, dg = vjp(dqn_ref[:, ls])
            dmain_ref[:, ls] = dx.astype(BF16)
            dgq_ref[...] += dg
            _, vjp = jax.vjp(_fox_qk_tile, qk_ref[:, lk], gk_ref[...])
            dx, dg = vjp(dkn_ref[:, ls])
            dmain_ref[:, lk] = dx.astype(BF16)
            dgk_ref[...] += dg
        dmain_ref[:, 2 * W:3 * W] = dvv_ref[...].astype(BF16)
        dmain_ref[:, 3 * W:] = dz_ref[...]
        dcv = dc_ref[...]
        triu = (_iota((trr, trr), 0) <= _iota((trr, trr), 1)).astype(F32)
        dlf = _raw_mm(triu, dcv, "nn", "ca") + carry[0:1, :]
        carry[0:1, :] = dlf[0:1, :]
        _, vjp = jax.vjp(lf_fn, hs_ref[...], bf_ref[...])
        dhs, dbf = vjp(dlf)
        dhs_ref[...] = dhs.astype(BF16)
        dbf_ref[...] += dbf

    qk = (h_main, lambda tr_, n: pl.BlockSpec((tr_, 2 * W), lambda i: (n - 1 - i, 0)))
    return _rowcall(body, [qk, h_small, dqn, dkn, dvv, dz, dc], [gq2, gk2, bf_row],
                    [(4 * W, BF16), (LANES, BF16)], [(1, LANES), (1, LANES), (1, LANES)],
                    tr=tr, name=name, reverse=True, scratch=[pltpu.VMEM((SUBLANES, LANES), F32)])


def _fox_attn_fwd(qn, kn, vb, c_b, c_rowp, *, tb, name, carry=None):
    S_len, W = qn.shape
    HP = W // LANES
    tb = min(tb, S_len)
    nb = S_len // tb
    scale = FOX_DH ** -0.5
    rb, cb = min(ATTN_ROWS, tb), min(ATTN_COLS, tb)
    nblk = 2 * (tb // rb)

    steps = [(i, j) for i in range(nb) for j in range(i + 1)]
    ti = jnp.asarray([s[0] for s in steps], jnp.int32)
    tj = jnp.asarray([s[1] for s in steps], jnp.int32)

    def body(*refs):
        if carry is None:
            (ti_ref, tj_ref, q_ref, k_ref, v_ref, cb0_ref, cb1_ref, cr_ref, o_ref, lse_ref,
             m_scr, l_scr, acc_scr, s_scr, p_scr, a_scr) = refs
        else:
            (ti_ref, tj_ref, q_ref, k_ref, v_ref, cb0_ref, cb1_ref, cr_ref, src_ref, o_ref, lse_ref, got_ref,
             m_scr, l_scr, acc_scr, s_scr, p_scr, a_scr) = refs[:18]
            xy = (src_ref, got_ref) + tuple(refs[18:]) + (carry[1],)
        t = pl.program_id(1)
        i, j = ti_ref[t], tj_ref[t]

        if carry is not None:
            @pl.when(jnp.logical_and(pl.program_id(0) == 0, t == 0))
            def _():
                _xy_start(*xy)

        @pl.when(j == 0)
        def _():
            m_scr[...] = jnp.full(m_scr.shape, -jnp.inf, F32)
            l_scr[...] = jnp.zeros(l_scr.shape, F32)
            acc_scr[...] = jnp.zeros(acc_scr.shape, F32)

        def compute(diag):
            lo = _iota((1, LANES), 1) < FOX_DH
            v = v_ref[...]
            lane = _iota((1, LANES), 1)
            blocks = [(hh, r) for hh in range(2) for r in range(tb // rb)]
            masks = [lo, jnp.logical_not(lo)]

            def visible(r):
                return ((r + 1) * rb - 1) // LANES + 1 if diag else tb // LANES

            for b, (hh, r) in enumerate(blocks):
                rows = slice(r * rb, (r + 1) * rb)
                qr = q_ref[rows, :]
                qh = jnp.where(masks[hh], qr * scale, jnp.zeros_like(qr))
                ctb = (cb0_ref if hh == 0 else cb1_ref)[rows, :]
                mx = None
                for c in range(tb // cb):
                    if c * cb // LANES >= visible(r):
                        continue
                    s2 = lax.dot_general(qh, k_ref[c * cb:(c + 1) * cb, :], _DIMS["nt"], preferred_element_type=F32)
                    for piece in range(c * cb // LANES, min((c + 1) * cb // LANES, visible(r))):
                        cols = slice(piece * LANES, (piece + 1) * LANES)
                        s = s2[:, piece * LANES - c * cb:(piece + 1) * LANES - c * cb] + ctb - cr_ref[hh:hh + 1, cols]
                        if diag and (piece + 1) * LANES - 1 > r * rb:
                            keep = piece * LANES + _iota((rb, LANES), 1) <= r * rb + _iota((rb, LANES), 0)
                            s = jnp.where(keep, s, -jnp.inf)
                        s_scr[b, :, cols] = s
                        mx = s if mx is None else jnp.maximum(mx, s)
                m_prev = m_scr[hh, rows, :]
                m_new = jnp.maximum(m_prev, jnp.broadcast_to(jnp.max(mx, -1, keepdims=True), (rb, LANES)))
                a_scr[b] = jnp.exp(m_prev - m_new)
                m_scr[hh, rows, :] = m_new
            for b, (hh, r) in enumerate(blocks):
                m_new = m_scr[hh, r * rb:(r + 1) * rb, :]
                for piece in range(visible(r)):
                    cols = slice(piece * LANES, (piece + 1) * LANES)
                    p_scr[b, :, cols] = jnp.exp(s_scr[b, :, cols] - m_new).astype(BF16)
            for b, (hh, r) in enumerate(blocks):
                rows = slice(r * rb, (r + 1) * rb)
                nkv = visible(r) * LANES
                pb = p_scr[b, :, :nkv]
                spare = (1 - hh) * FOX_DH
                vh = jnp.where(masks[hh], v[:nkv], (lane == spare).astype(BF16))
                pv = lax.dot_general(pb, vh, _DIMS["nn"], preferred_element_type=F32)
                psum = jnp.broadcast_to(pv[:, spare:spare + 1], (rb, LANES))
                alpha = a_scr[b]
                l_scr[hh, rows, :] = alpha * l_scr[hh, rows, :] + psum
                acc = acc_scr[rows, :]
                acc_scr[rows, :] = jnp.where(masks[hh], acc * alpha + pv, acc)

        @pl.when(j < i)
        def _():
            compute(False)

        @pl.when(j == i)
        def _():
            compute(True)
            lo = _iota((1, LANES), 1) < FOX_DH
            o_ref[...] = acc_scr[...] / jnp.where(lo, l_scr[0], l_scr[1])
            lse_ref[...] = m_scr[...] + jnp.log(l_scr[...])

        if carry is not None:
            @pl.when(jnp.logical_and(pl.program_id(0) == HP - 1, t == len(steps) - 1))
            def _():
                _xy_wait(*xy)

    qs = pl.BlockSpec((tb, LANES), lambda h, t, ti_, tj_: (ti_[t], h))
    qs1 = pl.BlockSpec((tb, LANES), lambda h, t, ti_, tj_: (ti_[t], HP + h))
    ks = pl.BlockSpec((tb, LANES), lambda h, t, ti_, tj_: (tj_[t], h))
    crs = pl.BlockSpec((None, SUBLANES, tb), lambda h, t, ti_, tj_: (h, 0, tj_[t]))
    extra = carry is not None
    return pl.pallas_call(
        body, name=name,
        grid_spec=pltpu.PrefetchScalarGridSpec(
            num_scalar_prefetch=2, grid=(HP, len(steps)),
            in_specs=[qs, ks, ks, qs, qs1, crs] + ([_ANY] if extra else []),
            out_specs=[qs, pl.BlockSpec((2, tb, LANES), lambda h, t, ti_, tj_: (0, ti_[t], h))]
            + ([_ANY] if extra else []),
            scratch_shapes=[pltpu.VMEM((2, tb, LANES), F32), pltpu.VMEM((2, tb, LANES), F32),
                            pltpu.VMEM((tb, LANES), F32), pltpu.VMEM((nblk, rb, tb), F32),
                            pltpu.VMEM((nblk, rb, tb), BF16), pltpu.VMEM((nblk, rb, LANES), F32)]
            + (_xy_sems() if extra else [])),
        out_shape=[jax.ShapeDtypeStruct((S_len, W), F32), jax.ShapeDtypeStruct((2, S_len, W), F32)]
        + ([_xy_out_shape(carry[0])] if extra else []),
        compiler_params=_params("arbitrary" if extra else "parallel", "arbitrary"),
    )(ti, tj, qn, kn, vb, c_b, c_b, c_rowp, *((carry[0],) if extra else ()))


def _fox_attn_bwd(qn, kn, vb, c_b, c_rowp, lse_b, delta_b, do, *, tb, name):
    S_len, W = qn.shape
    HP = W // LANES
    tb = min(tb, S_len)
    nb = S_len // tb
    scale = FOX_DH ** -0.5
    rb, cb = min(ATTN_ROWS, tb), min(ATTN_COLS, tb)

    steps = [(j, i) for j in range(nb) for i in range(j, nb)]
    tj = jnp.asarray([s[0] for s in steps], jnp.int32)
    ti = jnp.asarray([s[1] for s in steps], jnp.int32)

    def body(tj_ref, ti_ref, q_ref, k_ref, v_ref, cb0_ref, cb1_ref, cr_ref, lse_ref, dl0_ref, dl1_ref, do_ref,
             dq_ref, dk_ref, dv_ref, dcr_ref, dct_ref, dk_scr, dv_scr, dc_scr, p_scr, ds_scr):
        t = pl.program_id(1)
        j, i = tj_ref[t], ti_ref[t]

        @pl.when(t == 0)
        def _():
            dq_ref[...] = jnp.zeros(dq_ref.shape, F32)
            dct_ref[...] = jnp.zeros(dct_ref.shape, F32)

        @pl.when(i == j)
        def _():
            dk_scr[...] = jnp.zeros(dk_scr.shape, F32)
            dv_scr[...] = jnp.zeros(dv_scr.shape, F32)
            dc_scr[...] = jnp.zeros(dc_scr.shape, F32)

        def compute(diag):
            lo = _iota((1, LANES), 1) < FOX_DH
            masks = [lo, jnp.logical_not(lo)]
            row0 = pl.multiple_of(i * tb, tb)
            npiece = tb // LANES
            colsum = [[None] * npiece for _ in range(2)]

            def visible(r):
                return ((r + 1) * rb - 1) // LANES + 1 if diag else npiece

            for hh in range(2):
                for r in range(tb // rb):
                    rows = slice(r * rb, (r + 1) * rb)
                    qr = q_ref[rows, :]
                    qh = jnp.where(masks[hh], qr * scale, jnp.zeros_like(qr))
                    doh = jnp.where(masks[hh], do_ref[rows, :], 0.0).astype(BF16)
                    bq = (cb0_ref if hh == 0 else cb1_ref)[rows, :] - lse_ref[hh, rows, :]
                    dlt = (dl0_ref if hh == 0 else dl1_ref)[rows, :]
                    rsum = None
                    for c in range(tb // cb):
                        first, last = c * cb // LANES, min((c + 1) * cb // LANES, visible(r))
                        for piece in range(max(first, last), (c + 1) * cb // LANES):
                            cols = slice(piece * LANES, (piece + 1) * LANES)
                            p_scr[hh, rows, cols] = jnp.zeros((rb, LANES), BF16)
                            ds_scr[hh, rows, cols] = jnp.zeros((rb, LANES), BF16)
                        if first >= last:
                            continue
                        s2 = lax.dot_general(qh, k_ref[c * cb:(c + 1) * cb, :], _DIMS["nt"], preferred_element_type=F32)
                        dp2 = lax.dot_general(doh, v_ref[c * cb:(c + 1) * cb, :], _DIMS["nt"], preferred_element_type=F32)
                        for piece in range(first, last):
                            cols = slice(piece * LANES, (piece + 1) * LANES)
                            sub = slice(piece * LANES - c * cb, (piece + 1) * LANES - c * cb)
                            s = s2[:, sub] + bq - cr_ref[hh:hh + 1, cols]
                            if diag and (piece + 1) * LANES - 1 > r * rb:
                                keep = piece * LANES + _iota((rb, LANES), 1) <= r * rb + _iota((rb, LANES), 0)
                                s = jnp.where(keep, s, -jnp.inf)
                            p = jnp.exp(s)
                            ds = p * (dp2[:, sub] - dlt)
                            p_scr[hh, rows, cols] = p.astype(BF16)
                            ds_scr[hh, rows, cols] = ds.astype(BF16)
                            rsum = ds if rsum is None else rsum + ds
                            csum = jnp.sum(ds, axis=0, keepdims=True)
                            colsum[hh][piece] = csum if colsum[hh][piece] is None else colsum[hh][piece] + csum
                    grow = pl.ds(row0 + r * rb, rb)
                    dct_ref[grow, :] += jnp.where(_iota((1, SUBLANES), 1) == hh, jnp.sum(rsum, -1, keepdims=True), 0.0)
            k = k_ref[...]
            qf = q_ref[...]
            dof = do_ref[...]
            dq_part = jnp.zeros((tb, LANES), F32)
            for hh in range(2):
                kh = jnp.where(masks[hh], k, jnp.zeros_like(k))
                qhf = jnp.where(masks[hh], qf * scale, jnp.zeros_like(qf))
                dohf = jnp.where(masks[hh], dof, 0.0).astype(BF16)
                dv_scr[...] += lax.dot_general(p_scr[hh], dohf, _DIMS["tn"], preferred_element_type=F32)
                dk_scr[...] += lax.dot_general(ds_scr[hh], qhf, _DIMS["tn"], preferred_element_type=F32)
                dq_part = dq_part + lax.dot_general(ds_scr[hh], kh, _DIMS["nn"], preferred_element_type=F32)
                for piece in range(npiece):
                    if colsum[hh][piece] is not None:
                        dc_scr[hh:hh + 1, piece * LANES:(piece + 1) * LANES] -= colsum[hh][piece]
            dq_ref[pl.ds(row0, tb), :] += dq_part * scale

        @pl.when(i > j)
        def _():
            compute(False)

        @pl.when(i == j)
        def _():
            compute(True)

        @pl.when(i == nb - 1)
        def _():
            dk_ref[...] = dk_scr[...]
            dv_ref[...] = dv_scr[...]
            dcr_ref[...] = dc_scr[...]

    qs = pl.BlockSpec((tb, LANES), lambda h, t, tj_, ti_: (ti_[t], h))
    qs1 = pl.BlockSpec((tb, LANES), lambda h, t, tj_, ti_: (ti_[t], HP + h))
    ks = pl.BlockSpec((tb, LANES), lambda h, t, tj_, ti_: (tj_[t], h))
    crs = pl.BlockSpec((None, SUBLANES, tb), lambda h, t, tj_, ti_: (h, 0, tj_[t]))
    whole = pl.BlockSpec((S_len, LANES), lambda h, t, tj_, ti_: (0, h))
    return pl.pallas_call(
        body, name=name,
        grid_spec=pltpu.PrefetchScalarGridSpec(
            num_scalar_prefetch=2, grid=(HP, len(steps)),
            in_specs=[qs, ks, ks, qs, qs1, crs, pl.BlockSpec((2, tb, LANES), lambda h, t, tj_, ti_: (0, ti_[t], h)),
                      qs, qs1, qs],
            out_specs=[whole, ks, ks, crs, pl.BlockSpec((None, S_len, SUBLANES), lambda h, t, tj_, ti_: (h, 0, 0))],
            scratch_shapes=[pltpu.VMEM((tb, LANES), F32), pltpu.VMEM((tb, LANES), F32),
                            pltpu.VMEM((SUBLANES, tb), F32), pltpu.VMEM((2, tb, tb), BF16),
                            pltpu.VMEM((2, tb, tb), BF16)]),
        out_shape=[jax.ShapeDtypeStruct((S_len, W), F32)] * 3 + [jax.ShapeDtypeStruct((HP, SUBLANES, S_len), F32),
                                                                 jax.ShapeDtypeStruct((HP, S_len, SUBLANES), F32)],
        compiler_params=_params("parallel", "arbitrary"),
    )(tj, ti, qn, kn, vb, c_b, c_b, c_rowp, lse_b, delta_b, delta_b, do)


def _fox_post_tile(o, z):
    return o * _silu(z)


def _fox_post_fwd(o, h_main, *, tr, name):
    W = o.shape[1]

    def body(rows, consts, orows, oaccs, scr, step, blk):
        orows[0][...] = _fox_post_tile(rows[0][...], rows[1][...]).astype(BF16)

    z = (h_main, lambda tr_, n: pl.BlockSpec((tr_, W), lambda i: (i, 3)))
    return _rowcall(body, [o, z], [], [(W, BF16)], [], tr=tr, name=name)[0]


def _fox_post_bwd(o, h_main, du, w_out, *, tr, name):
    W = o.shape[1]

    def body(rows, consts, orows, oaccs, scr, step, blk):
        d_og = lax.dot_general(rows[2][...].astype(BF16), consts[0][...], _DIMS["nt"], preferred_element_type=F32)
        _, vjp = jax.vjp(_fox_post_tile, rows[0][...], rows[1][...])
        d_o, d_z = vjp(d_og)
        orows[0][...] = d_o
        orows[1][...] = d_z.astype(BF16)
        lo_rows = (_iota((LANES, LANES), 0) < FOX_DH)
        for s in range(W // LANES):
            ls = slice(s * LANES, (s + 1) * LANES)
            prod = d_o[:, ls] * rows[0][:, ls]
            orows[2][:, ls] = _raw_mm(prod, lo_rows.astype(F32), "nn", "cb")
            orows[2][:, W + s * LANES:W + (s + 1) * LANES] = _raw_mm(prod, jnp.logical_not(lo_rows).astype(F32), "nn", "cb")

    z = (h_main, lambda tr_, n: pl.BlockSpec((tr_, W), lambda i: (i, 3)))
    return _rowcall(body, [o, z, du], [w_out], [(W, F32), (W, BF16), (2 * W, F32)], [], tr=tr, name=name)


MESH_IDS = pl.DeviceIdType.MESH
N_CHIPS = 4
N_DEV = 8
_ANY = pl.BlockSpec(memory_space=pl.ANY)


def _xy_exchange(src, *, gather, name):
    def body(src_ref, out_ref, send_sems, recv_sems, local_sem):
        _xy_start(src_ref, out_ref, send_sems, recv_sems, local_sem, gather)
        _xy_wait(src_ref, out_ref, send_sems, recv_sems, local_sem, gather)

    return pl.pallas_call(
        body, name=name, in_specs=[_ANY], out_specs=_ANY,
        out_shape=_xy_out_shape(src), scratch_shapes=_xy_sems(),
    )(src)


def _xy_out_shape(src):
    return jax.ShapeDtypeStruct((N_CHIPS,) + tuple(src.shape[-2:]), src.dtype)


def _xy_sems():
    return [pltpu.SemaphoreType.DMA((N_CHIPS - 1,)), pltpu.SemaphoreType.DMA((N_CHIPS - 1,)), pltpu.SemaphoreType.DMA]


def _xy_copies(src_ref, out_ref, send_sems, recv_sems, local_sem, gather, with_arrivals=True):
    x, y, c = lax.axis_index("x"), lax.axis_index("y"), lax.axis_index("c")
    me = 2 * x + y
    peers = [(1 - x, y), (x, 1 - y), (1 - x, 1 - y)]

    def outgoing(px, py):
        return src_ref if gather else src_ref.at[2 * px + py]

    def copy(j, px, py, slot):
        return pltpu.make_async_remote_copy(
            src_ref=outgoing(px, py), dst_ref=out_ref.at[slot], send_sem=send_sems.at[j], recv_sem=recv_sems.at[j],
            device_id=(px, py, c), device_id_type=MESH_IDS)

    mine = pltpu.make_async_copy(outgoing(x, y), out_ref.at[me], local_sem)
    sends = [copy(j, px, py, me) for j, (px, py) in enumerate(peers)]
    arrivals = [copy(j, px, py, 2 * px + py) for j, (px, py) in enumerate(peers)] if with_arrivals else []
    return mine, sends, arrivals


def _xy_start(src_ref, out_ref, send_sems, recv_sems, local_sem, gather):
    mine, sends, _ = _xy_copies(src_ref, out_ref, send_sems, recv_sems, local_sem, gather, with_arrivals=False)
    mine.start()
    for cp in sends:
        cp.start()


def _xy_wait(src_ref, out_ref, send_sems, recv_sems, local_sem, gather):
    mine, sends, arrivals = _xy_copies(src_ref, out_ref, send_sems, recv_sems, local_sem, gather)
    for cp in arrivals:
        cp.wait_recv()
    for cp in sends:
        cp.wait_send()
    mine.wait()


def _c_swap(src, *, name):
    def body(src_ref, out_ref, send_sem, recv_sem):
        x, y, c = lax.axis_index("x"), lax.axis_index("y"), lax.axis_index("c")
        cp = pltpu.make_async_remote_copy(
            src_ref=src_ref, dst_ref=out_ref, send_sem=send_sem, recv_sem=recv_sem,
            device_id=(x, y, 1 - c), device_id_type=MESH_IDS)
        cp.start()
        cp.wait()

    return pl.pallas_call(
        body, name=name, in_specs=[_ANY], out_specs=_ANY,
        out_shape=jax.ShapeDtypeStruct(src.shape, src.dtype),
        scratch_shapes=[pltpu.SemaphoreType.DMA, pltpu.SemaphoreType.DMA],
    )(src)


def _all_gather8(blk, *, name):
    m_per, n = blk.shape

    def body(x_ref, out_ref, send_sems, recv_sems, local_sem):
        x, y, c = lax.axis_index("x"), lax.axis_index("y"), lax.axis_index("c")
        me, sibling = (x, y, c), (x, y, 1 - c)
        chips = [(1 - x, y), (x, 1 - y), (1 - x, 1 - y)]

        def rows(px, py, pc):
            return out_ref.at[pl.ds((4 * px + 2 * py + pc) * m_per, m_per), :]

        def copy(k, block, to, src=None):
            return pltpu.make_async_remote_copy(
                src_ref=rows(*block) if src is None else src, dst_ref=rows(*block),
                send_sem=send_sems.at[k], recv_sem=recv_sems.at[k], device_id=to, device_id_type=MESH_IDS)

        mine = pltpu.make_async_copy(x_ref, rows(*me), local_sem)
        mine.start()
        first = [copy(0, me, sibling, src=x_ref)]
        first += [copy(1 + j, me, (*chip, c), src=x_ref) for j, chip in enumerate(chips)]
        for cp in first:
            cp.start()
        passed = [copy(4 + j, (*chip, c), sibling) for j, chip in enumerate(chips)]
        for j, chip in enumerate(chips):
            copy(1 + j, (*chip, c), me).wait_recv()
            passed[j].start()
        copy(0, sibling, me).wait_recv()
        for j, chip in enumerate(chips):
            copy(4 + j, (*chip, 1 - c), me).wait_recv()
        for cp in first + passed:
            cp.wait_send()
        mine.wait()

    return pl.pallas_call(
        body, name=name,
        out_shape=jax.ShapeDtypeStruct((N_DEV * m_per, n), blk.dtype),
        in_specs=[pl.BlockSpec(memory_space=pltpu.VMEM)], out_specs=pl.BlockSpec(memory_space=pltpu.VMEM),
        scratch_shapes=[pltpu.SemaphoreType.DMA((7,)), pltpu.SemaphoreType.DMA((7,)), pltpu.SemaphoreType.DMA],
    )(blk)


def _sum_slots(parts, *, tr, name):
    n, R, _ = parts.shape
    pack = 2 * SUBLANES
    tr = max(t for t in range(pack, min(tr, R) + 1, pack) if R % t == 0) if R % pack == 0 else R

    def body(p_ref, o_ref):
        tot = p_ref[0].astype(F32)
        for s in range(1, n):
            tot = tot + p_ref[s].astype(F32)
        o_ref[...] = tot

    return pl.pallas_call(
        body, name=name, grid=(R // tr,),
        in_specs=[pl.BlockSpec((n, tr, LANES), lambda i: (0, i, 0))], out_specs=pl.BlockSpec((tr, LANES), lambda i: (i, 0)),
        out_shape=jax.ShapeDtypeStruct((R, LANES), F32), compiler_params=_params("parallel"),
    )(parts)


def _adamw(w, g_parts, m, v, *, name):
    shape = w.shape
    as2d = lambda a: a.reshape(-1, shape[-1])
    w2, m2, v2 = as2d(w), as2d(m), as2d(v)
    gs = [as2d(g) for g in g_parts]
    R, C = w2.shape
    tr = R
    while tr * C * 4 > (1 << 20) and tr % 2 == 0 and (tr // 2) % SUBLANES == 0:
        tr //= 2
    ng = len(gs)

    def body(*refs):
        w_ref, m_ref, v_ref = refs[:3]
        g_refs = refs[3:3 + ng]
        go_ref, d_ref, mo_ref, vo_ref = refs[3 + ng:]
        g = g_refs[0][...]
        for r in g_refs[1:]:
            g = g + r[...]
        mn = ADAM_B1 * m_ref[...] + (1.0 - ADAM_B1) * g
        vn = ADAM_B2 * v_ref[...] + (1.0 - ADAM_B2) * jnp.square(g)
        m_hat = mn / (1.0 - ADAM_B1 ** ADAM_STEP)
        v_hat = vn / (1.0 - ADAM_B2 ** ADAM_STEP)
        go_ref[...] = g
        d_ref[...] = -ADAM_LR * (m_hat / (jnp.sqrt(v_hat) + ADAM_EPS) + ADAM_WD * w_ref[...])
        mo_ref[...] = mn
        vo_ref[...] = vn

    spec = pl.BlockSpec((tr, C), lambda i: (i, 0))
    outs = pl.pallas_call(
        body, name=name, grid=(R // tr,), in_specs=[spec] * (3 + ng), out_specs=[spec] * 4,
        out_shape=[jax.ShapeDtypeStruct((R, C), F32)] * 4, compiler_params=_params("parallel"),
    )(w2, m2, v2, *gs)
    return tuple(o.reshape(shape) for o in outs)


TR = 256
ATTN_TILE = 1024
ATTN_ROWS = 256
ATTN_COLS = 256


def _mm_nn(a, b, name, **kw):
    return _matmul(a, b, tm=2048, tn=1024, tk=1024, name=name, **kw)


def _mm_tn(a, b, name, **kw):
    return _matmul(a, b, ta=True, tm=1024, tn=2048, tk=512, name=name, **kw)


def _c_rows(c, Hf):
    S_len = c.shape[0]
    ct = c[:, :Hf].T.reshape(Hf // 2, 2, S_len)
    return jnp.pad(ct, ((0, 0), (0, SUBLANES - 2), (0, 0)))


def _local_step(x, p, target, wts, late_weights=None, early_grads=None):
    L = len(wts["ln_g"])
    alpha = (2 * L) ** 0.25
    Hg = wts["gdn_a_log"][0].shape[-1]
    Hf = wts["fox_b_f"][0].shape[-1]
    Wg_ = Hg * GDN_DK
    Wf_ = Hf * FOX_DH
    saved = []
    for i in range(L):
        j = i // 2
        sv = {"x": x}
        if i % 2 == 0:
            w_in = wts["gdn_w_in"][j]
            wm, ws = w_in[:, :4 * Wg_], _pad_lanes(w_in[:, 4 * Wg_:])
            cw8 = jnp.pad(wts["gdn_conv_w"][j], ((0, SUBLANES - GDN_CONV), (0, 0)))
            alog = _pad_lanes(_row(wts["gdn_a_log"][j]), offset=Hg)
            dtb = _pad_lanes(_row(wts["gdn_dt_bias"][j]), offset=Hg)
            hm, hs = _in_proj_fwd(x, wm, ws, tr=2 * TR, name=f"gdn{j}_in")
            q, k, v, bg = _gdn_pre_fwd(hm, hs, cw8, alog, dtb, Hg, tr=TR, name=f"gdn{j}_pre")
            if i == 0 and late_weights is not None:
                packed_w, unpack_w = late_weights[0]
                o, states, tinv, got = _gdn_rule_fwd(q, k, v, bg, Hg, name=f"gdn{j}_rule", carry=(packed_w, True))
                for (wname_, idx), arr in unpack_w(got).items():
                    wts[wname_][idx] = arr
            else:
                o, states, tinv = _gdn_rule_fwd(q, k, v, bg, Hg, name=f"gdn{j}_rule")
            on = _gdn_post_fwd(o, hm, wts["gdn_norm_g"][j], Hg, tr=TR, name=f"gdn{j}_post")
            w_out = wts["gdn_w_out"][j]
            sv.update(wm=wm, ws=ws, cw8=cw8, alog=alog, dtb=dtb, hm=hm, hs=hs, q=q, k=k, v=v, bg=bg, o=o,
                      states=states, tinv=tinv, on=on)
        else:
            w_in = wts["fox_w_in"][j]
            wm, ws = w_in[:, :4 * Wf_], _pad_lanes(w_in[:, 4 * Wf_:])
            gq2 = _row(jnp.tile(wts["fox_q_norm_g"][j], 2))
            gk2 = _row(jnp.tile(wts["fox_k_norm_g"][j], 2))
            bf = _pad_lanes(_row(wts["fox_b_f"][j]))
            hm, hs = _in_proj_fwd(x, wm, ws, tr=2 * TR, name=f"fox{j}_in")
            qn, kn, vb, c, c_b = _fox_pre_fwd(hm, hs, gq2, gk2, bf, Hf, tr=TR, name=f"fox{j}_pre")
            c_rowp = _c_rows(c, Hf)
            if i == 1 and late_weights is not None:
                packed_w, unpack_w = late_weights[1]
                o, lse_b, got = _fox_attn_fwd(qn, kn, vb, c_b, c_rowp, tb=ATTN_TILE, name=f"fox{j}_attn",
                                              carry=(packed_w, True))
                for (wname_, idx), arr in unpack_w(got).items():
                    wts[wname_][idx] = arr
            else:
                o, lse_b = _fox_attn_fwd(qn, kn, vb, c_b, c_rowp, tb=ATTN_TILE, name=f"fox{j}_attn")
            on = _fox_post_fwd(o, hm, tr=TR, name=f"fox{j}_post")
            w_out = wts["fox_w_out"][j]
            sv.update(wm=wm, ws=ws, gq2=gq2, gk2=gk2, bf=bf, hm=hm, hs=hs, qn=qn, kn=kn, vb=vb, c_b=c_b,
                      c_rowp=c_rowp, o=o, lse_b=lse_b, on=on)
        y, x_ln = _out_ln_fwd(on, w_out, x, wts["ln_g"][i], wts["ln_b"][i], alpha, tr=2 * TR, name=f"out_ln{i}")
        pp = _mm_nn(p[i], wts["ple_w_proj"][i], f"ple{i}_proj")
        gp, x_out = _gate_mix_fwd(x_ln, wts["ple_w_gate"][i], pp, tr=2 * TR, name=f"ple{i}_gate_mix")
        sv.update(y=y, x_ln=x_ln, gp=gp, pp=pp)
        saved.append(sv)
        x = x_out

    loss_row, dx = _loss_fwd_bwd(x, target, tr=TR, name="loss")

    g = {n: [None] * len(wts[n]) for n in wts}
    got_early = None
    for i in reversed(range(L)):
        j = i // 2
        sv = saved[i]
        d_pre, d_pp, du, g["ln_g"][i], g["ln_b"][i] = _ple_ln_bwd(
            dx, sv["gp"], sv["pp"], wts["ple_w_gate"][i], sv["x"], sv["y"], wts["ln_g"][i], wts["ln_b"][i], alpha,
            tr=2 * TR, name=f"ple_ln{i}_bwd")
        g["ple_w_gate"][i] = _mm_tn(sv["x_ln"], d_pre, f"ple{i}_gate_dw")
        g["ple_w_proj"][i] = _mm_tn(p[i], d_pp, f"ple{i}_proj_dw")
        if i % 2 == 0:
            g["gdn_w_out"][j] = _mm_tn(sv["on"], du, f"gdn{j}_out_dw")
            d_o, d_z, d_ng = _gdn_post_bwd(sv["o"], sv["hm"], wts["gdn_norm_g"][j], du, wts["gdn_w_out"][j], Hg,
                                           tr=TR, name=f"gdn{j}_post_bwd")
            rule_args = (sv["q"], sv["k"], sv["v"], sv["bg"], sv["states"], sv["tinv"], d_o, Hg)
            if i == 0 and early_grads is not None:
                dq, dk, dv, dbg, got_early = _gdn_rule_bwd(*rule_args, name=f"gdn{j}_rule_bwd",
                                                           carry=(early_grads(g), False))
            else:
                dq, dk, dv, dbg = _gdn_rule_bwd(*rule_args, name=f"gdn{j}_rule_bwd")
            d_hm, d_hs, d_cw, d_al, d_dtb = _gdn_pre_bwd(sv["hm"], sv["hs"], sv["cw8"], sv["alog"], sv["dtb"],
                                                         dq, dk, dv, dbg, d_z, Hg, tr=TR, name=f"gdn{j}_pre_bwd")
            g["gdn_norm_g"][j] = d_ng[0]
            g["gdn_conv_w"][j] = d_cw[:GDN_CONV]
            g["gdn_a_log"][j] = d_al[0, Hg:2 * Hg]
            g["gdn_dt_bias"][j] = d_dtb[0, Hg:2 * Hg]
            wname, nsmall = "gdn_w_in", 2 * Hg
        else:
            g["fox_w_out"][j] = _mm_tn(sv["on"], du, f"fox{j}_out_dw")
            d_o, d_z, delta_b = _fox_post_bwd(sv["o"], sv["hm"], du, wts["fox_w_out"][j], tr=TR, name=f"fox{j}_post_bwd")
            dqn, dkn, dvv, dcr, dct = _fox_attn_bwd(sv["qn"], sv["kn"], sv["vb"], sv["c_b"], sv["c_rowp"], sv["lse_b"],
                                                    delta_b, d_o, tb=ATTN_TILE, name=f"fox{j}_attn_bwd")
            dc = _pad_lanes(dcr[:, :2, :].reshape(Hf, -1).T + dct[:, :, :2].transpose(1, 0, 2).reshape(-1, Hf))
            d_hm, d_hs, d_gq, d_gk, d_bf = _fox_pre_bwd(sv["hm"], sv["hs"], sv["gq2"], sv["gk2"], sv["bf"],
                                                        dqn, dkn, dvv, d_z, dc, Hf, tr=TR, name=f"fox{j}_pre_bwd")
            g["fox_q_norm_g"][j] = d_gq[0, :FOX_DH] + d_gq[0, FOX_DH:]
            g["fox_k_norm_g"][j] = d_gk[0, :FOX_DH] + d_gk[0, FOX_DH:]
            g["fox_b_f"][j] = d_bf[0, :Hf]
            wname, nsmall = "fox_w_in", Hf
        dwm = _mm_tn(sv["x"], d_hm, f"{wname}{j}_main_dw")
        dws = _mm_tn(sv["x"], d_hs, f"{wname}{j}_small_dw")
        g[wname][j] = jnp.concatenate([dwm, dws[:, :nsmall]], axis=1)
        dx = _in_proj_bwd_x(d_hm, d_hs, du, sv["wm"], sv["ws"], alpha, tr=2 * TR, name=f"{wname}{j}_dx")
    return loss_row, dx, g, got_early


_SHARDED = (("ple_w_gate", 1), ("ple_w_proj", 2), ("gdn_w_in", 2), ("gdn_conv_w", 2), ("gdn_w_out", 1),
            ("fox_w_in", 2), ("fox_w_out", 1))
_SHARD_AXIS = dict(_SHARDED)
_REPLICATED = ("ln_g", "ln_b", "gdn_a_log", "gdn_dt_bias", "gdn_norm_g", "fox_b_f", "fox_q_norm_g", "fox_k_norm_g")
_EXACT = ("gdn_conv_w",)
_ORDER = ("ln_g", "ln_b", "ple_w_gate", "ple_w_proj", "gdn_w_in", "gdn_conv_w", "gdn_a_log", "gdn_dt_bias",
          "gdn_norm_g", "gdn_w_out", "fox_w_in", "fox_b_f", "fox_q_norm_g", "fox_k_norm_g", "fox_w_out")
_FIRST_WEIGHTS = (("gdn_w_in", 0), ("gdn_conv_w", 0))
_LAST_GRADS = (("gdn_w_in", 0), ("gdn_conv_w", 0))
PACK_ROWS = 2 * SUBLANES
SUM_ROWS = 2048


def _as_rows(a):
    rows = a.reshape(-1, LANES)
    return jnp.pad(rows, ((0, -rows.shape[0] % PACK_ROWS), (0, 0)))


def _n_elements(shape):
    n = 1
    for d in shape:
        n *= d
    return n


def _pack_weights(local, items):
    parts = []
    for name, idx in items:
        w = local[name][idx]
        parts.append(_as_rows(lax.bitcast_convert_type(w, BF16) if name in _EXACT else w.astype(BF16)))
    return jnp.concatenate(parts, axis=0), [q.shape[0] for q in parts]


def _unpack_weights(got, local, items, sizes):
    out, r0 = {}, 0
    for (name, idx), nrow in zip(items, sizes):
        shp = tuple(local[name].shape[1:])
        n_el = _n_elements(shp) * (2 if name in _EXACT else 1)
        seg = got[:, r0:r0 + nrow].reshape(N_CHIPS, -1)[:, :n_el]
        r0 += nrow
        if name in _EXACT:
            blocks = lax.bitcast_convert_type(seg.reshape((N_CHIPS,) + shp + (2,)), F32)
        else:
            blocks = seg.reshape((N_CHIPS,) + shp)
        axis = _SHARD_AXIS[name] - 1
        joined = shp[:axis] + (N_CHIPS * shp[axis],) + shp[axis + 1:]
        out[(name, idx)] = jnp.moveaxis(blocks, 0, axis).reshape(joined)
    return out


def _pack_grads(g, items):
    parts = []
    for name, idx in items:
        gfull = g[name][idx]
        axis = _SHARD_AXIS[name] - 1
        shp = gfull.shape
        split = shp[:axis] + (N_CHIPS, shp[axis] // N_CHIPS) + shp[axis + 1:]
        rows = jnp.moveaxis(gfull.reshape(split), axis, 0).astype(BF16).reshape(N_CHIPS, -1, LANES)
        parts.append(jnp.pad(rows, ((0, 0), (0, -rows.shape[1] % PACK_ROWS), (0, 0))))
    sizes = [q.shape[1] for q in parts]
    tail = -sum(sizes) % SUM_ROWS
    if tail:
        parts.append(jnp.zeros((N_CHIPS, tail, LANES), BF16))
    return jnp.concatenate(parts, axis=1), sizes


def _unpack_grads(flat, local, items, sizes):
    out, r0 = {}, 0
    for (name, idx), nrow in zip(items, sizes):
        shp = tuple(local[name].shape[1:])
        out[(name, idx)] = flat[r0:r0 + nrow].reshape(-1)[:_n_elements(shp)].reshape(shp)
        r0 += nrow
    return out


def _reduce_replicated(grads, loss_part):
    rows = [_pad_lanes(jnp.reshape(loss_part, (1, 1)))]
    for name in _REPLICATED:
        gr = grads[name]
        rows.append(gr.reshape(-1, LANES) if gr.shape[-1] % LANES == 0 else _pad_lanes(gr))
    sizes = [r.shape[0] for r in rows]
    blk = jnp.concatenate(rows, axis=0)
    nrow = blk.shape[0]
    npad = -nrow % SUBLANES
    blk = jnp.pad(blk, ((0, npad), (0, 0)))
    allb = _all_gather8(blk, name="gather_small_grads").reshape(N_DEV, nrow + npad, LANES)
    tot = _sum_slots(allb, tr=nrow + npad, name="sum_small_grads")
    out, r0 = {}, sizes[0]
    loss = tot[0, 0]
    for name, n in zip(_REPLICATED, sizes[1:]):
        gr = grads[name]
        seg = tot[r0:r0 + n]
        out[name] = seg.reshape(gr.shape) if gr.shape[-1] % LANES == 0 else seg[:, :gr.shape[-1]]
        r0 += n
    return loss, out


def kernel(x, p, ln_g, ln_b, ple_w_gate, ple_w_proj, gdn_w_in, gdn_conv_w, gdn_a_log, gdn_dt_bias, gdn_norm_g, gdn_w_out, fox_w_in, fox_b_f, fox_q_norm_g, fox_k_norm_g, fox_w_out, loss_target, m_ln_g, m_ln_b, m_ple_w_gate, m_ple_w_proj, m_gdn_w_in, m_gdn_conv_w, m_gdn_a_log, m_gdn_dt_bias, m_gdn_norm_g, m_gdn_w_out, m_fox_w_in, m_fox_b_f, m_fox_q_norm_g, m_fox_k_norm_g, m_fox_w_out, v_ln_g, v_ln_b, v_ple_w_gate, v_ple_w_proj, v_gdn_w_in, v_gdn_conv_w, v_gdn_a_log, v_gdn_dt_bias, v_gdn_norm_g, v_gdn_w_out, v_fox_w_in, v_fox_b_f, v_fox_q_norm_g, v_fox_k_norm_g, v_fox_w_out):
    local = dict(ln_g=ln_g, ln_b=ln_b, ple_w_gate=ple_w_gate, ple_w_proj=ple_w_proj, gdn_w_in=gdn_w_in,
                 gdn_conv_w=gdn_conv_w, gdn_a_log=gdn_a_log, gdn_dt_bias=gdn_dt_bias, gdn_norm_g=gdn_norm_g,
                 gdn_w_out=gdn_w_out, fox_w_in=fox_w_in, fox_b_f=fox_b_f, fox_q_norm_g=fox_q_norm_g,
                 fox_k_norm_g=fox_k_norm_g, fox_w_out=fox_w_out)
    mom_m = dict(ln_g=m_ln_g, ln_b=m_ln_b, ple_w_gate=m_ple_w_gate, ple_w_proj=m_ple_w_proj, gdn_w_in=m_gdn_w_in,
                 gdn_conv_w=m_gdn_conv_w, gdn_a_log=m_gdn_a_log, gdn_dt_bias=m_gdn_dt_bias, gdn_norm_g=m_gdn_norm_g,
                 gdn_w_out=m_gdn_w_out, fox_w_in=m_fox_w_in, fox_b_f=m_fox_b_f, fox_q_norm_g=m_fox_q_norm_g,
                 fox_k_norm_g=m_fox_k_norm_g, fox_w_out=m_fox_w_out)
    mom_v = dict(ln_g=v_ln_g, ln_b=v_ln_b, ple_w_gate=v_ple_w_gate, ple_w_proj=v_ple_w_proj, gdn_w_in=v_gdn_w_in,
                 gdn_conv_w=v_gdn_conv_w, gdn_a_log=v_gdn_a_log, gdn_dt_bias=v_gdn_dt_bias, gdn_norm_g=v_gdn_norm_g,
                 gdn_w_out=v_gdn_w_out, fox_w_in=v_fox_w_in, fox_b_f=v_fox_b_f, fox_q_norm_g=v_fox_q_norm_g,
                 fox_k_norm_g=v_fox_k_norm_g, fox_w_out=v_fox_w_out)

    items = [(name, idx) for name, _ in _SHARDED for idx in range(local[name].shape[0])]
    w_first = [it for it in items if it in _FIRST_WEIGHTS]
    w_later = [it for it in items if it not in _FIRST_WEIGHTS]
    g_early = [it for it in items if it not in _LAST_GRADS]
    g_last = [it for it in items if it in _LAST_GRADS]

    wts = {name: [None] * local[name].shape[0] for name, _ in _SHARDED}
    for name in _REPLICATED:
        wts[name] = local[name]
    packed, sizes = _pack_weights(local, w_first)
    got = _xy_exchange(packed, gather=True, name="gather_weights_first")
    for (name, idx), arr in _unpack_weights(got, local, w_first, sizes).items():
        wts[name][idx] = arr
    def layer_of(item):
        name, idx = item
        return idx if name.startswith("ple") else 2 * idx + (1 if name.startswith("fox") else 0)

    late_weights = []
    for group in ([it for it in w_later if layer_of(it) < 2], [it for it in w_later if layer_of(it) >= 2]):
        packed_w, sizes_w = _pack_weights(local, group)
        late_weights.append((packed_w, functools.partial(_unpack_weights, local=local, items=group, sizes=sizes_w)))
    early_sizes = []

    def pack_early(g):
        packed_g, sz = _pack_grads(g, g_early)
        early_sizes.extend(sz)
        return packed_g

    loss_row, dx, g, got_early = _local_step(
        x[0], p[:, 0], loss_target[0], wts,
        late_weights=late_weights,
        early_grads=pack_early)

    loss, small = _reduce_replicated({name: jnp.stack(g[name]) for name in _REPLICATED}, jnp.sum(loss_row))
    packed_last, last_sizes = _pack_grads(g, g_last)
    got_last = _xy_exchange(packed_last, gather=False, name="exchange_grads_last")
    mine, other = {}, {}
    for tag, res, its, szs in (("early", got_early, g_early, early_sizes), ("last", got_last, g_last, last_sizes)):
        part = _sum_slots(res, tr=SUM_ROWS, name=f"sum_grads_{tag}")
        sib = _c_swap(part, name=f"swap_grads_{tag}")
        mine.update(_unpack_grads(part, local, its, szs))
        other.update(_unpack_grads(sib, local, its, szs))

    outs = {}
    for name in _ORDER:
        if name in _SHARD_AXIS:
            n_layers = local[name].shape[0]
            parts = [jnp.stack([mine[(name, i)] for i in range(n_layers)]),
                     jnp.stack([other[(name, i)] for i in range(n_layers)])]
        else:
            parts = [small[name]]
        outs[name] = _adamw(local[name], parts, mom_m[name], mom_v[name], name=f"adamw_{name}")
    return (loss, dx[None], *[outs[n][0] for n in _ORDER], *[outs[n][1] for n in _ORDER],
            *[outs[n][2] for n in _ORDER], *[outs[n][3] for n in _ORDER])
```
